```python
import jax, jax.numpy as jnp
from jax import lax
import numpy as np

D_MODEL = 1024
BATCH = 8
SEQ = 2048
DEPTH = 4
DEC_BATCH = 128
DEC_SEQ = 4
PAST_LEN = 16384
PAGE_SIZE = 128

GROUP_WIDTH = D_MODEL // 4
HEAD_DIM = 64
RW_HEADS = GROUP_WIDTH // HEAD_DIM
RW_W_LORA = 32
RW_A_LORA = 32
RW_G_LORA = 64
RW_GN_EPS = 64e-5
HG_HEADS = 4
HG_DK = GROUP_WIDTH // HG_HEADS
HG_DV = GROUP_WIDTH // HG_HEADS
GLA_HEADS = 4
GLA_DK = GROUP_WIDTH // (2 * GLA_HEADS)
GLA_DV = GROUP_WIDTH // GLA_HEADS
GLA_GK_LORA = 16
GLA_GATE_NORM = 16.0
CHUNK = 64
S5_GROUP = 16
S5_NGROUPS = GROUP_WIDTH // S5_GROUP
S5_STATE = 64
N_EXPERTS = 32
TOP_K = 4
D_FF = D_MODEL
SWIGLU_ALPHA = 1.702
SWIGLU_LIMIT = 7.0
DN_ALPHA = (2.0 * DEPTH) ** 0.25
DN_BETA = (8.0 * DEPTH) ** -0.25
LN_EPS = 1e-5

RW_COLS = 3 * GROUP_WIDTH + RW_W_LORA + RW_A_LORA + RW_G_LORA
HG_COLS = 2 * HG_HEADS * HG_DK + 2 * HG_HEADS * HG_DV
GLA_COLS = 2 * GLA_HEADS * GLA_DK + 2 * GLA_HEADS * GLA_DV + GLA_GK_LORA
S5_COLS = GROUP_WIDTH
IN_COLS = RW_COLS + HG_COLS + GLA_COLS + S5_COLS

kernel_name = 'hybrid_rwkv7_hgrn2_gla_s5_moe_step'


def _split(t, widths):
    idx = [int(i) for i in np.cumsum(widths)[:-1]]
    return jnp.split(t, idx, axis=-1)


def _heads(t, n_heads):
    return t.reshape(t.shape[:-1] + (n_heads, t.shape[-1] // n_heads))


def _layer_norm(x, w, b):
    xf = x.astype(jnp.float32)
    xc = xf - jnp.mean(xf, axis=-1, keepdims=True)
    var = jnp.mean(xc * xc, axis=-1, keepdims=True)
    return (xc * lax.rsqrt(var + LN_EPS) * w + b).astype(x.dtype)


def _group_norm_heads(y, w, b, eps):
    yf = y.astype(jnp.float32)
    yc = yf - jnp.mean(yf, axis=-1, keepdims=True)
    var = jnp.mean(yc * yc, axis=-1, keepdims=True)
    return (yc * lax.rsqrt(var + eps)).reshape(y.shape[:-2] + (-1,)) * w + b


def _rms_norm_heads(y, w):
    yf = y.astype(jnp.float32)
    yn = yf * lax.rsqrt(jnp.mean(yf * yf, axis=-1, keepdims=True) + LN_EPS)
    return yn.reshape(y.shape[:-2] + (-1,)) * w


def _rwkv7_scan(r, w, k, v, a_vec, b_vec, s0):
    def step(S, inp):
        rt, wt, kt, vt, at, bt = inp
        sa = jnp.einsum('bhvk,bhk->bhv', S, at)
        S = S * wt[:, :, None, :] + sa[..., None] * bt[:, :, None, :] + vt[..., None] * kt[:, :, None, :]
        return S, jnp.einsum('bhvk,bhk->bhv', S, rt)
    xs = tuple(jnp.moveaxis(t.astype(jnp.float32), 1, 0) for t in (r, w, k, v, a_vec, b_vec))
    S, ys = lax.scan(step, s0.astype(jnp.float32), xs)
    return jnp.moveaxis(ys, 0, 1), S


def _chunked_gla(q, k, v, log_decay, s0, scale):
    f32 = jnp.float32
    Bsz, T, H, _ = q.shape
    dv = v.shape[-1]
    c = min(CHUNK, T)
    n = -(-T // c)
    pad = n * c - T

    def prep(t):
        t = jnp.pad(t.astype(f32), ((0, 0), (0, pad), (0, 0), (0, 0)))
        return t.reshape(Bsz, n, c, H, t.shape[-1]).transpose(1, 0, 3, 2, 4)

    qc, kc, vc, gc = prep(q * scale), prep(k), prep(v), prep(log_decay)
    causal = jnp.tril(jnp.ones((c, c), dtype=bool))[:, :, None]

    def step(S, inp):
        qi, ki, vi, gi = inp
        b = jnp.cumsum(gi, axis=2)
        inter = jnp.einsum('bhtk,bhkv->bhtv', qi * jnp.exp(b), S)
        rel = jnp.where(causal, b[:, :, :, None, :] - b[:, :, None, :, :], -jnp.inf)
        att = jnp.einsum('bhtk,bhsk,bhtsk->bhts', qi, ki, jnp.exp(rel))
        o = inter + jnp.einsum('bhts,bhsv->bhtv', att, vi)
        b_last = b[:, :, -1]
        S = S * jnp.exp(b_last)[..., None] + jnp.einsum('bhsk,bhsv->bhkv', ki * jnp.exp(b_last[:, :, None] - b), vi)
        return S, o

    S, oc = lax.scan(step, s0.astype(f32), (qc, kc, vc, gc))
    o = oc.transpose(1, 0, 3, 2, 4).reshape(Bsz, n * c, H, dv)[:, :T]
    return o, S


def _s5(u, x0_re, x0_im, A_re, A_im, log_dt, B_re, B_im, C_re, C_im, D, w_glu):
    f32 = jnp.float32
    Bsz, T, _ = u.shape
    A_re, A_im, B_re, B_im, C_re, C_im = (t.astype(f32) for t in (A_re, A_im, B_re, B_im, C_re, C_im))
    ug = u.reshape(Bsz, T, S5_NGROUPS, S5_GROUP).astype(f32)
    dt = jnp.exp(log_dt.astype(f32))[:, None]
    mag = jnp.exp(A_re * dt)
    ab_re, ab_im = mag * jnp.cos(A_im * dt), mag * jnp.sin(A_im * dt)
    den = A_re * A_re + A_im * A_im
    nr, ni = ab_re - 1.0, ab_im
    coef_re = (nr * A_re + ni * A_im) / den
    coef_im = (ni * A_re - nr * A_im) / den
    bb_re = coef_re[..., None] * B_re - coef_im[..., None] * B_im
    bb_im = coef_re[..., None] * B_im + coef_im[..., None] * B_re
    bu_re = jnp.einsum('gph,btgh->btgp', bb_re, ug)
    bu_im = jnp.einsum('gph,btgh->btgp', bb_im, ug)
    x0_re, x0_im = x0_re.astype(f32), x0_im.astype(f32)
    bu_re = bu_re.at[:, 0].add(ab_re * x0_re - ab_im * x0_im)
    bu_im = bu_im.at[:, 0].add(ab_re * x0_im + ab_im * x0_re)
    a_re = jnp.broadcast_to(ab_re, bu_re.shape)
    a_im = jnp.broadcast_to(ab_im, bu_im.shape)

    def combine(e1, e2):
        a1r, a1i, b1r, b1i = e1
        a2r, a2i, b2r, b2i = e2
        return (a2r * a1r - a2i * a1i, a2r * a1i + a2i * a1r,
                a2r * b1r - a2i * b1i + b2r, a2r * b1i + a2i * b1r + b2i)

    _, _, xr, xi = lax.associative_scan(combine, (a_re, a_im, bu_re, bu_im), axis=1)
    y = jnp.einsum('ghp,btgp->btgh', C_re, xr) - jnp.einsum('ghp,btgp->btgh', C_im, xi)
    y = y.reshape(Bsz, T, S5_COLS) + D * u.astype(f32)
    yg = jax.nn.gelu(y)
    out = yg * jax.nn.sigmoid(yg @ w_glu.astype(f32))
    return out, xr[:, -1], xi[:, -1]


def _moe(x, router_w, router_b, w1, b1, w2, b2):
    Bsz, T, D = x.shape
    xt = x.reshape(Bsz * T, D)
    logits = (xt @ router_w + router_b).astype(jnp.float32)
    top_vals, top_idx = lax.top_k(logits, TOP_K)
    gates = jax.nn.softmax(top_vals, axis=-1)
    combine = jnp.sum(jax.nn.one_hot(top_idx, N_EXPERTS, dtype=jnp.float32) * gates[..., None], axis=1)
    out = jnp.zeros((Bsz * T, D), jnp.float32)
    for e in range(N_EXPERTS):
        hdn = xt @ w1[e] + b1[e]
        h_glu = jnp.minimum(hdn[:, 0::2], SWIGLU_LIMIT)
        h_lin = jnp.clip(hdn[:, 1::2], -SWIGLU_LIMIT, SWIGLU_LIMIT)
        act = h_glu * jax.nn.sigmoid(SWIGLU_ALPHA * h_glu) * (h_lin + 1.0)
        out = out + combine[:, e:e + 1] * (act @ w2[e] + b2[e])
    return out.astype(x.dtype).reshape(Bsz, T, D)


def _token_mixers(x, l, lb, st_rw, st_sh, st_hg, st_gl, st_re, st_im, P):
    f32 = jnp.float32
    Bsz, T, _ = x.shape
    h = x @ P['w_in'][l]
    p_rw, p_hg, p_gl, u = _split(h, [RW_COLS, HG_COLS, GLA_COLS, S5_COLS])

    prev = jnp.concatenate([st_sh[:, None].astype(h.dtype), p_rw[:, :-1]], axis=1)
    xs = p_rw + (prev - p_rw) * P['rw_mu'][l]
    r, k, v, w_lo, a_lo, g_lo = _split(xs, [GROUP_WIDTH] * 3 + [RW_W_LORA, RW_A_LORA, RW_G_LORA])
    w_log = -jax.nn.softplus(-(P['rw_w0'][l] + jnp.tanh(w_lo) @ P['rw_w2'][l]).astype(f32)) - 0.5
    decay = jnp.exp(-jnp.exp(w_log))
    a = jax.nn.sigmoid((P['rw_a0'][l] + a_lo @ P['rw_a2'][l]).astype(f32))
    g_rw = jax.nn.sigmoid(g_lo) @ P['rw_g2'][l]
    kk = _heads((k * P['rw_kk'][l]).astype(f32), RW_HEADS)
    kk = kk * lax.rsqrt(jnp.maximum(jnp.sum(kk * kk, axis=-1, keepdims=True), 1e-24))
    k = k.astype(f32) * (1.0 + (a - 1.0) * P['rw_ka'][l])
    r_h, k_h, v_h, a_h, w_h = (_heads(t.astype(f32), RW_HEADS) for t in (r, k, v, a, decay))
    y_rw, S_rw = _rwkv7_scan(r_h, w_h, k_h, v_h, -kk, kk * a_h, st_rw)
    y_rw = _group_norm_heads(y_rw, P['rw_lnx_w'][l], P['rw_lnx_b'][l], RW_GN_EPS)
    bonus = jnp.sum(r_h * k_h * _heads(P['rw_rk'][l].astype(f32), RW_HEADS), axis=-1, keepdims=True) * v_h
    out_a = (y_rw + bonus.reshape(Bsz, T, GROUP_WIDTH)) * g_rw

    q, fx, iv, g_hg = _split(p_hg, [HG_HEADS * HG_DK] * 2 + [HG_HEADS * HG_DV] * 2)
    fx = fx.astype(f32)
    log_f = jnp.logaddexp(jnp.log(lb), jnp.log1p(-lb) + jax.nn.log_sigmoid(fx))
    key = (1.0 - lb) * jax.nn.sigmoid(-fx)
    o_hg, S_hg = _chunked_gla(_heads(jax.nn.silu(q), HG_HEADS), _heads(key, HG_HEADS),
                              _heads(iv, HG_HEADS), _heads(log_f, HG_HEADS), st_hg, HG_DK ** -0.5)
    out_b = _rms_norm_heads(o_hg, P['hg_norm_w'][l]) * jax.nn.silu(g_hg.astype(f32))

    qg, kg, vg, gk_lo, g_gl = _split(p_gl, [GLA_HEADS * GLA_DK] * 2 + [GLA_HEADS * GLA_DV, GLA_GK_LORA, GLA_HEADS * GLA_DV])
    gk = jax.nn.log_sigmoid((gk_lo @ P['gla_w_gk2'][l] + P['gla_b_gk'][l]).astype(f32)) / GLA_GATE_NORM
    o_gl, S_gl = _chunked_gla(_heads(qg, GLA_HEADS), _heads(kg, GLA_HEADS), _heads(vg, GLA_HEADS),
                              _heads(gk, GLA_HEADS), st_gl, GLA_DK ** -0.5)
    out_c = _rms_norm_heads(o_gl, P['gla_norm_w'][l]) * jax.nn.silu(g_gl.astype(f32))

    out_d, s_re, s_im = _s5(u, st_re, st_im, P['s5_A_re'][l], P['s5_A_im'][l], P['s5_log_dt'][l],
                            P['s5_B_re'][l], P['s5_B_im'][l], P['s5_C_re'][l], P['s5_C_im'][l],
                            P['s5_D'][l], P['s5_w_glu'][l])

    merged = jnp.concatenate([out_a, out_b, out_c, out_d], axis=-1).astype(x.dtype)
    mix = merged @ P['w_out'][l]
    new_states = (S_rw.astype(st_rw.dtype), p_rw[:, -1].astype(st_sh.dtype), S_hg.astype(st_hg.dtype),
                  S_gl.astype(st_gl.dtype), s_re.astype(st_re.dtype), s_im.astype(st_im.dtype))
    return mix, new_states


def _trunk(x, s_rw, s_sh, s_hg, s_gl, s_re, s_im, P):
    lbs = jnp.cumsum(jax.nn.softmax(P['hg_lb_logits'].astype(jnp.float32), axis=0), axis=0)
    lbs = lbs - lbs[:1]
    collected = ([], [], [], [], [], [])
    for l in range(DEPTH):
        mix, st = _token_mixers(x, l, lbs[l], s_rw[l], s_sh[l], s_hg[l], s_gl[l], s_re[l], s_im[l], P)
        x = _layer_norm(DN_ALPHA * x + mix, P['ln1_w'][l], P['ln1_b'][l])
        ffn = _moe(x, P['router_w'][l], P['router_b'][l], P['exp_w1'][l], P['exp_b1'][l], P['exp_w2'][l], P['exp_b2'][l])
        x = _layer_norm(DN_ALPHA * x + ffn, P['ln2_w'][l], P['ln2_b'][l])
        for bucket, s in zip(collected, st):
            bucket.append(s)
    return x, [jnp.stack(bucket) for bucket in collected]


def setup_inputs(seed: int = 0) -> dict:
    key = jax.random.key(seed)
    ks = iter(jax.random.split(key, 64))
    f32 = jnp.float32

    def nrm(shape, scale):
        return jax.random.normal(next(ks), shape, f32) * scale

    def gain(shape):
        return 1.0 + nrm(shape, 0.02)

    G, P_, E = S5_NGROUPS, S5_STATE, N_EXPERTS
    inp = {}
    inp['x_prompt'] = nrm((BATCH, SEQ, D_MODEL), 1.0)
    inp['x_sample'] = nrm((DEC_BATCH, DEC_SEQ, D_MODEL), 1.0)
    inp['state_rwkv'] = nrm((DEPTH, DEC_BATCH, RW_HEADS, HEAD_DIM, HEAD_DIM), 0.5)
    inp['state_rwkv_shift'] = nrm((DEPTH, DEC_BATCH, RW_COLS), 1.0)
    inp['state_hgrn'] = nrm((DEPTH, DEC_BATCH, HG_HEADS, HG_DK, HG_DV), 0.5)
    inp['state_gla'] = nrm((DEPTH, DEC_BATCH, GLA_HEADS, GLA_DK, GLA_DV), 0.5)
    inp['state_s5_re'] = nrm((DEPTH, DEC_BATCH, G, P_), 0.1)
    inp['state_s5_im'] = nrm((DEPTH, DEC_BATCH, G, P_), 0.1)
    inp['w_in'] = nrm((DEPTH, D_MODEL, IN_COLS), D_MODEL ** -0.5)
    inp['rw_mu'] = jax.random.uniform(next(ks), (DEPTH, RW_COLS), f32)
    inp['rw_w0'] = nrm((DEPTH, GROUP_WIDTH), 0.5) - 0.5
    inp['rw_w2'] = nrm((DEPTH, RW_W_LORA, GROUP_WIDTH), 0.1)
    inp['rw_a0'] = nrm((DEPTH, GROUP_WIDTH), 0.1)
    inp['rw_a2'] = nrm((DEPTH, RW_A_LORA, GROUP_WIDTH), 0.1)
    inp['rw_g2'] = nrm((DEPTH, RW_G_LORA, GROUP_WIDTH), RW_G_LORA ** -0.5)
    inp['rw_kk'] = 0.85 + nrm((DEPTH, GROUP_WIDTH), 0.02)
    inp['rw_ka'] = gain((DEPTH, GROUP_WIDTH))
    inp['rw_rk'] = nrm((DEPTH, GROUP_WIDTH), 0.1)
    inp['rw_lnx_w'] = gain((DEPTH, GROUP_WIDTH))
    inp['rw_lnx_b'] = nrm((DEPTH, GROUP_WIDTH), 0.02)
    inp['hg_lb_logits'] = nrm((DEPTH, HG_HEADS * HG_DK), 0.5)
    inp['hg_norm_w'] = gain((DEPTH, HG_HEADS * HG_DV))
    inp['gla_w_gk2'] = nrm((DEPTH, GLA_GK_LORA, GLA_HEADS * GLA_DK), GLA_GK_LORA ** -0.5)
    inp['gla_b_gk'] = nrm((DEPTH, GLA_HEADS * GLA_DK), 0.1)
    inp['gla_norm_w'] = gain((DEPTH, GLA_HEADS * GLA_DV))
    inp['s5_A_re'] = -0.5 + nrm((DEPTH, G, P_), 0.01)
    inp['s5_A_im'] = jnp.pi * jnp.arange(P_, dtype=f32) + nrm((DEPTH, G, P_), 0.01)
    inp['s5_log_dt'] = jax.random.uniform(next(ks), (DEPTH, G), f32, minval=float(np.log(1e-3)), maxval=float(np.log(1e-1)))
    inp['s5_B_re'] = nrm((DEPTH, G, P_, S5_GROUP), (2.0 * S5_GROUP) ** -0.5)
    inp['s5_B_im'] = nrm((DEPTH, G, P_, S5_GROUP), (2.0 * S5_GROUP) ** -0.5)
    inp['s5_C_re'] = nrm((DEPTH, G, S5_GROUP, P_), (2.0 * P_) ** -0.5)
    inp['s5_C_im'] = nrm((DEPTH, G, S5_GROUP, P_), (2.0 * P_) ** -0.5)
    inp['s5_D'] = nrm((DEPTH, S5_COLS), 0.5)
    inp['s5_w_glu'] = nrm((DEPTH, S5_COLS, S5_COLS), S5_COLS ** -0.5)
    inp['w_out'] = nrm((DEPTH, D_MODEL, D_MODEL), DN_BETA * D_MODEL ** -0.5)
    inp['ln1_w'] = gain((DEPTH, D_MODEL))
    inp['ln1_b'] = nrm((DEPTH, D_MODEL), 0.02)
    inp['router_w'] = nrm((DEPTH, D_MODEL, E), D_MODEL ** -0.5)
    inp['router_b'] = nrm((DEPTH, E), 0.01)
    inp['exp_w1'] = nrm((DEPTH, E, D_MODEL, 2 * D_FF), D_MODEL ** -0.5)
    inp['exp_b1'] = nrm((DEPTH, E, 2 * D_FF), 0.01)
    inp['exp_w2'] = nrm((DEPTH, E, D_FF, D_MODEL), DN_BETA * D_FF ** -0.5)
    inp['exp_b2'] = nrm((DEPTH, E, D_MODEL), 0.01)
    inp['ln2_w'] = gain((DEPTH, D_MODEL))
    inp['ln2_b'] = nrm((DEPTH, D_MODEL), 0.02)
    return inp


def reference(x_prompt, x_sample, state_rwkv, state_rwkv_shift, state_hgrn, state_gla, state_s5_re, state_s5_im,
              w_in, rw_mu, rw_w0, rw_w2, rw_a0, rw_a2, rw_g2, rw_kk, rw_ka, rw_rk, rw_lnx_w, rw_lnx_b,
              hg_lb_logits, hg_norm_w, gla_w_gk2, gla_b_gk, gla_norm_w,
              s5_A_re, s5_A_im, s5_log_dt, s5_B_re, s5_B_im, s5_C_re, s5_C_im, s5_D, s5_w_glu,
              w_out, ln1_w, ln1_b, router_w, router_b, exp_w1, exp_b1, exp_w2, exp_b2, ln2_w, ln2_b):
    P = {'w_in': w_in, 'rw_mu': rw_mu, 'rw_w0': rw_w0, 'rw_w2': rw_w2, 'rw_a0': rw_a0, 'rw_a2': rw_a2,
         'rw_g2': rw_g2, 'rw_kk': rw_kk, 'rw_ka': rw_ka, 'rw_rk': rw_rk, 'rw_lnx_w': rw_lnx_w, 'rw_lnx_b': rw_lnx_b,
         'hg_lb_logits': hg_lb_logits, 'hg_norm_w': hg_norm_w, 'gla_w_gk2': gla_w_gk2, 'gla_b_gk': gla_b_gk,
         'gla_norm_w': gla_norm_w, 's5_A_re': s5_A_re, 's5_A_im': s5_A_im, 's5_log_dt': s5_log_dt,
         's5_B_re': s5_B_re, 's5_B_im': s5_B_im, 's5_C_re': s5_C_re, 's5_C_im': s5_C_im, 's5_D': s5_D,
         's5_w_glu': s5_w_glu, 'w_out': w_out, 'ln1_w': ln1_w, 'ln1_b': ln1_b, 'router_w': router_w,
         'router_b': router_b, 'exp_w1': exp_w1, 'exp_b1': exp_b1, 'exp_w2': exp_w2, 'exp_b2': exp_b2,
         'ln2_w': ln2_w, 'ln2_b': ln2_b}
    bp, dt = x_prompt.shape[0], x_prompt.dtype
    z_rw = jnp.zeros((DEPTH, bp, RW_HEADS, HEAD_DIM, HEAD_DIM), dt)
    z_sh = jnp.zeros((DEPTH, bp, RW_COLS), dt)
    z_hg = jnp.zeros((DEPTH, bp, HG_HEADS, HG_DK, HG_DV), dt)
    z_gl = jnp.zeros((DEPTH, bp, GLA_HEADS, GLA_DK, GLA_DV), dt)
    z_s5 = jnp.zeros((DEPTH, bp, S5_NGROUPS, S5_STATE), dt)
    y_prompt, ps = _trunk(x_prompt, z_rw, z_sh, z_hg, z_gl, z_s5, z_s5, P)
    y_sample, ss = _trunk(x_sample, state_rwkv, state_rwkv_shift, state_hgrn, state_gla, state_s5_re, state_s5_im, P)
    return (y_prompt, y_sample, ps[0], ps[1], ps[2], ps[3], ps[4], ps[5], ss[0], ss[1], ss[2], ss[3], ss[4], ss[5])
```

```python
import functools

import jax
import jax.numpy as jnp
import numpy as np
from jax import lax
from jax.experimental import pallas as pl
from jax.experimental.pallas import tpu as pltpu

F32 = jnp.float32
BF16 = jnp.bfloat16
I32 = jnp.int32

D_MODEL = 1024
DEPTH = 4
GROUP_WIDTH = 256
HEAD_DIM = 64
N_HEADS = 4
RW_COLS = 896
RW_GN_EPS = 64e-5
HG_DK = 64
GLA_DK = 32
GLA_GK_LORA = 16
GLA_GATE_NORM = 16.0
S5_NGROUPS = 16
S5_GROUP = 16
S5_STATE = 64
S5_CH = S5_NGROUPS * S5_STATE
N_EXPERTS = 32
TOP_K = 4
D_FF = 1024
SWIGLU_ALPHA = 1.702
SWIGLU_LIMIT = 7.0
DN_ALPHA = (2.0 * DEPTH) ** 0.25
LN_EPS = 1e-5

SUBLANES = 8
LANES = 128
VMEM_LIMIT_BYTES = 48 * 1024 * 1024

OFF_HG = 0
OFF_GLA_VG = 1024
OFF_GLA_QK = 1536
OFF_RW = 1920
OFF_S5 = 2816
IN_PAD = 3072

ROW_TILE = 256
SUB_CHUNK = 16
SEQ_GROUP = SUBLANES
MOE_TILE = 256
DISPATCH_TOKENS = 512


def _cparams(n_axes):
    return pltpu.CompilerParams(dimension_semantics=("arbitrary",) * n_axes,
                                vmem_limit_bytes=VMEM_LIMIT_BYTES)


def _dot(a, b):
    return jnp.dot(a, b, preferred_element_type=F32)


def _split2(x):
    hi = x.astype(BF16)
    lo = (x - hi.astype(F32)).astype(BF16)
    return hi, lo


def _split3(x):
    hi = x.astype(BF16)
    r = x - hi.astype(F32)
    mid = r.astype(BF16)
    lo = (r - mid.astype(F32)).astype(BF16)
    return hi, mid, lo


def _dot3(a, b):
    ah, al = _split2(a)
    bh, bl = _split2(b)
    return _dot(ah, bh) + (_dot(ah, bl) + _dot(al, bh))


def _seg_ones(n_in, seg_in, n_out, seg_out):
    r = lax.broadcasted_iota(I32, (n_in, n_out), 0) // seg_in
    c = lax.broadcasted_iota(I32, (n_in, n_out), 1) // seg_out
    return (r == c).astype(BF16)


def _segsum(x, seg):
    rows = x.shape[0]
    hi, lo = _split2(x)
    both = _dot(jnp.concatenate([hi, lo], axis=0), seg)
    return both[:rows] + both[rows:]


def _sigmoid(x):
    return 1.0 / (1.0 + jnp.exp(-x))


def _log_sigmoid(x):
    return jnp.minimum(x, 0.0) - jnp.log1p(jnp.exp(-jnp.abs(x)))


def _softplus(x):
    return jnp.maximum(x, 0.0) + jnp.log1p(jnp.exp(-jnp.abs(x)))


def _layer_norm(x, w, b):
    xc = x - jnp.mean(x, axis=-1, keepdims=True)
    var = jnp.mean(xc * xc, axis=-1, keepdims=True)
    return xc * lax.rsqrt(var + LN_EPS) * w + b


def _inproj_kernel(x_ref, w_ref, o_ref):
    o_ref[...] = _dot(x_ref[...].astype(BF16), w_ref[...])


def _inproj(x_all, w_in_l):
    n = x_all.shape[0]
    return pl.pallas_call(
        _inproj_kernel,
        grid=(n // ROW_TILE,),
        in_specs=[pl.BlockSpec((ROW_TILE, D_MODEL), lambda i: (i, 0)),
                  pl.BlockSpec((D_MODEL, IN_PAD), lambda i: (0, 0))],
        out_specs=pl.BlockSpec((ROW_TILE, IN_PAD), lambda i: (i, 0)),
        out_shape=jax.ShapeDtypeStruct((n, IN_PAD), F32),
        compiler_params=_cparams(1),
        name="inproj",
    )(x_all, w_in_l)


def _gated_tile(q, k, v, g, st_ref, o_ref, q_s, k_s, v_s, b_s, qh_s, kh_s, dt_s, *, dk, n_blk):
    c = SUB_CHUNK
    rows, hk = q.shape
    rr = lax.broadcasted_iota(I32, (rows, rows), 0)
    cc = lax.broadcasted_iota(I32, (rows, rows), 1)
    same = (rr // c) == (cc // c)
    tri = jnp.concatenate([(same & (cc <= rr)).astype(BF16), same.astype(BF16)], axis=0)
    g3 = jnp.concatenate(_split3(g), axis=1)
    p = _dot(tri, g3)
    b = p[:rows, :hk] + p[:rows, hk:2 * hk] + p[:rows, 2 * hk:]
    btot = p[rows:, :hk] + p[rows:, hk:2 * hk] + p[rows:, 2 * hk:]
    q_s[...] = q
    k_s[...] = k
    v_s[...] = v
    b_s[...] = b
    qh_s[...] = q * jnp.exp(b)
    kh_s[...] = k * jnp.exp(btot - b)
    dt_s[...] = jnp.exp(btot)
    seg = _seg_ones(hk, dk, GROUP_WIDTH, HEAD_DIM)
    bd_mask = (lax.broadcasted_iota(I32, (GROUP_WIDTH, hk), 0) // HEAD_DIM
               == lax.broadcasted_iota(I32, (GROUP_WIDTH, hk), 1) // dk).astype(F32)
    t_iota = lax.broadcasted_iota(I32, (c, hk), 0)

    def block(i, carry):
        r0 = pl.multiple_of(i * c, c)
        qb = q_s[pl.ds(r0, c), :]
        kb = k_s[pl.ds(r0, c), :]
        vb = v_s[pl.ds(r0, c), :]
        bb = b_s[pl.ds(r0, c), :]
        pieces = []
        for s in range(c):
            d = jnp.where(t_iota >= s, bb - bb[s:s + 1, :], -jnp.inf)
            pieces.append(jnp.exp(d) * qb * kb[s:s + 1, :])
        att = _segsum(jnp.concatenate(pieces, axis=0), seg)
        o = att[0:c, :] * vb[0:1, :]
        for s in range(1, c):
            o = o + att[s * c:(s + 1) * c, :] * vb[s:s + 1, :]
        st = st_ref[...]
        o = o + lax.dot_general(qh_s[pl.ds(r0, c), :].astype(BF16), st.astype(BF16),
                                (((1,), (1,)), ((), ())), preferred_element_type=F32)
        upd = lax.dot_general(vb.astype(BF16), kh_s[pl.ds(r0, c), :].astype(BF16),
                              (((0,), (0,)), ((), ())), preferred_element_type=F32)
        st_ref[...] = st * dt_s[pl.ds(r0, 1), :] + upd * bd_mask
        o_ref[pl.ds(r0, c), :] = o
        return carry

    lax.fori_loop(0, n_blk, block, 0)


def _rms_heads(o, w, gate, seg):
    ms = _segsum(o * o, seg) * (1.0 / HEAD_DIM)
    return o * lax.rsqrt(ms + LN_EPS) * w * (gate * _sigmoid(gate))


def _load_rows(ref, sample, pad_s, t_valid):
    if not sample:
        return ref[...]
    pad_s[...] = jnp.zeros_like(pad_s)
    pad_s[0:t_valid, :] = ref[0]
    return pad_s[...]


def _hgrn_kernel(h_ref, s0_ref, lb_ref, nw_ref, o_ref, sT_ref,
                 st_s, o_s, q_s, k_s, v_s, b_s, qh_s, kh_s, dt_s, *pad, sample, t_valid, n_blk):
    i = pl.program_id(1)

    @pl.when(i == 0)
    def _():
        st_s[...] = s0_ref[0]

    x = _load_rows(h_ref, sample, pad[0] if sample else None, t_valid)
    rows = x.shape[0]
    q = x[:, 0:256]
    fx = x[:, 256:512]
    iv = x[:, 512:768]
    gate = x[:, 768:1024]
    lb = lb_ref[0:1, :]
    log_lb = lb_ref[1:2, :]
    log1m_lb = lb_ref[2:3, :]
    cterm = log1m_lb + _log_sigmoid(fx)
    log_f = jnp.maximum(log_lb, cterm) + jnp.log1p(jnp.exp(-jnp.abs(log_lb - cterm)))
    key = (1.0 - lb) * _sigmoid(-fx)
    qs = q * _sigmoid(q) * (HG_DK ** -0.5)
    if sample:
        valid = lax.broadcasted_iota(I32, (rows, GROUP_WIDTH), 0) < t_valid
        log_f = jnp.where(valid, log_f, 0.0)
        key = jnp.where(valid, key, 0.0)
        iv = jnp.where(valid, iv, 0.0)
    _gated_tile(qs, key, iv, log_f, st_s, o_s, q_s, k_s, v_s, b_s, qh_s, kh_s, dt_s, dk=HG_DK, n_blk=n_blk)
    seg = _seg_ones(GROUP_WIDTH, HEAD_DIM, GROUP_WIDTH, HEAD_DIM)
    out = _rms_heads(o_s[...], nw_ref[...], gate, seg)
    if sample:
        o_ref[0] = out[0:t_valid, :]
    else:
        o_ref[...] = out
    sT_ref[0] = st_s[...]


def _gla_kernel(hvg_ref, hqk_ref, s0_ref, wgk_ref, bgk_ref, nw_ref, o_ref, sT_ref,
                st_s, o_s, q_s, k_s, v_s, b_s, qh_s, kh_s, dt_s, *pad, sample, t_valid, n_blk):
    i = pl.program_id(1)

    @pl.when(i == 0)
    def _():
        st_s[...] = s0_ref[0]

    xvg = _load_rows(hvg_ref, sample, pad[0] if sample else None, t_valid)
    xqk = _load_rows(hqk_ref, sample, pad[1] if sample else None, t_valid)
    rows = xvg.shape[0]
    v = xvg[:, 0:256]
    gate = xvg[:, 256:512]
    q = xqk[:, 0:128] * (GLA_DK ** -0.5)
    k = xqk[:, 128:256]
    lo = xqk[:, 256:384]
    gk = _log_sigmoid(_dot3(lo, wgk_ref[...]) + bgk_ref[...]) * (1.0 / GLA_GATE_NORM)
    if sample:
        valid = lax.broadcasted_iota(I32, (rows, 128), 0) < t_valid
        gk = jnp.where(valid, gk, 0.0)
        k = jnp.where(valid, k, 0.0)
    _gated_tile(q, k, v, gk, st_s, o_s, q_s, k_s, v_s, b_s, qh_s, kh_s, dt_s, dk=GLA_DK, n_blk=n_blk)
    seg = _seg_ones(GROUP_WIDTH, HEAD_DIM, GROUP_WIDTH, HEAD_DIM)
    out = _rms_heads(o_s[...], nw_ref[...], gate, seg)
    if sample:
        o_ref[0] = out[0:t_valid, :]
    else:
        o_ref[...] = out
    sT_ref[0] = st_s[...]


def _gated_call(kind, h_all, row0, n_seq, t_len, s0_bd, params):
    sample = t_len < SUB_CHUNK
    hk = N_HEADS * (HG_DK if kind == "hgrn" else GLA_DK)
    if sample:
        rows, n_t, n_blk = SUB_CHUNK, 1, 1
        h_view = h_all.reshape(h_all.shape[0] // t_len, t_len, IN_PAD)
        seq0 = row0 // t_len

        def hspec(width, col_block):
            return pl.BlockSpec((1, t_len, width), lambda b, i: (seq0 + b, 0, col_block))

        o_shape = jax.ShapeDtypeStruct((n_seq, t_len, GROUP_WIDTH), F32)
        o_spec = pl.BlockSpec((1, t_len, GROUP_WIDTH), lambda b, i: (b, 0, 0))
    else:
        rows = min(t_len, ROW_TILE)
        n_t, n_blk = t_len // rows, rows // SUB_CHUNK
        h_view = h_all
        blk0 = row0 // rows

        def hspec(width, col_block):
            return pl.BlockSpec((rows, width), lambda b, i: (blk0 + b * n_t + i, col_block))

        o_shape = jax.ShapeDtypeStruct((n_seq * t_len, GROUP_WIDTH), F32)
        o_spec = pl.BlockSpec((rows, GROUP_WIDTH), lambda b, i: (b * n_t + i, 0))

    st_spec = pl.BlockSpec((1, GROUP_WIDTH, hk), lambda b, i: (b, 0, 0))
    st_shape = jax.ShapeDtypeStruct((n_seq, GROUP_WIDTH, hk), F32)

    def full(a):
        return pl.BlockSpec(a.shape, lambda b, i: (0,) * a.ndim)

    scratch = [pltpu.VMEM((GROUP_WIDTH, hk), F32), pltpu.VMEM((rows, GROUP_WIDTH), F32),
               pltpu.VMEM((rows, hk), F32), pltpu.VMEM((rows, hk), F32), pltpu.VMEM((rows, GROUP_WIDTH), F32),
               pltpu.VMEM((rows, hk), F32), pltpu.VMEM((rows, hk), F32), pltpu.VMEM((rows, hk), F32),
               pltpu.VMEM((rows, hk), F32)]
    if kind == "hgrn":
        lb3, nw = params
        body = functools.partial(_hgrn_kernel, sample=sample, t_valid=t_len, n_blk=n_blk)
        in_specs = [hspec(1024, OFF_HG // 1024), st_spec, full(lb3), full(nw)]
        args = (h_view, s0_bd, lb3, nw)
        if sample:
            scratch.append(pltpu.VMEM((rows, 1024), F32))
    else:
        wgk, bgk, nw = params
        body = functools.partial(_gla_kernel, sample=sample, t_valid=t_len, n_blk=n_blk)
        in_specs = [hspec(512, OFF_GLA_VG // 512), hspec(384, OFF_GLA_QK // 384), st_spec,
                    full(wgk), full(bgk), full(nw)]
        args = (h_view, h_view, s0_bd, wgk, bgk, nw)
        if sample:
            scratch += [pltpu.VMEM((rows, 512), F32), pltpu.VMEM((rows, 384), F32)]
    out, st = pl.pallas_call(
        body,
        grid=(n_seq, n_t),
        in_specs=in_specs,
        out_specs=(o_spec, st_spec),
        out_shape=(o_shape, st_shape),
        scratch_shapes=scratch,
        compiler_params=_cparams(2),
        name=kind + ("_sample" if sample else "_prompt"),
    )(*args)
    return out.reshape(n_seq * t_len, GROUP_WIDTH), st


def _rwkv_kernel(p_ref, sh0_ref, s0_ref, mu_ref, lora_ref, vec_ref, o_ref, s_out_ref,
                 s_s, prev_s, seg_s, lm_s, a_s, b_s, w_s, k_s, r_s, v_s, y_s, *, n_tok):
    i = pl.program_id(1)
    g8 = SEQ_GROUP

    @pl.when(i == 0)
    def _():
        s_s[...] = s0_ref[0]
        prev_s[...] = sh0_ref[0]

    seg_s[...] = _seg_ones(GROUP_WIDTH, HEAD_DIM, GROUP_WIDTH, HEAD_DIM)
    lm_s[...] = (lax.broadcasted_iota(I32, (HEAD_DIM, g8, GROUP_WIDTH), 2) % HEAD_DIM
                 == lax.broadcasted_iota(I32, (HEAD_DIM, g8, GROUP_WIDTH), 0)).astype(F32)
    seg = seg_s[...]

    p = p_ref[...]
    if n_tok > 1:
        prev = jnp.concatenate([prev_s[...], p[:-g8, :]], axis=0)
    else:
        prev = prev_s[...]
    prev_s[...] = p[(n_tok - 1) * g8:, :]
    xs = p + (prev - p) * mu_ref[...]
    r = xs[:, 0:256]
    k = xs[:, 256:512]
    v = xs[:, 512:768]
    lo = xs[:, 768:896]
    w0, a0, kkp, ka = vec_ref[0:1, :], vec_ref[1:2, :], vec_ref[2:3, :], vec_ref[3:4, :]
    rk, lnw, lnb = vec_ref[4:5, :], vec_ref[5:6, :], vec_ref[6:7, :]
    w_log = -_softplus(-(w0 + _dot3(jnp.tanh(lo), lora_ref[0]))) - 0.5
    decay = jnp.exp(-jnp.exp(w_log))
    a = _sigmoid(a0 + _dot3(lo, lora_ref[1]))
    g = _dot3(_sigmoid(lo), lora_ref[2])
    kk = k * kkp
    kk = kk * lax.rsqrt(jnp.maximum(_segsum(kk * kk, seg), 1e-24))
    k2 = k * (1.0 + (a - 1.0) * ka)
    a_s[...] = -kk
    b_s[...] = kk * a
    w_s[...] = decay
    k_s[...] = k2
    r_s[...] = r
    v_s[...] = v

    def step(t, carry):
        r0 = pl.multiple_of(t * g8, g8)
        s = s_s[...]
        lm = lm_s[...]
        sg = seg_s[...]
        pa = (s * a_s[pl.ds(r0, g8), :][None]).reshape(HEAD_DIM * g8, GROUP_WIDTH)
        vm = (v_s[pl.ds(r0, g8), :][None] * lm).reshape(HEAD_DIM * g8, GROUP_WIDTH)
        vh, vl = _split2(vm)
        both = _dot(jnp.concatenate([pa.astype(BF16), vh, vl], axis=0), sg)
        n = HEAD_DIM * g8
        sa = both[0:n].reshape(HEAD_DIM, g8, GROUP_WIDTH)
        vb = (both[n:2 * n] + both[2 * n:]).reshape(HEAD_DIM, g8, GROUP_WIDTH)
        s_new = (s * w_s[pl.ds(r0, g8), :][None] + sa * b_s[pl.ds(r0, g8), :][None]
                 + vb * k_s[pl.ds(r0, g8), :][None])
        s_s[...] = s_new
        yb = _dot((s_new * r_s[pl.ds(r0, g8), :][None]).reshape(n, GROUP_WIDTH).astype(BF16), sg)
        y_s[pl.ds(r0, g8), :] = jnp.sum(yb.reshape(HEAD_DIM, g8, GROUP_WIDTH) * lm, axis=0)
        return carry

    lax.fori_loop(0, n_tok, step, 0)

    y = y_s[...]
    mean = _segsum(y, seg) * (1.0 / HEAD_DIM)
    yc = y - mean
    var = _segsum(yc * yc, seg) * (1.0 / HEAD_DIM)
    yn = yc * lax.rsqrt(var + RW_GN_EPS) * lnw + lnb
    bonus = _segsum(r * k2 * rk, seg) * v
    o_ref[...] = (yn + bonus) * g
    s_out_ref[0] = s_s[...]


def _rwkv_call(p_tm, shift0, s0, mu, lora, vec, n_grp, t_len):
    n_tok = min(t_len, 64)
    n_t = t_len // n_tok
    rows = n_tok * SEQ_GROUP

    def full(a):
        return pl.BlockSpec(a.shape, lambda b, i: (0,) * a.ndim)

    st_spec = pl.BlockSpec((1, HEAD_DIM, SEQ_GROUP, GROUP_WIDTH), lambda b, i: (b, 0, 0, 0))
    vm = lambda shape, dt=F32: pltpu.VMEM(shape, dt)
    out, s_out = pl.pallas_call(
        functools.partial(_rwkv_kernel, n_tok=n_tok),
        grid=(n_grp, n_t),
        in_specs=[pl.BlockSpec((None, rows, RW_COLS), lambda b, i: (b, i, 0)),
                  pl.BlockSpec((1, SEQ_GROUP, RW_COLS), lambda b, i: (b, 0, 0)),
                  st_spec, full(mu), full(lora), full(vec)],
        out_specs=(pl.BlockSpec((None, rows, GROUP_WIDTH), lambda b, i: (b, i, 0)), st_spec),
        out_shape=(jax.ShapeDtypeStruct((n_grp, t_len * SEQ_GROUP, GROUP_WIDTH), F32),
                   jax.ShapeDtypeStruct((n_grp, HEAD_DIM, SEQ_GROUP, GROUP_WIDTH), F32)),
        scratch_shapes=[vm((HEAD_DIM, SEQ_GROUP, GROUP_WIDTH)), vm((SEQ_GROUP, RW_COLS)),
                        vm((GROUP_WIDTH, GROUP_WIDTH), BF16), vm((HEAD_DIM, SEQ_GROUP, GROUP_WIDTH)),
                        vm((rows, GROUP_WIDTH)), vm((rows, GROUP_WIDTH)), vm((rows, GROUP_WIDTH)),
                        vm((rows, GROUP_WIDTH)), vm((rows, GROUP_WIDTH)), vm((rows, GROUP_WIDTH)),
                        vm((rows, GROUP_WIDTH))],
        compiler_params=_cparams(2),
        name="rwkv_t%d" % t_len,
    )(p_tm, shift0, s0, mu, lora, vec)
    return out, s_out


def _s5_prep_kernel(are_ref, aim_ref, ldt_ref, bre_ref, bim_ref, abre_ref, abim_ref, bbre_ref, bbim_ref):
    a_re, a_im = are_ref[...], aim_ref[...]
    dt = jnp.exp(ldt_ref[...])
    mag = jnp.exp(a_re * dt)
    ab_re = mag * jnp.cos(a_im * dt)
    ab_im = mag * jnp.sin(a_im * dt)
    den = a_re * a_re + a_im * a_im
    nr, ni = ab_re - 1.0, ab_im
    coef_re = (nr * a_re + ni * a_im) / den
    coef_im = (ni * a_re - nr * a_im) / den
    b_re, b_im = bre_ref[...], bim_ref[...]
    abre_ref[...] = ab_re
    abim_ref[...] = ab_im
    bbre_ref[...] = coef_re * b_re - coef_im * b_im
    bbim_ref[...] = coef_re * b_im + coef_im * b_re


def _s5_prep(a_re, a_im, log_dt, b_re, b_im):
    rows = DEPTH * S5_NGROUPS
    cols = S5_STATE * S5_GROUP
    rep = lambda t: jnp.repeat(t.reshape(rows, S5_STATE), S5_GROUP, axis=1)
    ldt = jnp.broadcast_to(log_dt.reshape(rows, 1), (rows, cols))
    shp = jax.ShapeDtypeStruct((rows, cols), F32)
    ab_re, ab_im, bb_re, bb_im = pl.pallas_call(
        _s5_prep_kernel, out_shape=(shp, shp, shp, shp), name="s5_prep",
    )(rep(a_re), rep(a_im), ldt, b_re.reshape(rows, cols), b_im.reshape(rows, cols))
    pick = lambda t: t.reshape(DEPTH, S5_NGROUPS, S5_STATE, S5_GROUP)[..., 0].reshape(DEPTH, 1, S5_CH)
    bb = lambda t: t.reshape(DEPTH, S5_NGROUPS, S5_STATE, S5_GROUP)
    return pick(ab_re), pick(ab_im), bb(bb_re), bb(bb_im)


def _s5_kernel(u_ref, x0_ref, ab_ref, wb_ref, wc_ref, d_ref, wglu_ref, o_ref, xT_ref,
               x_s, bu_s, xs_s, *, n_tok):
    i = pl.program_id(1)
    g8 = SEQ_GROUP

    @pl.when(i == 0)
    def _():
        x_s[...] = x0_ref[0]

    u = u_ref[...]
    bu_s[...] = _dot(u.astype(BF16), wb_ref[...])
    a_re = jnp.broadcast_to(ab_ref[0:1, :], (g8, S5_CH))
    a_im = jnp.broadcast_to(ab_ref[1:2, :], (g8, S5_CH))

    def step(t, carry):
        x_re, x_im = carry
        r0 = pl.multiple_of(t * g8, g8)
        n_re = a_re * x_re - a_im * x_im + bu_s[pl.ds(r0, g8), 0:S5_CH]
        n_im = a_re * x_im + a_im * x_re + bu_s[pl.ds(r0, g8), S5_CH:]
        xs_s[pl.ds(r0, g8), 0:S5_CH] = n_re
        xs_s[pl.ds(r0, g8), S5_CH:] = n_im
        return n_re, n_im

    x_re, x_im = lax.fori_loop(0, n_tok, step, (x_s[:, 0:S5_CH], x_s[:, S5_CH:]))
    x_s[:, 0:S5_CH] = x_re
    x_s[:, S5_CH:] = x_im
    y = _dot(xs_s[...].astype(BF16), wc_ref[...]) + d_ref[...] * u
    yg = 0.5 * y * (1.0 + jnp.tanh(0.7978845608028654 * (y + 0.044715 * (y * y * y))))
    o_ref[...] = yg * _sigmoid(_dot(yg.astype(BF16), wglu_ref[...]))
    xT_ref[0] = x_s[...]


def _s5_call(u_tm, x0, ab, wb, wc, dvec, wglu, n_grp, t_len):
    n_tok = min(t_len, 64)
    n_t = t_len // n_tok
    rows = n_tok * SEQ_GROUP

    def full(a):
        return pl.BlockSpec(a.shape, lambda b, i: (0,) * a.ndim)

    st_spec = pl.BlockSpec((1, SEQ_GROUP, 2 * S5_CH), lambda b, i: (b, 0, 0))
    return pl.pallas_call(
        functools.partial(_s5_kernel, n_tok=n_tok),
        grid=(n_grp, n_t),
        in_specs=[pl.BlockSpec((None, rows, GROUP_WIDTH), lambda b, i: (b, i, 0)), st_spec,
                  full(ab), full(wb), full(wc), full(dvec), full(wglu)],
        out_specs=(pl.BlockSpec((None, rows, GROUP_WIDTH), lambda b, i: (b, i, 0)), st_spec),
        out_shape=(jax.ShapeDtypeStruct((n_grp, t_len * SEQ_GROUP, GROUP_WIDTH), F32),
                   jax.ShapeDtypeStruct((n_grp, SEQ_GROUP, 2 * S5_CH), F32)),
        scratch_shapes=[pltpu.VMEM((SEQ_GROUP, 2 * S5_CH), F32), pltpu.VMEM((rows, 2 * S5_CH), F32),
                        pltpu.VMEM((rows, 2 * S5_CH), F32)],
        compiler_params=_cparams(2),
        name="s5_t%d" % t_len,
    )(u_tm, x0, ab, wb, wc, dvec, wglu)


CHUNKS = D_MODEL // LANES


def _store_chunked(ref, x):
    rows = x.shape[0]
    for j in range(CHUNKS):
        ref[pl.ds(j, rows, stride=CHUNKS), :] = x[:, j * LANES:(j + 1) * LANES]


def _load_chunked(ref, rows):
    return jnp.concatenate([ref[pl.ds(j, rows, stride=CHUNKS), :] for j in range(CHUNKS)], axis=1)


def _post_mix_kernel(x_ref, oa_ref, ob_ref, oc_ref, od_ref, wout_ref, lnw_ref, lnb_ref, rw_ref, rb_ref,
                     x1_ref, x1c_ref, idx_ref, rank_ref, gate_ref, cnt_ref, carry_s):
    i = pl.program_id(0)

    @pl.when(i == 0)
    def _():
        carry_s[...] = jnp.zeros_like(carry_s)

    mix = _dot(oa_ref[...].astype(BF16), wout_ref[0:256, :])
    mix += _dot(ob_ref[...].astype(BF16), wout_ref[256:512, :])
    mix += _dot(oc_ref[...].astype(BF16), wout_ref[512:768, :])
    mix += _dot(od_ref[...].astype(BF16), wout_ref[768:1024, :])
    x1 = _layer_norm(DN_ALPHA * x_ref[...] + mix, lnw_ref[...], lnb_ref[...])
    x1_ref[...] = x1
    _store_chunked(x1c_ref, x1)

    xh, xl = _split2(x1)
    logits = _dot(xh, rw_ref[0]) + (_dot(xh, rw_ref[1]) + _dot(xl, rw_ref[0])) + rb_ref[...]
    rows = logits.shape[0]
    lane = lax.broadcasted_iota(I32, (rows, LANES), 1)
    work = logits
    sel_i, sel_v = [], []
    for _ in range(TOP_K):
        m = jnp.max(work, axis=-1, keepdims=True)
        j = jnp.min(jnp.where(work == m, lane, LANES), axis=-1, keepdims=True)
        sel_i.append(j)
        sel_v.append(m)
        work = jnp.where(lane == j, -jnp.inf, work)
    e = [jnp.exp(v - sel_v[0]) for v in sel_v]
    den = (e[0] + e[1]) + (e[2] + e[3])
    onehot = jnp.zeros((rows, LANES), F32)
    for j in sel_i:
        onehot = onehot + (lane == j).astype(F32)
    rr = lax.broadcasted_iota(I32, (rows, rows), 0)
    cc = lax.broadcasted_iota(I32, (rows, rows), 1)
    before = _dot((cc < rr).astype(BF16), onehot.astype(BF16)) + carry_s[0:1, :]
    carry_s[0:1, :] = carry_s[0:1, :] + jnp.sum(onehot, axis=0, keepdims=True)
    idx_o = jnp.zeros((rows, LANES), I32)
    rank_o = jnp.zeros((rows, LANES), I32)
    gate_o = jnp.zeros((rows, LANES), F32)
    for slot in range(TOP_K):
        j = sel_i[slot]
        rank = jnp.sum(jnp.where(lane == j, before, 0.0), axis=-1, keepdims=True)
        idx_o = jnp.where(lane == slot, j, idx_o)
        rank_o = jnp.where(lane == slot, rank.astype(I32), rank_o)
        gate_o = jnp.where(lane == slot, e[slot] / den, gate_o)
    idx_ref[...] = idx_o
    rank_ref[...] = rank_o
    gate_ref[...] = gate_o
    cnt_ref[...] = jnp.broadcast_to(carry_s[0:1, :], cnt_ref.shape)


def _post_mix(x_all, outs, wout_l, lnw, lnb, rw_l, rb_l):
    n = x_all.shape[0]
    row = lambda w: pl.BlockSpec((ROW_TILE, w), lambda i: (i, 0))

    def full(a):
        return pl.BlockSpec(a.shape, lambda i: (0,) * a.ndim)

    meta = jax.ShapeDtypeStruct((n, LANES), I32)
    return pl.pallas_call(
        _post_mix_kernel,
        grid=(n // ROW_TILE,),
        in_specs=[row(D_MODEL), row(256), row(256), row(256), row(256), full(wout_l), full(lnw), full(lnb),
                  full(rw_l), full(rb_l)],
        out_specs=(row(D_MODEL), pl.BlockSpec((ROW_TILE * CHUNKS, LANES), lambda i: (i, 0)),
                   row(LANES), row(LANES), row(LANES), pl.BlockSpec((SUBLANES, LANES), lambda i: (0, 0))),
        out_shape=(jax.ShapeDtypeStruct((n, D_MODEL), F32), jax.ShapeDtypeStruct((n * CHUNKS, LANES), F32),
                   meta, meta, jax.ShapeDtypeStruct((n, LANES), F32),
                   jax.ShapeDtypeStruct((SUBLANES, LANES), F32)),
        scratch_shapes=[pltpu.VMEM((SUBLANES, LANES), F32)],
        compiler_params=_cparams(1),
        name="post_mix",
    )(x_all, *outs, wout_l, lnw, lnb, rw_l, rb_l)


def _dispatch_kernel(gend_ref, dest_ref, x_hbm, xs_hbm, zero_s, sem, *, tokens, n_tiles):
    i = pl.program_id(0)

    def zero_tile(first_row):
        start = pl.multiple_of(first_row * CHUNKS, MOE_TILE * CHUNKS)
        return pltpu.make_async_copy(zero_s, xs_hbm.at[pl.ds(start, MOE_TILE * CHUNKS)], sem)

    @pl.when(i == 0)
    def _():
        zero_s[...] = jnp.zeros_like(zero_s)
        for e in range(N_EXPERTS):
            @pl.when(gend_ref[e + 1] > gend_ref[e])
            def _():
                zero_tile(gend_ref[e + 1] - MOE_TILE).start()
        for e in range(N_EXPERTS):
            @pl.when(gend_ref[e + 1] > gend_ref[e])
            def _():
                zero_tile(gend_ref[e + 1] - MOE_TILE).wait()

        def tail(t, carry):
            cp = zero_tile(t * MOE_TILE)
            cp.start()
            cp.wait()
            return carry

        lax.fori_loop(gend_ref[N_EXPERTS] // MOE_TILE, n_tiles, tail, 0)

    base = i * tokens

    def row_copy(n, slot):
        src = pl.multiple_of((base + n) * CHUNKS, CHUNKS)
        dst = pl.multiple_of(dest_ref[n * TOP_K + slot] * CHUNKS, CHUNKS)
        return pltpu.make_async_copy(x_hbm.at[pl.ds(src, CHUNKS)], xs_hbm.at[pl.ds(dst, CHUNKS)], sem)

    def issue(n, carry):
        for slot in range(TOP_K):
            row_copy(n, slot).start()
        return carry

    def drain(n, carry):
        for slot in range(TOP_K):
            row_copy(n, slot).wait()
        return carry

    lax.fori_loop(0, tokens, issue, 0)
    lax.fori_loop(0, tokens, drain, 0)


def _dispatch(x1c, dest_flat, gend, n_rows):
    n = x1c.shape[0] // CHUNKS
    tokens = DISPATCH_TOKENS if n % DISPATCH_TOKENS == 0 else ROW_TILE
    return pl.pallas_call(
        functools.partial(_dispatch_kernel, tokens=tokens, n_tiles=n_rows // MOE_TILE),
        grid_spec=pltpu.PrefetchScalarGridSpec(
            num_scalar_prefetch=1,
            grid=(n // tokens,),
            in_specs=[pl.BlockSpec((tokens * TOP_K,), lambda i, ge: (i,), memory_space=pltpu.SMEM),
                      pl.BlockSpec(memory_space=pl.ANY)],
            out_specs=pl.BlockSpec(memory_space=pl.ANY),
            scratch_shapes=[pltpu.VMEM((MOE_TILE * CHUNKS, LANES), F32), pltpu.SemaphoreType.DMA(())],
        ),
        out_shape=jax.ShapeDtypeStruct((n_rows * CHUNKS, LANES), F32),
        compiler_params=_cparams(1),
        name="moe_dispatch",
    )(gend, dest_flat, x1c)


def _expert_kernel(te_ref, nreal_ref, xs_ref, w1_ref, b1_ref, w2_ref, b2_ref, o_ref, acc_s):
    i = pl.program_id(0)

    @pl.when(i < nreal_ref[0])
    def _():
        x = _load_chunked(xs_ref, MOE_TILE).astype(BF16)
        acc_s[...] = jnp.broadcast_to(b2_ref[...], acc_s.shape)
        for c in range(D_FF // 256):
            cols = slice(c * 256, (c + 1) * 256)
            lin_cols = slice(D_FF + c * 256, D_FF + (c + 1) * 256)
            h_glu = jnp.minimum(_dot(x, w1_ref[:, cols]) + b1_ref[:, cols], SWIGLU_LIMIT)
            h_lin = jnp.clip(_dot(x, w1_ref[:, lin_cols]) + b1_ref[:, lin_cols], -SWIGLU_LIMIT, SWIGLU_LIMIT)
            act = h_glu * _sigmoid(SWIGLU_ALPHA * h_glu) * (h_lin + 1.0)
            acc_s[...] += _dot(act.astype(BF16), w2_ref[cols, :])
        _store_chunked(o_ref, acc_s[...])

    @pl.when(i >= nreal_ref[0])
    def _():
        o_ref[...] = jnp.zeros_like(o_ref)


def _experts(xs, te, nreal, w1d, b1d, w2b, b2, layer):
    n_tiles = xs.shape[0] // (MOE_TILE * CHUNKS)
    clamp = lambda i, nr: jnp.minimum(i, nr[0] - 1)
    return pl.pallas_call(
        _expert_kernel,
        grid_spec=pltpu.PrefetchScalarGridSpec(
            num_scalar_prefetch=2,
            grid=(n_tiles,),
            in_specs=[pl.BlockSpec((MOE_TILE * CHUNKS, LANES), lambda i, te, nr: (clamp(i, nr), 0)),
                      pl.BlockSpec((None, None, D_MODEL, 2 * D_FF), lambda i, te, nr: (layer, te[i], 0, 0)),
                      pl.BlockSpec((None, None, 1, 2 * D_FF), lambda i, te, nr: (layer, te[i], 0, 0)),
                      pl.BlockSpec((None, None, D_FF, D_MODEL), lambda i, te, nr: (layer, te[i], 0, 0)),
                      pl.BlockSpec((None, None, 1, D_MODEL), lambda i, te, nr: (layer, te[i], 0, 0))],
            out_specs=pl.BlockSpec((MOE_TILE * CHUNKS, LANES), lambda i, te, nr: (i, 0)),
            scratch_shapes=[pltpu.VMEM((MOE_TILE, D_MODEL), F32)],
        ),
        out_shape=jax.ShapeDtypeStruct(xs.shape, F32),
        compiler_params=_cparams(1),
        name="moe_experts",
    )(te, nreal, xs, w1d, b1d, w2b, b2)


def _combine_kernel(dest_ref, gate_ref, x1_ref, lnw_ref, lnb_ref, ys_hbm, o_ref, buf_s, sem, *, tokens):
    def row_copy(n, slot):
        src = pl.multiple_of(dest_ref[n * TOP_K + slot] * CHUNKS, CHUNKS)
        dst = pl.multiple_of(n * CHUNKS, CHUNKS)
        return pltpu.make_async_copy(ys_hbm.at[pl.ds(src, CHUNKS)], buf_s.at[slot, pl.ds(dst, CHUNKS)], sem)

    def issue(n, carry):
        for slot in range(TOP_K):
            row_copy(n, slot).start()
        return carry

    def drain(n, carry):
        for slot in range(TOP_K):
            row_copy(n, slot).wait()
        return carry

    lax.fori_loop(0, tokens, issue, 0)
    lax.fori_loop(0, tokens, drain, 0)
    gate = gate_ref[...]
    ffn = gate[:, 0:1] * _load_chunked(buf_s.at[0], tokens)
    for slot in range(1, TOP_K):
        ffn = ffn + gate[:, slot:slot + 1] * _load_chunked(buf_s.at[slot], tokens)
    o_ref[...] = _layer_norm(DN_ALPHA * x1_ref[...] + ffn, lnw_ref[...], lnb_ref[...])


def _combine(dest_flat, gate, x1, lnw, lnb, ys):
    n = x1.shape[0]
    tokens = ROW_TILE
    return pl.pallas_call(
        functools.partial(_combine_kernel, tokens=tokens),
        grid=(n // tokens,),
        in_specs=[pl.BlockSpec((tokens * TOP_K,), lambda i: (i,), memory_space=pltpu.SMEM),
                  pl.BlockSpec((tokens, LANES), lambda i: (i, 0)),
                  pl.BlockSpec((tokens, D_MODEL), lambda i: (i, 0)),
                  pl.BlockSpec((1, D_MODEL), lambda i: (0, 0)),
                  pl.BlockSpec((1, D_MODEL), lambda i: (0, 0)),
                  pl.BlockSpec(memory_space=pl.ANY)],
        out_specs=pl.BlockSpec((tokens, D_MODEL), lambda i: (i, 0)),
        out_shape=jax.ShapeDtypeStruct((n, D_MODEL), F32),
        scratch_shapes=[pltpu.VMEM((TOP_K, tokens * CHUNKS, LANES), F32), pltpu.SemaphoreType.DMA(())],
        compiler_params=_cparams(1),
        name="moe_combine",
    )(dest_flat, gate, x1, lnw, lnb, ys)


def _moe(x1, x1c, idx, rank, gate, counts, w1d, b1d, w2b, b2, lnw, lnb, layer):
    n = x1.shape[0]
    n_tiles = -(-(n * TOP_K + N_EXPERTS * (MOE_TILE - 1)) // MOE_TILE)
    cnt = counts[0, :N_EXPERTS].astype(I32)
    gsz = ((cnt + (MOE_TILE - 1)) // MOE_TILE) * MOE_TILE
    gend = jnp.cumsum(gsz)
    goff = gend - gsz
    dest = (goff[idx[:, :TOP_K]] + rank[:, :TOP_K]).reshape(-1)
    gend0 = jnp.concatenate([jnp.zeros((1,), I32), gend])
    nreal = (gend[-1:] // MOE_TILE).astype(I32)
    te = jnp.minimum(jnp.searchsorted(gend, jnp.arange(n_tiles, dtype=I32) * MOE_TILE, side="right"),
                     N_EXPERTS - 1).astype(I32)
    xs = _dispatch(x1c, dest, gend0, n_tiles * MOE_TILE)
    ys = _experts(xs, te, nreal, w1d, b1d, w2b, b2, layer)
    return _combine(dest, gate, x1, lnw, lnb, ys)


def _to_time_major(rows, n_seq, t_len):
    c = rows.shape[-1]
    x = rows.reshape(n_seq // SEQ_GROUP, SEQ_GROUP, t_len, c)
    return jnp.transpose(x, (0, 2, 1, 3)).reshape(n_seq // SEQ_GROUP, t_len * SEQ_GROUP, c)


def _from_time_major(x, n_seq, t_len):
    c = x.shape[-1]
    x = x.reshape(n_seq // SEQ_GROUP, t_len, SEQ_GROUP, c)
    return jnp.transpose(x, (0, 2, 1, 3)).reshape(n_seq * t_len, c)


def _block_diag_state(s, dk):
    st = jnp.swapaxes(s, 2, 3)
    eye = jnp.eye(N_HEADS, dtype=s.dtype)
    return jnp.einsum("bhvk,hg->bhvgk", st, eye).reshape(s.shape[0], GROUP_WIDTH, N_HEADS * dk)


def _unblock_state(st, dk):
    b = st.shape[0]
    x = st.reshape(b, N_HEADS, HEAD_DIM, N_HEADS, dk)
    x = jnp.stack([x[:, h, :, h, :] for h in range(N_HEADS)], axis=1)
    return jnp.swapaxes(x, 2, 3)


def _rwkv_state_in(s):
    b = s.shape[0]
    x = s.reshape(b // SEQ_GROUP, SEQ_GROUP, N_HEADS, HEAD_DIM, HEAD_DIM)
    return jnp.transpose(x, (0, 3, 1, 2, 4)).reshape(b // SEQ_GROUP, HEAD_DIM, SEQ_GROUP, GROUP_WIDTH)


def _rwkv_state_out(x, b):
    x = x.reshape(b // SEQ_GROUP, HEAD_DIM, SEQ_GROUP, N_HEADS, HEAD_DIM)
    return jnp.transpose(x, (0, 2, 3, 1, 4)).reshape(b, N_HEADS, HEAD_DIM, HEAD_DIM)


def _pad_rows(w, row0, n_rows):
    out = jnp.zeros((n_rows, w.shape[1]), w.dtype)
    return out.at[row0:row0 + w.shape[0]].set(w)


def kernel(x_prompt, x_sample, state_rwkv, state_rwkv_shift, state_hgrn, state_gla, state_s5_re, state_s5_im,
           w_in, rw_mu, rw_w0, rw_w2, rw_a0, rw_a2, rw_g2, rw_kk, rw_ka, rw_rk, rw_lnx_w, rw_lnx_b,
           hg_lb_logits, hg_norm_w, gla_w_gk2, gla_b_gk, gla_norm_w,
           s5_A_re, s5_A_im, s5_log_dt, s5_B_re, s5_B_im, s5_C_re, s5_C_im, s5_D, s5_w_glu,
           w_out, ln1_w, ln1_b, router_w, router_b, exp_w1, exp_b1, exp_w2, exp_b2, ln2_w, ln2_b):
    bp, tp, _ = x_prompt.shape
    bs, ts, _ = x_sample.shape
    n_p, n_s = bp * tp, bs * ts
    groups = ((0, bp, tp), (n_p, bs, ts))

    c = np.cumsum([0, RW_COLS, 1024, 784, 256])
    rw_c, hg_c, gl_c, s5_c = (w_in[:, :, c[j]:c[j + 1]] for j in range(4))
    gl_q, gl_k, gl_v, gl_lo, gl_g = (gl_c[:, :, a:b] for a, b in
                                     ((0, 128), (128, 256), (256, 512), (512, 528), (528, 784)))
    zpad = jnp.zeros((DEPTH, D_MODEL, 128 - GLA_GK_LORA), w_in.dtype)
    w_in_p = jnp.concatenate([hg_c, gl_v, gl_g, gl_q, gl_k, gl_lo, zpad, rw_c, s5_c], axis=2).astype(BF16)
    w_out_b = w_out.astype(BF16)
    w1d = jnp.concatenate([exp_w1[..., 0::2], exp_w1[..., 1::2]], axis=-1).astype(BF16)
    b1d = jnp.concatenate([exp_b1[..., 0::2], exp_b1[..., 1::2]], axis=-1)[:, :, None, :]
    w2b = exp_w2.astype(BF16)
    b2r = exp_b2[:, :, None, :]
    rw_pad = jnp.pad(router_w, ((0, 0), (0, 0), (0, LANES - N_EXPERTS)))
    rw_hi = rw_pad.astype(BF16)
    rw_lo = (rw_pad - rw_hi.astype(F32)).astype(BF16)
    rw_split = jnp.stack([rw_hi, rw_lo], axis=1)
    rb_pad = jnp.pad(router_b, ((0, 0), (0, LANES - N_EXPERTS)), constant_values=-1e30)[:, None, :]

    lbs = jnp.cumsum(jax.nn.softmax(hg_lb_logits.astype(F32), axis=0), axis=0)
    lbs = lbs - lbs[:1]
    lb3 = jnp.stack([lbs, jnp.log(lbs), jnp.log1p(-lbs)], axis=1)
    lb3 = jnp.pad(lb3, ((0, 0), (0, SUBLANES - 3), (0, 0)))

    ab_re, ab_im, bb_re, bb_im = _s5_prep(s5_A_re, s5_A_im, s5_log_dt, s5_B_re, s5_B_im)
    eye_g = jnp.eye(S5_NGROUPS, dtype=F32)
    wb = jnp.stack([jnp.einsum("lgph,gk->lghkp", t, eye_g).reshape(DEPTH, GROUP_WIDTH, S5_CH)
                    for t in (bb_re, bb_im)], axis=2).reshape(DEPTH, GROUP_WIDTH, 2 * S5_CH).astype(BF16)
    wc = jnp.concatenate([jnp.einsum("lghp,gk->lgpkh", t, eye_g).reshape(DEPTH, S5_CH, GROUP_WIDTH)
                          for t in (s5_C_re, -s5_C_im)], axis=1).astype(BF16)
    ab = jnp.concatenate([ab_re, ab_im], axis=1)
    ab = jnp.pad(ab, ((0, 0), (0, SUBLANES - 2), (0, 0)))
    wglu_b = s5_w_glu.astype(BF16)

    lora = jnp.stack([jnp.stack([_pad_rows(rw_w2[l], 0, 128), _pad_rows(rw_a2[l], 32, 128),
                                 _pad_rows(rw_g2[l], 64, 128)]) for l in range(DEPTH)])
    rw_vec = jnp.stack([rw_w0, rw_a0, rw_kk, rw_ka, rw_rk, rw_lnx_w, rw_lnx_b, jnp.zeros_like(rw_w0)], axis=1)
    wgk = jnp.stack([_pad_rows(gla_w_gk2[l], 0, 128) for l in range(DEPTH)])

    zeros = lambda shape: jnp.zeros(shape, F32)
    st_in = (
        dict(rw=zeros((DEPTH, bp, N_HEADS, HEAD_DIM, HEAD_DIM)), sh=zeros((DEPTH, bp, RW_COLS)),
             hg=zeros((DEPTH, bp, N_HEADS, HG_DK, HEAD_DIM)), gl=zeros((DEPTH, bp, N_HEADS, GLA_DK, HEAD_DIM)),
             re=zeros((DEPTH, bp, S5_NGROUPS, S5_STATE)), im=zeros((DEPTH, bp, S5_NGROUPS, S5_STATE))),
        dict(rw=state_rwkv, sh=state_rwkv_shift, hg=state_hgrn, gl=state_gla, re=state_s5_re, im=state_s5_im),
    )
    collected = ([], [])

    x_all = jnp.concatenate([x_prompt.reshape(n_p, D_MODEL), x_sample.reshape(n_s, D_MODEL)], axis=0)
    for l in range(DEPTH):
        h_all = _inproj(x_all, w_in_p[l])
        outs = [[], [], [], []]
        for gi, (row0, n_seq, t_len) in enumerate(groups):
            st = st_in[gi]
            n_grp = n_seq // SEQ_GROUP
            rows = slice(row0, row0 + n_seq * t_len)
            p_rw = h_all[rows, OFF_RW:OFF_RW + RW_COLS]
            p_tm = _to_time_major(p_rw, n_seq, t_len)
            sh0 = st["sh"][l].reshape(n_grp, SEQ_GROUP, RW_COLS)
            oa_tm, s_rw = _rwkv_call(p_tm, sh0, _rwkv_state_in(st["rw"][l]), rw_mu[l][None, :], lora[l],
                                     rw_vec[l], n_grp, t_len)
            outs[0].append(_from_time_major(oa_tm, n_seq, t_len))
            new_rw = _rwkv_state_out(s_rw, n_seq)
            new_sh = p_rw.reshape(n_seq, t_len, RW_COLS)[:, -1]
            ob, s_hg = _gated_call("hgrn", h_all, row0, n_seq, t_len, _block_diag_state(st["hg"][l], HG_DK),
                                   (lb3[l], hg_norm_w[l][None, :]))
            outs[1].append(ob)
            oc, s_gl = _gated_call("gla", h_all, row0, n_seq, t_len, _block_diag_state(st["gl"][l], GLA_DK),
                                   (wgk[l], gla_b_gk[l][None, :], gla_norm_w[l][None, :]))
            outs[2].append(oc)
            u_tm = _to_time_major(h_all[rows, OFF_S5:OFF_S5 + GROUP_WIDTH], n_seq, t_len)
            x0 = jnp.concatenate([st["re"][l].reshape(n_grp, SEQ_GROUP, S5_CH),
                                  st["im"][l].reshape(n_grp, SEQ_GROUP, S5_CH)], axis=-1)
            od_tm, x_t = _s5_call(u_tm, x0, ab[l], wb[l], wc[l], s5_D[l][None, :], wglu_b[l], n_grp, t_len)
            outs[3].append(_from_time_major(od_tm, n_seq, t_len))
            x_t = x_t.reshape(n_seq, 2, S5_NGROUPS, S5_STATE)
            collected[gi].append((new_rw, new_sh, _unblock_state(s_hg, HG_DK), _unblock_state(s_gl, GLA_DK),
                                  x_t[:, 0], x_t[:, 1]))
        merged = [jnp.concatenate(o, axis=0) for o in outs]
        x1, x1c, idx, rank, gate, counts = _post_mix(x_all, merged, w_out_b[l], ln1_w[l][None, :], ln1_b[l][None, :],
                                                rw_split[l], rb_pad[l])
        x_all = _moe(x1, x1c, idx, rank, gate, counts, w1d, b1d, w2b, b2r, ln2_w[l][None, :], ln2_b[l][None, :], l)

    y_prompt = x_all[:n_p].reshape(bp, tp, D_MODEL)
    y_sample = x_all[n_p:].reshape(bs, ts, D_MODEL)
    ps = [jnp.stack([layer[j] for layer in collected[0]]) for j in range(6)]
    ss = [jnp.stack([layer[j] for layer in collected[1]]) for j in range(6)]
    return (y_prompt, y_sample, *ps, *ss)
```

```python
import functools

import jax
import jax.numpy as jnp
import numpy as np
from jax import lax
from jax.experimental import pallas as pl
from jax.experimental.pallas import tpu as pltpu

F32 = jnp.float32
BF16 = jnp.bfloat16
I32 = jnp.int32

D_MODEL = 1024
DEPTH = 4
GROUP_WIDTH = 256
HEAD_DIM = 64
N_HEADS = 4
RW_COLS = 896
RW_GN_EPS = 64e-5
HG_DK = 64
GLA_DK = 32
GLA_GK_LORA = 16
GLA_GATE_NORM = 16.0
S5_NGROUPS = 16
S5_GROUP = 16
S5_STATE = 64
S5_CH = S5_NGROUPS * S5_STATE
N_EXPERTS = 32
TOP_K = 4
D_FF = 1024
SWIGLU_ALPHA = 1.702
SWIGLU_LIMIT = 7.0
DN_ALPHA = (2.0 * DEPTH) ** 0.25
LN_EPS = 1e-5

SUBLANES = 8
LANES = 128
VMEM_LIMIT_BYTES = 48 * 1024 * 1024

OFF_HG = 0
OFF_GLA_VG = 1024
OFF_GLA_QK = 1536
OFF_RW = 1920
OFF_S5 = 2816
IN_PAD = 3072

ROW_TILE = 256
SUB_CHUNK = 16
SEQ_GROUP = SUBLANES
MOE_TILE = 256
DISPATCH_TOKENS = 512


def _cparams(n_axes):
    return pltpu.CompilerParams(dimension_semantics=("arbitrary",) * n_axes,
                                vmem_limit_bytes=VMEM_LIMIT_BYTES)


def _dot(a, b):
    return jnp.dot(a, b, preferred_element_type=F32)


def _split2(x):
    hi = x.astype(BF16)
    lo = (x - hi.astype(F32)).astype(BF16)
    return hi, lo


def _split3(x):
    hi = x.astype(BF16)
    r = x - hi.astype(F32)
    mid = r.astype(BF16)
    lo = (r - mid.astype(F32)).astype(BF16)
    return hi, mid, lo


def _dot3(a, b):
    ah, al = _split2(a)
    bh, bl = _split2(b)
    return _dot(ah, bh) + (_dot(ah, bl) + _dot(al, bh))


def _seg_ones(n_in, seg_in, n_out, seg_out):
    r = lax.broadcasted_iota(I32, (n_in, n_out), 0) // seg_in
    c = lax.broadcasted_iota(I32, (n_in, n_out), 1) // seg_out
    return (r == c).astype(BF16)


def _segsum(x, seg):
    rows = x.shape[0]
    hi, lo = _split2(x)
    both = _dot(jnp.concatenate([hi, lo], axis=0), seg)
    return both[:rows] + both[rows:]


def _sigmoid(x):
    return 1.0 / (1.0 + jnp.exp(-x))


def _log_sigmoid(x):
    return jnp.minimum(x, 0.0) - jnp.log1p(jnp.exp(-jnp.abs(x)))


def _softplus(x):
    return jnp.maximum(x, 0.0) + jnp.log1p(jnp.exp(-jnp.abs(x)))


def _layer_norm(x, w, b):
    xc = x - jnp.mean(x, axis=-1, keepdims=True)
    var = jnp.mean(xc * xc, axis=-1, keepdims=True)
    return xc * lax.rsqrt(var + LN_EPS) * w + b


def _inproj_kernel(x_ref, w_ref, o_ref):
    o_ref[...] = _dot(x_ref[...].astype(BF16), w_ref[...])


def _inproj(x_all, w_in_l):
    n = x_all.shape[0]
    return pl.pallas_call(
        _inproj_kernel,
        grid=(n // ROW_TILE,),
        in_specs=[pl.BlockSpec((ROW_TILE, D_MODEL), lambda i: (i, 0)),
                  pl.BlockSpec((D_MODEL, IN_PAD), lambda i: (0, 0))],
        out_specs=pl.BlockSpec((ROW_TILE, IN_PAD), lambda i: (i, 0)),
        out_shape=jax.ShapeDtypeStruct((n, IN_PAD), F32),
        compiler_params=_cparams(1),
        name="inproj",
    )(x_all, w_in_l)


def _gated_tile(q, k, v, g, st_ref, o_ref, q_s, k_s, v_s, b_s, qh_s, kh_s, dt_s, *, dk, n_blk):
    c = SUB_CHUNK
    rows, hk = q.shape
    rr = lax.broadcasted_iota(I32, (rows, rows), 0)
    cc = lax.broadcasted_iota(I32, (rows, rows), 1)
    same = (rr // c) == (cc // c)
    tri = jnp.concatenate([(same & (cc <= rr)).astype(BF16), same.astype(BF16)], axis=0)
    g3 = jnp.concatenate(_split3(g), axis=1)
    p = _dot(tri, g3)
    b = p[:rows, :hk] + p[:rows, hk:2 * hk] + p[:rows, 2 * hk:]
    btot = p[rows:, :hk] + p[rows:, hk:2 * hk] + p[rows:, 2 * hk:]
    q_s[...] = q
    k_s[...] = k
    v_s[...] = v
    b_s[...] = b
    qh_s[...] = q * jnp.exp(b)
    kh_s[...] = k * jnp.exp(btot - b)
    dt_s[...] = jnp.exp(btot)
    seg = _seg_ones(hk, dk, GROUP_WIDTH, HEAD_DIM)
    bd_mask = (lax.broadcasted_iota(I32, (GROUP_WIDTH, hk), 0) // HEAD_DIM
               == lax.broadcasted_iota(I32, (GROUP_WIDTH, hk), 1) // dk).astype(F32)
    t_iota = lax.broadcasted_iota(I32, (c, hk), 0)

    def block(i, carry):
        r0 = pl.multiple_of(i * c, c)
        qb = q_s[pl.ds(r0, c), :]
        kb = k_s[pl.ds(r0, c), :]
        vb = v_s[pl.ds(r0, c), :]
        bb = b_s[pl.ds(r0, c), :]
        pieces = []
        for s in range(c):
            d = jnp.where(t_iota >= s, bb - bb[s:s + 1, :], -jnp.inf)
            pieces.append(jnp.exp(d) * qb * kb[s:s + 1, :])
        att = _segsum(jnp.concatenate(pieces, axis=0), seg)
        o = att[0:c, :] * vb[0:1, :]
        for s in range(1, c):
            o = o + att[s * c:(s + 1) * c, :] * vb[s:s + 1, :]
        st = st_ref[...]
        o = o + lax.dot_general(qh_s[pl.ds(r0, c), :].astype(BF16), st.astype(BF16),
                                (((1,), (1,)), ((), ())), preferred_element_type=F32)
        upd = lax.dot_general(vb.astype(BF16), kh_s[pl.ds(r0, c), :].astype(BF16),
                              (((0,), (0,)), ((), ())), preferred_element_type=F32)
        st_ref[...] = st * dt_s[pl.ds(r0, 1), :] + upd * bd_mask
        o_ref[pl.ds(r0, c), :] = o
        return carry

    lax.fori_loop(0, n_blk, block, 0)


def _rms_heads(o, w, gate, seg):
    ms = _segsum(o * o, seg) * (1.0 / HEAD_DIM)
    return o * lax.rsqrt(ms + LN_EPS) * w * (gate * _sigmoid(gate))


def _load_rows(ref, sample, pad_s, t_valid):
    if not sample:
        return ref[...]
    pad_s[...] = jnp.zeros_like(pad_s)
    pad_s[0:t_valid, :] = ref[0]
    return pad_s[...]


def _hgrn_kernel(h_ref, s0_ref, lb_ref, nw_ref, o_ref, sT_ref,
                 st_s, o_s, q_s, k_s, v_s, b_s, qh_s, kh_s, dt_s, *pad, sample, t_valid, n_blk):
    i = pl.program_id(1)

    @pl.when(i == 0)
    def _():
        st_s[...] = s0_ref[0]

    x = _load_rows(h_ref, sample, pad[0] if sample else None, t_valid)
    rows = x.shape[0]
    q = x[:, 0:256]
    fx = x[:, 256:512]
    iv = x[:, 512:768]
    gate = x[:, 768:1024]
    lb = lb_ref[0:1, :]
    log_lb = lb_ref[1:2, :]
    log1m_lb = lb_ref[2:3, :]
    cterm = log1m_lb + _log_sigmoid(fx)
    log_f = jnp.maximum(log_lb, cterm) + jnp.log1p(jnp.exp(-jnp.abs(log_lb - cterm)))
    key = (1.0 - lb) * _sigmoid(-fx)
    qs = q * _sigmoid(q) * (HG_DK ** -0.5)
    if sample:
        valid = lax.broadcasted_iota(I32, (rows, GROUP_WIDTH), 0) < t_valid
        log_f = jnp.where(valid, log_f, 0.0)
        key = jnp.where(valid, key, 0.0)
        iv = jnp.where(valid, iv, 0.0)
    _gated_tile(qs, key, iv, log_f, st_s, o_s, q_s, k_s, v_s, b_s, qh_s, kh_s, dt_s, dk=HG_DK, n_blk=n_blk)
    seg = _seg_ones(GROUP_WIDTH, HEAD_DIM, GROUP_WIDTH, HEAD_DIM)
    out = _rms_heads(o_s[...], nw_ref[...], gate, seg)
    if sample:
        o_ref[0] = out[0:t_valid, :]
    else:
        o_ref[...] = out
    sT_ref[0] = st_s[...]


def _gla_kernel(hvg_ref, hqk_ref, s0_ref, wgk_ref, bgk_ref, nw_ref, o_ref, sT_ref,
                st_s, o_s, q_s, k_s, v_s, b_s, qh_s, kh_s, dt_s, *pad, sample, t_valid, n_blk):
    i = pl.program_id(1)

    @pl.when(i == 0)
    def _():
        st_s[...] = s0_ref[0]

    xvg = _load_rows(hvg_ref, sample, pad[0] if sample else None, t_valid)
    xqk = _load_rows(hqk_ref, sample, pad[1] if sample else None, t_valid)
    rows = xvg.shape[0]
    v = xvg[:, 0:256]
    gate = xvg[:, 256:512]
    q = xqk[:, 0:128] * (GLA_DK ** -0.5)
    k = xqk[:, 128:256]
    lo = xqk[:, 256:384]
    gk = _log_sigmoid(_dot3(lo, wgk_ref[...]) + bgk_ref[...]) * (1.0 / GLA_GATE_NORM)
    if sample:
        valid = lax.broadcasted_iota(I32, (rows, 128), 0) < t_valid
        gk = jnp.where(valid, gk, 0.0)
        k = jnp.where(valid, k, 0.0)
    _gated_tile(q, k, v, gk, st_s, o_s, q_s, k_s, v_s, b_s, qh_s, kh_s, dt_s, dk=GLA_DK, n_blk=n_blk)
    seg = _seg_ones(GROUP_WIDTH, HEAD_DIM, GROUP_WIDTH, HEAD_DIM)
    out = _rms_heads(o_s[...], nw_ref[...], gate, seg)
    if sample:
        o_ref[0] = out[0:t_valid, :]
    else:
        o_ref[...] = out
    sT_ref[0] = st_s[...]


def _gated_call(kind, h_all, row0, n_seq, t_len, s0_bd, params):
    sample = t_len < SUB_CHUNK
    hk = N_HEADS * (HG_DK if kind == "hgrn" else GLA_DK)
    if sample:
        rows, n_t, n_blk = SUB_CHUNK, 1, 1
        h_view = h_all.reshape(h_all.shape[0] // t_len, t_len, IN_PAD)
        seq0 = row0 // t_len

        def hspec(width, col_block):
            return pl.BlockSpec((1, t_len, width), lambda b, i: (seq0 + b, 0, col_block))

        o_shape = jax.ShapeDtypeStruct((n_seq, t_len, GROUP_WIDTH), F32)
        o_spec = pl.BlockSpec((1, t_len, GROUP_WIDTH), lambda b, i: (b, 0, 0))
    else:
        rows = min(t_len, ROW_TILE)
        n_t, n_blk = t_len // rows, rows // SUB_CHUNK
        h_view = h_all
        blk0 = row0 // rows

        def hspec(width, col_block):
            return pl.BlockSpec((rows, width), lambda b, i: (blk0 + b * n_t + i, col_block))

        o_shape = jax.ShapeDtypeStruct((n_seq * t_len, GROUP_WIDTH), F32)
        o_spec = pl.BlockSpec((rows, GROUP_WIDTH), lambda b, i: (b * n_t + i, 0))

    st_spec = pl.BlockSpec((1, GROUP_WIDTH, hk), lambda b, i: (b, 0, 0))
    st_shape = jax.ShapeDtypeStruct((n_seq, GROUP_WIDTH, hk), F32)

    def full(a):
        return pl.BlockSpec(a.shape, lambda b, i: (0,) * a.ndim)

    scratch = [pltpu.VMEM((GROUP_WIDTH, hk), F32), pltpu.VMEM((rows, GROUP_WIDTH), F32),
               pltpu.VMEM((rows, hk), F32), pltpu.VMEM((rows, hk), F32), pltpu.VMEM((rows, GROUP_WIDTH), F32),
               pltpu.VMEM((rows, hk), F32), pltpu.VMEM((rows, hk), F32), pltpu.VMEM((rows, hk), F32),
               pltpu.VMEM((rows, hk), F32)]
    if kind == "hgrn":
        lb3, nw = params
        body = functools.partial(_hgrn_kernel, sample=sample, t_valid=t_len, n_blk=n_blk)
        in_specs = [hspec(1024, OFF_HG // 1024), st_spec, full(lb3), full(nw)]
        args = (h_view, s0_bd, lb3, nw)
        if sample:
            scratch.append(pltpu.VMEM((rows, 1024), F32))
    else:
        wgk, bgk, nw = params
        body = functools.partial(_gla_kernel, sample=sample, t_valid=t_len, n_blk=n_blk)
        in_specs = [hspec(512, OFF_GLA_VG // 512), hspec(384, OFF_GLA_QK // 384), st_spec,
                    full(wgk), full(bgk), full(nw)]
        args = (h_view, h_view, s0_bd, wgk, bgk, nw)
        if sample:
            scratch += [pltpu.VMEM((rows, 512), F32), pltpu.VMEM((rows, 384), F32)]
    out, st = pl.pallas_call(
        body,
        grid=(n_seq, n_t),
        in_specs=in_specs,
        out_specs=(o_spec, st_spec),
        out_shape=(o_shape, st_shape),
        scratch_shapes=scratch,
        compiler_params=_cparams(2),
        name=kind + ("_sample" if sample else "_prompt"),
    )(*args)
    return out.reshape(n_seq * t_len, GROUP_WIDTH), st


def _rwkv_kernel(p_ref, sh0_ref, s0_ref, mu_ref, lora_ref, vec_ref, o_ref, s_out_ref,
                 s_s, prev_s, seg_s, lm_s, a_s, b_s, w_s, k_s, r_s, v_s, y_s, *, n_tok):
    i = pl.program_id(1)
    g8 = SEQ_GROUP

    @pl.when(i == 0)
    def _():
        s_s[...] = s0_ref[0]
        prev_s[...] = sh0_ref[0]

    seg_s[...] = _seg_ones(GROUP_WIDTH, HEAD_DIM, GROUP_WIDTH, HEAD_DIM)
    lm_s[...] = (lax.broadcasted_iota(I32, (HEAD_DIM, g8, GROUP_WIDTH), 2) % HEAD_DIM
                 == lax.broadcasted_iota(I32, (HEAD_DIM, g8, GROUP_WIDTH), 0)).astype(F32)
    seg = seg_s[...]

    p = p_ref[...]
    if n_tok > 1:
        prev = jnp.concatenate([prev_s[...], p[:-g8, :]], axis=0)
    else:
        prev = prev_s[...]
    prev_s[...] = p[(n_tok - 1) * g8:, :]
    xs = p + (prev - p) * mu_ref[...]
    r = xs[:, 0:256]
    k = xs[:, 256:512]
    v = xs[:, 512:768]
    lo = xs[:, 768:896]
    w0, a0, kkp, ka = vec_ref[0:1, :], vec_ref[1:2, :], vec_ref[2:3, :], vec_ref[3:4, :]
    rk, lnw, lnb = vec_ref[4:5, :], vec_ref[5:6, :], vec_ref[6:7, :]
    w_log = -_softplus(-(w0 + _dot3(jnp.tanh(lo), lora_ref[0]))) - 0.5
    decay = jnp.exp(-jnp.exp(w_log))
    a = _sigmoid(a0 + _dot3(lo, lora_ref[1]))
    g = _dot3(_sigmoid(lo), lora_ref[2])
    kk = k * kkp
    kk = kk * lax.rsqrt(jnp.maximum(_segsum(kk * kk, seg), 1e-24))
    k2 = k * (1.0 + (a - 1.0) * ka)
    a_s[...] = -kk
    b_s[...] = kk * a
    w_s[...] = decay
    k_s[...] = k2
    r_s[...] = r
    v_s[...] = v

    n = HEAD_DIM * g8
    slab = (HEAD_DIM, g8, GROUP_WIDTH)

    def readout(s, row):
        yb = _dot((s * r_s[pl.ds(row, g8), :][None]).reshape(n, GROUP_WIDTH).astype(BF16), seg_s[...])
        return jnp.sum(yb.reshape(slab) * lm_s[...], axis=0)

    def step(t, carry):
        r0 = pl.multiple_of(t * g8, g8)
        rp = pl.multiple_of(jnp.maximum(t - 1, 0) * g8, g8)
        s = s_s[...]
        lm = lm_s[...]
        sg = seg_s[...]
        y_s[pl.ds(rp, g8), :] = readout(s, rp)
        sa = _dot((s * a_s[pl.ds(r0, g8), :][None]).reshape(n, GROUP_WIDTH).astype(BF16), sg).reshape(slab)
        vh, vl = _split2(v_s[pl.ds(r0, g8), :])
        vm = jnp.concatenate([(vh.astype(F32)[None] * lm).reshape(n, GROUP_WIDTH).astype(BF16),
                              (vl.astype(F32)[None] * lm).reshape(n, GROUP_WIDTH).astype(BF16)], axis=0)
        vb2 = _dot(vm, sg)
        vb = (vb2[0:n] + vb2[n:]).reshape(slab)
        s_s[...] = (s * w_s[pl.ds(r0, g8), :][None] + sa * b_s[pl.ds(r0, g8), :][None]
                    + vb * k_s[pl.ds(r0, g8), :][None])
        return carry

    lax.fori_loop(0, n_tok, step, 0)
    last = (n_tok - 1) * g8
    y_s[pl.ds(last, g8), :] = readout(s_s[...], last)

    y = y_s[...]
    mean = _segsum(y, seg) * (1.0 / HEAD_DIM)
    yc = y - mean
    var = _segsum(yc * yc, seg) * (1.0 / HEAD_DIM)
    yn = yc * lax.rsqrt(var + RW_GN_EPS) * lnw + lnb
    bonus = _segsum(r * k2 * rk, seg) * v
    o_ref[...] = (yn + bonus) * g
    s_out_ref[0] = s_s[...]


def _rwkv_call(p_tm, shift0, s0, mu, lora, vec, n_grp, t_len):
    n_tok = min(t_len, 64)
    n_t = t_len // n_tok
    rows = n_tok * SEQ_GROUP

    def full(a):
        return pl.BlockSpec(a.shape, lambda b, i: (0,) * a.ndim)

    st_spec = pl.BlockSpec((1, HEAD_DIM, SEQ_GROUP, GROUP_WIDTH), lambda b, i: (b, 0, 0, 0))
    vm = lambda shape, dt=F32: pltpu.VMEM(shape, dt)
    out, s_out = pl.pallas_call(
        functools.partial(_rwkv_kernel, n_tok=n_tok),
        grid=(n_grp, n_t),
        in_specs=[pl.BlockSpec((None, rows, RW_COLS), lambda b, i: (b, i, 0)),
                  pl.BlockSpec((1, SEQ_GROUP, RW_COLS), lambda b, i: (b, 0, 0)),
                  st_spec, full(mu), full(lora), full(vec)],
        out_specs=(pl.BlockSpec((None, rows, GROUP_WIDTH), lambda b, i: (b, i, 0)), st_spec),
        out_shape=(jax.ShapeDtypeStruct((n_grp, t_len * SEQ_GROUP, GROUP_WIDTH), F32),
                   jax.ShapeDtypeStruct((n_grp, HEAD_DIM, SEQ_GROUP, GROUP_WIDTH), F32)),
        scratch_shapes=[vm((HEAD_DIM, SEQ_GROUP, GROUP_WIDTH)), vm((SEQ_GROUP, RW_COLS)),
                        vm((GROUP_WIDTH, GROUP_WIDTH), BF16), vm((HEAD_DIM, SEQ_GROUP, GROUP_WIDTH)),
                        vm((rows, GROUP_WIDTH)), vm((rows, GROUP_WIDTH)), vm((rows, GROUP_WIDTH)),
                        vm((rows, GROUP_WIDTH)), vm((rows, GROUP_WIDTH)), vm((rows, GROUP_WIDTH)),
                        vm((rows, GROUP_WIDTH))],
        compiler_params=_cparams(2),
        name="rwkv_t%d" % t_len,
    )(p_tm, shift0, s0, mu, lora, vec)
    return out, s_out


def _s5_prep_kernel(are_ref, aim_ref, ldt_ref, bre_ref, bim_ref, abre_ref, abim_ref, bbre_ref, bbim_ref):
    a_re, a_im = are_ref[...], aim_ref[...]
    dt = jnp.exp(ldt_ref[...])
    mag = jnp.exp(a_re * dt)
    ab_re = mag * jnp.cos(a_im * dt)
    ab_im = mag * jnp.sin(a_im * dt)
    den = a_re * a_re + a_im * a_im
    nr, ni = ab_re - 1.0, ab_im
    coef_re = (nr * a_re + ni * a_im) / den
    coef_im = (ni * a_re - nr * a_im) / den
    b_re, b_im = bre_ref[...], bim_ref[...]
    abre_ref[...] = ab_re
    abim_ref[...] = ab_im
    bbre_ref[...] = coef_re * b_re - coef_im * b_im
    bbim_ref[...] = coef_re * b_im + coef_im * b_re


def _s5_prep(a_re, a_im, log_dt, b_re, b_im):
    rows = DEPTH * S5_NGROUPS
    cols = S5_STATE * S5_GROUP
    rep = lambda t: jnp.repeat(t.reshape(rows, S5_STATE), S5_GROUP, axis=1)
    ldt = jnp.broadcast_to(log_dt.reshape(rows, 1), (rows, cols))
    shp = jax.ShapeDtypeStruct((rows, cols), F32)
    ab_re, ab_im, bb_re, bb_im = pl.pallas_call(
        _s5_prep_kernel, out_shape=(shp, shp, shp, shp), name="s5_prep",
    )(rep(a_re), rep(a_im), ldt, b_re.reshape(rows, cols), b_im.reshape(rows, cols))
    pick = lambda t: t.reshape(DEPTH, S5_NGROUPS, S5_STATE, S5_GROUP)[..., 0].reshape(DEPTH, 1, S5_CH)
    bb = lambda t: t.reshape(DEPTH, S5_NGROUPS, S5_STATE, S5_GROUP)
    return pick(ab_re), pick(ab_im), bb(bb_re), bb(bb_im)


def _s5_kernel(u_ref, x0_ref, ab_ref, wb_ref, wc_ref, d_ref, wglu_ref, o_ref, xT_ref,
               x_s, bu_s, xs_s, *, n_tok):
    i = pl.program_id(1)
    g8 = SEQ_GROUP

    @pl.when(i == 0)
    def _():
        x_s[...] = x0_ref[0]

    u = u_ref[...]
    bu_s[...] = _dot(u.astype(BF16), wb_ref[...])
    a_re = jnp.broadcast_to(ab_ref[0:1, :], (g8, S5_CH))
    a_im = jnp.broadcast_to(ab_ref[1:2, :], (g8, S5_CH))

    def step(t, carry):
        x_re, x_im = carry
        r0 = pl.multiple_of(t * g8, g8)
        n_re = a_re * x_re - a_im * x_im + bu_s[pl.ds(r0, g8), 0:S5_CH]
        n_im = a_re * x_im + a_im * x_re + bu_s[pl.ds(r0, g8), S5_CH:]
        xs_s[pl.ds(r0, g8), 0:S5_CH] = n_re
        xs_s[pl.ds(r0, g8), S5_CH:] = n_im
        return n_re, n_im

    x_re, x_im = lax.fori_loop(0, n_tok, step, (x_s[:, 0:S5_CH], x_s[:, S5_CH:]))
    x_s[:, 0:S5_CH] = x_re
    x_s[:, S5_CH:] = x_im
    y = _dot(xs_s[...].astype(BF16), wc_ref[...]) + d_ref[...] * u
    yg = 0.5 * y * (1.0 + jnp.tanh(0.7978845608028654 * (y + 0.044715 * (y * y * y))))
    o_ref[...] = yg * _sigmoid(_dot(yg.astype(BF16), wglu_ref[...]))
    xT_ref[0] = x_s[...]


def _s5_call(u_tm, x0, ab, wb, wc, dvec, wglu, n_grp, t_len):
    n_tok = min(t_len, 64)
    n_t = t_len // n_tok
    rows = n_tok * SEQ_GROUP

    def full(a):
        return pl.BlockSpec(a.shape, lambda b, i: (0,) * a.ndim)

    st_spec = pl.BlockSpec((1, SEQ_GROUP, 2 * S5_CH), lambda b, i: (b, 0, 0))
    return pl.pallas_call(
        functools.partial(_s5_kernel, n_tok=n_tok),
        grid=(n_grp, n_t),
        in_specs=[pl.BlockSpec((None, rows, GROUP_WIDTH), lambda b, i: (b, i, 0)), st_spec,
                  full(ab), full(wb), full(wc), full(dvec), full(wglu)],
        out_specs=(pl.BlockSpec((None, rows, GROUP_WIDTH), lambda b, i: (b, i, 0)), st_spec),
        out_shape=(jax.ShapeDtypeStruct((n_grp, t_len * SEQ_GROUP, GROUP_WIDTH), F32),
                   jax.ShapeDtypeStruct((n_grp, SEQ_GROUP, 2 * S5_CH), F32)),
        scratch_shapes=[pltpu.VMEM((SEQ_GROUP, 2 * S5_CH), F32), pltpu.VMEM((rows, 2 * S5_CH), F32),
                        pltpu.VMEM((rows, 2 * S5_CH), F32)],
        compiler_params=_cparams(2),
        name="s5_t%d" % t_len,
    )(u_tm, x0, ab, wb, wc, dvec, wglu)


CHUNKS = D_MODEL // LANES


def _store_chunked(ref, x):
    rows = x.shape[0]
    for j in range(CHUNKS):
        ref[pl.ds(j, rows, stride=CHUNKS), :] = x[:, j * LANES:(j + 1) * LANES]


def _load_chunked(ref, rows):
    return jnp.concatenate([ref[pl.ds(j, rows, stride=CHUNKS), :] for j in range(CHUNKS)], axis=1)


def _post_mix_kernel(x_ref, oa_ref, ob_ref, oc_ref, od_ref, wout_ref, lnw_ref, lnb_ref, rw_ref, rb_ref,
                     x1_ref, x1c_ref, idx_ref, rank_ref, gate_ref, cnt_ref, carry_s):
    i = pl.program_id(0)

    @pl.when(i == 0)
    def _():
        carry_s[...] = jnp.zeros_like(carry_s)

    mix = _dot(oa_ref[...].astype(BF16), wout_ref[0:256, :])
    mix += _dot(ob_ref[...].astype(BF16), wout_ref[256:512, :])
    mix += _dot(oc_ref[...].astype(BF16), wout_ref[512:768, :])
    mix += _dot(od_ref[...].astype(BF16), wout_ref[768:1024, :])
    x1 = _layer_norm(DN_ALPHA * x_ref[...] + mix, lnw_ref[...], lnb_ref[...])
    x1_ref[...] = x1
    _store_chunked(x1c_ref, x1)

    xh, xl = _split2(x1)
    logits = _dot(xh, rw_ref[0]) + (_dot(xh, rw_ref[1]) + _dot(xl, rw_ref[0])) + rb_ref[...]
    rows = logits.shape[0]
    lane = lax.broadcasted_iota(I32, (rows, LANES), 1)
    work = logits
    sel_i, sel_v = [], []
    for _ in range(TOP_K):
        m = jnp.max(work, axis=-1, keepdims=True)
        j = jnp.min(jnp.where(work == m, lane, LANES), axis=-1, keepdims=True)
        sel_i.append(j)
        sel_v.append(m)
        work = jnp.where(lane == j, -jnp.inf, work)
    e = [jnp.exp(v - sel_v[0]) for v in sel_v]
    den = (e[0] + e[1]) + (e[2] + e[3])
    onehot = jnp.zeros((rows, LANES), F32)
    for j in sel_i:
        onehot = onehot + (lane == j).astype(F32)
    rr = lax.broadcasted_iota(I32, (rows, rows), 0)
    cc = lax.broadcasted_iota(I32, (rows, rows), 1)
    before = _dot((cc < rr).astype(BF16), onehot.astype(BF16)) + carry_s[0:1, :]
    carry_s[0:1, :] = carry_s[0:1, :] + jnp.sum(onehot, axis=0, keepdims=True)
    idx_o = jnp.zeros((rows, LANES), I32)
    rank_o = jnp.zeros((rows, LANES), I32)
    gate_o = jnp.zeros((rows, LANES), F32)
    for slot in range(TOP_K):
        j = sel_i[slot]
        rank = jnp.sum(jnp.where(lane == j, before, 0.0), axis=-1, keepdims=True)
        idx_o = jnp.where(lane == slot, j, idx_o)
        rank_o = jnp.where(lane == slot, rank.astype(I32), rank_o)
        gate_o = jnp.where(lane == slot, e[slot] / den, gate_o)
    idx_ref[...] = idx_o
    rank_ref[...] = rank_o
    gate_ref[...] = gate_o
    cnt_ref[...] = jnp.broadcast_to(carry_s[0:1, :], cnt_ref.shape)


def _post_mix(x_all, outs, wout_l, lnw, lnb, rw_l, rb_l):
    n = x_all.shape[0]
    row = lambda w: pl.BlockSpec((ROW_TILE, w), lambda i: (i, 0))

    def full(a):
        return pl.BlockSpec(a.shape, lambda i: (0,) * a.ndim)

    meta = jax.ShapeDtypeStruct((n, LANES), I32)
    return pl.pallas_call(
        _post_mix_kernel,
        grid=(n // ROW_TILE,),
        in_specs=[row(D_MODEL), row(256), row(256), row(256), row(256), full(wout_l), full(lnw), full(lnb),
                  full(rw_l), full(rb_l)],
        out_specs=(row(D_MODEL), pl.BlockSpec((ROW_TILE * CHUNKS, LANES), lambda i: (i, 0)),
                   row(LANES), row(LANES), row(LANES), pl.BlockSpec((SUBLANES, LANES), lambda i: (0, 0))),
        out_shape=(jax.ShapeDtypeStruct((n, D_MODEL), F32), jax.ShapeDtypeStruct((n * CHUNKS, LANES), F32),
                   meta, meta, jax.ShapeDtypeStruct((n, LANES), F32),
                   jax.ShapeDtypeStruct((SUBLANES, LANES), F32)),
        scratch_shapes=[pltpu.VMEM((SUBLANES, LANES), F32)],
        compiler_params=_cparams(1),
        name="post_mix",
    )(x_all, *outs, wout_l, lnw, lnb, rw_l, rb_l)


def _dispatch_kernel(gend_ref, dest_ref, x_ref, xs_hbm, zero_s, sem, *, tokens, n_tiles):
    i = pl.program_id(0)

    def zero_tile(first_row):
        start = pl.multiple_of(first_row * CHUNKS, MOE_TILE * CHUNKS)
        return pltpu.make_async_copy(zero_s, xs_hbm.at[pl.ds(start, MOE_TILE * CHUNKS)], sem)

    @pl.when(i == 0)
    def _():
        zero_s[...] = jnp.zeros_like(zero_s)
        for e in range(N_EXPERTS):
            @pl.when(gend_ref[e + 1] > gend_ref[e])
            def _():
                zero_tile(gend_ref[e + 1] - MOE_TILE).start()
        for e in range(N_EXPERTS):
            @pl.when(gend_ref[e + 1] > gend_ref[e])
            def _():
                zero_tile(gend_ref[e + 1] - MOE_TILE).wait()

        def tail(t, carry):
            cp = zero_tile(t * MOE_TILE)
            cp.start()
            cp.wait()
            return carry

        lax.fori_loop(gend_ref[N_EXPERTS] // MOE_TILE, n_tiles, tail, 0)

    def row_copy(n, slot):
        src = pl.multiple_of(n * CHUNKS, CHUNKS)
        dst = pl.multiple_of(dest_ref[n * TOP_K + slot] * CHUNKS, CHUNKS)
        return pltpu.make_async_copy(x_ref.at[pl.ds(src, CHUNKS)], xs_hbm.at[pl.ds(dst, CHUNKS)], sem)

    def issue(n, carry):
        for slot in range(TOP_K):
            row_copy(n, slot).start()
        return carry

    def drain(n, carry):
        for slot in range(TOP_K):
            row_copy(n, slot).wait()
        return carry

    lax.fori_loop(0, tokens, issue, 0)
    lax.fori_loop(0, tokens, drain, 0)


def _dispatch(x1c, dest_flat, gend, n_rows):
    n = x1c.shape[0] // CHUNKS
    tokens = DISPATCH_TOKENS if n % DISPATCH_TOKENS == 0 else ROW_TILE
    return pl.pallas_call(
        functools.partial(_dispatch_kernel, tokens=tokens, n_tiles=n_rows // MOE_TILE),
        grid_spec=pltpu.PrefetchScalarGridSpec(
            num_scalar_prefetch=1,
            grid=(n // tokens,),
            in_specs=[pl.BlockSpec((tokens * TOP_K,), lambda i, ge: (i,), memory_space=pltpu.SMEM),
                      pl.BlockSpec((tokens * CHUNKS, LANES), lambda i, ge: (i, 0))],
            out_specs=pl.BlockSpec(memory_space=pl.ANY),
            scratch_shapes=[pltpu.VMEM((MOE_TILE * CHUNKS, LANES), F32), pltpu.SemaphoreType.DMA(())],
        ),
        out_shape=jax.ShapeDtypeStruct((n_rows * CHUNKS, LANES), F32),
        compiler_params=_cparams(1),
        name="moe_dispatch",
    )(gend, dest_flat, x1c)


PAIR_BLOCK = 2 * LANES


def _expert_prep_kernel(w1_ref, w2_ref, w1p_ref, w2b_ref):
    src = lax.broadcasted_iota(I32, (PAIR_BLOCK, PAIR_BLOCK), 0)
    dst = lax.broadcasted_iota(I32, (PAIR_BLOCK, PAIR_BLOCK), 1)
    perm = (src == jnp.where(dst < LANES, 2 * dst, 2 * (dst - LANES) + 1)).astype(BF16)
    for c in range(2 * D_FF // PAIR_BLOCK):
        cols = slice(c * PAIR_BLOCK, (c + 1) * PAIR_BLOCK)
        w1p_ref[:, cols] = _dot(w1_ref[:, cols].astype(BF16), perm).astype(BF16)
    w2b_ref[...] = w2_ref[...].astype(BF16)


def _expert_prep(exp_w1, exp_w2):
    n_l, n_e = exp_w1.shape[:2]
    spec = lambda r, c: pl.BlockSpec((None, None, r, c), lambda i: (i // n_e, i % n_e, 0, 0))
    return pl.pallas_call(
        _expert_prep_kernel,
        grid=(n_l * n_e,),
        in_specs=[spec(D_MODEL, 2 * D_FF), spec(D_FF, D_MODEL)],
        out_specs=(spec(D_MODEL, 2 * D_FF), spec(D_FF, D_MODEL)),
        out_shape=(jax.ShapeDtypeStruct(exp_w1.shape, BF16), jax.ShapeDtypeStruct(exp_w2.shape, BF16)),
        compiler_params=_cparams(1),
        name="expert_prep",
    )(exp_w1, exp_w2)


def _expert_kernel(te_ref, nreal_ref, xs_ref, w1_ref, b1_ref, w2_ref, b2_ref, o_ref, acc_s):
    i = pl.program_id(0)

    @pl.when(i < nreal_ref[0])
    def _():
        x = _load_chunked(xs_ref, MOE_TILE).astype(BF16)
        acc_s[...] = jnp.broadcast_to(b2_ref[...], acc_s.shape)
        for c in range(D_FF // PAIR_BLOCK):
            cols = slice(2 * c * PAIR_BLOCK, 2 * (c + 1) * PAIR_BLOCK)
            h = _dot(x, w1_ref[:, cols]) + b1_ref[:, cols]
            h_glu = jnp.minimum(jnp.concatenate([h[:, 0:128], h[:, 256:384]], axis=1), SWIGLU_LIMIT)
            h_lin = jnp.clip(jnp.concatenate([h[:, 128:256], h[:, 384:512]], axis=1),
                             -SWIGLU_LIMIT, SWIGLU_LIMIT)
            act = h_glu * _sigmoid(SWIGLU_ALPHA * h_glu) * (h_lin + 1.0)
            acc_s[...] += _dot(act.astype(BF16), w2_ref[c * PAIR_BLOCK:(c + 1) * PAIR_BLOCK, :])
        _store_chunked(o_ref, acc_s[...])

    @pl.when(i >= nreal_ref[0])
    def _():
        o_ref[...] = jnp.zeros_like(o_ref)


def _experts(xs, te, nreal, w1d, b1d, w2b, b2, layer):
    n_tiles = xs.shape[0] // (MOE_TILE * CHUNKS)
    clamp = lambda i, nr: jnp.minimum(i, nr[0] - 1)
    return pl.pallas_call(
        _expert_kernel,
        grid_spec=pltpu.PrefetchScalarGridSpec(
            num_scalar_prefetch=2,
            grid=(n_tiles,),
            in_specs=[pl.BlockSpec((MOE_TILE * CHUNKS, LANES), lambda i, te, nr: (clamp(i, nr), 0)),
                      pl.BlockSpec((None, None, D_MODEL, 2 * D_FF), lambda i, te, nr: (layer, te[i], 0, 0)),
                      pl.BlockSpec((None, None, 1, 2 * D_FF), lambda i, te, nr: (layer, te[i], 0, 0)),
                      pl.BlockSpec((None, None, D_FF, D_MODEL), lambda i, te, nr: (layer, te[i], 0, 0)),
                      pl.BlockSpec((None, None, 1, D_MODEL), lambda i, te, nr: (layer, te[i], 0, 0))],
            out_specs=pl.BlockSpec((MOE_TILE * CHUNKS, LANES), lambda i, te, nr: (i, 0)),
            scratch_shapes=[pltpu.VMEM((MOE_TILE, D_MODEL), F32)],
        ),
        out_shape=jax.ShapeDtypeStruct(xs.shape, F32),
        compiler_params=_cparams(1),
        name="moe_experts",
    )(te, nreal, xs, w1d, b1d, w2b, b2)


def _combine_kernel(dest_ref, gate_ref, x1_ref, lnw_ref, lnb_ref, ys_hbm, o_ref, buf_s, sem, *, tokens):
    def row_copy(n, slot):
        src = pl.multiple_of(dest_ref[n * TOP_K + slot] * CHUNKS, CHUNKS)
        dst = pl.multiple_of(n * CHUNKS, CHUNKS)
        return pltpu.make_async_copy(ys_hbm.at[pl.ds(src, CHUNKS)], buf_s.at[slot, pl.ds(dst, CHUNKS)], sem)

    def issue(n, carry):
        for slot in range(TOP_K):
            row_copy(n, slot).start()
        return carry

    def drain(n, carry):
        for slot in range(TOP_K):
            row_copy(n, slot).wait()
        return carry

    lax.fori_loop(0, tokens, issue, 0)
    lax.fori_loop(0, tokens, drain, 0)
    gate = gate_ref[...]
    ffn = gate[:, 0:1] * _load_chunked(buf_s.at[0], tokens)
    for slot in range(1, TOP_K):
        ffn = ffn + gate[:, slot:slot + 1] * _load_chunked(buf_s.at[slot], tokens)
    o_ref[...] = _layer_norm(DN_ALPHA * x1_ref[...] + ffn, lnw_ref[...], lnb_ref[...])


def _combine(dest_flat, gate, x1, lnw, lnb, ys):
    n = x1.shape[0]
    tokens = ROW_TILE
    return pl.pallas_call(
        functools.partial(_combine_kernel, tokens=tokens),
        grid=(n // tokens,),
        in_specs=[pl.BlockSpec((tokens * TOP_K,), lambda i: (i,), memory_space=pltpu.SMEM),
                  pl.BlockSpec((tokens, LANES), lambda i: (i, 0)),
                  pl.BlockSpec((tokens, D_MODEL), lambda i: (i, 0)),
                  pl.BlockSpec((1, D_MODEL), lambda i: (0, 0)),
                  pl.BlockSpec((1, D_MODEL), lambda i: (0, 0)),
                  pl.BlockSpec(memory_space=pl.ANY)],
        out_specs=pl.BlockSpec((tokens, D_MODEL), lambda i: (i, 0)),
        out_shape=jax.ShapeDtypeStruct((n, D_MODEL), F32),
        scratch_shapes=[pltpu.VMEM((TOP_K, tokens * CHUNKS, LANES), F32), pltpu.SemaphoreType.DMA(())],
        compiler_params=_cparams(1),
        name="moe_combine",
    )(dest_flat, gate, x1, lnw, lnb, ys)


def _moe(x1, x1c, idx, rank, gate, counts, w1d, b1d, w2b, b2, lnw, lnb, layer):
    n = x1.shape[0]
    n_tiles = -(-(n * TOP_K + N_EXPERTS * (MOE_TILE - 1)) // MOE_TILE)
    cnt = counts[0, :N_EXPERTS].astype(I32)
    gsz = ((cnt + (MOE_TILE - 1)) // MOE_TILE) * MOE_TILE
    gend = jnp.cumsum(gsz)
    goff = gend - gsz
    dest = (goff[idx[:, :TOP_K]] + rank[:, :TOP_K]).reshape(-1)
    gend0 = jnp.concatenate([jnp.zeros((1,), I32), gend])
    nreal = (gend[-1:] // MOE_TILE).astype(I32)
    tile_start = jnp.arange(n_tiles, dtype=I32) * MOE_TILE
    te = jnp.minimum(jnp.sum((gend[None, :] <= tile_start[:, None]).astype(I32), axis=1), N_EXPERTS - 1)
    xs = _dispatch(x1c, dest, gend0, n_tiles * MOE_TILE)
    ys = _experts(xs, te, nreal, w1d, b1d, w2b, b2, layer)
    return _combine(dest, gate, x1, lnw, lnb, ys)


def _to_time_major(rows, n_seq, t_len):
    c = rows.shape[-1]
    x = rows.reshape(n_seq // SEQ_GROUP, SEQ_GROUP, t_len, c)
    return jnp.transpose(x, (0, 2, 1, 3)).reshape(n_seq // SEQ_GROUP, t_len * SEQ_GROUP, c)


def _from_time_major(x, n_seq, t_len):
    c = x.shape[-1]
    x = x.reshape(n_seq // SEQ_GROUP, t_len, SEQ_GROUP, c)
    return jnp.transpose(x, (0, 2, 1, 3)).reshape(n_seq * t_len, c)


def _block_diag_state(s, dk):
    st = jnp.swapaxes(s, 2, 3)
    eye = jnp.eye(N_HEADS, dtype=s.dtype)
    return jnp.einsum("bhvk,hg->bhvgk", st, eye).reshape(s.shape[0], GROUP_WIDTH, N_HEADS * dk)


def _unblock_state(st, dk):
    b = st.shape[0]
    x = st.reshape(b, N_HEADS, HEAD_DIM, N_HEADS, dk)
    x = jnp.stack([x[:, h, :, h, :] for h in range(N_HEADS)], axis=1)
    return jnp.swapaxes(x, 2, 3)


def _rwkv_state_in(s):
    b = s.shape[0]
    x = s.reshape(b // SEQ_GROUP, SEQ_GROUP, N_HEADS, HEAD_DIM, HEAD_DIM)
    return jnp.transpose(x, (0, 3, 1, 2, 4)).reshape(b // SEQ_GROUP, HEAD_DIM, SEQ_GROUP, GROUP_WIDTH)


def _rwkv_state_out(x, b):
    x = x.reshape(b // SEQ_GROUP, HEAD_DIM, SEQ_GROUP, N_HEADS, HEAD_DIM)
    return jnp.transpose(x, (0, 2, 3, 1, 4)).reshape(b, N_HEADS, HEAD_DIM, HEAD_DIM)


def _pad_rows(w, row0, n_rows):
    out = jnp.zeros((n_rows, w.shape[1]), w.dtype)
    return out.at[row0:row0 + w.shape[0]].set(w)


def kernel(x_prompt, x_sample, state_rwkv, state_rwkv_shift, state_hgrn, state_gla, state_s5_re, state_s5_im,
           w_in, rw_mu, rw_w0, rw_w2, rw_a0, rw_a2, rw_g2, rw_kk, rw_ka, rw_rk, rw_lnx_w, rw_lnx_b,
           hg_lb_logits, hg_norm_w, gla_w_gk2, gla_b_gk, gla_norm_w,
           s5_A_re, s5_A_im, s5_log_dt, s5_B_re, s5_B_im, s5_C_re, s5_C_im, s5_D, s5_w_glu,
           w_out, ln1_w, ln1_b, router_w, router_b, exp_w1, exp_b1, exp_w2, exp_b2, ln2_w, ln2_b):
    bp, tp, _ = x_prompt.shape
    bs, ts, _ = x_sample.shape
    n_p, n_s = bp * tp, bs * ts
    groups = ((0, bp, tp), (n_p, bs, ts))

    c = np.cumsum([0, RW_COLS, 1024, 784, 256])
    rw_c, hg_c, gl_c, s5_c = (w_in[:, :, c[j]:c[j + 1]] for j in range(4))
    gl_q, gl_k, gl_v, gl_lo, gl_g = (gl_c[:, :, a:b] for a, b in
                                     ((0, 128), (128, 256), (256, 512), (512, 528), (528, 784)))
    zpad = jnp.zeros((DEPTH, D_MODEL, 128 - GLA_GK_LORA), w_in.dtype)
    w_in_p = jnp.concatenate([hg_c, gl_v, gl_g, gl_q, gl_k, gl_lo, zpad, rw_c, s5_c], axis=2).astype(BF16)
    w_out_b = w_out.astype(BF16)
    w1d, w2b = _expert_prep(exp_w1, exp_w2)
    b1d = jnp.swapaxes(exp_b1.reshape(DEPTH, N_EXPERTS, 2 * D_FF // PAIR_BLOCK, LANES, 2), -1, -2)
    b1d = b1d.reshape(DEPTH, N_EXPERTS, 1, 2 * D_FF)
    b2r = exp_b2[:, :, None, :]
    rw_pad = jnp.pad(router_w, ((0, 0), (0, 0), (0, LANES - N_EXPERTS)))
    rw_hi = rw_pad.astype(BF16)
    rw_lo = (rw_pad - rw_hi.astype(F32)).astype(BF16)
    rw_split = jnp.stack([rw_hi, rw_lo], axis=1)
    rb_pad = jnp.pad(router_b, ((0, 0), (0, LANES - N_EXPERTS)), constant_values=-1e30)[:, None, :]

    lbs = jnp.cumsum(jax.nn.softmax(hg_lb_logits.astype(F32), axis=0), axis=0)
    lbs = lbs - lbs[:1]
    lb3 = jnp.stack([lbs, jnp.log(lbs), jnp.log1p(-lbs)], axis=1)
    lb3 = jnp.pad(lb3, ((0, 0), (0, SUBLANES - 3), (0, 0)))

    ab_re, ab_im, bb_re, bb_im = _s5_prep(s5_A_re, s5_A_im, s5_log_dt, s5_B_re, s5_B_im)
    eye_g = jnp.eye(S5_NGROUPS, dtype=F32)
    wb = jnp.stack([jnp.einsum("lgph,gk->lghkp", t, eye_g).reshape(DEPTH, GROUP_WIDTH, S5_CH)
                    for t in (bb_re, bb_im)], axis=2).reshape(DEPTH, GROUP_WIDTH, 2 * S5_CH).astype(BF16)
    wc = jnp.concatenate([jnp.einsum("lghp,gk->lgpkh", t, eye_g).reshape(DEPTH, S5_CH, GROUP_WIDTH)
                          for t in (s5_C_re, -s5_C_im)], axis=1).astype(BF16)
    ab = jnp.concatenate([ab_re, ab_im], axis=1)
    ab = jnp.pad(ab, ((0, 0), (0, SUBLANES - 2), (0, 0)))
    wglu_b = s5_w_glu.astype(BF16)

    lora = jnp.stack([jnp.stack([_pad_rows(rw_w2[l], 0, 128), _pad_rows(rw_a2[l], 32, 128),
                                 _pad_rows(rw_g2[l], 64, 128)]) for l in range(DEPTH)])
    rw_vec = jnp.stack([rw_w0, rw_a0, rw_kk, rw_ka, rw_rk, rw_lnx_w, rw_lnx_b, jnp.zeros_like(rw_w0)], axis=1)
    wgk = jnp.stack([_pad_rows(gla_w_gk2[l], 0, 128) for l in range(DEPTH)])

    zeros = lambda shape: jnp.zeros(shape, F32)
    st_in = (
        dict(rw=zeros((DEPTH, bp, N_HEADS, HEAD_DIM, HEAD_DIM)), sh=zeros((DEPTH, bp, RW_COLS)),
             hg=zeros((DEPTH, bp, N_HEADS, HG_DK, HEAD_DIM)), gl=zeros((DEPTH, bp, N_HEADS, GLA_DK, HEAD_DIM)),
             re=zeros((DEPTH, bp, S5_NGROUPS, S5_STATE)), im=zeros((DEPTH, bp, S5_NGROUPS, S5_STATE))),
        dict(rw=state_rwkv, sh=state_rwkv_shift, hg=state_hgrn, gl=state_gla, re=state_s5_re, im=state_s5_im),
    )
    collected = ([], [])

    x_all = jnp.concatenate([x_prompt.reshape(n_p, D_MODEL), x_sample.reshape(n_s, D_MODEL)], axis=0)
    for l in range(DEPTH):
        h_all = _inproj(x_all, w_in_p[l])
        outs = [[], [], [], []]
        for gi, (row0, n_seq, t_len) in enumerate(groups):
            st = st_in[gi]
            n_grp = n_seq // SEQ_GROUP
            rows = slice(row0, row0 + n_seq * t_len)
            p_rw = h_all[rows, OFF_RW:OFF_RW + RW_COLS]
            p_tm = _to_time_major(p_rw, n_seq, t_len)
            sh0 = st["sh"][l].reshape(n_grp, SEQ_GROUP, RW_COLS)
            oa_tm, s_rw = _rwkv_call(p_tm, sh0, _rwkv_state_in(st["rw"][l]), rw_mu[l][None, :], lora[l],
                                     rw_vec[l], n_grp, t_len)
            outs[0].append(_from_time_major(oa_tm, n_seq, t_len))
            new_rw = _rwkv_state_out(s_rw, n_seq)
            new_sh = p_rw.reshape(n_seq, t_len, RW_COLS)[:, -1]
            ob, s_hg = _gated_call("hgrn", h_all, row0, n_seq, t_len, _block_diag_state(st["hg"][l], HG_DK),
                                   (lb3[l], hg_norm_w[l][None, :]))
            outs[1].append(ob)
            oc, s_gl = _gated_call("gla", h_all, row0, n_seq, t_len, _block_diag_state(st["gl"][l], GLA_DK),
                                   (wgk[l], gla_b_gk[l][None, :], gla_norm_w[l][None, :]))
            outs[2].append(oc)
            u_tm = _to_time_major(h_all[rows, OFF_S5:OFF_S5 + GROUP_WIDTH], n_seq, t_len)
            x0 = jnp.concatenate([st["re"][l].reshape(n_grp, SEQ_GROUP, S5_CH),
                                  st["im"][l].reshape(n_grp, SEQ_GROUP, S5_CH)], axis=-1)
            od_tm, x_t = _s5_call(u_tm, x0, ab[l], wb[l], wc[l], s5_D[l][None, :], wglu_b[l], n_grp, t_len)
            outs[3].append(_from_time_major(od_tm, n_seq, t_len))
            x_t = x_t.reshape(n_seq, 2, S5_NGROUPS, S5_STATE)
            collected[gi].append((new_rw, new_sh, _unblock_state(s_hg, HG_DK), _unblock_state(s_gl, GLA_DK),
                                  x_t[:, 0], x_t[:, 1]))
        merged = [jnp.concatenate(o, axis=0) for o in outs]
        x1, x1c, idx, rank, gate, counts = _post_mix(x_all, merged, w_out_b[l], ln1_w[l][None, :], ln1_b[l][None, :],
                                                rw_split[l], rb_pad[l])
        x_all = _moe(x1, x1c, idx, rank, gate, counts, w1d, b1d, w2b, b2r, ln2_w[l][None, :], ln2_b[l][None, :], l)

    y_prompt = x_all[:n_p].reshape(bp, tp, D_MODEL)
    y_sample = x_all[n_p:].reshape(bs, ts, D_MODEL)
    ps = [jnp.stack([layer[j] for layer in collected[0]]) for j in range(6)]
    ss = [jnp.stack([layer[j] for layer in collected[1]]) for j in range(6)]
    return (y_prompt, y_sample, *ps, *ss)
```

```python
import functools

import jax
import jax.numpy as jnp
import numpy as np
from jax import lax
from jax.experimental import pallas as pl
from jax.experimental.pallas import tpu as pltpu

F32 = jnp.float32
BF16 = jnp.bfloat16
I32 = jnp.int32

D_MODEL = 1024
DEPTH = 4
GROUP_WIDTH = 256
HEAD_DIM = 64
N_HEADS = 4
RW_COLS = 896
RW_GN_EPS = 64e-5
HG_DK = 64
GLA_DK = 32
GLA_GK_LORA = 16
GLA_GATE_NORM = 16.0
S5_NGROUPS = 16
S5_GROUP = 16
S5_STATE = 64
S5_CH = S5_NGROUPS * S5_STATE
N_EXPERTS = 32
TOP_K = 4
D_FF = 1024
SWIGLU_ALPHA = 1.702
SWIGLU_LIMIT = 7.0
DN_ALPHA = (2.0 * DEPTH) ** 0.25
LN_EPS = 1e-5

SUBLANES = 8
LANES = 128
VMEM_LIMIT_BYTES = 48 * 1024 * 1024

OFF_HG = 0
OFF_GLA_VG = 1024
OFF_GLA_QK = 1536
OFF_RW = 1920
OFF_S5 = 2816
IN_PAD = 3072

ROW_TILE = 256
SUB_CHUNK = 16
SEQ_GROUP = SUBLANES
MOE_TILE = 256
DISPATCH_TOKENS = 512


def _cparams(n_axes):
    return pltpu.CompilerParams(dimension_semantics=("arbitrary",) * n_axes,
                                vmem_limit_bytes=VMEM_LIMIT_BYTES)


def _dot(a, b):
    return jnp.dot(a, b, preferred_element_type=F32)


def _split2(x):
    hi = x.astype(BF16)
    lo = (x - hi.astype(F32)).astype(BF16)
    return hi, lo


def _split3(x):
    hi = x.astype(BF16)
    r = x - hi.astype(F32)
    mid = r.astype(BF16)
    lo = (r - mid.astype(F32)).astype(BF16)
    return hi, mid, lo


def _dot3(a, b):
    ah, al = _split2(a)
    bh, bl = _split2(b)
    return _dot(ah, bh) + (_dot(ah, bl) + _dot(al, bh))


def _seg_ones(n_in, seg_in, n_out, seg_out):
    r = lax.broadcasted_iota(I32, (n_in, n_out), 0) // seg_in
    c = lax.broadcasted_iota(I32, (n_in, n_out), 1) // seg_out
    return (r == c).astype(BF16)


def _segsum(x, seg):
    rows = x.shape[0]
    hi, lo = _split2(x)
    both = _dot(jnp.concatenate([hi, lo], axis=0), seg)
    return both[:rows] + both[rows:]


def _sigmoid(x):
    return 1.0 / (1.0 + jnp.exp(-x))


def _log_sigmoid(x):
    return jnp.minimum(x, 0.0) - jnp.log1p(jnp.exp(-jnp.abs(x)))


def _softplus(x):
    return jnp.maximum(x, 0.0) + jnp.log1p(jnp.exp(-jnp.abs(x)))


def _layer_norm(x, w, b):
    xc = x - jnp.mean(x, axis=-1, keepdims=True)
    var = jnp.mean(xc * xc, axis=-1, keepdims=True)
    return xc * lax.rsqrt(var + LN_EPS) * w + b


def _inproj_kernel(x_ref, w_ref, *o_refs):
    h = _dot(x_ref[...].astype(BF16), w_ref[...])
    col = 0
    for o_ref in o_refs:
        o_ref[...] = h[:, col:col + o_ref.shape[-1]]
        col += o_ref.shape[-1]


def _inproj_prompt(x_all, w_in_l, n_seq, t_len):
    n_t = t_len // ROW_TILE
    return pl.pallas_call(
        _inproj_kernel,
        grid=(n_seq, n_t),
        in_specs=[pl.BlockSpec((ROW_TILE, D_MODEL), lambda b, i: (b * n_t + i, 0)),
                  pl.BlockSpec((D_MODEL, IN_PAD), lambda b, i: (0, 0))],
        out_specs=(pl.BlockSpec((ROW_TILE, OFF_RW), lambda b, i: (b * n_t + i, 0)),
                   pl.BlockSpec((ROW_TILE, RW_COLS), lambda b, i: (i, b)),
                   pl.BlockSpec((ROW_TILE, GROUP_WIDTH), lambda b, i: (i, b))),
        out_shape=(jax.ShapeDtypeStruct((n_seq * t_len, OFF_RW), F32),
                   jax.ShapeDtypeStruct((t_len, n_seq * RW_COLS), F32),
                   jax.ShapeDtypeStruct((t_len, n_seq * GROUP_WIDTH), F32)),
        compiler_params=_cparams(2),
        name="inproj_prompt",
    )(x_all, w_in_l)


def _inproj_rows(x_all, w_in_l, row0, n_rows):
    blk0 = row0 // ROW_TILE
    return pl.pallas_call(
        _inproj_kernel,
        grid=(n_rows // ROW_TILE,),
        in_specs=[pl.BlockSpec((ROW_TILE, D_MODEL), lambda i: (blk0 + i, 0)),
                  pl.BlockSpec((D_MODEL, IN_PAD), lambda i: (0, 0))],
        out_specs=pl.BlockSpec((ROW_TILE, IN_PAD), lambda i: (i, 0)),
        out_shape=jax.ShapeDtypeStruct((n_rows, IN_PAD), F32),
        compiler_params=_cparams(1),
        name="inproj_rows",
    )(x_all, w_in_l)


def _gated_tile(q, k, v, g, st_ref, o_ref, q_s, k_s, v_s, b_s, qh_s, kh_s, dt_s, *, dk, n_blk):
    c = SUB_CHUNK
    rows, hk = q.shape
    rr = lax.broadcasted_iota(I32, (rows, rows), 0)
    cc = lax.broadcasted_iota(I32, (rows, rows), 1)
    same = (rr // c) == (cc // c)
    tri = jnp.concatenate([(same & (cc <= rr)).astype(BF16), same.astype(BF16)], axis=0)
    g3 = jnp.concatenate(_split3(g), axis=1)
    p = _dot(tri, g3)
    b = p[:rows, :hk] + p[:rows, hk:2 * hk] + p[:rows, 2 * hk:]
    btot = p[rows:, :hk] + p[rows:, hk:2 * hk] + p[rows:, 2 * hk:]
    q_s[...] = q
    k_s[...] = k
    v_s[...] = v
    b_s[...] = b
    qh_s[...] = q * jnp.exp(b)
    kh_s[...] = k * jnp.exp(btot - b)
    dt_s[...] = jnp.exp(btot)
    seg = _seg_ones(hk, dk, GROUP_WIDTH, HEAD_DIM)
    bd_mask = (lax.broadcasted_iota(I32, (GROUP_WIDTH, hk), 0) // HEAD_DIM
               == lax.broadcasted_iota(I32, (GROUP_WIDTH, hk), 1) // dk).astype(F32)
    t_iota = lax.broadcasted_iota(I32, (c, hk), 0)

    def block(i, carry):
        r0 = pl.multiple_of(i * c, c)
        qb = q_s[pl.ds(r0, c), :]
        kb = k_s[pl.ds(r0, c), :]
        vb = v_s[pl.ds(r0, c), :]
        bb = b_s[pl.ds(r0, c), :]
        pieces = []
        for s in range(c):
            d = jnp.where(t_iota >= s, bb - bb[s:s + 1, :], -jnp.inf)
            pieces.append(jnp.exp(d) * qb * kb[s:s + 1, :])
        att = _segsum(jnp.concatenate(pieces, axis=0), seg)
        o = att[0:c, :] * vb[0:1, :]
        for s in range(1, c):
            o = o + att[s * c:(s + 1) * c, :] * vb[s:s + 1, :]
        st = st_ref[...]
        o = o + lax.dot_general(qh_s[pl.ds(r0, c), :].astype(BF16), st.astype(BF16),
                                (((1,), (1,)), ((), ())), preferred_element_type=F32)
        upd = lax.dot_general(vb.astype(BF16), kh_s[pl.ds(r0, c), :].astype(BF16),
                              (((0,), (0,)), ((), ())), preferred_element_type=F32)
        st_ref[...] = st * dt_s[pl.ds(r0, 1), :] + upd * bd_mask
        o_ref[pl.ds(r0, c), :] = o
        return carry

    lax.fori_loop(0, n_blk, block, 0)


def _rms_heads(o, w, gate, seg):
    ms = _segsum(o * o, seg) * (1.0 / HEAD_DIM)
    return o * lax.rsqrt(ms + LN_EPS) * w * (gate * _sigmoid(gate))


def _load_rows(ref, sample, pad_s, t_valid):
    if not sample:
        return ref[...]
    pad_s[...] = jnp.zeros_like(pad_s)
    pad_s[0:t_valid, :] = ref[0]
    return pad_s[...]


def _hgrn_kernel(h_ref, s0_ref, lb_ref, nw_ref, o_ref, sT_ref,
                 st_s, o_s, q_s, k_s, v_s, b_s, qh_s, kh_s, dt_s, *pad, sample, t_valid, n_blk):
    i = pl.program_id(1)

    @pl.when(i == 0)
    def _():
        st_s[...] = s0_ref[0]

    x = _load_rows(h_ref, sample, pad[0] if sample else None, t_valid)
    rows = x.shape[0]
    q = x[:, 0:256]
    fx = x[:, 256:512]
    iv = x[:, 512:768]
    gate = x[:, 768:1024]
    lb = lb_ref[0:1, :]
    log_lb = lb_ref[1:2, :]
    log1m_lb = lb_ref[2:3, :]
    cterm = log1m_lb + _log_sigmoid(fx)
    log_f = jnp.maximum(log_lb, cterm) + jnp.log1p(jnp.exp(-jnp.abs(log_lb - cterm)))
    key = (1.0 - lb) * _sigmoid(-fx)
    qs = q * _sigmoid(q) * (HG_DK ** -0.5)
    if sample:
        valid = lax.broadcasted_iota(I32, (rows, GROUP_WIDTH), 0) < t_valid
        log_f = jnp.where(valid, log_f, 0.0)
        key = jnp.where(valid, key, 0.0)
        iv = jnp.where(valid, iv, 0.0)
    _gated_tile(qs, key, iv, log_f, st_s, o_s, q_s, k_s, v_s, b_s, qh_s, kh_s, dt_s, dk=HG_DK, n_blk=n_blk)
    seg = _seg_ones(GROUP_WIDTH, HEAD_DIM, GROUP_WIDTH, HEAD_DIM)
    out = _rms_heads(o_s[...], nw_ref[...], gate, seg)
    if sample:
        o_ref[0] = out[0:t_valid, :]
    else:
        o_ref[...] = out
    sT_ref[0] = st_s[...]


def _gla_kernel(hvg_ref, hqk_ref, s0_ref, wgk_ref, bgk_ref, nw_ref, o_ref, sT_ref,
                st_s, o_s, q_s, k_s, v_s, b_s, qh_s, kh_s, dt_s, *pad, sample, t_valid, n_blk):
    i = pl.program_id(1)

    @pl.when(i == 0)
    def _():
        st_s[...] = s0_ref[0]

    xvg = _load_rows(hvg_ref, sample, pad[0] if sample else None, t_valid)
    xqk = _load_rows(hqk_ref, sample, pad[1] if sample else None, t_valid)
    rows = xvg.shape[0]
    v = xvg[:, 0:256]
    gate = xvg[:, 256:512]
    q = xqk[:, 0:128] * (GLA_DK ** -0.5)
    k = xqk[:, 128:256]
    lo = xqk[:, 256:384]
    gk = _log_sigmoid(_dot3(lo, wgk_ref[...]) + bgk_ref[...]) * (1.0 / GLA_GATE_NORM)
    if sample:
        valid = lax.broadcasted_iota(I32, (rows, 128), 0) < t_valid
        gk = jnp.where(valid, gk, 0.0)
        k = jnp.where(valid, k, 0.0)
    _gated_tile(q, k, v, gk, st_s, o_s, q_s, k_s, v_s, b_s, qh_s, kh_s, dt_s, dk=GLA_DK, n_blk=n_blk)
    seg = _seg_ones(GROUP_WIDTH, HEAD_DIM, GROUP_WIDTH, HEAD_DIM)
    out = _rms_heads(o_s[...], nw_ref[...], gate, seg)
    if sample:
        o_ref[0] = out[0:t_valid, :]
    else:
        o_ref[...] = out
    sT_ref[0] = st_s[...]


def _gated_call(kind, h_all, row0, n_seq, t_len, s0_bd, params):
    sample = t_len < SUB_CHUNK
    hk = N_HEADS * (HG_DK if kind == "hgrn" else GLA_DK)
    if sample:
        rows, n_t, n_blk = SUB_CHUNK, 1, 1
        h_view = h_all.reshape(h_all.shape[0] // t_len, t_len, h_all.shape[1])
        seq0 = row0 // t_len

        def hspec(width, col_block):
            return pl.BlockSpec((1, t_len, width), lambda b, i: (seq0 + b, 0, col_block))

        o_shape = jax.ShapeDtypeStruct((n_seq, t_len, GROUP_WIDTH), F32)
        o_spec = pl.BlockSpec((1, t_len, GROUP_WIDTH), lambda b, i: (b, 0, 0))
    else:
        rows = min(t_len, ROW_TILE)
        n_t, n_blk = t_len // rows, rows // SUB_CHUNK
        h_view = h_all
        blk0 = row0 // rows

        def hspec(width, col_block):
            return pl.BlockSpec((rows, width), lambda b, i: (blk0 + b * n_t + i, col_block))

        o_shape = jax.ShapeDtypeStruct((n_seq * t_len, GROUP_WIDTH), F32)
        o_spec = pl.BlockSpec((rows, GROUP_WIDTH), lambda b, i: (b * n_t + i, 0))

    st_spec = pl.BlockSpec((1, GROUP_WIDTH, hk), lambda b, i: (b, 0, 0))
    st_shape = jax.ShapeDtypeStruct((n_seq, GROUP_WIDTH, hk), F32)

    def full(a):
        return pl.BlockSpec(a.shape, lambda b, i: (0,) * a.ndim)

    scratch = [pltpu.VMEM((GROUP_WIDTH, hk), F32), pltpu.VMEM((rows, GROUP_WIDTH), F32),
               pltpu.VMEM((rows, hk), F32), pltpu.VMEM((rows, hk), F32), pltpu.VMEM((rows, GROUP_WIDTH), F32),
               pltpu.VMEM((rows, hk), F32), pltpu.VMEM((rows, hk), F32), pltpu.VMEM((rows, hk), F32),
               pltpu.VMEM((rows, hk), F32)]
    if kind == "hgrn":
        lb3, nw = params
        body = functools.partial(_hgrn_kernel, sample=sample, t_valid=t_len, n_blk=n_blk)
        in_specs = [hspec(1024, OFF_HG // 1024), st_spec, full(lb3), full(nw)]
        args = (h_view, s0_bd, lb3, nw)
        if sample:
            scratch.append(pltpu.VMEM((rows, 1024), F32))
    else:
        wgk, bgk, nw = params
        body = functools.partial(_gla_kernel, sample=sample, t_valid=t_len, n_blk=n_blk)
        in_specs = [hspec(512, OFF_GLA_VG // 512), hspec(384, OFF_GLA_QK // 384), st_spec,
                    full(wgk), full(bgk), full(nw)]
        args = (h_view, h_view, s0_bd, wgk, bgk, nw)
        if sample:
            scratch += [pltpu.VMEM((rows, 512), F32), pltpu.VMEM((rows, 384), F32)]
    out, st = pl.pallas_call(
        body,
        grid=(n_seq, n_t),
        in_specs=in_specs,
        out_specs=(o_spec, st_spec),
        out_shape=(o_shape, st_shape),
        scratch_shapes=scratch,
        compiler_params=_cparams(2),
        name=kind + ("_sample" if sample else "_prompt"),
    )(*args)
    return out.reshape(n_seq * t_len, GROUP_WIDTH), st


def _rwkv_kernel(p_ref, sh0_ref, s0_ref, mu_ref, lora_ref, vec_ref, o_ref, s_out_ref,
                 s_s, prev_s, seg_s, lm_s, a_s, b_s, w_s, k_s, r_s, v_s, y_s, *, n_tok):
    i = pl.program_id(1)
    g8 = SEQ_GROUP

    @pl.when(i == 0)
    def _():
        s_s[...] = s0_ref[0]
        prev_s[...] = sh0_ref[0]

    seg_s[...] = _seg_ones(GROUP_WIDTH, HEAD_DIM, GROUP_WIDTH, HEAD_DIM)
    lm_s[...] = (lax.broadcasted_iota(I32, (HEAD_DIM, g8, GROUP_WIDTH), 2) % HEAD_DIM
                 == lax.broadcasted_iota(I32, (HEAD_DIM, g8, GROUP_WIDTH), 0)).astype(F32)
    seg = seg_s[...]

    p = p_ref[...]
    if n_tok > 1:
        prev = jnp.concatenate([prev_s[...], p[:-g8, :]], axis=0)
    else:
        prev = prev_s[...]
    prev_s[...] = p[(n_tok - 1) * g8:, :]
    xs = p + (prev - p) * mu_ref[...]
    r = xs[:, 0:256]
    k = xs[:, 256:512]
    v = xs[:, 512:768]
    lo = xs[:, 768:896]
    w0, a0, kkp, ka = vec_ref[0:1, :], vec_ref[1:2, :], vec_ref[2:3, :], vec_ref[3:4, :]
    rk, lnw, lnb = vec_ref[4:5, :], vec_ref[5:6, :], vec_ref[6:7, :]
    w_log = -_softplus(-(w0 + _dot3(jnp.tanh(lo), lora_ref[0]))) - 0.5
    decay = jnp.exp(-jnp.exp(w_log))
    a = _sigmoid(a0 + _dot3(lo, lora_ref[1]))
    g = _dot3(_sigmoid(lo), lora_ref[2])
    kk = k * kkp
    kk = kk * lax.rsqrt(jnp.maximum(_segsum(kk * kk, seg), 1e-24))
    k2 = k * (1.0 + (a - 1.0) * ka)
    a_s[...] = -kk
    b_s[...] = kk * a
    w_s[...] = decay
    k_s[...] = k2
    r_s[...] = r
    v_s[...] = v

    n = HEAD_DIM * g8
    slab = (HEAD_DIM, g8, GROUP_WIDTH)

    def readout(s, row):
        yb = _dot((s * r_s[pl.ds(row, g8), :][None]).reshape(n, GROUP_WIDTH).astype(BF16), seg_s[...])
        return jnp.sum(yb.reshape(slab) * lm_s[...], axis=0)

    def step(t, carry):
        r0 = pl.multiple_of(t * g8, g8)
        rp = pl.multiple_of(jnp.maximum(t - 1, 0) * g8, g8)
        s = s_s[...]
        lm = lm_s[...]
        sg = seg_s[...]
        y_s[pl.ds(rp, g8), :] = readout(s, rp)
        sa = _dot((s * a_s[pl.ds(r0, g8), :][None]).reshape(n, GROUP_WIDTH).astype(BF16), sg).reshape(slab)
        vh, vl = _split2(v_s[pl.ds(r0, g8), :])
        vm = jnp.concatenate([(vh.astype(F32)[None] * lm).reshape(n, GROUP_WIDTH).astype(BF16),
                              (vl.astype(F32)[None] * lm).reshape(n, GROUP_WIDTH).astype(BF16)], axis=0)
        vb2 = _dot(vm, sg)
        vb = (vb2[0:n] + vb2[n:]).reshape(slab)
        s_s[...] = (s * w_s[pl.ds(r0, g8), :][None] + sa * b_s[pl.ds(r0, g8), :][None]
                    + vb * k_s[pl.ds(r0, g8), :][None])
        return carry

    lax.fori_loop(0, n_tok, step, 0, unroll=4)
    last = (n_tok - 1) * g8
    y_s[pl.ds(last, g8), :] = readout(s_s[...], last)

    y = y_s[...]
    mean = _segsum(y, seg) * (1.0 / HEAD_DIM)
    yc = y - mean
    var = _segsum(yc * yc, seg) * (1.0 / HEAD_DIM)
    yn = yc * lax.rsqrt(var + RW_GN_EPS) * lnw + lnb
    bonus = _segsum(r * k2 * rk, seg) * v
    o_ref[...] = (yn + bonus) * g
    s_out_ref[0] = s_s[...]


def _rwkv_call(p_tm, shift0, s0, mu, lora, vec, n_grp, t_len):
    n_tok = min(t_len, 64)
    n_t = t_len // n_tok
    rows = n_tok * SEQ_GROUP

    def full(a):
        return pl.BlockSpec(a.shape, lambda b, i: (0,) * a.ndim)

    st_spec = pl.BlockSpec((1, HEAD_DIM, SEQ_GROUP, GROUP_WIDTH), lambda b, i: (b, 0, 0, 0))
    vm = lambda shape, dt=F32: pltpu.VMEM(shape, dt)
    out, s_out = pl.pallas_call(
        functools.partial(_rwkv_kernel, n_tok=n_tok),
        grid=(n_grp, n_t),
        in_specs=[pl.BlockSpec((None, rows, RW_COLS), lambda b, i: (b, i, 0)),
                  pl.BlockSpec((1, SEQ_GROUP, RW_COLS), lambda b, i: (b, 0, 0)),
                  st_spec, full(mu), full(lora), full(vec)],
        out_specs=(pl.BlockSpec((None, rows, GROUP_WIDTH), lambda b, i: (b, i, 0)), st_spec),
        out_shape=(jax.ShapeDtypeStruct((n_grp, t_len * SEQ_GROUP, GROUP_WIDTH), F32),
                   jax.ShapeDtypeStruct((n_grp, HEAD_DIM, SEQ_GROUP, GROUP_WIDTH), F32)),
        scratch_shapes=[vm((HEAD_DIM, SEQ_GROUP, GROUP_WIDTH)), vm((SEQ_GROUP, RW_COLS)),
                        vm((GROUP_WIDTH, GROUP_WIDTH), BF16), vm((HEAD_DIM, SEQ_GROUP, GROUP_WIDTH)),
                        vm((rows, GROUP_WIDTH)), vm((rows, GROUP_WIDTH)), vm((rows, GROUP_WIDTH)),
                        vm((rows, GROUP_WIDTH)), vm((rows, GROUP_WIDTH)), vm((rows, GROUP_WIDTH)),
                        vm((rows, GROUP_WIDTH))],
        compiler_params=_cparams(2),
        name="rwkv_t%d" % t_len,
    )(p_tm, shift0, s0, mu, lora, vec)
    return out, s_out


def _s5_prep_kernel(are_ref, aim_ref, ldt_ref, bre_ref, bim_ref, abre_ref, abim_ref, bbre_ref, bbim_ref):
    a_re, a_im = are_ref[...], aim_ref[...]
    dt = jnp.exp(ldt_ref[...])
    mag = jnp.exp(a_re * dt)
    ab_re = mag * jnp.cos(a_im * dt)
    ab_im = mag * jnp.sin(a_im * dt)
    den = a_re * a_re + a_im * a_im
    nr, ni = ab_re - 1.0, ab_im
    coef_re = (nr * a_re + ni * a_im) / den
    coef_im = (ni * a_re - nr * a_im) / den
    b_re, b_im = bre_ref[...], bim_ref[...]
    abre_ref[...] = ab_re
    abim_ref[...] = ab_im
    bbre_ref[...] = coef_re * b_re - coef_im * b_im
    bbim_ref[...] = coef_re * b_im + coef_im * b_re


def _s5_prep(a_re, a_im, log_dt, b_re, b_im):
    rows = DEPTH * S5_NGROUPS
    cols = S5_STATE * S5_GROUP
    rep = lambda t: jnp.repeat(t.reshape(rows, S5_STATE), S5_GROUP, axis=1)
    ldt = jnp.broadcast_to(log_dt.reshape(rows, 1), (rows, cols))
    shp = jax.ShapeDtypeStruct((rows, cols), F32)
    ab_re, ab_im, bb_re, bb_im = pl.pallas_call(
        _s5_prep_kernel, out_shape=(shp, shp, shp, shp), name="s5_prep",
    )(rep(a_re), rep(a_im), ldt, b_re.reshape(rows, cols), b_im.reshape(rows, cols))
    pick = lambda t: t.reshape(DEPTH, S5_NGROUPS, S5_STATE, S5_GROUP)[..., 0].reshape(DEPTH, 1, S5_CH)
    bb = lambda t: t.reshape(DEPTH, S5_NGROUPS, S5_STATE, S5_GROUP)
    return pick(ab_re), pick(ab_im), bb(bb_re), bb(bb_im)


def _s5_kernel(u_ref, x0_ref, ab_ref, wb_ref, wc_ref, d_ref, wglu_ref, o_ref, xT_ref,
               x_s, bu_s, xs_s, *, n_tok):
    i = pl.program_id(1)
    g8 = SEQ_GROUP

    @pl.when(i == 0)
    def _():
        x_s[...] = x0_ref[0]

    u = u_ref[...]
    bu_s[...] = _dot(u.astype(BF16), wb_ref[...])
    a_re = jnp.broadcast_to(ab_ref[0:1, :], (g8, S5_CH))
    a_im = jnp.broadcast_to(ab_ref[1:2, :], (g8, S5_CH))

    def step(t, carry):
        x_re, x_im = carry
        r0 = pl.multiple_of(t * g8, g8)
        n_re = a_re * x_re - a_im * x_im + bu_s[pl.ds(r0, g8), 0:S5_CH]
        n_im = a_re * x_im + a_im * x_re + bu_s[pl.ds(r0, g8), S5_CH:]
        xs_s[pl.ds(r0, g8), 0:S5_CH] = n_re
        xs_s[pl.ds(r0, g8), S5_CH:] = n_im
        return n_re, n_im

    x_re, x_im = lax.fori_loop(0, n_tok, step, (x_s[:, 0:S5_CH], x_s[:, S5_CH:]))
    x_s[:, 0:S5_CH] = x_re
    x_s[:, S5_CH:] = x_im
    y = _dot(xs_s[...].astype(BF16), wc_ref[...]) + d_ref[...] * u
    yg = 0.5 * y * (1.0 + jnp.tanh(0.7978845608028654 * (y + 0.044715 * (y * y * y))))
    o_ref[...] = yg * _sigmoid(_dot(yg.astype(BF16), wglu_ref[...]))
    xT_ref[0] = x_s[...]


def _s5_call(u_tm, x0, ab, wb, wc, dvec, wglu, n_grp, t_len):
    n_tok = min(t_len, 64)
    n_t = t_len // n_tok
    rows = n_tok * SEQ_GROUP

    def full(a):
        return pl.BlockSpec(a.shape, lambda b, i: (0,) * a.ndim)

    st_spec = pl.BlockSpec((1, SEQ_GROUP, 2 * S5_CH), lambda b, i: (b, 0, 0))
    return pl.pallas_call(
        functools.partial(_s5_kernel, n_tok=n_tok),
        grid=(n_grp, n_t),
        in_specs=[pl.BlockSpec((None, rows, GROUP_WIDTH), lambda b, i: (b, i, 0)), st_spec,
                  full(ab), full(wb), full(wc), full(dvec), full(wglu)],
        out_specs=(pl.BlockSpec((None, rows, GROUP_WIDTH), lambda b, i: (b, i, 0)), st_spec),
        out_shape=(jax.ShapeDtypeStruct((n_grp, t_len * SEQ_GROUP, GROUP_WIDTH), F32),
                   jax.ShapeDtypeStruct((n_grp, SEQ_GROUP, 2 * S5_CH), F32)),
        scratch_shapes=[pltpu.VMEM((SEQ_GROUP, 2 * S5_CH), F32), pltpu.VMEM((rows, 2 * S5_CH), F32),
                        pltpu.VMEM((rows, 2 * S5_CH), F32)],
        compiler_params=_cparams(2),
        name="s5_t%d" % t_len,
    )(u_tm, x0, ab, wb, wc, dvec, wglu)


CHUNKS = D_MODEL // LANES


def _store_chunked(ref, x):
    rows = x.shape[0]
    for j in range(CHUNKS):
        ref[pl.ds(j, rows, stride=CHUNKS), :] = x[:, j * LANES:(j + 1) * LANES]


def _load_chunked(ref, rows):
    return jnp.concatenate([ref[pl.ds(j, rows, stride=CHUNKS), :] for j in range(CHUNKS)], axis=1)


def _post_mix_kernel(x_ref, oap_ref, obp_ref, ocp_ref, odp_ref, oas_ref, obs_ref, ocs_ref, ods_ref,
                     wout_ref, lnw_ref, lnb_ref, rw_ref, rb_ref,
                     x1_ref, x1c_ref, idx_ref, rank_ref, gate_ref, cnt_ref, carry_s, *, prompt_tiles):
    i = pl.program_id(0)

    @pl.when(i == 0)
    def _():
        carry_s[...] = jnp.zeros_like(carry_s)

    is_prompt = i < prompt_tiles
    pick = lambda p_ref, s_ref: jnp.where(is_prompt, p_ref[...], s_ref[...]).astype(BF16)
    mix = _dot(pick(oap_ref, oas_ref), wout_ref[0:256, :])
    mix += _dot(pick(obp_ref, obs_ref), wout_ref[256:512, :])
    mix += _dot(pick(ocp_ref, ocs_ref), wout_ref[512:768, :])
    mix += _dot(pick(odp_ref, ods_ref), wout_ref[768:1024, :])
    x1 = _layer_norm(DN_ALPHA * x_ref[...] + mix, lnw_ref[...], lnb_ref[...])
    x1_ref[...] = x1
    _store_chunked(x1c_ref, x1)

    xh, xl = _split2(x1)
    logits = _dot(xh, rw_ref[0]) + (_dot(xh, rw_ref[1]) + _dot(xl, rw_ref[0])) + rb_ref[...]
    rows = logits.shape[0]
    lane = lax.broadcasted_iota(I32, (rows, LANES), 1)
    work = logits
    sel_i, sel_v = [], []
    for _ in range(TOP_K):
        m = jnp.max(work, axis=-1, keepdims=True)
        j = jnp.min(jnp.where(work == m, lane, LANES), axis=-1, keepdims=True)
        sel_i.append(j)
        sel_v.append(m)
        work = jnp.where(lane == j, -jnp.inf, work)
    e = [jnp.exp(v - sel_v[0]) for v in sel_v]
    den = (e[0] + e[1]) + (e[2] + e[3])
    onehot = jnp.zeros((rows, LANES), F32)
    for j in sel_i:
        onehot = onehot + (lane == j).astype(F32)
    rr = lax.broadcasted_iota(I32, (rows, rows), 0)
    cc = lax.broadcasted_iota(I32, (rows, rows), 1)
    before = _dot((cc < rr).astype(BF16), onehot.astype(BF16)) + carry_s[0:1, :]
    carry_s[0:1, :] = carry_s[0:1, :] + jnp.sum(onehot, axis=0, keepdims=True)
    idx_o = jnp.zeros((rows, LANES), I32)
    rank_o = jnp.zeros((rows, LANES), I32)
    gate_o = jnp.zeros((rows, LANES), F32)
    for slot in range(TOP_K):
        j = sel_i[slot]
        rank = jnp.sum(jnp.where(lane == j, before, 0.0), axis=-1, keepdims=True)
        idx_o = jnp.where(lane == slot, j, idx_o)
        rank_o = jnp.where(lane == slot, rank.astype(I32), rank_o)
        gate_o = jnp.where(lane == slot, e[slot] / den, gate_o)
    idx_ref[...] = idx_o
    rank_ref[...] = rank_o
    gate_ref[...] = gate_o
    cnt_ref[...] = jnp.broadcast_to(carry_s[0:1, :], cnt_ref.shape)


def _post_mix(x_all, outs_p, outs_s, n_seq_p, t_len_p, wout_l, lnw, lnb, rw_l, rb_l):
    n = x_all.shape[0]
    n_t = t_len_p // ROW_TILE
    prompt_tiles = n_seq_p * n_t
    row = lambda w: pl.BlockSpec((ROW_TILE, w), lambda i: (i, 0))
    w = GROUP_WIDTH
    p_idx = lambda i: jnp.minimum(i, prompt_tiles - 1)
    tm_p = pl.BlockSpec((ROW_TILE, w), lambda i: (p_idx(i) % n_t, p_idx(i) // n_t))
    bm_p = pl.BlockSpec((ROW_TILE, w), lambda i: (p_idx(i), 0))
    bm_s = pl.BlockSpec((ROW_TILE, w), lambda i: (jnp.maximum(i - prompt_tiles, 0), 0))

    def full(a):
        return pl.BlockSpec(a.shape, lambda i: (0,) * a.ndim)

    meta = jax.ShapeDtypeStruct((n, LANES), I32)
    return pl.pallas_call(
        functools.partial(_post_mix_kernel, prompt_tiles=prompt_tiles),
        grid=(n // ROW_TILE,),
        in_specs=[row(D_MODEL), tm_p, bm_p, bm_p, tm_p, bm_s, bm_s, bm_s, bm_s, full(wout_l), full(lnw), full(lnb),
                  full(rw_l), full(rb_l)],
        out_specs=(row(D_MODEL), pl.BlockSpec((ROW_TILE * CHUNKS, LANES), lambda i: (i, 0)),
                   row(LANES), row(LANES), row(LANES), pl.BlockSpec((SUBLANES, LANES), lambda i: (0, 0))),
        out_shape=(jax.ShapeDtypeStruct((n, D_MODEL), F32), jax.ShapeDtypeStruct((n * CHUNKS, LANES), F32),
                   meta, meta, jax.ShapeDtypeStruct((n, LANES), F32),
                   jax.ShapeDtypeStruct((SUBLANES, LANES), F32)),
        scratch_shapes=[pltpu.VMEM((SUBLANES, LANES), F32)],
        compiler_params=_cparams(1),
        name="post_mix",
    )(x_all, *outs_p, *outs_s, wout_l, lnw, lnb, rw_l, rb_l)


def _dispatch_kernel(gend_ref, dest_ref, x_ref, xs_hbm, zero_s, sem, *, tokens, n_tiles):
    i = pl.program_id(0)

    def zero_tile(first_row):
        start = pl.multiple_of(first_row * CHUNKS, MOE_TILE * CHUNKS)
        return pltpu.make_async_copy(zero_s, xs_hbm.at[pl.ds(start, MOE_TILE * CHUNKS)], sem)

    @pl.when(i == 0)
    def _():
        zero_s[...] = jnp.zeros_like(zero_s)
        for e in range(N_EXPERTS):
            @pl.when(gend_ref[e + 1] > gend_ref[e])
            def _():
                zero_tile(gend_ref[e + 1] - MOE_TILE).start()
        for e in range(N_EXPERTS):
            @pl.when(gend_ref[e + 1] > gend_ref[e])
            def _():
                zero_tile(gend_ref[e + 1] - MOE_TILE).wait()

        def tail(t, carry):
            cp = zero_tile(t * MOE_TILE)
            cp.start()
            cp.wait()
            return carry

        lax.fori_loop(gend_ref[N_EXPERTS] // MOE_TILE, n_tiles, tail, 0)

    def row_copy(n, slot):
        src = pl.multiple_of(n * CHUNKS, CHUNKS)
        dst = pl.multiple_of(dest_ref[n * TOP_K + slot] * CHUNKS, CHUNKS)
        return pltpu.make_async_copy(x_ref.at[pl.ds(src, CHUNKS)], xs_hbm.at[pl.ds(dst, CHUNKS)], sem)

    def issue(n, carry):
        for slot in range(TOP_K):
            row_copy(n, slot).start()
        return carry

    lax.fori_loop(0, tokens, issue, 0, unroll=4)
    for slot in range(TOP_K):
        pltpu.make_async_copy(x_ref, xs_hbm.at[pl.ds(0, tokens * CHUNKS)], sem).wait()


def _dispatch(x1c, dest_flat, gend, n_rows):
    n = x1c.shape[0] // CHUNKS
    tokens = DISPATCH_TOKENS if n % DISPATCH_TOKENS == 0 else ROW_TILE
    return pl.pallas_call(
        functools.partial(_dispatch_kernel, tokens=tokens, n_tiles=n_rows // MOE_TILE),
        grid_spec=pltpu.PrefetchScalarGridSpec(
            num_scalar_prefetch=1,
            grid=(n // tokens,),
            in_specs=[pl.BlockSpec((tokens * TOP_K,), lambda i, ge: (i,), memory_space=pltpu.SMEM),
                      pl.BlockSpec((tokens * CHUNKS, LANES), lambda i, ge: (i, 0))],
            out_specs=pl.BlockSpec(memory_space=pl.ANY),
            scratch_shapes=[pltpu.VMEM((MOE_TILE * CHUNKS, LANES), F32), pltpu.SemaphoreType.DMA(())],
        ),
        out_shape=jax.ShapeDtypeStruct((n_rows * CHUNKS, LANES), F32),
        compiler_params=_cparams(1),
        name="moe_dispatch",
    )(gend, dest_flat, x1c)


PAIR_BLOCK = 2 * LANES


def _expert_prep_kernel(w1_ref, w2_ref, w1p_ref, w2b_ref):
    src = lax.broadcasted_iota(I32, (PAIR_BLOCK, PAIR_BLOCK), 0)
    dst = lax.broadcasted_iota(I32, (PAIR_BLOCK, PAIR_BLOCK), 1)
    perm = (src == jnp.where(dst < LANES, 2 * dst, 2 * (dst - LANES) + 1)).astype(BF16)
    for c in range(2 * D_FF // PAIR_BLOCK):
        cols = slice(c * PAIR_BLOCK, (c + 1) * PAIR_BLOCK)
        w1p_ref[:, cols] = _dot(w1_ref[:, cols].astype(BF16), perm).astype(BF16)
    w2b_ref[...] = w2_ref[...].astype(BF16)


def _expert_prep(exp_w1, exp_w2):
    n_l, n_e = exp_w1.shape[:2]
    spec = lambda r, c: pl.BlockSpec((None, None, r, c), lambda i: (i // n_e, i % n_e, 0, 0))
    return pl.pallas_call(
        _expert_prep_kernel,
        grid=(n_l * n_e,),
        in_specs=[spec(D_MODEL, 2 * D_FF), spec(D_FF, D_MODEL)],
        out_specs=(spec(D_MODEL, 2 * D_FF), spec(D_FF, D_MODEL)),
        out_shape=(jax.ShapeDtypeStruct(exp_w1.shape, BF16), jax.ShapeDtypeStruct(exp_w2.shape, BF16)),
        compiler_params=_cparams(1),
        name="expert_prep",
    )(exp_w1, exp_w2)


def _expert_kernel(te_ref, nreal_ref, xs_ref, w1_ref, b1_ref, w2_ref, b2_ref, o_ref, acc_s):
    i = pl.program_id(0)

    @pl.when(i < nreal_ref[0])
    def _():
        x = _load_chunked(xs_ref, MOE_TILE).astype(BF16)
        acc_s[...] = jnp.broadcast_to(b2_ref[...], acc_s.shape)
        for c in range(D_FF // PAIR_BLOCK):
            cols = slice(2 * c * PAIR_BLOCK, 2 * (c + 1) * PAIR_BLOCK)
            h = _dot(x, w1_ref[:, cols]) + b1_ref[:, cols]
            h_glu = jnp.minimum(jnp.concatenate([h[:, 0:128], h[:, 256:384]], axis=1), SWIGLU_LIMIT)
            h_lin = jnp.clip(jnp.concatenate([h[:, 128:256], h[:, 384:512]], axis=1),
                             -SWIGLU_LIMIT, SWIGLU_LIMIT)
            act = h_glu * _sigmoid(SWIGLU_ALPHA * h_glu) * (h_lin + 1.0)
            acc_s[...] += _dot(act.astype(BF16), w2_ref[c * PAIR_BLOCK:(c + 1) * PAIR_BLOCK, :])
        _store_chunked(o_ref, acc_s[...])

    @pl.when(i >= nreal_ref[0])
    def _():
        o_ref[...] = jnp.zeros_like(o_ref)


def _experts(xs, te, nreal, w1d, b1d, w2b, b2, layer):
    n_tiles = xs.shape[0] // (MOE_TILE * CHUNKS)
    clamp = lambda i, nr: jnp.minimum(i, nr[0] - 1)
    return pl.pallas_call(
        _expert_kernel,
        grid_spec=pltpu.PrefetchScalarGridSpec(
            num_scalar_prefetch=2,
            grid=(n_tiles,),
            in_specs=[pl.BlockSpec((MOE_TILE * CHUNKS, LANES), lambda i, te, nr: (clamp(i, nr), 0)),
                      pl.BlockSpec((None, None, D_MODEL, 2 * D_FF), lambda i, te, nr: (layer, te[i], 0, 0)),
                      pl.BlockSpec((None, None, 1, 2 * D_FF), lambda i, te, nr: (layer, te[i], 0, 0)),
                      pl.BlockSpec((None, None, D_FF, D_MODEL), lambda i, te, nr: (layer, te[i], 0, 0)),
                      pl.BlockSpec((None, None, 1, D_MODEL), lambda i, te, nr: (layer, te[i], 0, 0))],
            out_specs=pl.BlockSpec((MOE_TILE * CHUNKS, LANES), lambda i, te, nr: (i, 0)),
            scratch_shapes=[pltpu.VMEM((MOE_TILE, D_MODEL), F32)],
        ),
        out_shape=jax.ShapeDtypeStruct(xs.shape, F32),
        compiler_params=_cparams(1),
        name="moe_experts",
    )(te, nreal, xs, w1d, b1d, w2b, b2)


def _combine_kernel(dest_ref, dest_next_ref, gate_ref, x1_ref, lnw_ref, lnb_ref, ys_hbm, o_ref, buf_s, sem,
                    *, tokens):
    i = pl.program_id(0)
    cur = i % 2

    def fetch(dref, half):
        def issue(n, carry):
            dst = pl.multiple_of(n * CHUNKS, CHUNKS)
            for slot in range(TOP_K):
                src = pl.multiple_of(dref[n * TOP_K + slot] * CHUNKS, CHUNKS)
                pltpu.make_async_copy(ys_hbm.at[pl.ds(src, CHUNKS)], buf_s.at[half, slot, pl.ds(dst, CHUNKS)],
                                      sem.at[half]).start()
            return carry

        lax.fori_loop(0, tokens, issue, 0, unroll=4)

    @pl.when(i == 0)
    def _():
        fetch(dest_ref, 0)

    @pl.when(i + 1 < pl.num_programs(0))
    def _():
        fetch(dest_next_ref, 1 - cur)

    for slot in range(TOP_K):
        pltpu.make_async_copy(ys_hbm.at[pl.ds(0, tokens * CHUNKS)], buf_s.at[cur, slot], sem.at[cur]).wait()
    gate = gate_ref[...]
    ffn = gate[:, 0:1] * _load_chunked(buf_s.at[cur, 0], tokens)
    for slot in range(1, TOP_K):
        ffn = ffn + gate[:, slot:slot + 1] * _load_chunked(buf_s.at[cur, slot], tokens)
    o_ref[...] = _layer_norm(DN_ALPHA * x1_ref[...] + ffn, lnw_ref[...], lnb_ref[...])


def _combine(dest_flat, gate, x1, lnw, lnb, ys):
    n = x1.shape[0]
    tokens = ROW_TILE
    last = n // tokens - 1
    return pl.pallas_call(
        functools.partial(_combine_kernel, tokens=tokens),
        grid=(n // tokens,),
        in_specs=[pl.BlockSpec((tokens * TOP_K,), lambda i: (i,), memory_space=pltpu.SMEM),
                  pl.BlockSpec((tokens * TOP_K,), lambda i: (jnp.minimum(i + 1, last),), memory_space=pltpu.SMEM),
                  pl.BlockSpec((tokens, LANES), lambda i: (i, 0)),
                  pl.BlockSpec((tokens, D_MODEL), lambda i: (i, 0)),
                  pl.BlockSpec((1, D_MODEL), lambda i: (0, 0)),
                  pl.BlockSpec((1, D_MODEL), lambda i: (0, 0)),
                  pl.BlockSpec(memory_space=pl.ANY)],
        out_specs=pl.BlockSpec((tokens, D_MODEL), lambda i: (i, 0)),
        out_shape=jax.ShapeDtypeStruct((n, D_MODEL), F32),
        scratch_shapes=[pltpu.VMEM((2, TOP_K, tokens * CHUNKS, LANES), F32), pltpu.SemaphoreType.DMA((2,))],
        compiler_params=_cparams(1),
        name="moe_combine",
    )(dest_flat, dest_flat, gate, x1, lnw, lnb, ys)


def _moe(x1, x1c, idx, rank, gate, counts, w1d, b1d, w2b, b2, lnw, lnb, layer):
    n = x1.shape[0]
    n_tiles = -(-(n * TOP_K + N_EXPERTS * (MOE_TILE - 1)) // MOE_TILE)
    cnt = counts[0, :N_EXPERTS].astype(I32)
    gsz = ((cnt + (MOE_TILE - 1)) // MOE_TILE) * MOE_TILE
    gend = jnp.cumsum(gsz)
    goff = gend - gsz
    dest = (goff[idx[:, :TOP_K]] + rank[:, :TOP_K]).reshape(-1)
    gend0 = jnp.concatenate([jnp.zeros((1,), I32), gend])
    nreal = (gend[-1:] // MOE_TILE).astype(I32)
    tile_start = jnp.arange(n_tiles, dtype=I32) * MOE_TILE
    te = jnp.minimum(jnp.sum((gend[None, :] <= tile_start[:, None]).astype(I32), axis=1), N_EXPERTS - 1)
    xs = _dispatch(x1c, dest, gend0, n_tiles * MOE_TILE)
    ys = _experts(xs, te, nreal, w1d, b1d, w2b, b2, layer)
    return _combine(dest, gate, x1, lnw, lnb, ys)


def _to_time_major(rows, n_seq, t_len):
    c = rows.shape[-1]
    x = rows.reshape(n_seq // SEQ_GROUP, SEQ_GROUP, t_len, c)
    return jnp.transpose(x, (0, 2, 1, 3)).reshape(n_seq // SEQ_GROUP, t_len * SEQ_GROUP, c)


def _from_time_major(x, n_seq, t_len):
    c = x.shape[-1]
    x = x.reshape(n_seq // SEQ_GROUP, t_len, SEQ_GROUP, c)
    return jnp.transpose(x, (0, 2, 1, 3)).reshape(n_seq * t_len, c)


def _block_diag_state(s, dk):
    st = jnp.swapaxes(s, 2, 3)
    eye = jnp.eye(N_HEADS, dtype=s.dtype)
    return jnp.einsum("bhvk,hg->bhvgk", st, eye).reshape(s.shape[0], GROUP_WIDTH, N_HEADS * dk)


def _unblock_state(st, dk):
    b = st.shape[0]
    x = st.reshape(b, N_HEADS, HEAD_DIM, N_HEADS, dk)
    x = jnp.stack([x[:, h, :, h, :] for h in range(N_HEADS)], axis=1)
    return jnp.swapaxes(x, 2, 3)


def _rwkv_state_in(s):
    b = s.shape[0]
    x = s.reshape(b // SEQ_GROUP, SEQ_GROUP, N_HEADS, HEAD_DIM, HEAD_DIM)
    return jnp.transpose(x, (0, 3, 1, 2, 4)).reshape(b // SEQ_GROUP, HEAD_DIM, SEQ_GROUP, GROUP_WIDTH)


def _rwkv_state_out(x, b):
    x = x.reshape(b // SEQ_GROUP, HEAD_DIM, SEQ_GROUP, N_HEADS, HEAD_DIM)
    return jnp.transpose(x, (0, 2, 3, 1, 4)).reshape(b, N_HEADS, HEAD_DIM, HEAD_DIM)


def _pad_rows(w, row0, n_rows):
    out = jnp.zeros((n_rows, w.shape[1]), w.dtype)
    return out.at[row0:row0 + w.shape[0]].set(w)


def kernel(x_prompt, x_sample, state_rwkv, state_rwkv_shift, state_hgrn, state_gla, state_s5_re, state_s5_im,
           w_in, rw_mu, rw_w0, rw_w2, rw_a0, rw_a2, rw_g2, rw_kk, rw_ka, rw_rk, rw_lnx_w, rw_lnx_b,
           hg_lb_logits, hg_norm_w, gla_w_gk2, gla_b_gk, gla_norm_w,
           s5_A_re, s5_A_im, s5_log_dt, s5_B_re, s5_B_im, s5_C_re, s5_C_im, s5_D, s5_w_glu,
           w_out, ln1_w, ln1_b, router_w, router_b, exp_w1, exp_b1, exp_w2, exp_b2, ln2_w, ln2_b):
    bp, tp, _ = x_prompt.shape
    bs, ts, _ = x_sample.shape
    n_p, n_s = bp * tp, bs * ts
    groups = ((0, bp, tp), (n_p, bs, ts))
    assert bp == SEQ_GROUP and tp % ROW_TILE == 0 and bs % SEQ_GROUP == 0 and n_s % ROW_TILE == 0
    assert ts <= SUB_CHUNK

    c = np.cumsum([0, RW_COLS, 1024, 784, 256])
    rw_c, hg_c, gl_c, s5_c = (w_in[:, :, c[j]:c[j + 1]] for j in range(4))
    gl_q, gl_k, gl_v, gl_lo, gl_g = (gl_c[:, :, a:b] for a, b in
                                     ((0, 128), (128, 256), (256, 512), (512, 528), (528, 784)))
    zpad = jnp.zeros((DEPTH, D_MODEL, 128 - GLA_GK_LORA), w_in.dtype)
    w_in_p = jnp.concatenate([hg_c, gl_v, gl_g, gl_q, gl_k, gl_lo, zpad, rw_c, s5_c], axis=2).astype(BF16)
    w_out_b = w_out.astype(BF16)
    w1d, w2b = _expert_prep(exp_w1, exp_w2)
    b1d = jnp.swapaxes(exp_b1.reshape(DEPTH, N_EXPERTS, 2 * D_FF // PAIR_BLOCK, LANES, 2), -1, -2)
    b1d = b1d.reshape(DEPTH, N_EXPERTS, 1, 2 * D_FF)
    b2r = exp_b2[:, :, None, :]
    rw_pad = jnp.pad(router_w, ((0, 0), (0, 0), (0, LANES - N_EXPERTS)))
    rw_hi = rw_pad.astype(BF16)
    rw_lo = (rw_pad - rw_hi.astype(F32)).astype(BF16)
    rw_split = jnp.stack([rw_hi, rw_lo], axis=1)
    rb_pad = jnp.pad(router_b, ((0, 0), (0, LANES - N_EXPERTS)), constant_values=-1e30)[:, None, :]

    lbs = jnp.cumsum(jax.nn.softmax(hg_lb_logits.astype(F32), axis=0), axis=0)
    lbs = lbs - lbs[:1]
    lb3 = jnp.stack([lbs, jnp.log(lbs), jnp.log1p(-lbs)], axis=1)
    lb3 = jnp.pad(lb3, ((0, 0), (0, SUBLANES - 3), (0, 0)))

    ab_re, ab_im, bb_re, bb_im = _s5_prep(s5_A_re, s5_A_im, s5_log_dt, s5_B_re, s5_B_im)
    eye_g = jnp.eye(S5_NGROUPS, dtype=F32)
    wb = jnp.stack([jnp.einsum("lgph,gk->lghkp", t, eye_g).reshape(DEPTH, GROUP_WIDTH, S5_CH)
                    for t in (bb_re, bb_im)], axis=2).reshape(DEPTH, GROUP_WIDTH, 2 * S5_CH).astype(BF16)
    wc = jnp.concatenate([jnp.einsum("lghp,gk->lgpkh", t, eye_g).reshape(DEPTH, S5_CH, GROUP_WIDTH)
                          for t in (s5_C_re, -s5_C_im)], axis=1).astype(BF16)
    ab = jnp.concatenate([ab_re, ab_im], axis=1)
    ab = jnp.pad(ab, ((0, 0), (0, SUBLANES - 2), (0, 0)))
    wglu_b = s5_w_glu.astype(BF16)

    lora = jnp.stack([jnp.stack([_pad_rows(rw_w2[l], 0, 128), _pad_rows(rw_a2[l], 32, 128),
                                 _pad_rows(rw_g2[l], 64, 128)]) for l in range(DEPTH)])
    rw_vec = jnp.stack([rw_w0, rw_a0, rw_kk, rw_ka, rw_rk, rw_lnx_w, rw_lnx_b, jnp.zeros_like(rw_w0)], axis=1)
    wgk = jnp.stack([_pad_rows(gla_w_gk2[l], 0, 128) for l in range(DEPTH)])

    zeros = lambda shape: jnp.zeros(shape, F32)
    st_in = (
        dict(rw=zeros((DEPTH, bp, N_HEADS, HEAD_DIM, HEAD_DIM)), sh=zeros((DEPTH, bp, RW_COLS)),
             hg=zeros((DEPTH, bp, N_HEADS, HG_DK, HEAD_DIM)), gl=zeros((DEPTH, bp, N_HEADS, GLA_DK, HEAD_DIM)),
             re=zeros((DEPTH, bp, S5_NGROUPS, S5_STATE)), im=zeros((DEPTH, bp, S5_NGROUPS, S5_STATE))),
        dict(rw=state_rwkv, sh=state_rwkv_shift, hg=state_hgrn, gl=state_gla, re=state_s5_re, im=state_s5_im),
    )
    collected = ([], [])

    x_all = jnp.concatenate([x_prompt.reshape(n_p, D_MODEL), x_sample.reshape(n_s, D_MODEL)], axis=0)
    for l in range(DEPTH):
        h_gate_p, p_tm_p, u_tm_p = _inproj_prompt(x_all, w_in_p[l], bp, tp)
        h_s = _inproj_rows(x_all, w_in_p[l], n_p, n_s)
        outs = ([], [])
        for gi, (row0, n_seq, t_len) in enumerate(groups):
            st = st_in[gi]
            n_grp = n_seq // SEQ_GROUP
            if gi == 0:
                h_gate = h_gate_p
                p_tm = p_tm_p.reshape(1, t_len * n_seq, RW_COLS)
                u_tm = u_tm_p.reshape(1, t_len * n_seq, GROUP_WIDTH)
                new_sh = p_tm_p[t_len - 1].reshape(n_seq, RW_COLS)
                to_rows = lambda x_tm: x_tm.reshape(t_len, n_seq * GROUP_WIDTH)
            else:
                h_gate = h_s
                p_rw = h_s[:, OFF_RW:OFF_RW + RW_COLS]
                p_tm = _to_time_major(p_rw, n_seq, t_len)
                u_tm = _to_time_major(h_s[:, OFF_S5:OFF_S5 + GROUP_WIDTH], n_seq, t_len)
                new_sh = p_rw.reshape(n_seq, t_len, RW_COLS)[:, -1]
                to_rows = lambda x_tm: _from_time_major(x_tm, n_seq, t_len)
            sh0 = st["sh"][l].reshape(n_grp, SEQ_GROUP, RW_COLS)
            oa_tm, s_rw = _rwkv_call(p_tm, sh0, _rwkv_state_in(st["rw"][l]), rw_mu[l][None, :], lora[l],
                                     rw_vec[l], n_grp, t_len)
            new_rw = _rwkv_state_out(s_rw, n_seq)
            ob, s_hg = _gated_call("hgrn", h_gate, 0, n_seq, t_len, _block_diag_state(st["hg"][l], HG_DK),
                                   (lb3[l], hg_norm_w[l][None, :]))
            oc, s_gl = _gated_call("gla", h_gate, 0, n_seq, t_len, _block_diag_state(st["gl"][l], GLA_DK),
                                   (wgk[l], gla_b_gk[l][None, :], gla_norm_w[l][None, :]))
            x0 = jnp.concatenate([st["re"][l].reshape(n_grp, SEQ_GROUP, S5_CH),
                                  st["im"][l].reshape(n_grp, SEQ_GROUP, S5_CH)], axis=-1)
            od_tm, x_t = _s5_call(u_tm, x0, ab[l], wb[l], wc[l], s5_D[l][None, :], wglu_b[l], n_grp, t_len)
            outs[gi].extend([to_rows(oa_tm), ob, oc, to_rows(od_tm)])
            x_t = x_t.reshape(n_seq, 2, S5_NGROUPS, S5_STATE)
            collected[gi].append((new_rw, new_sh, _unblock_state(s_hg, HG_DK), _unblock_state(s_gl, GLA_DK),
                                  x_t[:, 0], x_t[:, 1]))
        x1, x1c, idx, rank, gate, counts = _post_mix(x_all, outs[0], outs[1], bp, tp, w_out_b[l],
                                                     ln1_w[l][None, :], ln1_b[l][None, :],
                                                     rw_split[l], rb_pad[l])
        x_all = _moe(x1, x1c, idx, rank, gate, counts, w1d, b1d, w2b, b2r, ln2_w[l][None, :], ln2_b[l][None, :], l)

    y_prompt = x_all[:n_p].reshape(bp, tp, D_MODEL)
    y_sample = x_all[n_p:].reshape(bs, ts, D_MODEL)
    ps = [jnp.stack([layer[j] for layer in collected[0]]) for j in range(6)]
    ss = [jnp.stack([layer[j] for layer in collected[1]]) for j in range(6)]
    return (y_prompt, y_sample, *ps, *ss)
```

```python
import functools

import jax
import jax.numpy as jnp
import numpy as np
from jax import lax
from jax.experimental import pallas as pl
from jax.experimental.pallas import tpu as pltpu

F32 = jnp.float32
BF16 = jnp.bfloat16
I32 = jnp.int32

D_MODEL = 1024
DEPTH = 4
GROUP_WIDTH = 256
HEAD_DIM = 64
N_HEADS = 4
RW_COLS = 896
RW_GN_EPS = 64e-5
HG_DK = 64
GLA_DK = 32
GLA_GK_LORA = 16
GLA_GATE_NORM = 16.0
S5_NGROUPS = 16
S5_GROUP = 16
S5_STATE = 64
S5_CH = S5_NGROUPS * S5_STATE
N_EXPERTS = 32
TOP_K = 4
D_FF = 1024
SWIGLU_ALPHA = 1.702
SWIGLU_LIMIT = 7.0
DN_ALPHA = (2.0 * DEPTH) ** 0.25
LN_EPS = 1e-5

SUBLANES = 8
LANES = 128
VMEM_LIMIT_BYTES = 48 * 1024 * 1024

OFF_HG = 0
OFF_GLA_VG = 1024
OFF_GLA_QK = 1536
OFF_RW = 1920
OFF_S5 = 2816
IN_PAD = 3072

ROW_TILE = 256
SUB_CHUNK = 16
SEQ_GROUP = SUBLANES
MOE_TILE = 512
DISPATCH_TOKENS = 512


def _cparams(n_axes):
    return pltpu.CompilerParams(dimension_semantics=("arbitrary",) * n_axes,
                                vmem_limit_bytes=VMEM_LIMIT_BYTES)


def _dot(a, b):
    return jnp.dot(a, b, preferred_element_type=F32)


def _split2(x):
    hi = x.astype(BF16)
    lo = (x - hi.astype(F32)).astype(BF16)
    return hi, lo


def _split3(x):
    hi = x.astype(BF16)
    r = x - hi.astype(F32)
    mid = r.astype(BF16)
    lo = (r - mid.astype(F32)).astype(BF16)
    return hi, mid, lo


def _dot3(a, b):
    ah, al = _split2(a)
    bh, bl = _split2(b)
    return _dot(ah, bh) + (_dot(ah, bl) + _dot(al, bh))


def _seg_ones(n_in, seg_in, n_out, seg_out):
    r = lax.broadcasted_iota(I32, (n_in, n_out), 0) // seg_in
    c = lax.broadcasted_iota(I32, (n_in, n_out), 1) // seg_out
    return (r == c).astype(BF16)


def _segsum(x, seg):
    rows = x.shape[0]
    hi, lo = _split2(x)
    both = _dot(jnp.concatenate([hi, lo], axis=0), seg)
    return both[:rows] + both[rows:]


def _sigmoid(x):
    return 1.0 / (1.0 + jnp.exp(-x))


def _log_sigmoid(x):
    return jnp.minimum(x, 0.0) - jnp.log1p(jnp.exp(-jnp.abs(x)))


def _softplus(x):
    return jnp.maximum(x, 0.0) + jnp.log1p(jnp.exp(-jnp.abs(x)))


def _layer_norm(x, w, b):
    xc = x - jnp.mean(x, axis=-1, keepdims=True)
    var = jnp.mean(xc * xc, axis=-1, keepdims=True)
    return xc * lax.rsqrt(var + LN_EPS) * w + b


def _inproj_kernel(x_ref, w_ref, *o_refs):
    h = _dot(x_ref[...].astype(BF16), w_ref[...])
    col = 0
    for o_ref in o_refs:
        o_ref[...] = h[:, col:col + o_ref.shape[-1]]
        col += o_ref.shape[-1]


def _inproj_prompt(x_all, w_in_l, n_seq, t_len):
    n_t = t_len // ROW_TILE
    return pl.pallas_call(
        _inproj_kernel,
        grid=(n_seq, n_t),
        in_specs=[pl.BlockSpec((ROW_TILE, D_MODEL), lambda b, i: (b * n_t + i, 0)),
                  pl.BlockSpec((D_MODEL, IN_PAD), lambda b, i: (0, 0))],
        out_specs=(pl.BlockSpec((ROW_TILE, OFF_RW), lambda b, i: (b * n_t + i, 0)),
                   pl.BlockSpec((ROW_TILE, RW_COLS), lambda b, i: (i, b)),
                   pl.BlockSpec((ROW_TILE, GROUP_WIDTH), lambda b, i: (i, b))),
        out_shape=(jax.ShapeDtypeStruct((n_seq * t_len, OFF_RW), F32),
                   jax.ShapeDtypeStruct((t_len, n_seq * RW_COLS), F32),
                   jax.ShapeDtypeStruct((t_len, n_seq * GROUP_WIDTH), F32)),
        compiler_params=_cparams(2),
        name="inproj_prompt",
    )(x_all, w_in_l)


def _inproj_rows(x_all, w_in_l, row0, n_rows):
    blk0 = row0 // ROW_TILE
    return pl.pallas_call(
        _inproj_kernel,
        grid=(n_rows // ROW_TILE,),
        in_specs=[pl.BlockSpec((ROW_TILE, D_MODEL), lambda i: (blk0 + i, 0)),
                  pl.BlockSpec((D_MODEL, IN_PAD), lambda i: (0, 0))],
        out_specs=pl.BlockSpec((ROW_TILE, IN_PAD), lambda i: (i, 0)),
        out_shape=jax.ShapeDtypeStruct((n_rows, IN_PAD), F32),
        compiler_params=_cparams(1),
        name="inproj_rows",
    )(x_all, w_in_l)


def _gated_tile(q, k, v, g, st_ref, o_ref, q_s, k_s, v_s, b_s, qh_s, kh_s, dt_s, *, dk, n_blk):
    c = SUB_CHUNK
    rows, hk = q.shape
    rr = lax.broadcasted_iota(I32, (rows, rows), 0)
    cc = lax.broadcasted_iota(I32, (rows, rows), 1)
    same = (rr // c) == (cc // c)
    tri = jnp.concatenate([(same & (cc <= rr)).astype(BF16), same.astype(BF16)], axis=0)
    g3 = jnp.concatenate(_split3(g), axis=1)
    p = _dot(tri, g3)
    b = p[:rows, :hk] + p[:rows, hk:2 * hk] + p[:rows, 2 * hk:]
    btot = p[rows:, :hk] + p[rows:, hk:2 * hk] + p[rows:, 2 * hk:]
    q_s[...] = q
    k_s[...] = k
    v_s[...] = v
    b_s[...] = b
    qh_s[...] = q * jnp.exp(b)
    kh_s[...] = k * jnp.exp(btot - b)
    dt_s[...] = jnp.exp(btot)
    seg = _seg_ones(hk, dk, GROUP_WIDTH, HEAD_DIM)
    bd_mask = (lax.broadcasted_iota(I32, (GROUP_WIDTH, hk), 0) // HEAD_DIM
               == lax.broadcasted_iota(I32, (GROUP_WIDTH, hk), 1) // dk).astype(F32)
    t_iota = lax.broadcasted_iota(I32, (c, hk), 0)

    def block(i, carry):
        r0 = pl.multiple_of(i * c, c)
        qb = q_s[pl.ds(r0, c), :]
        kb = k_s[pl.ds(r0, c), :]
        vb = v_s[pl.ds(r0, c), :]
        bb = b_s[pl.ds(r0, c), :]
        pieces = []
        for s in range(c):
            d = jnp.where(t_iota >= s, bb - bb[s:s + 1, :], -jnp.inf)
            pieces.append(jnp.exp(d) * qb * kb[s:s + 1, :])
        att = _segsum(jnp.concatenate(pieces, axis=0), seg)
        o = att[0:c, :] * vb[0:1, :]
        for s in range(1, c):
            o = o + att[s * c:(s + 1) * c, :] * vb[s:s + 1, :]
        si = i if st_ref.shape[0] > 1 else 0
        st = st_ref[si]
        o = o + lax.dot_general(qh_s[pl.ds(r0, c), :].astype(BF16), st.astype(BF16),
                                (((1,), (1,)), ((), ())), preferred_element_type=F32)
        upd = lax.dot_general(vb.astype(BF16), kh_s[pl.ds(r0, c), :].astype(BF16),
                              (((0,), (0,)), ((), ())), preferred_element_type=F32)
        st_ref[si] = st * dt_s[pl.ds(r0, 1), :] + upd * bd_mask
        o_ref[pl.ds(r0, c), :] = o
        return carry

    lax.fori_loop(0, n_blk, block, 0, unroll=min(n_blk, 4))


def _rms_heads(o, w, gate, seg):
    ms = _segsum(o * o, seg) * (1.0 / HEAD_DIM)
    return o * lax.rsqrt(ms + LN_EPS) * w * (gate * _sigmoid(gate))


def _load_rows(ref, sample, pad_s, t_valid):
    if not sample:
        return ref[...]
    pad_s[...] = jnp.zeros_like(pad_s)
    for g in range(ref.shape[0]):
        pad_s[g * SUB_CHUNK:g * SUB_CHUNK + t_valid, :] = ref[g]
    return pad_s[...]


def _store_rows(o_ref, out, sample, t_valid):
    if not sample:
        o_ref[...] = out
        return
    for g in range(o_ref.shape[0]):
        o_ref[g] = out[g * SUB_CHUNK:g * SUB_CHUNK + t_valid, :]


def _valid_rows(rows, width, t_valid):
    return lax.broadcasted_iota(I32, (rows, width), 0) % SUB_CHUNK < t_valid


def _init_state(st_s, s0_ref, sample):
    if sample:
        st_s[...] = s0_ref[...]
    else:
        @pl.when(pl.program_id(1) == 0)
        def _():
            st_s[...] = s0_ref[...]


def _hgrn_kernel(h_ref, s0_ref, lb_ref, nw_ref, o_ref, sT_ref,
                 st_s, o_s, q_s, k_s, v_s, b_s, qh_s, kh_s, dt_s, *pad, sample, t_valid, n_blk):
    _init_state(st_s, s0_ref, sample)
    x = _load_rows(h_ref, sample, pad[0] if sample else None, t_valid)
    rows = x.shape[0]
    q = x[:, 0:256]
    fx = x[:, 256:512]
    iv = x[:, 512:768]
    gate = x[:, 768:1024]
    lb = lb_ref[0:1, :]
    log_lb = lb_ref[1:2, :]
    log1m_lb = lb_ref[2:3, :]
    cterm = log1m_lb + _log_sigmoid(fx)
    log_f = jnp.maximum(log_lb, cterm) + jnp.log1p(jnp.exp(-jnp.abs(log_lb - cterm)))
    key = (1.0 - lb) * _sigmoid(-fx)
    qs = q * _sigmoid(q) * (HG_DK ** -0.5)
    if sample:
        valid = _valid_rows(rows, GROUP_WIDTH, t_valid)
        log_f = jnp.where(valid, log_f, 0.0)
        key = jnp.where(valid, key, 0.0)
        iv = jnp.where(valid, iv, 0.0)
    _gated_tile(qs, key, iv, log_f, st_s, o_s, q_s, k_s, v_s, b_s, qh_s, kh_s, dt_s, dk=HG_DK, n_blk=n_blk)
    seg = _seg_ones(GROUP_WIDTH, HEAD_DIM, GROUP_WIDTH, HEAD_DIM)
    _store_rows(o_ref, _rms_heads(o_s[...], nw_ref[...], gate, seg), sample, t_valid)
    sT_ref[...] = st_s[...]


def _gla_kernel(hvg_ref, hqk_ref, s0_ref, wgk_ref, bgk_ref, nw_ref, o_ref, sT_ref,
                st_s, o_s, q_s, k_s, v_s, b_s, qh_s, kh_s, dt_s, *pad, sample, t_valid, n_blk):
    _init_state(st_s, s0_ref, sample)
    xvg = _load_rows(hvg_ref, sample, pad[0] if sample else None, t_valid)
    xqk = _load_rows(hqk_ref, sample, pad[1] if sample else None, t_valid)
    rows = xvg.shape[0]
    v = xvg[:, 0:256]
    gate = xvg[:, 256:512]
    q = xqk[:, 0:128] * (GLA_DK ** -0.5)
    k = xqk[:, 128:256]
    lo = xqk[:, 256:384]
    gk = _log_sigmoid(_dot3(lo, wgk_ref[...]) + bgk_ref[...]) * (1.0 / GLA_GATE_NORM)
    if sample:
        valid = _valid_rows(rows, 128, t_valid)
        gk = jnp.where(valid, gk, 0.0)
        k = jnp.where(valid, k, 0.0)
    _gated_tile(q, k, v, gk, st_s, o_s, q_s, k_s, v_s, b_s, qh_s, kh_s, dt_s, dk=GLA_DK, n_blk=n_blk)
    seg = _seg_ones(GROUP_WIDTH, HEAD_DIM, GROUP_WIDTH, HEAD_DIM)
    _store_rows(o_ref, _rms_heads(o_s[...], nw_ref[...], gate, seg), sample, t_valid)
    sT_ref[...] = st_s[...]


def _gated_call(kind, h_all, row0, n_seq, t_len, s0_bd, params):
    sample = t_len < SUB_CHUNK
    hk = N_HEADS * (HG_DK if kind == "hgrn" else GLA_DK)
    if sample:
        per_step = SEQ_GROUP
        rows, n_t, n_blk = per_step * SUB_CHUNK, 1, per_step
        h_view = h_all.reshape(h_all.shape[0] // t_len, t_len, h_all.shape[1])
        seq0 = row0 // (t_len * per_step)

        def hspec(width, col_block):
            return pl.BlockSpec((per_step, t_len, width), lambda b, i: (seq0 + b, 0, col_block))

        o_shape = jax.ShapeDtypeStruct((n_seq, t_len, GROUP_WIDTH), F32)
        o_spec = pl.BlockSpec((per_step, t_len, GROUP_WIDTH), lambda b, i: (b, 0, 0))
    else:
        per_step = 1
        rows = min(t_len, ROW_TILE)
        n_t, n_blk = t_len // rows, rows // SUB_CHUNK
        h_view = h_all
        blk0 = row0 // rows

        def hspec(width, col_block):
            return pl.BlockSpec((rows, width), lambda b, i: (blk0 + b * n_t + i, col_block))

        o_shape = jax.ShapeDtypeStruct((n_seq * t_len, GROUP_WIDTH), F32)
        o_spec = pl.BlockSpec((rows, GROUP_WIDTH), lambda b, i: (b * n_t + i, 0))

    st_spec = pl.BlockSpec((per_step, GROUP_WIDTH, hk), lambda b, i: (b, 0, 0))
    st_shape = jax.ShapeDtypeStruct((n_seq, GROUP_WIDTH, hk), F32)

    def full(a):
        return pl.BlockSpec(a.shape, lambda b, i: (0,) * a.ndim)

    scratch = [pltpu.VMEM((per_step, GROUP_WIDTH, hk), F32), pltpu.VMEM((rows, GROUP_WIDTH), F32),
               pltpu.VMEM((rows, hk), F32), pltpu.VMEM((rows, hk), F32), pltpu.VMEM((rows, GROUP_WIDTH), F32),
               pltpu.VMEM((rows, hk), F32), pltpu.VMEM((rows, hk), F32), pltpu.VMEM((rows, hk), F32),
               pltpu.VMEM((rows, hk), F32)]
    if kind == "hgrn":
        lb3, nw = params
        body = functools.partial(_hgrn_kernel, sample=sample, t_valid=t_len, n_blk=n_blk)
        in_specs = [hspec(1024, OFF_HG // 1024), st_spec, full(lb3), full(nw)]
        args = (h_view, s0_bd, lb3, nw)
        if sample:
            scratch.append(pltpu.VMEM((rows, 1024), F32))
    else:
        wgk, bgk, nw = params
        body = functools.partial(_gla_kernel, sample=sample, t_valid=t_len, n_blk=n_blk)
        in_specs = [hspec(512, OFF_GLA_VG // 512), hspec(384, OFF_GLA_QK // 384), st_spec,
                    full(wgk), full(bgk), full(nw)]
        args = (h_view, h_view, s0_bd, wgk, bgk, nw)
        if sample:
            scratch += [pltpu.VMEM((rows, 512), F32), pltpu.VMEM((rows, 384), F32)]
    out, st = pl.pallas_call(
        body,
        grid=(n_seq // per_step, n_t),
        in_specs=in_specs,
        out_specs=(o_spec, st_spec),
        out_shape=(o_shape, st_shape),
        scratch_shapes=scratch,
        compiler_params=_cparams(2),
        name=kind + ("_sample" if sample else "_prompt"),
    )(*args)
    return out.reshape(n_seq * t_len, GROUP_WIDTH), st


def _rwkv_kernel(p_ref, sh0_ref, s0_ref, mu_ref, lora_ref, vec_ref, o_ref, s_out_ref,
                 s_s, prev_s, seg_s, lm_s, a_s, b_s, w_s, k_s, r_s, v_s, y_s, *, n_tok):
    i = pl.program_id(1)
    g8 = SEQ_GROUP

    @pl.when(i == 0)
    def _():
        s_s[...] = s0_ref[0]
        prev_s[...] = sh0_ref[0]

    seg_s[...] = _seg_ones(GROUP_WIDTH, HEAD_DIM, GROUP_WIDTH, HEAD_DIM)
    lm_s[...] = (lax.broadcasted_iota(I32, (HEAD_DIM, g8, GROUP_WIDTH), 2) % HEAD_DIM
                 == lax.broadcasted_iota(I32, (HEAD_DIM, g8, GROUP_WIDTH), 0)).astype(F32)
    seg = seg_s[...]

    p = p_ref[...]
    if n_tok > 1:
        prev = jnp.concatenate([prev_s[...], p[:-g8, :]], axis=0)
    else:
        prev = prev_s[...]
    prev_s[...] = p[(n_tok - 1) * g8:, :]
    xs = p + (prev - p) * mu_ref[...]
    r = xs[:, 0:256]
    k = xs[:, 256:512]
    v = xs[:, 512:768]
    lo = xs[:, 768:896]
    w0, a0, kkp, ka = vec_ref[0:1, :], vec_ref[1:2, :], vec_ref[2:3, :], vec_ref[3:4, :]
    rk, lnw, lnb = vec_ref[4:5, :], vec_ref[5:6, :], vec_ref[6:7, :]
    w_log = -_softplus(-(w0 + _dot3(jnp.tanh(lo), lora_ref[0]))) - 0.5
    decay = jnp.exp(-jnp.exp(w_log))
    a = _sigmoid(a0 + _dot3(lo, lora_ref[1]))
    g = _dot3(_sigmoid(lo), lora_ref[2])
    kk = k * kkp
    kk = kk * lax.rsqrt(jnp.maximum(_segsum(kk * kk, seg), 1e-24))
    k2 = k * (1.0 + (a - 1.0) * ka)
    a_s[...] = -kk
    b_s[...] = kk * a
    w_s[...] = decay
    k_s[...] = k2
    r_s[...] = r
    v_s[...] = v

    n = HEAD_DIM * g8
    slab = (HEAD_DIM, g8, GROUP_WIDTH)

    def readout(s, row):
        yb = _dot((s * r_s[pl.ds(row, g8), :][None]).reshape(n, GROUP_WIDTH).astype(BF16), seg_s[...])
        return jnp.sum(yb.reshape(slab) * lm_s[...], axis=0)

    def step(t, carry):
        r0 = pl.multiple_of(t * g8, g8)
        rp = pl.multiple_of(jnp.maximum(t - 1, 0) * g8, g8)
        s = s_s[...]
        lm = lm_s[...]
        sg = seg_s[...]
        y_s[pl.ds(rp, g8), :] = readout(s, rp)
        sa = _dot((s * a_s[pl.ds(r0, g8), :][None]).reshape(n, GROUP_WIDTH).astype(BF16), sg).reshape(slab)
        vm = (v_s[pl.ds(r0, g8), :][None] * lm).reshape(n, GROUP_WIDTH).astype(BF16)
        vb = _dot(vm, sg).reshape(slab)
        s_s[...] = (s * w_s[pl.ds(r0, g8), :][None] + sa * b_s[pl.ds(r0, g8), :][None]
                    + vb * k_s[pl.ds(r0, g8), :][None])
        return carry

    lax.fori_loop(0, n_tok, step, 0, unroll=4)
    last = (n_tok - 1) * g8
    y_s[pl.ds(last, g8), :] = readout(s_s[...], last)

    y = y_s[...]
    mean = _segsum(y, seg) * (1.0 / HEAD_DIM)
    yc = y - mean
    var = _segsum(yc * yc, seg) * (1.0 / HEAD_DIM)
    yn = yc * lax.rsqrt(var + RW_GN_EPS) * lnw + lnb
    bonus = _segsum(r * k2 * rk, seg) * v
    o_ref[...] = (yn + bonus) * g
    s_out_ref[0] = s_s[...]


def _rwkv_call(p_tm, shift0, s0, mu, lora, vec, n_grp, t_len):
    n_tok = min(t_len, 64)
    n_t = t_len // n_tok
    rows = n_tok * SEQ_GROUP

    def full(a):
        return pl.BlockSpec(a.shape, lambda b, i: (0,) * a.ndim)

    st_spec = pl.BlockSpec((1, HEAD_DIM, SEQ_GROUP, GROUP_WIDTH), lambda b, i: (b, 0, 0, 0))
    vm = lambda shape, dt=F32: pltpu.VMEM(shape, dt)
    out, s_out = pl.pallas_call(
        functools.partial(_rwkv_kernel, n_tok=n_tok),
        grid=(n_grp, n_t),
        in_specs=[pl.BlockSpec((None, rows, RW_COLS), lambda b, i: (b, i, 0)),
                  pl.BlockSpec((1, SEQ_GROUP, RW_COLS), lambda b, i: (b, 0, 0)),
                  st_spec, full(mu), full(lora), full(vec)],
        out_specs=(pl.BlockSpec((None, rows, GROUP_WIDTH), lambda b, i: (b, i, 0)), st_spec),
        out_shape=(jax.ShapeDtypeStruct((n_grp, t_len * SEQ_GROUP, GROUP_WIDTH), F32),
                   jax.ShapeDtypeStruct((n_grp, HEAD_DIM, SEQ_GROUP, GROUP_WIDTH), F32)),
        scratch_shapes=[vm((HEAD_DIM, SEQ_GROUP, GROUP_WIDTH)), vm((SEQ_GROUP, RW_COLS)),
                        vm((GROUP_WIDTH, GROUP_WIDTH), BF16), vm((HEAD_DIM, SEQ_GROUP, GROUP_WIDTH)),
                        vm((rows, GROUP_WIDTH)), vm((rows, GROUP_WIDTH)), vm((rows, GROUP_WIDTH)),
                        vm((rows, GROUP_WIDTH)), vm((rows, GROUP_WIDTH)), vm((rows, GROUP_WIDTH)),
                        vm((rows, GROUP_WIDTH))],
        compiler_params=_cparams(2),
        name="rwkv_t%d" % t_len,
    )(p_tm, shift0, s0, mu, lora, vec)
    return out, s_out


def _s5_prep_kernel(are_ref, aim_ref, ldt_ref, bre_ref, bim_ref, abre_ref, abim_ref, bbre_ref, bbim_ref):
    a_re, a_im = are_ref[...], aim_ref[...]
    dt = jnp.exp(ldt_ref[...])
    mag = jnp.exp(a_re * dt)
    ab_re = mag * jnp.cos(a_im * dt)
    ab_im = mag * jnp.sin(a_im * dt)
    den = a_re * a_re + a_im * a_im
    nr, ni = ab_re - 1.0, ab_im
    coef_re = (nr * a_re + ni * a_im) / den
    coef_im = (ni * a_re - nr * a_im) / den
    b_re, b_im = bre_ref[...], bim_ref[...]
    abre_ref[...] = ab_re
    abim_ref[...] = ab_im
    bbre_ref[...] = coef_re * b_re - coef_im * b_im
    bbim_ref[...] = coef_re * b_im + coef_im * b_re


def _s5_prep(a_re, a_im, log_dt, b_re, b_im):
    rows = DEPTH * S5_NGROUPS
    cols = S5_STATE * S5_GROUP
    rep = lambda t: jnp.repeat(t.reshape(rows, S5_STATE), S5_GROUP, axis=1)
    ldt = jnp.broadcast_to(log_dt.reshape(rows, 1), (rows, cols))
    shp = jax.ShapeDtypeStruct((rows, cols), F32)
    ab_re, ab_im, bb_re, bb_im = pl.pallas_call(
        _s5_prep_kernel, out_shape=(shp, shp, shp, shp), name="s5_prep",
    )(rep(a_re), rep(a_im), ldt, b_re.reshape(rows, cols), b_im.reshape(rows, cols))
    pick = lambda t: t.reshape(DEPTH, S5_NGROUPS, S5_STATE, S5_GROUP)[..., 0].reshape(DEPTH, 1, S5_CH)
    bb = lambda t: t.reshape(DEPTH, S5_NGROUPS, S5_STATE, S5_GROUP)
    return pick(ab_re), pick(ab_im), bb(bb_re), bb(bb_im)


def _s5_kernel(u_ref, x0_ref, ab_ref, wb_ref, wc_ref, d_ref, wglu_ref, o_ref, xT_ref,
               x_s, bu_s, xs_s, *, n_tok):
    i = pl.program_id(1)
    g8 = SEQ_GROUP

    @pl.when(i == 0)
    def _():
        x_s[...] = x0_ref[0]

    u = u_ref[...]
    bu_s[...] = _dot(u.astype(BF16), wb_ref[...])
    a_re = jnp.broadcast_to(ab_ref[0:1, :], (g8, S5_CH))
    a_im = jnp.broadcast_to(ab_ref[1:2, :], (g8, S5_CH))

    def step(t, carry):
        x_re, x_im = carry
        r0 = pl.multiple_of(t * g8, g8)
        n_re = a_re * x_re - a_im * x_im + bu_s[pl.ds(r0, g8), 0:S5_CH]
        n_im = a_re * x_im + a_im * x_re + bu_s[pl.ds(r0, g8), S5_CH:]
        xs_s[pl.ds(r0, g8), 0:S5_CH] = n_re
        xs_s[pl.ds(r0, g8), S5_CH:] = n_im
        return n_re, n_im

    x_re, x_im = lax.fori_loop(0, n_tok, step, (x_s[:, 0:S5_CH], x_s[:, S5_CH:]))
    x_s[:, 0:S5_CH] = x_re
    x_s[:, S5_CH:] = x_im
    y = _dot(xs_s[...].astype(BF16), wc_ref[...]) + d_ref[...] * u
    yg = 0.5 * y * (1.0 + jnp.tanh(0.7978845608028654 * (y + 0.044715 * (y * y * y))))
    o_ref[...] = yg * _sigmoid(_dot(yg.astype(BF16), wglu_ref[...]))
    xT_ref[0] = x_s[...]


def _s5_call(u_tm, x0, ab, wb, wc, dvec, wglu, n_grp, t_len):
    n_tok = min(t_len, 64)
    n_t = t_len // n_tok
    rows = n_tok * SEQ_GROUP

    def full(a):
        return pl.BlockSpec(a.shape, lambda b, i: (0,) * a.ndim)

    st_spec = pl.BlockSpec((1, SEQ_GROUP, 2 * S5_CH), lambda b, i: (b, 0, 0))
    return pl.pallas_call(
        functools.partial(_s5_kernel, n_tok=n_tok),
        grid=(n_grp, n_t),
        in_specs=[pl.BlockSpec((None, rows, GROUP_WIDTH), lambda b, i: (b, i, 0)), st_spec,
                  full(ab), full(wb), full(wc), full(dvec), full(wglu)],
        out_specs=(pl.BlockSpec((None, rows, GROUP_WIDTH), lambda b, i: (b, i, 0)), st_spec),
        out_shape=(jax.ShapeDtypeStruct((n_grp, t_len * SEQ_GROUP, GROUP_WIDTH), F32),
                   jax.ShapeDtypeStruct((n_grp, SEQ_GROUP, 2 * S5_CH), F32)),
        scratch_shapes=[pltpu.VMEM((SEQ_GROUP, 2 * S5_CH), F32), pltpu.VMEM((rows, 2 * S5_CH), F32),
                        pltpu.VMEM((rows, 2 * S5_CH), F32)],
        compiler_params=_cparams(2),
        name="s5_t%d" % t_len,
    )(u_tm, x0, ab, wb, wc, dvec, wglu)


CHUNKS = D_MODEL // LANES


def _store_chunked(ref, x):
    rows = x.shape[0]
    for j in range(CHUNKS):
        ref[pl.ds(j, rows, stride=CHUNKS), :] = x[:, j * LANES:(j + 1) * LANES]


def _load_chunked(ref, rows):
    return jnp.concatenate([ref[pl.ds(j, rows, stride=CHUNKS), :] for j in range(CHUNKS)], axis=1)


def _post_mix_kernel(x_ref, oap_ref, obp_ref, ocp_ref, odp_ref, oas_ref, obs_ref, ocs_ref, ods_ref,
                     wout_ref, lnw_ref, lnb_ref, rw_ref, rb_ref,
                     x1_ref, x1c_ref, idx_ref, rank_ref, gate_ref, cnt_ref, carry_s, *, prompt_tiles):
    i = pl.program_id(0)

    @pl.when(i == 0)
    def _():
        carry_s[...] = jnp.zeros_like(carry_s)

    is_prompt = i < prompt_tiles
    pick = lambda p_ref, s_ref: jnp.where(is_prompt, p_ref[...], s_ref[...]).astype(BF16)
    mix = _dot(pick(oap_ref, oas_ref), wout_ref[0:256, :])
    mix += _dot(pick(obp_ref, obs_ref), wout_ref[256:512, :])
    mix += _dot(pick(ocp_ref, ocs_ref), wout_ref[512:768, :])
    mix += _dot(pick(odp_ref, ods_ref), wout_ref[768:1024, :])
    x1 = _layer_norm(DN_ALPHA * x_ref[...] + mix, lnw_ref[...], lnb_ref[...])
    x1_ref[...] = x1
    _store_chunked(x1c_ref, x1)

    xh, xl = _split2(x1)
    logits = _dot(xh, rw_ref[0]) + (_dot(xh, rw_ref[1]) + _dot(xl, rw_ref[0])) + rb_ref[...]
    rows = logits.shape[0]
    lane = lax.broadcasted_iota(I32, (rows, LANES), 1)
    work = logits
    sel_i, sel_v = [], []
    for _ in range(TOP_K):
        m = jnp.max(work, axis=-1, keepdims=True)
        j = jnp.min(jnp.where(work == m, lane, LANES), axis=-1, keepdims=True)
        sel_i.append(j)
        sel_v.append(m)
        work = jnp.where(lane == j, -jnp.inf, work)
    e = [jnp.exp(v - sel_v[0]) for v in sel_v]
    den = (e[0] + e[1]) + (e[2] + e[3])
    onehot = jnp.zeros((rows, LANES), F32)
    for j in sel_i:
        onehot = onehot + (lane == j).astype(F32)
    rr = lax.broadcasted_iota(I32, (rows, rows), 0)
    cc = lax.broadcasted_iota(I32, (rows, rows), 1)
    before = _dot((cc < rr).astype(BF16), onehot.astype(BF16)) + carry_s[0:1, :]
    carry_s[0:1, :] = carry_s[0:1, :] + jnp.sum(onehot, axis=0, keepdims=True)
    idx_o = jnp.zeros((rows, LANES), I32)
    rank_o = jnp.zeros((rows, LANES), I32)
    gate_o = jnp.zeros((rows, LANES), F32)
    for slot in range(TOP_K):
        j = sel_i[slot]
        rank = jnp.sum(jnp.where(lane == j, before, 0.0), axis=-1, keepdims=True)
        idx_o = jnp.where(lane == slot, j, idx_o)
        rank_o = jnp.where(lane == slot, rank.astype(I32), rank_o)
        gate_o = jnp.where(lane == slot, e[slot] / den, gate_o)
    idx_ref[...] = idx_o
    rank_ref[...] = rank_o
    gate_ref[...] = gate_o
    cnt_ref[...] = jnp.broadcast_to(carry_s[0:1, :], cnt_ref.shape)


def _post_mix(x_all, outs_p, outs_s, n_seq_p, t_len_p, wout_l, lnw, lnb, rw_l, rb_l):
    n = x_all.shape[0]
    n_t = t_len_p // ROW_TILE
    prompt_tiles = n_seq_p * n_t
    row = lambda w: pl.BlockSpec((ROW_TILE, w), lambda i: (i, 0))
    w = GROUP_WIDTH
    p_idx = lambda i: jnp.minimum(i, prompt_tiles - 1)
    tm_p = pl.BlockSpec((ROW_TILE, w), lambda i: (p_idx(i) % n_t, p_idx(i) // n_t))
    bm_p = pl.BlockSpec((ROW_TILE, w), lambda i: (p_idx(i), 0))
    bm_s = pl.BlockSpec((ROW_TILE, w), lambda i: (jnp.maximum(i - prompt_tiles, 0), 0))

    def full(a):
        return pl.BlockSpec(a.shape, lambda i: (0,) * a.ndim)

    meta = jax.ShapeDtypeStruct((n, LANES), I32)
    return pl.pallas_call(
        functools.partial(_post_mix_kernel, prompt_tiles=prompt_tiles),
        grid=(n // ROW_TILE,),
        in_specs=[row(D_MODEL), tm_p, bm_p, bm_p, tm_p, bm_s, bm_s, bm_s, bm_s, full(wout_l), full(lnw), full(lnb),
                  full(rw_l), full(rb_l)],
        out_specs=(row(D_MODEL), pl.BlockSpec((ROW_TILE * CHUNKS, LANES), lambda i: (i, 0)),
                   row(LANES), row(LANES), row(LANES), pl.BlockSpec((SUBLANES, LANES), lambda i: (0, 0))),
        out_shape=(jax.ShapeDtypeStruct((n, D_MODEL), F32), jax.ShapeDtypeStruct((n * CHUNKS, LANES), F32),
                   meta, meta, jax.ShapeDtypeStruct((n, LANES), F32),
                   jax.ShapeDtypeStruct((SUBLANES, LANES), F32)),
        scratch_shapes=[pltpu.VMEM((SUBLANES, LANES), F32)],
        compiler_params=_cparams(1),
        name="post_mix",
    )(x_all, *outs_p, *outs_s, wout_l, lnw, lnb, rw_l, rb_l)


def _dispatch_kernel(gend_ref, dest_ref, x_ref, xs_hbm, zero_s, sem, *, tokens, n_tiles):
    i = pl.program_id(0)

    def zero_tile(first_row):
        start = pl.multiple_of(first_row * CHUNKS, MOE_TILE * CHUNKS)
        return pltpu.make_async_copy(zero_s, xs_hbm.at[pl.ds(start, MOE_TILE * CHUNKS)], sem)

    @pl.when(i == 0)
    def _():
        zero_s[...] = jnp.zeros_like(zero_s)
        for e in range(N_EXPERTS):
            @pl.when(gend_ref[e + 1] > gend_ref[e])
            def _():
                zero_tile(gend_ref[e + 1] - MOE_TILE).start()
        for e in range(N_EXPERTS):
            @pl.when(gend_ref[e + 1] > gend_ref[e])
            def _():
                zero_tile(gend_ref[e + 1] - MOE_TILE).wait()

        def tail(t, carry):
            cp = zero_tile(t * MOE_TILE)
            cp.start()
            cp.wait()
            return carry

        lax.fori_loop(gend_ref[N_EXPERTS] // MOE_TILE, n_tiles, tail, 0)

    def row_copy(n, slot):
        src = pl.multiple_of(n * CHUNKS, CHUNKS)
        dst = pl.multiple_of(dest_ref[n * TOP_K + slot] * CHUNKS, CHUNKS)
        return pltpu.make_async_copy(x_ref.at[pl.ds(src, CHUNKS)], xs_hbm.at[pl.ds(dst, CHUNKS)], sem)

    def issue(n, carry):
        for slot in range(TOP_K):
            row_copy(n, slot).start(priority=slot % 2)
        return carry

    lax.fori_loop(0, tokens, issue, 0, unroll=4)
    for slot in range(TOP_K):
        pltpu.make_async_copy(x_ref, xs_hbm.at[pl.ds(0, tokens * CHUNKS)], sem).wait()


def _dispatch(x1c, dest_flat, gend, n_rows):
    n = x1c.shape[0] // CHUNKS
    tokens = DISPATCH_TOKENS if n % DISPATCH_TOKENS == 0 else ROW_TILE
    return pl.pallas_call(
        functools.partial(_dispatch_kernel, tokens=tokens, n_tiles=n_rows // MOE_TILE),
        grid_spec=pltpu.PrefetchScalarGridSpec(
            num_scalar_prefetch=1,
            grid=(n // tokens,),
            in_specs=[pl.BlockSpec((tokens * TOP_K,), lambda i, ge: (i,), memory_space=pltpu.SMEM),
                      pl.BlockSpec((tokens * CHUNKS, LANES), lambda i, ge: (i, 0))],
            out_specs=pl.BlockSpec(memory_space=pl.ANY),
            scratch_shapes=[pltpu.VMEM((MOE_TILE * CHUNKS, LANES), F32), pltpu.SemaphoreType.DMA(())],
        ),
        out_shape=jax.ShapeDtypeStruct((n_rows * CHUNKS, LANES), F32),
        compiler_params=_cparams(1),
        name="moe_dispatch",
    )(gend, dest_flat, x1c)


PAIR_BLOCK = 2 * LANES


def _expert_prep_kernel(w1_ref, w2_ref, w1p_ref, w2b_ref):
    src = lax.broadcasted_iota(I32, (PAIR_BLOCK, PAIR_BLOCK), 0)
    dst = lax.broadcasted_iota(I32, (PAIR_BLOCK, PAIR_BLOCK), 1)
    perm = (src == jnp.where(dst < LANES, 2 * dst, 2 * (dst - LANES) + 1)).astype(BF16)
    for c in range(2 * D_FF // PAIR_BLOCK):
        cols = slice(c * PAIR_BLOCK, (c + 1) * PAIR_BLOCK)
        w1p_ref[:, cols] = _dot(w1_ref[:, cols].astype(BF16), perm).astype(BF16)
    w2b_ref[...] = w2_ref[...].astype(BF16)


def _expert_prep(exp_w1, exp_w2):
    n_l, n_e = exp_w1.shape[:2]
    spec = lambda r, c: pl.BlockSpec((None, None, r, c), lambda i: (i // n_e, i % n_e, 0, 0))
    return pl.pallas_call(
        _expert_prep_kernel,
        grid=(n_l * n_e,),
        in_specs=[spec(D_MODEL, 2 * D_FF), spec(D_FF, D_MODEL)],
        out_specs=(spec(D_MODEL, 2 * D_FF), spec(D_FF, D_MODEL)),
        out_shape=(jax.ShapeDtypeStruct(exp_w1.shape, BF16), jax.ShapeDtypeStruct(exp_w2.shape, BF16)),
        compiler_params=_cparams(1),
        name="expert_prep",
    )(exp_w1, exp_w2)


def _expert_kernel(te_ref, nreal_ref, xs_ref, w1_ref, b1_ref, w2_ref, b2_ref, o_ref, acc_s):
    i = pl.program_id(0)

    @pl.when(i < nreal_ref[0])
    def _():
        x = _load_chunked(xs_ref, MOE_TILE).astype(BF16)
        acc_s[...] = jnp.broadcast_to(b2_ref[...], acc_s.shape)
        for c in range(D_FF // PAIR_BLOCK):
            cols = slice(2 * c * PAIR_BLOCK, 2 * (c + 1) * PAIR_BLOCK)
            h = _dot(x, w1_ref[:, cols]) + b1_ref[:, cols]
            h_glu = jnp.minimum(jnp.concatenate([h[:, 0:128], h[:, 256:384]], axis=1), SWIGLU_LIMIT)
            h_lin = jnp.clip(jnp.concatenate([h[:, 128:256], h[:, 384:512]], axis=1),
                             -SWIGLU_LIMIT, SWIGLU_LIMIT)
            act = h_glu * _sigmoid(SWIGLU_ALPHA * h_glu) * (h_lin + 1.0)
            acc_s[...] += _dot(act.astype(BF16), w2_ref[c * PAIR_BLOCK:(c + 1) * PAIR_BLOCK, :])
        _store_chunked(o_ref, acc_s[...])

    @pl.when(i >= nreal_ref[0])
    def _():
        o_ref[...] = jnp.zeros_like(o_ref)


def _experts(xs, te, nreal, w1d, b1d, w2b, b2, layer):
    n_tiles = xs.shape[0] // (MOE_TILE * CHUNKS)
    clamp = lambda i, nr: jnp.minimum(i, nr[0] - 1)
    return pl.pallas_call(
        _expert_kernel,
        grid_spec=pltpu.PrefetchScalarGridSpec(
            num_scalar_prefetch=2,
            grid=(n_tiles,),
            in_specs=[pl.BlockSpec((MOE_TILE * CHUNKS, LANES), lambda i, te, nr: (clamp(i, nr), 0)),
                      pl.BlockSpec((None, None, D_MODEL, 2 * D_FF), lambda i, te, nr: (layer, te[i], 0, 0)),
                      pl.BlockSpec((None, None, 1, 2 * D_FF), lambda i, te, nr: (layer, te[i], 0, 0)),
                      pl.BlockSpec((None, None, D_FF, D_MODEL), lambda i, te, nr: (layer, te[i], 0, 0)),
                      pl.BlockSpec((None, None, 1, D_MODEL), lambda i, te, nr: (layer, te[i], 0, 0))],
            out_specs=pl.BlockSpec((MOE_TILE * CHUNKS, LANES), lambda i, te, nr: (i, 0)),
            scratch_shapes=[pltpu.VMEM((MOE_TILE, D_MODEL), F32)],
        ),
        out_shape=jax.ShapeDtypeStruct(xs.shape, F32),
        compiler_params=_cparams(1),
        name="moe_experts",
    )(te, nreal, xs, w1d, b1d, w2b, b2)


def _combine_kernel(dest_ref, dest_next_ref, gate_ref, x1_ref, lnw_ref, lnb_ref, ys_hbm, o_ref, buf_s, sem,
                    *, tokens):
    i = pl.program_id(0)
    cur = i % 2

    def fetch(dref, half):
        def issue(n, carry):
            dst = pl.multiple_of(n * CHUNKS, CHUNKS)
            for slot in range(TOP_K):
                src = pl.multiple_of(dref[n * TOP_K + slot] * CHUNKS, CHUNKS)
                pltpu.make_async_copy(ys_hbm.at[pl.ds(src, CHUNKS)], buf_s.at[half, slot, pl.ds(dst, CHUNKS)],
                                      sem.at[half]).start(priority=slot % 2)
            return carry

        lax.fori_loop(0, tokens, issue, 0, unroll=4)

    @pl.when(i == 0)
    def _():
        fetch(dest_ref, 0)

    @pl.when(i + 1 < pl.num_programs(0))
    def _():
        fetch(dest_next_ref, 1 - cur)

    for slot in range(TOP_K):
        pltpu.make_async_copy(ys_hbm.at[pl.ds(0, tokens * CHUNKS)], buf_s.at[cur, slot], sem.at[cur]).wait()
    gate = gate_ref[...]
    ffn = gate[:, 0:1] * _load_chunked(buf_s.at[cur, 0], tokens)
    for slot in range(1, TOP_K):
        ffn = ffn + gate[:, slot:slot + 1] * _load_chunked(buf_s.at[cur, slot], tokens)
    o_ref[...] = _layer_norm(DN_ALPHA * x1_ref[...] + ffn, lnw_ref[...], lnb_ref[...])


def _combine(dest_flat, gate, x1, lnw, lnb, ys):
    n = x1.shape[0]
    tokens = ROW_TILE
    last = n // tokens - 1
    return pl.pallas_call(
        functools.partial(_combine_kernel, tokens=tokens),
        grid=(n // tokens,),
        in_specs=[pl.BlockSpec((tokens * TOP_K,), lambda i: (i,), memory_space=pltpu.SMEM),
                  pl.BlockSpec((tokens * TOP_K,), lambda i: (jnp.minimum(i + 1, last),), memory_space=pltpu.SMEM),
                  pl.BlockSpec((tokens, LANES), lambda i: (i, 0)),
                  pl.BlockSpec((tokens, D_MODEL), lambda i: (i, 0)),
                  pl.BlockSpec((1, D_MODEL), lambda i: (0, 0)),
                  pl.BlockSpec((1, D_MODEL), lambda i: (0, 0)),
                  pl.BlockSpec(memory_space=pl.ANY)],
        out_specs=pl.BlockSpec((tokens, D_MODEL), lambda i: (i, 0)),
        out_shape=jax.ShapeDtypeStruct((n, D_MODEL), F32),
        scratch_shapes=[pltpu.VMEM((2, TOP_K, tokens * CHUNKS, LANES), F32), pltpu.SemaphoreType.DMA((2,))],
        compiler_params=_cparams(1),
        name="moe_combine",
    )(dest_flat, dest_flat, gate, x1, lnw, lnb, ys)


def _moe(x1, x1c, idx, rank, gate, counts, w1d, b1d, w2b, b2, lnw, lnb, layer):
    n = x1.shape[0]
    n_tiles = -(-(n * TOP_K + N_EXPERTS * (MOE_TILE - 1)) // MOE_TILE)
    cnt = counts[0, :N_EXPERTS].astype(I32)
    gsz = ((cnt + (MOE_TILE - 1)) // MOE_TILE) * MOE_TILE
    gend = jnp.cumsum(gsz)
    goff = gend - gsz
    dest = (goff[idx[:, :TOP_K]] + rank[:, :TOP_K]).reshape(-1)
    gend0 = jnp.concatenate([jnp.zeros((1,), I32), gend])
    nreal = (gend[-1:] // MOE_TILE).astype(I32)
    tile_start = jnp.arange(n_tiles, dtype=I32) * MOE_TILE
    te = jnp.minimum(jnp.sum((gend[None, :] <= tile_start[:, None]).astype(I32), axis=1), N_EXPERTS - 1)
    xs = _dispatch(x1c, dest, gend0, n_tiles * MOE_TILE)
    ys = _experts(xs, te, nreal, w1d, b1d, w2b, b2, layer)
    return _combine(dest, gate, x1, lnw, lnb, ys)


def _to_time_major(rows, n_seq, t_len):
    c = rows.shape[-1]
    x = rows.reshape(n_seq // SEQ_GROUP, SEQ_GROUP, t_len, c)
    return jnp.transpose(x, (0, 2, 1, 3)).reshape(n_seq // SEQ_GROUP, t_len * SEQ_GROUP, c)


def _from_time_major(x, n_seq, t_len):
    c = x.shape[-1]
    x = x.reshape(n_seq // SEQ_GROUP, t_len, SEQ_GROUP, c)
    return jnp.transpose(x, (0, 2, 1, 3)).reshape(n_seq * t_len, c)


def _block_diag_state(s, dk):
    st = jnp.swapaxes(s, 2, 3)
    eye = jnp.eye(N_HEADS, dtype=s.dtype)
    return jnp.einsum("bhvk,hg->bhvgk", st, eye).reshape(s.shape[0], GROUP_WIDTH, N_HEADS * dk)


def _unblock_state(st, dk):
    b = st.shape[0]
    x = st.reshape(b, N_HEADS, HEAD_DIM, N_HEADS, dk)
    x = jnp.stack([x[:, h, :, h, :] for h in range(N_HEADS)], axis=1)
    return jnp.swapaxes(x, 2, 3)


def _rwkv_state_in(s):
    b = s.shape[0]
    x = s.reshape(b // SEQ_GROUP, SEQ_GROUP, N_HEADS, HEAD_DIM, HEAD_DIM)
    return jnp.transpose(x, (0, 3, 1, 2, 4)).reshape(b // SEQ_GROUP, HEAD_DIM, SEQ_GROUP, GROUP_WIDTH)


def _rwkv_state_out(x, b):
    x = x.reshape(b // SEQ_GROUP, HEAD_DIM, SEQ_GROUP, N_HEADS, HEAD_DIM)
    return jnp.transpose(x, (0, 2, 3, 1, 4)).reshape(b, N_HEADS, HEAD_DIM, HEAD_DIM)


def _pad_rows(w, row0, n_rows):
    out = jnp.zeros((n_rows, w.shape[1]), w.dtype)
    return out.at[row0:row0 + w.shape[0]].set(w)


def kernel(x_prompt, x_sample, state_rwkv, state_rwkv_shift, state_hgrn, state_gla, state_s5_re, state_s5_im,
           w_in, rw_mu, rw_w0, rw_w2, rw_a0, rw_a2, rw_g2, rw_kk, rw_ka, rw_rk, rw_lnx_w, rw_lnx_b,
           hg_lb_logits, hg_norm_w, gla_w_gk2, gla_b_gk, gla_norm_w,
           s5_A_re, s5_A_im, s5_log_dt, s5_B_re, s5_B_im, s5_C_re, s5_C_im, s5_D, s5_w_glu,
           w_out, ln1_w, ln1_b, router_w, router_b, exp_w1, exp_b1, exp_w2, exp_b2, ln2_w, ln2_b):
    bp, tp, _ = x_prompt.shape
    bs, ts, _ = x_sample.shape
    n_p, n_s = bp * tp, bs * ts
    groups = ((0, bp, tp), (n_p, bs, ts))
    assert bp == SEQ_GROUP and tp % ROW_TILE == 0 and bs % SEQ_GROUP == 0 and n_s % ROW_TILE == 0
    assert ts <= SUB_CHUNK

    c = np.cumsum([0, RW_COLS, 1024, 784, 256])
    rw_c, hg_c, gl_c, s5_c = (w_in[:, :, c[j]:c[j + 1]] for j in range(4))
    gl_q, gl_k, gl_v, gl_lo, gl_g = (gl_c[:, :, a:b] for a, b in
                                     ((0, 128), (128, 256), (256, 512), (512, 528), (528, 784)))
    zpad = jnp.zeros((DEPTH, D_MODEL, 128 - GLA_GK_LORA), w_in.dtype)
    w_in_p = jnp.concatenate([hg_c, gl_v, gl_g, gl_q, gl_k, gl_lo, zpad, rw_c, s5_c], axis=2).astype(BF16)
    w_out_b = w_out.astype(BF16)
    w1d, w2b = _expert_prep(exp_w1, exp_w2)
    b1d = jnp.swapaxes(exp_b1.reshape(DEPTH, N_EXPERTS, 2 * D_FF // PAIR_BLOCK, LANES, 2), -1, -2)
    b1d = b1d.reshape(DEPTH, N_EXPERTS, 1, 2 * D_FF)
    b2r = exp_b2[:, :, None, :]
    rw_pad = jnp.pad(router_w, ((0, 0), (0, 0), (0, LANES - N_EXPERTS)))
    rw_hi = rw_pad.astype(BF16)
    rw_lo = (rw_pad - rw_hi.astype(F32)).astype(BF16)
    rw_split = jnp.stack([rw_hi, rw_lo], axis=1)
    rb_pad = jnp.pad(router_b, ((0, 0), (0, LANES - N_EXPERTS)), constant_values=-1e30)[:, None, :]

    lbs = jnp.cumsum(jax.nn.softmax(hg_lb_logits.astype(F32), axis=0), axis=0)
    lbs = lbs - lbs[:1]
    lb3 = jnp.stack([lbs, jnp.log(lbs), jnp.log1p(-lbs)], axis=1)
    lb3 = jnp.pad(lb3, ((0, 0), (0, SUBLANES - 3), (0, 0)))

    ab_re, ab_im, bb_re, bb_im = _s5_prep(s5_A_re, s5_A_im, s5_log_dt, s5_B_re, s5_B_im)
    eye_g = jnp.eye(S5_NGROUPS, dtype=F32)
    wb = jnp.stack([jnp.einsum("lgph,gk->lghkp", t, eye_g).reshape(DEPTH, GROUP_WIDTH, S5_CH)
                    for t in (bb_re, bb_im)], axis=2).reshape(DEPTH, GROUP_WIDTH, 2 * S5_CH).astype(BF16)
    wc = jnp.concatenate([jnp.einsum("lghp,gk->lgpkh", t, eye_g).reshape(DEPTH, S5_CH, GROUP_WIDTH)
                          for t in (s5_C_re, -s5_C_im)], axis=1).astype(BF16)
    ab = jnp.concatenate([ab_re, ab_im], axis=1)
    ab = jnp.pad(ab, ((0, 0), (0, SUBLANES - 2), (0, 0)))
    wglu_b = s5_w_glu.astype(BF16)

    lora = jnp.stack([jnp.stack([_pad_rows(rw_w2[l], 0, 128), _pad_rows(rw_a2[l], 32, 128),
                                 _pad_rows(rw_g2[l], 64, 128)]) for l in range(DEPTH)])
    rw_vec = jnp.stack([rw_w0, rw_a0, rw_kk, rw_ka, rw_rk, rw_lnx_w, rw_lnx_b, jnp.zeros_like(rw_w0)], axis=1)
    wgk = jnp.stack([_pad_rows(gla_w_gk2[l], 0, 128) for l in range(DEPTH)])

    zeros = lambda shape: jnp.zeros(shape, F32)
    st_in = (
        dict(rw=zeros((DEPTH, bp, N_HEADS, HEAD_DIM, HEAD_DIM)), sh=zeros((DEPTH, bp, RW_COLS)),
             hg=zeros((DEPTH, bp, N_HEADS, HG_DK, HEAD_DIM)), gl=zeros((DEPTH, bp, N_HEADS, GLA_DK, HEAD_DIM)),
             re=zeros((DEPTH, bp, S5_NGROUPS, S5_STATE)), im=zeros((DEPTH, bp, S5_NGROUPS, S5_STATE))),
        dict(rw=state_rwkv, sh=state_rwkv_shift, hg=state_hgrn, gl=state_gla, re=state_s5_re, im=state_s5_im),
    )
    collected = ([], [])

    x_all = jnp.concatenate([x_prompt.reshape(n_p, D_MODEL), x_sample.reshape(n_s, D_MODEL)], axis=0)
    for l in range(DEPTH):
        h_gate_p, p_tm_p, u_tm_p = _inproj_prompt(x_all, w_in_p[l], bp, tp)
        h_s = _inproj_rows(x_all, w_in_p[l], n_p, n_s)
        outs = ([], [])
        for gi, (row0, n_seq, t_len) in enumerate(groups):
            st = st_in[gi]
            n_grp = n_seq // SEQ_GROUP
            if gi == 0:
                h_gate = h_gate_p
                p_tm = p_tm_p.reshape(1, t_len * n_seq, RW_COLS)
                u_tm = u_tm_p.reshape(1, t_len * n_seq, GROUP_WIDTH)
                new_sh = p_tm_p[t_len - 1].reshape(n_seq, RW_COLS)
                to_rows = lambda x_tm: x_tm.reshape(t_len, n_seq * GROUP_WIDTH)
            else:
                h_gate = h_s
                p_rw = h_s[:, OFF_RW:OFF_RW + RW_COLS]
                p_tm = _to_time_major(p_rw, n_seq, t_len)
                u_tm = _to_time_major(h_s[:, OFF_S5:OFF_S5 + GROUP_WIDTH], n_seq, t_len)
                new_sh = p_rw.reshape(n_seq, t_len, RW_COLS)[:, -1]
                to_rows = lambda x_tm: _from_time_major(x_tm, n_seq, t_len)
            sh0 = st["sh"][l].reshape(n_grp, SEQ_GROUP, RW_COLS)
            oa_tm, s_rw = _rwkv_call(p_tm, sh0, _rwkv_state_in(st["rw"][l]), rw_mu[l][None, :], lora[l],
                                     rw_vec[l], n_grp, t_len)
            new_rw = _rwkv_state_out(s_rw, n_seq)
            ob, s_hg = _gated_call("hgrn", h_gate, 0, n_seq, t_len, _block_diag_state(st["hg"][l], HG_DK),
                                   (lb3[l], hg_norm_w[l][None, :]))
            oc, s_gl = _gated_call("gla", h_gate, 0, n_seq, t_len, _block_diag_state(st["gl"][l], GLA_DK),
                                   (wgk[l], gla_b_gk[l][None, :], gla_norm_w[l][None, :]))
            x0 = jnp.concatenate([st["re"][l].reshape(n_grp, SEQ_GROUP, S5_CH),
                                  st["im"][l].reshape(n_grp, SEQ_GROUP, S5_CH)], axis=-1)
            od_tm, x_t = _s5_call(u_tm, x0, ab[l], wb[l], wc[l], s5_D[l][None, :], wglu_b[l], n_grp, t_len)
            outs[gi].extend([to_rows(oa_tm), ob, oc, to_rows(od_tm)])
            x_t = x_t.reshape(n_seq, 2, S5_NGROUPS, S5_STATE)
            collected[gi].append((new_rw, new_sh, _unblock_state(s_hg, HG_DK), _unblock_state(s_gl, GLA_DK),
                                  x_t[:, 0], x_t[:, 1]))
        x1, x1c, idx, rank, gate, counts = _post_mix(x_all, outs[0], outs[1], bp, tp, w_out_b[l],
                                                     ln1_w[l][None, :], ln1_b[l][None, :],
                                                     rw_split[l], rb_pad[l])
        x_all = _moe(x1, x1c, idx, rank, gate, counts, w1d, b1d, w2b, b2r, ln2_w[l][None, :], ln2_b[l][None, :], l)

    y_prompt = x_all[:n_p].reshape(bp, tp, D_MODEL)
    y_sample = x_all[n_p:].reshape(bs, ts, D_MODEL)
    ps = [jnp.stack([layer[j] for layer in collected[0]]) for j in range(6)]
    ss = [jnp.stack([layer[j] for layer in collected[1]]) for j in range(6)]
    return (y_prompt, y_sample, *ps, *ss)
```

```python
import functools

import jax
import jax.numpy as jnp
import numpy as np
from jax import lax
from jax.experimental import pallas as pl
from jax.experimental.pallas import tpu as pltpu

F32 = jnp.float32
BF16 = jnp.bfloat16
I32 = jnp.int32

D_MODEL = 1024
DEPTH = 4
GROUP_WIDTH = 256
HEAD_DIM = 64
N_HEADS = 4
RW_COLS = 896
RW_GN_EPS = 64e-5
HG_DK = 64
GLA_DK = 32
GLA_GK_LORA = 16
GLA_GATE_NORM = 16.0
S5_NGROUPS = 16
S5_GROUP = 16
S5_STATE = 64
S5_CH = S5_NGROUPS * S5_STATE
N_EXPERTS = 32
TOP_K = 4
D_FF = 1024
SWIGLU_ALPHA = 1.702
SWIGLU_LIMIT = 7.0
DN_ALPHA = (2.0 * DEPTH) ** 0.25
LN_EPS = 1e-5

SUBLANES = 8
LANES = 128
VMEM_LIMIT_BYTES = 48 * 1024 * 1024

OFF_HG = 0
OFF_GLA_VG = 1024
OFF_GLA_QK = 1536
OFF_RW = 1920
OFF_S5 = 2816
IN_PAD = 3072

ROW_TILE = 256
SUB_CHUNK = 16
SEQ_GROUP = SUBLANES
MOE_TILE = 512
DISPATCH_TOKENS = 512


def _cparams(n_axes):
    return pltpu.CompilerParams(dimension_semantics=("arbitrary",) * n_axes,
                                vmem_limit_bytes=VMEM_LIMIT_BYTES)


def _dot(a, b):
    return jnp.dot(a, b, preferred_element_type=F32)


def _split2(x):
    hi = x.astype(BF16)
    lo = (x - hi.astype(F32)).astype(BF16)
    return hi, lo


def _split3(x):
    hi = x.astype(BF16)
    r = x - hi.astype(F32)
    mid = r.astype(BF16)
    lo = (r - mid.astype(F32)).astype(BF16)
    return hi, mid, lo


def _dot3(a, b):
    ah, al = _split2(a)
    bh, bl = _split2(b)
    return _dot(ah, bh) + (_dot(ah, bl) + _dot(al, bh))


def _seg_ones(n_in, seg_in, n_out, seg_out):
    r = lax.broadcasted_iota(I32, (n_in, n_out), 0) // seg_in
    c = lax.broadcasted_iota(I32, (n_in, n_out), 1) // seg_out
    return (r == c).astype(BF16)


def _segsum(x, seg):
    rows = x.shape[0]
    hi, lo = _split2(x)
    both = _dot(jnp.concatenate([hi, lo], axis=0), seg)
    return both[:rows] + both[rows:]


def _sigmoid(x):
    return 1.0 / (1.0 + jnp.exp(-x))


def _log_sigmoid(x):
    return jnp.minimum(x, 0.0) - jnp.log1p(jnp.exp(-jnp.abs(x)))


def _softplus(x):
    return jnp.maximum(x, 0.0) + jnp.log1p(jnp.exp(-jnp.abs(x)))


def _layer_norm(x, w, b):
    xc = x - jnp.mean(x, axis=-1, keepdims=True)
    var = jnp.mean(xc * xc, axis=-1, keepdims=True)
    return xc * lax.rsqrt(var + LN_EPS) * w + b


def _inproj_kernel(x_ref, w_ref, *o_refs):
    h = _dot(x_ref[...].astype(BF16), w_ref[...])
    col = 0
    for o_ref in o_refs:
        o_ref[...] = h[:, col:col + o_ref.shape[-1]]
        col += o_ref.shape[-1]


def _slab_store_rows(ref, seq, x):
    for j in range(ref.shape[0]):
        ref.at[j][pl.ds(seq, x.shape[0], stride=SEQ_GROUP), :] = x[:, j * LANES:(j + 1) * LANES]


def _slab_load_rows(ref, seq, tokens):
    return jnp.concatenate([ref.at[j][pl.ds(seq, tokens, stride=SEQ_GROUP), :] for j in range(ref.shape[0])],
                           axis=1)


def _slab_load(ref):
    return jnp.concatenate([ref[j] for j in range(ref.shape[0])], axis=1)


def _slab_store(ref, x):
    for j in range(ref.shape[0]):
        ref[j] = x[:, j * LANES:(j + 1) * LANES]


def _to_slabs(x):
    s = x.reshape(x.shape[:-1] + (x.shape[-1] // LANES, LANES))
    return jnp.swapaxes(s, -2, -3)


def _from_slabs(x):
    s = jnp.swapaxes(x, -2, -3)
    return s.reshape(s.shape[:-2] + (s.shape[-2] * LANES,))


def _inproj_prompt_kernel(x_ref, w_ref, g_ref, p_ref, u_ref):
    seq = pl.program_id(1)
    h = _dot(x_ref[...].astype(BF16), w_ref[...])
    g_ref[...] = h[:, 0:OFF_RW]
    _slab_store_rows(p_ref, seq, h[:, OFF_RW:OFF_RW + RW_COLS])
    _slab_store_rows(u_ref, seq, h[:, OFF_S5:OFF_S5 + GROUP_WIDTH])


def _inproj_prompt(x_all, w_in_l, n_seq, t_len):
    n_t = t_len // ROW_TILE
    slab = lambda c: pl.BlockSpec((c // LANES, ROW_TILE * SEQ_GROUP, LANES), lambda i, b: (0, i, 0))
    return pl.pallas_call(
        _inproj_prompt_kernel,
        grid=(n_t, n_seq),
        in_specs=[pl.BlockSpec((ROW_TILE, D_MODEL), lambda i, b: (b * n_t + i, 0)),
                  pl.BlockSpec((D_MODEL, IN_PAD), lambda i, b: (0, 0))],
        out_specs=(pl.BlockSpec((ROW_TILE, OFF_RW), lambda i, b: (b * n_t + i, 0)),
                   slab(RW_COLS), slab(GROUP_WIDTH)),
        out_shape=(jax.ShapeDtypeStruct((n_seq * t_len, OFF_RW), F32),
                   jax.ShapeDtypeStruct((RW_COLS // LANES, t_len * n_seq, LANES), F32),
                   jax.ShapeDtypeStruct((GROUP_WIDTH // LANES, t_len * n_seq, LANES), F32)),
        compiler_params=_cparams(2),
        name="inproj_prompt",
    )(x_all, w_in_l)


def _inproj_rows(x_all, w_in_l, row0, n_rows):
    blk0 = row0 // ROW_TILE
    return pl.pallas_call(
        _inproj_kernel,
        grid=(n_rows // ROW_TILE,),
        in_specs=[pl.BlockSpec((ROW_TILE, D_MODEL), lambda i: (blk0 + i, 0)),
                  pl.BlockSpec((D_MODEL, IN_PAD), lambda i: (0, 0))],
        out_specs=pl.BlockSpec((ROW_TILE, IN_PAD), lambda i: (i, 0)),
        out_shape=jax.ShapeDtypeStruct((n_rows, IN_PAD), F32),
        compiler_params=_cparams(1),
        name="inproj_rows",
    )(x_all, w_in_l)


def _gated_tile(q, k, v, g, st_ref, o_ref, q_s, k_s, v_s, b_s, qh_s, kh_s, dt_s, *, dk, n_blk):
    c = SUB_CHUNK
    rows, hk = q.shape
    rr = lax.broadcasted_iota(I32, (rows, rows), 0)
    cc = lax.broadcasted_iota(I32, (rows, rows), 1)
    same = (rr // c) == (cc // c)
    tri = jnp.concatenate([(same & (cc <= rr)).astype(BF16), same.astype(BF16)], axis=0)
    g3 = jnp.concatenate(_split3(g), axis=1)
    p = _dot(tri, g3)
    b = p[:rows, :hk] + p[:rows, hk:2 * hk] + p[:rows, 2 * hk:]
    btot = p[rows:, :hk] + p[rows:, hk:2 * hk] + p[rows:, 2 * hk:]
    q_s[...] = q
    k_s[...] = k
    v_s[...] = v
    b_s[...] = b
    qh_s[...] = q * jnp.exp(b)
    kh_s[...] = k * jnp.exp(btot - b)
    dt_s[...] = jnp.exp(btot)
    seg = _seg_ones(hk, dk, GROUP_WIDTH, HEAD_DIM)
    bd_mask = (lax.broadcasted_iota(I32, (GROUP_WIDTH, hk), 0) // HEAD_DIM
               == lax.broadcasted_iota(I32, (GROUP_WIDTH, hk), 1) // dk).astype(F32)
    t_iota = lax.broadcasted_iota(I32, (c, hk), 0)

    def block(i, carry):
        r0 = pl.multiple_of(i * c, c)
        qb = q_s[pl.ds(r0, c), :]
        kb = k_s[pl.ds(r0, c), :]
        vb = v_s[pl.ds(r0, c), :]
        bb = b_s[pl.ds(r0, c), :]
        pieces = []
        for s in range(c):
            d = jnp.where(t_iota >= s, bb - bb[s:s + 1, :], -jnp.inf)
            pieces.append(jnp.exp(d) * qb * kb[s:s + 1, :])
        att = _segsum(jnp.concatenate(pieces, axis=0), seg)
        o = att[0:c, :] * vb[0:1, :]
        for s in range(1, c):
            o = o + att[s * c:(s + 1) * c, :] * vb[s:s + 1, :]
        si = i if st_ref.shape[0] > 1 else 0
        st = st_ref[si]
        o = o + lax.dot_general(qh_s[pl.ds(r0, c), :].astype(BF16), st.astype(BF16),
                                (((1,), (1,)), ((), ())), preferred_element_type=F32)
        upd = lax.dot_general(vb.astype(BF16), kh_s[pl.ds(r0, c), :].astype(BF16),
                              (((0,), (0,)), ((), ())), preferred_element_type=F32)
        st_ref[si] = st * dt_s[pl.ds(r0, 1), :] + upd * bd_mask
        o_ref[pl.ds(r0, c), :] = o
        return carry

    lax.fori_loop(0, n_blk, block, 0, unroll=min(n_blk, 4))


def _rms_heads(o, w, gate, seg):
    ms = _segsum(o * o, seg) * (1.0 / HEAD_DIM)
    return o * lax.rsqrt(ms + LN_EPS) * w * (gate * _sigmoid(gate))


def _load_rows(ref, sample, pad_s, t_valid):
    if not sample:
        return ref[...]
    pad_s[...] = jnp.zeros_like(pad_s)
    for g in range(ref.shape[0]):
        pad_s[g * SUB_CHUNK:g * SUB_CHUNK + t_valid, :] = ref[g]
    return pad_s[...]


def _store_rows(o_ref, out, sample, t_valid):
    if not sample:
        o_ref[...] = out
        return
    for g in range(o_ref.shape[0]):
        o_ref[g] = out[g * SUB_CHUNK:g * SUB_CHUNK + t_valid, :]


def _valid_rows(rows, width, t_valid):
    return lax.broadcasted_iota(I32, (rows, width), 0) % SUB_CHUNK < t_valid


def _init_state(st_s, s0_ref, sample):
    if sample:
        st_s[...] = s0_ref[...]
    else:
        @pl.when(pl.program_id(1) == 0)
        def _():
            st_s[...] = s0_ref[...]


def _hgrn_kernel(h_ref, s0_ref, lb_ref, nw_ref, o_ref, sT_ref,
                 st_s, o_s, q_s, k_s, v_s, b_s, qh_s, kh_s, dt_s, *pad, sample, t_valid, n_blk):
    _init_state(st_s, s0_ref, sample)
    x = _load_rows(h_ref, sample, pad[0] if sample else None, t_valid)
    rows = x.shape[0]
    q = x[:, 0:256]
    fx = x[:, 256:512]
    iv = x[:, 512:768]
    gate = x[:, 768:1024]
    lb = lb_ref[0:1, :]
    log_lb = lb_ref[1:2, :]
    log1m_lb = lb_ref[2:3, :]
    cterm = log1m_lb + _log_sigmoid(fx)
    log_f = jnp.maximum(log_lb, cterm) + jnp.log1p(jnp.exp(-jnp.abs(log_lb - cterm)))
    key = (1.0 - lb) * _sigmoid(-fx)
    qs = q * _sigmoid(q) * (HG_DK ** -0.5)
    if sample:
        valid = _valid_rows(rows, GROUP_WIDTH, t_valid)
        log_f = jnp.where(valid, log_f, 0.0)
        key = jnp.where(valid, key, 0.0)
        iv = jnp.where(valid, iv, 0.0)
    _gated_tile(qs, key, iv, log_f, st_s, o_s, q_s, k_s, v_s, b_s, qh_s, kh_s, dt_s, dk=HG_DK, n_blk=n_blk)
    seg = _seg_ones(GROUP_WIDTH, HEAD_DIM, GROUP_WIDTH, HEAD_DIM)
    _store_rows(o_ref, _rms_heads(o_s[...], nw_ref[...], gate, seg), sample, t_valid)
    sT_ref[...] = st_s[...]


def _gla_kernel(hvg_ref, hqk_ref, s0_ref, wgk_ref, bgk_ref, nw_ref, o_ref, sT_ref,
                st_s, o_s, q_s, k_s, v_s, b_s, qh_s, kh_s, dt_s, *pad, sample, t_valid, n_blk):
    _init_state(st_s, s0_ref, sample)
    xvg = _load_rows(hvg_ref, sample, pad[0] if sample else None, t_valid)
    xqk = _load_rows(hqk_ref, sample, pad[1] if sample else None, t_valid)
    rows = xvg.shape[0]
    v = xvg[:, 0:256]
    gate = xvg[:, 256:512]
    q = xqk[:, 0:128] * (GLA_DK ** -0.5)
    k = xqk[:, 128:256]
    lo = xqk[:, 256:384]
    gk = _log_sigmoid(_dot3(lo, wgk_ref[...]) + bgk_ref[...]) * (1.0 / GLA_GATE_NORM)
    if sample:
        valid = _valid_rows(rows, 128, t_valid)
        gk = jnp.where(valid, gk, 0.0)
        k = jnp.where(valid, k, 0.0)
    _gated_tile(q, k, v, gk, st_s, o_s, q_s, k_s, v_s, b_s, qh_s, kh_s, dt_s, dk=GLA_DK, n_blk=n_blk)
    seg = _seg_ones(GROUP_WIDTH, HEAD_DIM, GROUP_WIDTH, HEAD_DIM)
    _store_rows(o_ref, _rms_heads(o_s[...], nw_ref[...], gate, seg), sample, t_valid)
    sT_ref[...] = st_s[...]


def _gated_call(kind, h_all, row0, n_seq, t_len, s0_bd, params):
    sample = t_len < SUB_CHUNK
    hk = N_HEADS * (HG_DK if kind == "hgrn" else GLA_DK)
    if sample:
        per_step = SEQ_GROUP
        rows, n_t, n_blk = per_step * SUB_CHUNK, 1, per_step
        h_view = h_all.reshape(h_all.shape[0] // t_len, t_len, h_all.shape[1])
        seq0 = row0 // (t_len * per_step)

        def hspec(width, col_block):
            return pl.BlockSpec((per_step, t_len, width), lambda b, i: (seq0 + b, 0, col_block))

        o_shape = jax.ShapeDtypeStruct((n_seq, t_len, GROUP_WIDTH), F32)
        o_spec = pl.BlockSpec((per_step, t_len, GROUP_WIDTH), lambda b, i: (b, 0, 0))
    else:
        per_step = 1
        rows = min(t_len, ROW_TILE)
        n_t, n_blk = t_len // rows, rows // SUB_CHUNK
        h_view = h_all
        blk0 = row0 // rows

        def hspec(width, col_block):
            return pl.BlockSpec((rows, width), lambda b, i: (blk0 + b * n_t + i, col_block))

        o_shape = jax.ShapeDtypeStruct((n_seq * t_len, GROUP_WIDTH), F32)
        o_spec = pl.BlockSpec((rows, GROUP_WIDTH), lambda b, i: (b * n_t + i, 0))

    st_spec = pl.BlockSpec((per_step, GROUP_WIDTH, hk), lambda b, i: (b, 0, 0))
    st_shape = jax.ShapeDtypeStruct((n_seq, GROUP_WIDTH, hk), F32)

    def full(a):
        return pl.BlockSpec(a.shape, lambda b, i: (0,) * a.ndim)

    scratch = [pltpu.VMEM((per_step, GROUP_WIDTH, hk), F32), pltpu.VMEM((rows, GROUP_WIDTH), F32),
               pltpu.VMEM((rows, hk), F32), pltpu.VMEM((rows, hk), F32), pltpu.VMEM((rows, GROUP_WIDTH), F32),
               pltpu.VMEM((rows, hk), F32), pltpu.VMEM((rows, hk), F32), pltpu.VMEM((rows, hk), F32),
               pltpu.VMEM((rows, hk), F32)]
    if kind == "hgrn":
        lb3, nw = params
        body = functools.partial(_hgrn_kernel, sample=sample, t_valid=t_len, n_blk=n_blk)
        in_specs = [hspec(1024, OFF_HG // 1024), st_spec, full(lb3), full(nw)]
        args = (h_view, s0_bd, lb3, nw)
        if sample:
            scratch.append(pltpu.VMEM((rows, 1024), F32))
    else:
        wgk, bgk, nw = params
        body = functools.partial(_gla_kernel, sample=sample, t_valid=t_len, n_blk=n_blk)
        in_specs = [hspec(512, OFF_GLA_VG // 512), hspec(384, OFF_GLA_QK // 384), st_spec,
                    full(wgk), full(bgk), full(nw)]
        args = (h_view, h_view, s0_bd, wgk, bgk, nw)
        if sample:
            scratch += [pltpu.VMEM((rows, 512), F32), pltpu.VMEM((rows, 384), F32)]
    out, st = pl.pallas_call(
        body,
        grid=(n_seq // per_step, n_t),
        in_specs=in_specs,
        out_specs=(o_spec, st_spec),
        out_shape=(o_shape, st_shape),
        scratch_shapes=scratch,
        compiler_params=_cparams(2),
        name=kind + ("_sample" if sample else "_prompt"),
    )(*args)
    return out.reshape(n_seq * t_len, GROUP_WIDTH), st


def _rwkv_kernel(p_ref, sh0_ref, s0_ref, mu_ref, lora_ref, vec_ref, o_ref, s_out_ref,
                 s_s, prev_s, seg_s, lm_s, a_s, b_s, w_s, k_s, r_s, v_s, y_s, *, n_tok):
    i = pl.program_id(1)
    g8 = SEQ_GROUP

    @pl.when(i == 0)
    def _():
        s_s[...] = s0_ref[0]
        prev_s[...] = sh0_ref[0]

    seg_s[...] = _seg_ones(GROUP_WIDTH, HEAD_DIM, GROUP_WIDTH, HEAD_DIM)
    lm_s[...] = (lax.broadcasted_iota(I32, (HEAD_DIM, g8, GROUP_WIDTH), 2) % HEAD_DIM
                 == lax.broadcasted_iota(I32, (HEAD_DIM, g8, GROUP_WIDTH), 0)).astype(F32)
    seg = seg_s[...]

    p = _slab_load(p_ref)
    if n_tok > 1:
        prev = jnp.concatenate([prev_s[...], p[:-g8, :]], axis=0)
    else:
        prev = prev_s[...]
    prev_s[...] = p[(n_tok - 1) * g8:, :]
    xs = p + (prev - p) * mu_ref[...]
    r = xs[:, 0:256]
    k = xs[:, 256:512]
    v = xs[:, 512:768]
    lo = xs[:, 768:896]
    w0, a0, kkp, ka = vec_ref[0:1, :], vec_ref[1:2, :], vec_ref[2:3, :], vec_ref[3:4, :]
    rk, lnw, lnb = vec_ref[4:5, :], vec_ref[5:6, :], vec_ref[6:7, :]
    w_log = -_softplus(-(w0 + _dot3(jnp.tanh(lo), lora_ref[0]))) - 0.5
    decay = jnp.exp(-jnp.exp(w_log))
    a = _sigmoid(a0 + _dot3(lo, lora_ref[1]))
    g = _dot3(_sigmoid(lo), lora_ref[2])
    kk = k * kkp
    kk = kk * lax.rsqrt(jnp.maximum(_segsum(kk * kk, seg), 1e-24))
    k2 = k * (1.0 + (a - 1.0) * ka)
    a_s[...] = -kk
    b_s[...] = kk * a
    w_s[...] = decay
    k_s[...] = k2
    r_s[...] = r
    v_s[...] = v

    n = HEAD_DIM * g8
    slab = (HEAD_DIM, g8, GROUP_WIDTH)

    def readout(s, row):
        yb = _dot((s * r_s[pl.ds(row, g8), :][None]).reshape(n, GROUP_WIDTH).astype(BF16), seg_s[...])
        return jnp.sum(yb.reshape(slab) * lm_s[...], axis=0)

    def step(t, carry):
        r0 = pl.multiple_of(t * g8, g8)
        rp = pl.multiple_of(jnp.maximum(t - 1, 0) * g8, g8)
        s = s_s[...]
        lm = lm_s[...]
        sg = seg_s[...]
        y_s[pl.ds(rp, g8), :] = readout(s, rp)
        sa = _dot((s * a_s[pl.ds(r0, g8), :][None]).reshape(n, GROUP_WIDTH).astype(BF16), sg).reshape(slab)
        vh, vl = _split2(v_s[pl.ds(r0, g8), :])
        vm = jnp.concatenate([(vh.astype(F32)[None] * lm).reshape(n, GROUP_WIDTH).astype(BF16),
                              (vl.astype(F32)[None] * lm).reshape(n, GROUP_WIDTH).astype(BF16)], axis=0)
        vb2 = _dot(vm, sg)
        vb = (vb2[0:n] + vb2[n:]).reshape(slab)
        s_s[...] = (s * w_s[pl.ds(r0, g8), :][None] + sa * b_s[pl.ds(r0, g8), :][None]
                    + vb * k_s[pl.ds(r0, g8), :][None])
        return carry

    lax.fori_loop(0, n_tok, step, 0, unroll=4)
    last = (n_tok - 1) * g8
    y_s[pl.ds(last, g8), :] = readout(s_s[...], last)

    y = y_s[...]
    mean = _segsum(y, seg) * (1.0 / HEAD_DIM)
    yc = y - mean
    var = _segsum(yc * yc, seg) * (1.0 / HEAD_DIM)
    yn = yc * lax.rsqrt(var + RW_GN_EPS) * lnw + lnb
    bonus = _segsum(r * k2 * rk, seg) * v
    _slab_store(o_ref, (yn + bonus) * g)
    s_out_ref[0] = s_s[...]


def _rwkv_call(p_tm, shift0, s0, mu, lora, vec, n_grp, t_len):
    n_tok = min(t_len, 64)
    n_t = t_len // n_tok
    rows = n_tok * SEQ_GROUP

    def full(a):
        return pl.BlockSpec(a.shape, lambda b, i: (0,) * a.ndim)

    st_spec = pl.BlockSpec((1, HEAD_DIM, SEQ_GROUP, GROUP_WIDTH), lambda b, i: (b, 0, 0, 0))
    vm = lambda shape, dt=F32: pltpu.VMEM(shape, dt)
    out, s_out = pl.pallas_call(
        functools.partial(_rwkv_kernel, n_tok=n_tok),
        grid=(n_grp, n_t),
        in_specs=[pl.BlockSpec((None, RW_COLS // LANES, rows, LANES), lambda b, i: (b, 0, i, 0)),
                  pl.BlockSpec((1, SEQ_GROUP, RW_COLS), lambda b, i: (b, 0, 0)),
                  st_spec, full(mu), full(lora), full(vec)],
        out_specs=(pl.BlockSpec((None, GROUP_WIDTH // LANES, rows, LANES), lambda b, i: (b, 0, i, 0)), st_spec),
        out_shape=(jax.ShapeDtypeStruct((n_grp, GROUP_WIDTH // LANES, t_len * SEQ_GROUP, LANES), F32),
                   jax.ShapeDtypeStruct((n_grp, HEAD_DIM, SEQ_GROUP, GROUP_WIDTH), F32)),
        scratch_shapes=[vm((HEAD_DIM, SEQ_GROUP, GROUP_WIDTH)), vm((SEQ_GROUP, RW_COLS)),
                        vm((GROUP_WIDTH, GROUP_WIDTH), BF16), vm((HEAD_DIM, SEQ_GROUP, GROUP_WIDTH)),
                        vm((rows, GROUP_WIDTH)), vm((rows, GROUP_WIDTH)), vm((rows, GROUP_WIDTH)),
                        vm((rows, GROUP_WIDTH)), vm((rows, GROUP_WIDTH)), vm((rows, GROUP_WIDTH)),
                        vm((rows, GROUP_WIDTH))],
        compiler_params=_cparams(2),
        name="rwkv_t%d" % t_len,
    )(p_tm, shift0, s0, mu, lora, vec)
    return out, s_out


def _s5_prep_kernel(are_ref, aim_ref, ldt_ref, bre_ref, bim_ref, abre_ref, abim_ref, bbre_ref, bbim_ref):
    a_re, a_im = are_ref[...], aim_ref[...]
    dt = jnp.exp(ldt_ref[...])
    mag = jnp.exp(a_re * dt)
    ab_re = mag * jnp.cos(a_im * dt)
    ab_im = mag * jnp.sin(a_im * dt)
    den = a_re * a_re + a_im * a_im
    nr, ni = ab_re - 1.0, ab_im
    coef_re = (nr * a_re + ni * a_im) / den
    coef_im = (ni * a_re - nr * a_im) / den
    b_re, b_im = bre_ref[...], bim_ref[...]
    abre_ref[...] = ab_re
    abim_ref[...] = ab_im
    bbre_ref[...] = coef_re * b_re - coef_im * b_im
    bbim_ref[...] = coef_re * b_im + coef_im * b_re


def _s5_prep(a_re, a_im, log_dt, b_re, b_im):
    rows = DEPTH * S5_NGROUPS
    cols = S5_STATE * S5_GROUP
    rep = lambda t: jnp.repeat(t.reshape(rows, S5_STATE), S5_GROUP, axis=1)
    ldt = jnp.broadcast_to(log_dt.reshape(rows, 1), (rows, cols))
    shp = jax.ShapeDtypeStruct((rows, cols), F32)
    ab_re, ab_im, bb_re, bb_im = pl.pallas_call(
        _s5_prep_kernel, out_shape=(shp, shp, shp, shp), name="s5_prep",
    )(rep(a_re), rep(a_im), ldt, b_re.reshape(rows, cols), b_im.reshape(rows, cols))
    pick = lambda t: t.reshape(DEPTH, S5_NGROUPS, S5_STATE, S5_GROUP)[..., 0].reshape(DEPTH, 1, S5_CH)
    bb = lambda t: t.reshape(DEPTH, S5_NGROUPS, S5_STATE, S5_GROUP)
    return pick(ab_re), pick(ab_im), bb(bb_re), bb(bb_im)


def _s5_kernel(u_ref, x0_ref, ab_ref, wb_ref, wc_ref, d_ref, wglu_ref, o_ref, xT_ref,
               x_s, bu_s, xs_s, *, n_tok):
    i = pl.program_id(1)
    g8 = SEQ_GROUP

    @pl.when(i == 0)
    def _():
        x_s[...] = x0_ref[0]

    u = _slab_load(u_ref)
    bu_s[...] = _dot(u.astype(BF16), wb_ref[...])
    a_re = jnp.broadcast_to(ab_ref[0:1, :], (g8, S5_CH))
    a_im = jnp.broadcast_to(ab_ref[1:2, :], (g8, S5_CH))

    def step(t, carry):
        x_re, x_im = carry
        r0 = pl.multiple_of(t * g8, g8)
        n_re = a_re * x_re - a_im * x_im + bu_s[pl.ds(r0, g8), 0:S5_CH]
        n_im = a_re * x_im + a_im * x_re + bu_s[pl.ds(r0, g8), S5_CH:]
        xs_s[pl.ds(r0, g8), 0:S5_CH] = n_re
        xs_s[pl.ds(r0, g8), S5_CH:] = n_im
        return n_re, n_im

    x_re, x_im = lax.fori_loop(0, n_tok, step, (x_s[:, 0:S5_CH], x_s[:, S5_CH:]))
    x_s[:, 0:S5_CH] = x_re
    x_s[:, S5_CH:] = x_im
    y = _dot(xs_s[...].astype(BF16), wc_ref[...]) + d_ref[...] * u
    yg = 0.5 * y * (1.0 + jnp.tanh(0.7978845608028654 * (y + 0.044715 * (y * y * y))))
    _slab_store(o_ref, yg * _sigmoid(_dot(yg.astype(BF16), wglu_ref[...])))
    xT_ref[0] = x_s[...]


def _s5_call(u_tm, x0, ab, wb, wc, dvec, wglu, n_grp, t_len):
    n_tok = min(t_len, 64)
    n_t = t_len // n_tok
    rows = n_tok * SEQ_GROUP

    def full(a):
        return pl.BlockSpec(a.shape, lambda b, i: (0,) * a.ndim)

    st_spec = pl.BlockSpec((1, SEQ_GROUP, 2 * S5_CH), lambda b, i: (b, 0, 0))
    return pl.pallas_call(
        functools.partial(_s5_kernel, n_tok=n_tok),
        grid=(n_grp, n_t),
        in_specs=[pl.BlockSpec((None, GROUP_WIDTH // LANES, rows, LANES), lambda b, i: (b, 0, i, 0)), st_spec,
                  full(ab), full(wb), full(wc), full(dvec), full(wglu)],
        out_specs=(pl.BlockSpec((None, GROUP_WIDTH // LANES, rows, LANES), lambda b, i: (b, 0, i, 0)), st_spec),
        out_shape=(jax.ShapeDtypeStruct((n_grp, GROUP_WIDTH // LANES, t_len * SEQ_GROUP, LANES), F32),
                   jax.ShapeDtypeStruct((n_grp, SEQ_GROUP, 2 * S5_CH), F32)),
        scratch_shapes=[pltpu.VMEM((SEQ_GROUP, 2 * S5_CH), F32), pltpu.VMEM((rows, 2 * S5_CH), F32),
                        pltpu.VMEM((rows, 2 * S5_CH), F32)],
        compiler_params=_cparams(2),
        name="s5_t%d" % t_len,
    )(u_tm, x0, ab, wb, wc, dvec, wglu)


CHUNKS = D_MODEL // LANES


def _store_chunked(ref, x):
    rows = x.shape[0]
    for j in range(CHUNKS):
        ref[pl.ds(j, rows, stride=CHUNKS), :] = x[:, j * LANES:(j + 1) * LANES]


def _load_chunked(ref, rows):
    return jnp.concatenate([ref[pl.ds(j, rows, stride=CHUNKS), :] for j in range(CHUNKS)], axis=1)


def _post_mix_kernel(x_ref, oap_ref, obp_ref, ocp_ref, odp_ref, oas_ref, obs_ref, ocs_ref, ods_ref,
                     wout_ref, lnw_ref, lnb_ref, rw_ref, rb_ref,
                     x1_ref, x1c_ref, idx_ref, rank_ref, gate_ref, cnt_ref, carry_s, *, prompt_tiles):
    i = pl.program_id(0)

    @pl.when(i == 0)
    def _():
        carry_s[...] = jnp.zeros_like(carry_s)

    is_prompt = i < prompt_tiles
    seq = i % SEQ_GROUP
    pick = lambda p, s_ref: jnp.where(is_prompt, p, s_ref[...]).astype(BF16)
    mix = _dot(pick(_slab_load_rows(oap_ref, seq, ROW_TILE), oas_ref), wout_ref[0:256, :])
    mix += _dot(pick(obp_ref[...], obs_ref), wout_ref[256:512, :])
    mix += _dot(pick(ocp_ref[...], ocs_ref), wout_ref[512:768, :])
    mix += _dot(pick(_slab_load_rows(odp_ref, seq, ROW_TILE), ods_ref), wout_ref[768:1024, :])
    x1 = _layer_norm(DN_ALPHA * x_ref[...] + mix, lnw_ref[...], lnb_ref[...])
    x1_ref[...] = x1
    _store_chunked(x1c_ref, x1)

    xh, xl = _split2(x1)
    logits = _dot(xh, rw_ref[0]) + (_dot(xh, rw_ref[1]) + _dot(xl, rw_ref[0])) + rb_ref[...]
    rows = logits.shape[0]
    lane = lax.broadcasted_iota(I32, (rows, LANES), 1)
    work = logits
    sel_i, sel_v = [], []
    for _ in range(TOP_K):
        m = jnp.max(work, axis=-1, keepdims=True)
        j = jnp.min(jnp.where(work == m, lane, LANES), axis=-1, keepdims=True)
        sel_i.append(j)
        sel_v.append(m)
        work = jnp.where(lane == j, -jnp.inf, work)
    e = [jnp.exp(v - sel_v[0]) for v in sel_v]
    den = (e[0] + e[1]) + (e[2] + e[3])
    onehot = jnp.zeros((rows, LANES), F32)
    for j in sel_i:
        onehot = onehot + (lane == j).astype(F32)
    rr = lax.broadcasted_iota(I32, (rows, rows), 0)
    cc = lax.broadcasted_iota(I32, (rows, rows), 1)
    before = _dot((cc < rr).astype(BF16), onehot.astype(BF16)) + carry_s[0:1, :]
    carry_s[0:1, :] = carry_s[0:1, :] + jnp.sum(onehot, axis=0, keepdims=True)
    idx_o = jnp.zeros((rows, LANES), I32)
    rank_o = jnp.zeros((rows, LANES), I32)
    gate_o = jnp.zeros((rows, LANES), F32)
    for slot in range(TOP_K):
        j = sel_i[slot]
        rank = jnp.sum(jnp.where(lane == j, before, 0.0), axis=-1, keepdims=True)
        idx_o = jnp.where(lane == slot, j, idx_o)
        rank_o = jnp.where(lane == slot, rank.astype(I32), rank_o)
        gate_o = jnp.where(lane == slot, e[slot] / den, gate_o)
    idx_ref[...] = idx_o
    rank_ref[...] = rank_o
    gate_ref[...] = gate_o
    cnt_ref[...] = jnp.broadcast_to(carry_s[0:1, :], cnt_ref.shape)


def _post_mix(x_all, outs_p, outs_s, n_seq_p, t_len_p, wout_l, lnw, lnb, rw_l, rb_l):
    n = x_all.shape[0]
    n_t = t_len_p // ROW_TILE
    prompt_tiles = n_seq_p * n_t
    w = GROUP_WIDTH
    p_idx = lambda i: jnp.minimum(i, prompt_tiles - 1)
    p_row = lambda i: (p_idx(i) % n_seq_p) * n_t + p_idx(i) // n_seq_p
    x_row = lambda i: jnp.where(i < prompt_tiles, p_row(i), i)
    row = lambda width: pl.BlockSpec((ROW_TILE, width), lambda i: (x_row(i), 0))
    tm_p = pl.BlockSpec((w // LANES, ROW_TILE * SEQ_GROUP, LANES), lambda i: (0, p_idx(i) // n_seq_p, 0))
    bm_p = pl.BlockSpec((ROW_TILE, w), lambda i: (p_row(i), 0))
    bm_s = pl.BlockSpec((ROW_TILE, w), lambda i: (jnp.maximum(i - prompt_tiles, 0), 0))

    def full(a):
        return pl.BlockSpec(a.shape, lambda i: (0,) * a.ndim)

    meta = jax.ShapeDtypeStruct((n, LANES), I32)
    return pl.pallas_call(
        functools.partial(_post_mix_kernel, prompt_tiles=prompt_tiles),
        grid=(n // ROW_TILE,),
        in_specs=[row(D_MODEL), tm_p, bm_p, bm_p, tm_p, bm_s, bm_s, bm_s, bm_s, full(wout_l), full(lnw), full(lnb),
                  full(rw_l), full(rb_l)],
        out_specs=(row(D_MODEL), pl.BlockSpec((ROW_TILE * CHUNKS, LANES), lambda i: (x_row(i), 0)),
                   row(LANES), row(LANES), row(LANES), pl.BlockSpec((SUBLANES, LANES), lambda i: (0, 0))),
        out_shape=(jax.ShapeDtypeStruct((n, D_MODEL), F32), jax.ShapeDtypeStruct((n * CHUNKS, LANES), F32),
                   meta, meta, jax.ShapeDtypeStruct((n, LANES), F32),
                   jax.ShapeDtypeStruct((SUBLANES, LANES), F32)),
        scratch_shapes=[pltpu.VMEM((SUBLANES, LANES), F32)],
        compiler_params=_cparams(1),
        name="post_mix",
    )(x_all, *outs_p, *outs_s, wout_l, lnw, lnb, rw_l, rb_l)


def _dispatch_kernel(gend_ref, dest_ref, x_ref, xs_hbm, zero_s, sem, *, tokens, n_tiles):
    i = pl.program_id(0)

    def zero_tile(first_row):
        start = pl.multiple_of(first_row * CHUNKS, MOE_TILE * CHUNKS)
        return pltpu.make_async_copy(zero_s, xs_hbm.at[pl.ds(start, MOE_TILE * CHUNKS)], sem)

    @pl.when(i == 0)
    def _():
        zero_s[...] = jnp.zeros_like(zero_s)
        for e in range(N_EXPERTS):
            @pl.when(gend_ref[e + 1] > gend_ref[e])
            def _():
                zero_tile(gend_ref[e + 1] - MOE_TILE).start()
        for e in range(N_EXPERTS):
            @pl.when(gend_ref[e + 1] > gend_ref[e])
            def _():
                zero_tile(gend_ref[e + 1] - MOE_TILE).wait()

        def tail(t, carry):
            cp = zero_tile(t * MOE_TILE)
            cp.start()
            cp.wait()
            return carry

        lax.fori_loop(gend_ref[N_EXPERTS] // MOE_TILE, n_tiles, tail, 0)

    def row_copy(n, slot):
        src = pl.multiple_of(n * CHUNKS, CHUNKS)
        dst = pl.multiple_of(dest_ref[n * TOP_K + slot] * CHUNKS, CHUNKS)
        return pltpu.make_async_copy(x_ref.at[pl.ds(src, CHUNKS)], xs_hbm.at[pl.ds(dst, CHUNKS)], sem)

    def issue(n, carry):
        for slot in range(TOP_K):
            row_copy(n, slot).start(priority=slot % 2)
        return carry

    lax.fori_loop(0, tokens, issue, 0, unroll=4)
    for slot in range(TOP_K):
        pltpu.make_async_copy(x_ref, xs_hbm.at[pl.ds(0, tokens * CHUNKS)], sem).wait()


def _dispatch(x1c, dest_flat, gend, n_rows):
    n = x1c.shape[0] // CHUNKS
    tokens = DISPATCH_TOKENS if n % DISPATCH_TOKENS == 0 else ROW_TILE
    return pl.pallas_call(
        functools.partial(_dispatch_kernel, tokens=tokens, n_tiles=n_rows // MOE_TILE),
        grid_spec=pltpu.PrefetchScalarGridSpec(
            num_scalar_prefetch=1,
            grid=(n // tokens,),
            in_specs=[pl.BlockSpec((tokens * TOP_K,), lambda i, ge: (i,), memory_space=pltpu.SMEM),
                      pl.BlockSpec((tokens * CHUNKS, LANES), lambda i, ge: (i, 0))],
            out_specs=pl.BlockSpec(memory_space=pl.ANY),
            scratch_shapes=[pltpu.VMEM((MOE_TILE * CHUNKS, LANES), F32), pltpu.SemaphoreType.DMA(())],
        ),
        out_shape=jax.ShapeDtypeStruct((n_rows * CHUNKS, LANES), F32),
        compiler_params=_cparams(1),
        name="moe_dispatch",
    )(gend, dest_flat, x1c)


PAIR_BLOCK = 2 * LANES


def _expert_prep_kernel(w1_ref, w2_ref, w1p_ref, w2b_ref):
    src = lax.broadcasted_iota(I32, (PAIR_BLOCK, PAIR_BLOCK), 0)
    dst = lax.broadcasted_iota(I32, (PAIR_BLOCK, PAIR_BLOCK), 1)
    perm = (src == jnp.where(dst < LANES, 2 * dst, 2 * (dst - LANES) + 1)).astype(BF16)
    for c in range(2 * D_FF // PAIR_BLOCK):
        cols = slice(c * PAIR_BLOCK, (c + 1) * PAIR_BLOCK)
        w1p_ref[:, cols] = _dot(w1_ref[:, cols].astype(BF16), perm).astype(BF16)
    w2b_ref[...] = w2_ref[...].astype(BF16)


def _expert_prep(exp_w1, exp_w2):
    n_l, n_e = exp_w1.shape[:2]
    spec = lambda r, c: pl.BlockSpec((None, None, r, c), lambda i: (i // n_e, i % n_e, 0, 0))
    return pl.pallas_call(
        _expert_prep_kernel,
        grid=(n_l * n_e,),
        in_specs=[spec(D_MODEL, 2 * D_FF), spec(D_FF, D_MODEL)],
        out_specs=(spec(D_MODEL, 2 * D_FF), spec(D_FF, D_MODEL)),
        out_shape=(jax.ShapeDtypeStruct(exp_w1.shape, BF16), jax.ShapeDtypeStruct(exp_w2.shape, BF16)),
        compiler_params=_cparams(1),
        name="expert_prep",
    )(exp_w1, exp_w2)


def _expert_kernel(te_ref, nreal_ref, xs_ref, w1_ref, b1_ref, w2_ref, b2_ref, o_ref, acc_s):
    i = pl.program_id(0)

    @pl.when(i < nreal_ref[0])
    def _():
        x = _load_chunked(xs_ref, MOE_TILE).astype(BF16)
        acc_s[...] = jnp.broadcast_to(b2_ref[...], acc_s.shape)
        for c in range(D_FF // PAIR_BLOCK):
            cols = slice(2 * c * PAIR_BLOCK, 2 * (c + 1) * PAIR_BLOCK)
            h = _dot(x, w1_ref[:, cols]) + b1_ref[:, cols]
            h_glu = jnp.minimum(jnp.concatenate([h[:, 0:128], h[:, 256:384]], axis=1), SWIGLU_LIMIT)
            h_lin = jnp.clip(jnp.concatenate([h[:, 128:256], h[:, 384:512]], axis=1),
                             -SWIGLU_LIMIT, SWIGLU_LIMIT)
            act = h_glu * _sigmoid(SWIGLU_ALPHA * h_glu) * (h_lin + 1.0)
            acc_s[...] += _dot(act.astype(BF16), w2_ref[c * PAIR_BLOCK:(c + 1) * PAIR_BLOCK, :])
        _store_chunked(o_ref, acc_s[...])

    @pl.when(i >= nreal_ref[0])
    def _():
        o_ref[...] = jnp.zeros_like(o_ref)


def _experts(xs, te, nreal, w1d, b1d, w2b, b2, layer):
    n_tiles = xs.shape[0] // (MOE_TILE * CHUNKS)
    clamp = lambda i, nr: jnp.minimum(i, nr[0] - 1)
    return pl.pallas_call(
        _expert_kernel,
        grid_spec=pltpu.PrefetchScalarGridSpec(
            num_scalar_prefetch=2,
            grid=(n_tiles,),
            in_specs=[pl.BlockSpec((MOE_TILE * CHUNKS, LANES), lambda i, te, nr: (clamp(i, nr), 0)),
                      pl.BlockSpec((None, None, D_MODEL, 2 * D_FF), lambda i, te, nr: (layer, te[i], 0, 0)),
                      pl.BlockSpec((None, None, 1, 2 * D_FF), lambda i, te, nr: (layer, te[i], 0, 0)),
                      pl.BlockSpec((None, None, D_FF, D_MODEL), lambda i, te, nr: (layer, te[i], 0, 0)),
                      pl.BlockSpec((None, None, 1, D_MODEL), lambda i, te, nr: (layer, te[i], 0, 0))],
            out_specs=pl.BlockSpec((MOE_TILE * CHUNKS, LANES), lambda i, te, nr: (i, 0)),
            scratch_shapes=[pltpu.VMEM((MOE_TILE, D_MODEL), F32)],
        ),
        out_shape=jax.ShapeDtypeStruct(xs.shape, F32),
        compiler_params=_cparams(1),
        name="moe_experts",
    )(te, nreal, xs, w1d, b1d, w2b, b2)


def _combine_kernel(dest_ref, dest_next_ref, gate_ref, x1_ref, lnw_ref, lnb_ref, ys_hbm, o_ref, buf_s, sem,
                    *, tokens):
    i = pl.program_id(0)
    cur = i % 2

    def fetch(dref, half):
        def issue(n, carry):
            dst = pl.multiple_of(n * CHUNKS, CHUNKS)
            for slot in range(TOP_K):
                src = pl.multiple_of(dref[n * TOP_K + slot] * CHUNKS, CHUNKS)
                pltpu.make_async_copy(ys_hbm.at[pl.ds(src, CHUNKS)], buf_s.at[half, slot, pl.ds(dst, CHUNKS)],
                                      sem.at[half]).start(priority=slot % 2)
            return carry

        lax.fori_loop(0, tokens, issue, 0, unroll=4)

    @pl.when(i == 0)
    def _():
        fetch(dest_ref, 0)

    @pl.when(i + 1 < pl.num_programs(0))
    def _():
        fetch(dest_next_ref, 1 - cur)

    for slot in range(TOP_K):
        pltpu.make_async_copy(ys_hbm.at[pl.ds(0, tokens * CHUNKS)], buf_s.at[cur, slot], sem.at[cur]).wait()
    gate = gate_ref[...]
    ffn = gate[:, 0:1] * _load_chunked(buf_s.at[cur, 0], tokens)
    for slot in range(1, TOP_K):
        ffn = ffn + gate[:, slot:slot + 1] * _load_chunked(buf_s.at[cur, slot], tokens)
    o_ref[...] = _layer_norm(DN_ALPHA * x1_ref[...] + ffn, lnw_ref[...], lnb_ref[...])


def _combine(dest_flat, gate, x1, lnw, lnb, ys):
    n = x1.shape[0]
    tokens = ROW_TILE
    last = n // tokens - 1
    return pl.pallas_call(
        functools.partial(_combine_kernel, tokens=tokens),
        grid=(n // tokens,),
        in_specs=[pl.BlockSpec((tokens * TOP_K,), lambda i: (i,), memory_space=pltpu.SMEM),
                  pl.BlockSpec((tokens * TOP_K,), lambda i: (jnp.minimum(i + 1, last),), memory_space=pltpu.SMEM),
                  pl.BlockSpec((tokens, LANES), lambda i: (i, 0)),
                  pl.BlockSpec((tokens, D_MODEL), lambda i: (i, 0)),
                  pl.BlockSpec((1, D_MODEL), lambda i: (0, 0)),
                  pl.BlockSpec((1, D_MODEL), lambda i: (0, 0)),
                  pl.BlockSpec(memory_space=pl.ANY)],
        out_specs=pl.BlockSpec((tokens, D_MODEL), lambda i: (i, 0)),
        out_shape=jax.ShapeDtypeStruct((n, D_MODEL), F32),
        scratch_shapes=[pltpu.VMEM((2, TOP_K, tokens * CHUNKS, LANES), F32), pltpu.SemaphoreType.DMA((2,))],
        compiler_params=_cparams(1),
        name="moe_combine",
    )(dest_flat, dest_flat, gate, x1, lnw, lnb, ys)


def _moe(x1, x1c, idx, rank, gate, counts, w1d, b1d, w2b, b2, lnw, lnb, layer):
    n = x1.shape[0]
    n_tiles = -(-(n * TOP_K + N_EXPERTS * (MOE_TILE - 1)) // MOE_TILE)
    cnt = counts[0, :N_EXPERTS].astype(I32)
    gsz = ((cnt + (MOE_TILE - 1)) // MOE_TILE) * MOE_TILE
    gend = jnp.cumsum(gsz)
    goff = gend - gsz
    dest = (goff[idx[:, :TOP_K]] + rank[:, :TOP_K]).reshape(-1)
    gend0 = jnp.concatenate([jnp.zeros((1,), I32), gend])
    nreal = (gend[-1:] // MOE_TILE).astype(I32)
    tile_start = jnp.arange(n_tiles, dtype=I32) * MOE_TILE
    te = jnp.minimum(jnp.sum((gend[None, :] <= tile_start[:, None]).astype(I32), axis=1), N_EXPERTS - 1)
    xs = _dispatch(x1c, dest, gend0, n_tiles * MOE_TILE)
    ys = _experts(xs, te, nreal, w1d, b1d, w2b, b2, layer)
    return _combine(dest, gate, x1, lnw, lnb, ys)


def _to_time_major(rows, n_seq, t_len):
    c = rows.shape[-1]
    x = rows.reshape(n_seq // SEQ_GROUP, SEQ_GROUP, t_len, c)
    return jnp.transpose(x, (0, 2, 1, 3)).reshape(n_seq // SEQ_GROUP, t_len * SEQ_GROUP, c)


def _from_time_major(x, n_seq, t_len):
    c = x.shape[-1]
    x = x.reshape(n_seq // SEQ_GROUP, t_len, SEQ_GROUP, c)
    return jnp.transpose(x, (0, 2, 1, 3)).reshape(n_seq * t_len, c)


def _block_diag_state(s, dk):
    st = jnp.swapaxes(s, 2, 3)
    eye = jnp.eye(N_HEADS, dtype=s.dtype)
    return jnp.einsum("bhvk,hg->bhvgk", st, eye).reshape(s.shape[0], GROUP_WIDTH, N_HEADS * dk)


def _unblock_state(st, dk):
    b = st.shape[0]
    x = st.reshape(b, N_HEADS, HEAD_DIM, N_HEADS, dk)
    x = jnp.stack([x[:, h, :, h, :] for h in range(N_HEADS)], axis=1)
    return jnp.swapaxes(x, 2, 3)


def _rwkv_state_in(s):
    b = s.shape[0]
    x = s.reshape(b // SEQ_GROUP, SEQ_GROUP, N_HEADS, HEAD_DIM, HEAD_DIM)
    return jnp.transpose(x, (0, 3, 1, 2, 4)).reshape(b // SEQ_GROUP, HEAD_DIM, SEQ_GROUP, GROUP_WIDTH)


def _rwkv_state_out(x, b):
    x = x.reshape(b // SEQ_GROUP, HEAD_DIM, SEQ_GROUP, N_HEADS, HEAD_DIM)
    return jnp.transpose(x, (0, 2, 3, 1, 4)).reshape(b, N_HEADS, HEAD_DIM, HEAD_DIM)


def _pad_rows(w, row0, n_rows):
    out = jnp.zeros((n_rows, w.shape[1]), w.dtype)
    return out.at[row0:row0 + w.shape[0]].set(w)


def kernel(x_prompt, x_sample, state_rwkv, state_rwkv_shift, state_hgrn, state_gla, state_s5_re, state_s5_im,
           w_in, rw_mu, rw_w0, rw_w2, rw_a0, rw_a2, rw_g2, rw_kk, rw_ka, rw_rk, rw_lnx_w, rw_lnx_b,
           hg_lb_logits, hg_norm_w, gla_w_gk2, gla_b_gk, gla_norm_w,
           s5_A_re, s5_A_im, s5_log_dt, s5_B_re, s5_B_im, s5_C_re, s5_C_im, s5_D, s5_w_glu,
           w_out, ln1_w, ln1_b, router_w, router_b, exp_w1, exp_b1, exp_w2, exp_b2, ln2_w, ln2_b):
    bp, tp, _ = x_prompt.shape
    bs, ts, _ = x_sample.shape
    n_p, n_s = bp * tp, bs * ts
    groups = ((0, bp, tp), (n_p, bs, ts))
    assert bp == SEQ_GROUP and tp % ROW_TILE == 0 and bs % SEQ_GROUP == 0 and n_s % ROW_TILE == 0
    assert ts <= SUB_CHUNK

    c = np.cumsum([0, RW_COLS, 1024, 784, 256])
    rw_c, hg_c, gl_c, s5_c = (w_in[:, :, c[j]:c[j + 1]] for j in range(4))
    gl_q, gl_k, gl_v, gl_lo, gl_g = (gl_c[:, :, a:b] for a, b in
                                     ((0, 128), (128, 256), (256, 512), (512, 528), (528, 784)))
    zpad = jnp.zeros((DEPTH, D_MODEL, 128 - GLA_GK_LORA), w_in.dtype)
    w_in_p = jnp.concatenate([hg_c, gl_v, gl_g, gl_q, gl_k, gl_lo, zpad, rw_c, s5_c], axis=2).astype(BF16)
    w_out_b = w_out.astype(BF16)
    w1d, w2b = _expert_prep(exp_w1, exp_w2)
    b1d = jnp.swapaxes(exp_b1.reshape(DEPTH, N_EXPERTS, 2 * D_FF // PAIR_BLOCK, LANES, 2), -1, -2)
    b1d = b1d.reshape(DEPTH, N_EXPERTS, 1, 2 * D_FF)
    b2r = exp_b2[:, :, None, :]
    rw_pad = jnp.pad(router_w, ((0, 0), (0, 0), (0, LANES - N_EXPERTS)))
    rw_hi = rw_pad.astype(BF16)
    rw_lo = (rw_pad - rw_hi.astype(F32)).astype(BF16)
    rw_split = jnp.stack([rw_hi, rw_lo], axis=1)
    rb_pad = jnp.pad(router_b, ((0, 0), (0, LANES - N_EXPERTS)), constant_values=-1e30)[:, None, :]

    lbs = jnp.cumsum(jax.nn.softmax(hg_lb_logits.astype(F32), axis=0), axis=0)
    lbs = lbs - lbs[:1]
    lb3 = jnp.stack([lbs, jnp.log(lbs), jnp.log1p(-lbs)], axis=1)
    lb3 = jnp.pad(lb3, ((0, 0), (0, SUBLANES - 3), (0, 0)))

    ab_re, ab_im, bb_re, bb_im = _s5_prep(s5_A_re, s5_A_im, s5_log_dt, s5_B_re, s5_B_im)
    eye_g = jnp.eye(S5_NGROUPS, dtype=F32)
    wb = jnp.stack([jnp.einsum("lgph,gk->lghkp", t, eye_g).reshape(DEPTH, GROUP_WIDTH, S5_CH)
                    for t in (bb_re, bb_im)], axis=2).reshape(DEPTH, GROUP_WIDTH, 2 * S5_CH).astype(BF16)
    wc = jnp.concatenate([jnp.einsum("lghp,gk->lgpkh", t, eye_g).reshape(DEPTH, S5_CH, GROUP_WIDTH)
                          for t in (s5_C_re, -s5_C_im)], axis=1).astype(BF16)
    ab = jnp.concatenate([ab_re, ab_im], axis=1)
    ab = jnp.pad(ab, ((0, 0), (0, SUBLANES - 2), (0, 0)))
    wglu_b = s5_w_glu.astype(BF16)

    lora = jnp.stack([jnp.stack([_pad_rows(rw_w2[l], 0, 128), _pad_rows(rw_a2[l], 32, 128),
                                 _pad_rows(rw_g2[l], 64, 128)]) for l in range(DEPTH)])
    rw_vec = jnp.stack([rw_w0, rw_a0, rw_kk, rw_ka, rw_rk, rw_lnx_w, rw_lnx_b, jnp.zeros_like(rw_w0)], axis=1)
    wgk = jnp.stack([_pad_rows(gla_w_gk2[l], 0, 128) for l in range(DEPTH)])

    zeros = lambda shape: jnp.zeros(shape, F32)
    st_in = (
        dict(rw=zeros((DEPTH, bp, N_HEADS, HEAD_DIM, HEAD_DIM)), sh=zeros((DEPTH, bp, RW_COLS)),
             hg=zeros((DEPTH, bp, N_HEADS, HG_DK, HEAD_DIM)), gl=zeros((DEPTH, bp, N_HEADS, GLA_DK, HEAD_DIM)),
             re=zeros((DEPTH, bp, S5_NGROUPS, S5_STATE)), im=zeros((DEPTH, bp, S5_NGROUPS, S5_STATE))),
        dict(rw=state_rwkv, sh=state_rwkv_shift, hg=state_hgrn, gl=state_gla, re=state_s5_re, im=state_s5_im),
    )
    collected = ([], [])

    x_all = jnp.concatenate([x_prompt.reshape(n_p, D_MODEL), x_sample.reshape(n_s, D_MODEL)], axis=0)
    for l in range(DEPTH):
        h_gate_p, p_tm_p, u_tm_p = _inproj_prompt(x_all, w_in_p[l], bp, tp)
        h_s = _inproj_rows(x_all, w_in_p[l], n_p, n_s)
        outs = ([], [])
        for gi, (row0, n_seq, t_len) in enumerate(groups):
            st = st_in[gi]
            n_grp = n_seq // SEQ_GROUP
            if gi == 0:
                h_gate = h_gate_p
                p_tm = p_tm_p[None]
                u_tm = u_tm_p[None]
                new_sh = _from_slabs(p_tm_p[:, (t_len - 1) * n_seq:, :])
                to_rows = lambda x_tm: x_tm[0]
            else:
                h_gate = h_s
                p_rw = h_s[:, OFF_RW:OFF_RW + RW_COLS]
                p_tm = _to_slabs(_to_time_major(p_rw, n_seq, t_len))
                u_tm = _to_slabs(_to_time_major(h_s[:, OFF_S5:OFF_S5 + GROUP_WIDTH], n_seq, t_len))
                new_sh = p_rw.reshape(n_seq, t_len, RW_COLS)[:, -1]
                to_rows = lambda x_tm: _from_time_major(_from_slabs(x_tm), n_seq, t_len)
            sh0 = st["sh"][l].reshape(n_grp, SEQ_GROUP, RW_COLS)
            oa_tm, s_rw = _rwkv_call(p_tm, sh0, _rwkv_state_in(st["rw"][l]), rw_mu[l][None, :], lora[l],
                                     rw_vec[l], n_grp, t_len)
            new_rw = _rwkv_state_out(s_rw, n_seq)
            ob, s_hg = _gated_call("hgrn", h_gate, 0, n_seq, t_len, _block_diag_state(st["hg"][l], HG_DK),
                                   (lb3[l], hg_norm_w[l][None, :]))
            oc, s_gl = _gated_call("gla", h_gate, 0, n_seq, t_len, _block_diag_state(st["gl"][l], GLA_DK),
                                   (wgk[l], gla_b_gk[l][None, :], gla_norm_w[l][None, :]))
            x0 = jnp.concatenate([st["re"][l].reshape(n_grp, SEQ_GROUP, S5_CH),
                                  st["im"][l].reshape(n_grp, SEQ_GROUP, S5_CH)], axis=-1)
            od_tm, x_t = _s5_call(u_tm, x0, ab[l], wb[l], wc[l], s5_D[l][None, :], wglu_b[l], n_grp, t_len)
            outs[gi].extend([to_rows(oa_tm), ob, oc, to_rows(od_tm)])
            x_t = x_t.reshape(n_seq, 2, S5_NGROUPS, S5_STATE)
            collected[gi].append((new_rw, new_sh, _unblock_state(s_hg, HG_DK), _unblock_state(s_gl, GLA_DK),
                                  x_t[:, 0], x_t[:, 1]))
        x1, x1c, idx, rank, gate, counts = _post_mix(x_all, outs[0], outs[1], bp, tp, w_out_b[l],
                                                     ln1_w[l][None, :], ln1_b[l][None, :],
                                                     rw_split[l], rb_pad[l])
        x_all = _moe(x1, x1c, idx, rank, gate, counts, w1d, b1d, w2b, b2r, ln2_w[l][None, :], ln2_b[l][None, :], l)

    y_prompt = x_all[:n_p].reshape(bp, tp, D_MODEL)
    y_sample = x_all[n_p:].reshape(bs, ts, D_MODEL)
    ps = [jnp.stack([layer[j] for layer in collected[0]]) for j in range(6)]
    ss = [jnp.stack([layer[j] for layer in collected[1]]) for j in range(6)]
    return (y_prompt, y_sample, *ps, *ss)
```

```python
import functools

import jax
import jax.numpy as jnp
import numpy as np
from jax import lax
from jax.experimental import pallas as pl
from jax.experimental.pallas import tpu as pltpu

F32 = jnp.float32
BF16 = jnp.bfloat16
I32 = jnp.int32

D_MODEL = 1024
DEPTH = 4
GROUP_WIDTH = 256
HEAD_DIM = 64
N_HEADS = 4
RW_COLS = 896
RW_GN_EPS = 64e-5
HG_DK = 64
GLA_DK = 32
GLA_GK_LORA = 16
GLA_GATE_NORM = 16.0
S5_NGROUPS = 16
S5_GROUP = 16
S5_STATE = 64
S5_CH = S5_NGROUPS * S5_STATE
N_EXPERTS = 32
TOP_K = 4
D_FF = 1024
SWIGLU_ALPHA = 1.702
SWIGLU_LIMIT = 7.0
DN_ALPHA = (2.0 * DEPTH) ** 0.25
LN_EPS = 1e-5

SUBLANES = 8
LANES = 128
VMEM_LIMIT_BYTES = 48 * 1024 * 1024

OFF_HG = 0
OFF_GLA_VG = 1024
OFF_GLA_QK = 1536
OFF_RW = 1920
OFF_S5 = 2816
IN_PAD = 3072

ROW_TILE = 256
SUB_CHUNK = 16
SEQ_GROUP = SUBLANES
MOE_TILE = 512
DISPATCH_TOKENS = 512


def _cparams(n_axes):
    return pltpu.CompilerParams(dimension_semantics=("arbitrary",) * n_axes,
                                vmem_limit_bytes=VMEM_LIMIT_BYTES)


def _dot(a, b):
    return jnp.dot(a, b, preferred_element_type=F32)


def _split2(x):
    hi = x.astype(BF16)
    lo = (x - hi.astype(F32)).astype(BF16)
    return hi, lo


def _split3(x):
    hi = x.astype(BF16)
    r = x - hi.astype(F32)
    mid = r.astype(BF16)
    lo = (r - mid.astype(F32)).astype(BF16)
    return hi, mid, lo


def _dot3(a, b):
    ah, al = _split2(a)
    bh, bl = _split2(b)
    return _dot(ah, bh) + (_dot(ah, bl) + _dot(al, bh))


def _seg_ones(n_in, seg_in, n_out, seg_out):
    r = lax.broadcasted_iota(I32, (n_in, n_out), 0) // seg_in
    c = lax.broadcasted_iota(I32, (n_in, n_out), 1) // seg_out
    return (r == c).astype(BF16)


def _segsum(x, seg):
    rows = x.shape[0]
    hi, lo = _split2(x)
    both = _dot(jnp.concatenate([hi, lo], axis=0), seg)
    return both[:rows] + both[rows:]


def _sigmoid(x):
    return 1.0 / (1.0 + jnp.exp(-x))


def _log_sigmoid(x):
    return jnp.minimum(x, 0.0) - jnp.log1p(jnp.exp(-jnp.abs(x)))


def _softplus(x):
    return jnp.maximum(x, 0.0) + jnp.log1p(jnp.exp(-jnp.abs(x)))


def _layer_norm(x, w, b):
    xc = x - jnp.mean(x, axis=-1, keepdims=True)
    var = jnp.mean(xc * xc, axis=-1, keepdims=True)
    return xc * lax.rsqrt(var + LN_EPS) * w + b


def _inproj_kernel(x_ref, w_ref, *o_refs):
    h = _dot(x_ref[...].astype(BF16), w_ref[...])
    col = 0
    for o_ref in o_refs:
        o_ref[...] = h[:, col:col + o_ref.shape[-1]]
        col += o_ref.shape[-1]


def _slab_store_rows(ref, seq, x):
    for j in range(ref.shape[0]):
        ref.at[j][pl.ds(seq, x.shape[0], stride=SEQ_GROUP), :] = x[:, j * LANES:(j + 1) * LANES]


def _slab_load_rows(ref, seq, tokens):
    return jnp.concatenate([ref.at[j][pl.ds(seq, tokens, stride=SEQ_GROUP), :] for j in range(ref.shape[0])],
                           axis=1)


def _slab_load(ref):
    return jnp.concatenate([ref[j] for j in range(ref.shape[0])], axis=1)


def _slab_store(ref, x):
    for j in range(ref.shape[0]):
        ref[j] = x[:, j * LANES:(j + 1) * LANES]


def _to_slabs(x):
    s = x.reshape(x.shape[:-1] + (x.shape[-1] // LANES, LANES))
    return jnp.swapaxes(s, -2, -3)


def _from_slabs(x):
    s = jnp.swapaxes(x, -2, -3)
    return s.reshape(s.shape[:-2] + (s.shape[-2] * LANES,))


def _inproj_prompt_kernel(x_ref, w_ref, g_ref, p_ref, u_ref):
    seq = pl.program_id(1)
    h = _dot(x_ref[...].astype(BF16), w_ref[...])
    g_ref[...] = h[:, 0:OFF_RW]
    _slab_store_rows(p_ref, seq, h[:, OFF_RW:OFF_RW + RW_COLS])
    _slab_store_rows(u_ref, seq, h[:, OFF_S5:OFF_S5 + GROUP_WIDTH])


def _inproj_prompt(x_all, w_in_l, n_seq, t_len):
    n_t = t_len // ROW_TILE
    slab = lambda c: pl.BlockSpec((c // LANES, ROW_TILE * SEQ_GROUP, LANES), lambda i, b: (0, i, 0))
    return pl.pallas_call(
        _inproj_prompt_kernel,
        grid=(n_t, n_seq),
        in_specs=[pl.BlockSpec((ROW_TILE, D_MODEL), lambda i, b: (b * n_t + i, 0)),
                  pl.BlockSpec((D_MODEL, IN_PAD), lambda i, b: (0, 0))],
        out_specs=(pl.BlockSpec((ROW_TILE, OFF_RW), lambda i, b: (b * n_t + i, 0)),
                   slab(RW_COLS), slab(GROUP_WIDTH)),
        out_shape=(jax.ShapeDtypeStruct((n_seq * t_len, OFF_RW), F32),
                   jax.ShapeDtypeStruct((RW_COLS // LANES, t_len * n_seq, LANES), F32),
                   jax.ShapeDtypeStruct((GROUP_WIDTH // LANES, t_len * n_seq, LANES), F32)),
        compiler_params=_cparams(2),
        name="inproj_prompt",
    )(x_all, w_in_l)


def _inproj_rows(x_all, w_in_l, row0, n_rows):
    blk0 = row0 // ROW_TILE
    return pl.pallas_call(
        _inproj_kernel,
        grid=(n_rows // ROW_TILE,),
        in_specs=[pl.BlockSpec((ROW_TILE, D_MODEL), lambda i: (blk0 + i, 0)),
                  pl.BlockSpec((D_MODEL, IN_PAD), lambda i: (0, 0))],
        out_specs=pl.BlockSpec((ROW_TILE, IN_PAD), lambda i: (i, 0)),
        out_shape=jax.ShapeDtypeStruct((n_rows, IN_PAD), F32),
        compiler_params=_cparams(1),
        name="inproj_rows",
    )(x_all, w_in_l)


def _gated_tile(q, k, v, g, st_ref, o_ref, q_s, k_s, v_s, b_s, qh_s, kh_s, dt_s, *, dk):
    c = SUB_CHUNK
    rows, hk = q.shape
    rr = lax.broadcasted_iota(I32, (rows, rows), 0)
    cc = lax.broadcasted_iota(I32, (rows, rows), 1)
    same = (rr // c) == (cc // c)
    tri = jnp.concatenate([(same & (cc <= rr)).astype(BF16), same.astype(BF16)], axis=0)
    g3 = jnp.concatenate(_split3(g), axis=1)
    p = _dot(tri, g3)
    b = p[:rows, :hk] + p[:rows, hk:2 * hk] + p[:rows, 2 * hk:]
    btot = p[rows:, :hk] + p[rows:, hk:2 * hk] + p[rows:, 2 * hk:]
    q_s[...] = q
    k_s[...] = k
    v_s[...] = v
    b_s[...] = b
    qh_s[...] = q * jnp.exp(b)
    kh_s[...] = k * jnp.exp(btot - b)
    dt_s[...] = jnp.exp(btot)
    seg = _seg_ones(hk, dk, GROUP_WIDTH, HEAD_DIM)
    bd_mask = (lax.broadcasted_iota(I32, (GROUP_WIDTH, hk), 0) // HEAD_DIM
               == lax.broadcasted_iota(I32, (GROUP_WIDTH, hk), 1) // dk).astype(F32)
    t_sub = lax.broadcasted_iota(I32, (SUBLANES, hk), 0)

    def block(i):
        r0 = pl.multiple_of(i * c, c)
        vb = v_s[pl.ds(r0, c), :]
        n_sub = c // SUBLANES
        q_sub = [q_s[pl.ds(r0 + j * SUBLANES, SUBLANES), :] for j in range(n_sub)]
        b_sub = [b_s[pl.ds(r0 + j * SUBLANES, SUBLANES), :] for j in range(n_sub)]
        pieces = []
        for s in range(c):
            b_row = b_s[pl.ds(r0 + s, 1), :]
            k_row = k_s[pl.ds(r0 + s, 1), :]
            for j in range(s // SUBLANES, n_sub):
                d = jnp.where(t_sub + j * SUBLANES >= s, b_sub[j] - b_row, -jnp.inf)
                pieces.append(jnp.exp(d) * q_sub[j] * k_row)
        att = _segsum(jnp.concatenate(pieces, axis=0), seg)
        o_parts = [None] * n_sub
        row = 0
        for s in range(c):
            v_row = v_s[pl.ds(r0 + s, 1), :]
            for j in range(s // SUBLANES, n_sub):
                part = att[row:row + SUBLANES, :] * v_row
                row += SUBLANES
                o_parts[j] = part if o_parts[j] is None else o_parts[j] + part
        o = jnp.concatenate(o_parts, axis=0)
        si = i if st_ref.shape[0] > 1 else 0
        st = st_ref[si]
        o = o + lax.dot_general(qh_s[pl.ds(r0, c), :].astype(BF16), st.astype(BF16),
                                (((1,), (1,)), ((), ())), preferred_element_type=F32)
        upd = lax.dot_general(vb.astype(BF16), kh_s[pl.ds(r0, c), :].astype(BF16),
                              (((0,), (0,)), ((), ())), preferred_element_type=F32)
        st_ref[si] = st * dt_s[pl.ds(r0, 1), :] + upd * bd_mask
        o_ref[pl.ds(r0, c), :] = o

    return block


def _run_blocks(blocks, n_blk):
    def body(i, carry):
        for blk in blocks:
            blk(i)
        return carry

    lax.fori_loop(0, n_blk, body, 0, unroll=min(n_blk, max(1, 4 // len(blocks))))


def _rms_heads(o, w, gate, seg):
    ms = _segsum(o * o, seg) * (1.0 / HEAD_DIM)
    return o * lax.rsqrt(ms + LN_EPS) * w * (gate * _sigmoid(gate))


def _load_rows(ref, sample, pad_s, t_valid):
    if not sample:
        return ref[...]
    pad_s[...] = jnp.zeros_like(pad_s)
    for g in range(ref.shape[0]):
        pad_s[g * SUB_CHUNK:g * SUB_CHUNK + t_valid, :] = ref[g]
    return pad_s[...]


def _store_rows(o_ref, out, sample, t_valid):
    if not sample:
        o_ref[...] = out
        return
    for g in range(o_ref.shape[0]):
        o_ref[g] = out[g * SUB_CHUNK:g * SUB_CHUNK + t_valid, :]


def _valid_rows(rows, width, t_valid):
    return lax.broadcasted_iota(I32, (rows, width), 0) % SUB_CHUNK < t_valid


def _init_state(st_s, s0_ref, sample):
    if sample:
        st_s[...] = s0_ref[...]
    else:
        @pl.when(pl.program_id(1) == 0)
        def _():
            st_s[...] = s0_ref[...]


def _hgrn_parts(h_ref, s0_ref, lb_ref, nw_ref, o_ref, sT_ref,
                st_s, o_s, q_s, k_s, v_s, b_s, qh_s, kh_s, dt_s, pad, *, sample, t_valid):
    _init_state(st_s, s0_ref, sample)
    x = _load_rows(h_ref, sample, pad[0] if sample else None, t_valid)
    rows = x.shape[0]
    q = x[:, 0:256]
    fx = x[:, 256:512]
    iv = x[:, 512:768]
    gate = x[:, 768:1024]
    lb = lb_ref[0:1, :]
    log_lb = lb_ref[1:2, :]
    log1m_lb = lb_ref[2:3, :]
    cterm = log1m_lb + _log_sigmoid(fx)
    log_f = jnp.maximum(log_lb, cterm) + jnp.log1p(jnp.exp(-jnp.abs(log_lb - cterm)))
    key = (1.0 - lb) * _sigmoid(-fx)
    qs = q * _sigmoid(q) * (HG_DK ** -0.5)
    if sample:
        valid = _valid_rows(rows, GROUP_WIDTH, t_valid)
        log_f = jnp.where(valid, log_f, 0.0)
        key = jnp.where(valid, key, 0.0)
        iv = jnp.where(valid, iv, 0.0)
    block = _gated_tile(qs, key, iv, log_f, st_s, o_s, q_s, k_s, v_s, b_s, qh_s, kh_s, dt_s, dk=HG_DK)

    def finish():
        seg = _seg_ones(GROUP_WIDTH, HEAD_DIM, GROUP_WIDTH, HEAD_DIM)
        _store_rows(o_ref, _rms_heads(o_s[...], nw_ref[...], gate, seg), sample, t_valid)
        sT_ref[...] = st_s[...]

    return block, finish


def _gla_parts(hvg_ref, hqk_ref, s0_ref, wgk_ref, bgk_ref, nw_ref, o_ref, sT_ref,
               st_s, o_s, q_s, k_s, v_s, b_s, qh_s, kh_s, dt_s, pad, *, sample, t_valid):
    _init_state(st_s, s0_ref, sample)
    xvg = _load_rows(hvg_ref, sample, pad[0] if sample else None, t_valid)
    xqk = _load_rows(hqk_ref, sample, pad[1] if sample else None, t_valid)
    rows = xvg.shape[0]
    v = xvg[:, 0:256]
    gate = xvg[:, 256:512]
    q = xqk[:, 0:128] * (GLA_DK ** -0.5)
    k = xqk[:, 128:256]
    lo = xqk[:, 256:384]
    gk = _log_sigmoid(_dot3(lo, wgk_ref[...]) + bgk_ref[...]) * (1.0 / GLA_GATE_NORM)
    if sample:
        valid = _valid_rows(rows, 128, t_valid)
        gk = jnp.where(valid, gk, 0.0)
        k = jnp.where(valid, k, 0.0)
    block = _gated_tile(q, k, v, gk, st_s, o_s, q_s, k_s, v_s, b_s, qh_s, kh_s, dt_s, dk=GLA_DK)

    def finish():
        seg = _seg_ones(GROUP_WIDTH, HEAD_DIM, GROUP_WIDTH, HEAD_DIM)
        _store_rows(o_ref, _rms_heads(o_s[...], nw_ref[...], gate, seg), sample, t_valid)
        sT_ref[...] = st_s[...]

    return block, finish


N_GATED_SCRATCH = 9


def _gated_pair_kernel(h_ref, hvg_ref, hqk_ref, s0h_ref, s0g_ref, lb_ref, nwh_ref, wgk_ref, bgk_ref, nwg_ref,
                       oh_ref, og_ref, sTh_ref, sTg_ref, *scr, sample, t_valid, n_blk):
    hg_s, gl_s, pad = scr[:N_GATED_SCRATCH], scr[N_GATED_SCRATCH:2 * N_GATED_SCRATCH], scr[2 * N_GATED_SCRATCH:]
    blk_h, fin_h = _hgrn_parts(h_ref, s0h_ref, lb_ref, nwh_ref, oh_ref, sTh_ref, *hg_s, pad[0:1],
                               sample=sample, t_valid=t_valid)
    blk_g, fin_g = _gla_parts(hvg_ref, hqk_ref, s0g_ref, wgk_ref, bgk_ref, nwg_ref, og_ref, sTg_ref, *gl_s, pad[1:3],
                              sample=sample, t_valid=t_valid)
    _run_blocks((blk_h, blk_g), n_blk)
    fin_h()
    fin_g()


def _gated_call(h_all, n_seq, t_len, s0_hg, s0_gl, hg_params, gl_params):
    sample = t_len < SUB_CHUNK
    row0 = 0
    if sample:
        per_step = SEQ_GROUP
        rows, n_t, n_blk = per_step * SUB_CHUNK, 1, per_step
        h_view = h_all.reshape(h_all.shape[0] // t_len, t_len, h_all.shape[1])
        seq0 = row0 // (t_len * per_step)

        def hspec(width, col_block):
            return pl.BlockSpec((per_step, t_len, width), lambda b, i: (seq0 + b, 0, col_block))

        o_shape = jax.ShapeDtypeStruct((n_seq, t_len, GROUP_WIDTH), F32)
        o_spec = pl.BlockSpec((per_step, t_len, GROUP_WIDTH), lambda b, i: (b, 0, 0))
    else:
        per_step = 1
        rows = min(t_len, ROW_TILE)
        n_t, n_blk = t_len // rows, rows // SUB_CHUNK
        h_view = h_all
        blk0 = row0 // rows

        def hspec(width, col_block):
            return pl.BlockSpec((rows, width), lambda b, i: (blk0 + b * n_t + i, col_block))

        o_shape = jax.ShapeDtypeStruct((n_seq * t_len, GROUP_WIDTH), F32)
        o_spec = pl.BlockSpec((rows, GROUP_WIDTH), lambda b, i: (b * n_t + i, 0))

    hk_hg, hk_gl = N_HEADS * HG_DK, N_HEADS * GLA_DK
    st_spec = lambda hk: pl.BlockSpec((per_step, GROUP_WIDTH, hk), lambda b, i: (b, 0, 0))
    st_shape = lambda hk: jax.ShapeDtypeStruct((n_seq, GROUP_WIDTH, hk), F32)

    def full(a):
        return pl.BlockSpec(a.shape, lambda b, i: (0,) * a.ndim)

    def recurrence_scratch(hk):
        vm = lambda r, w: pltpu.VMEM((r, w), F32)
        return [pltpu.VMEM((per_step, GROUP_WIDTH, hk), F32), vm(rows, GROUP_WIDTH), vm(rows, hk), vm(rows, hk),
                vm(rows, GROUP_WIDTH), vm(rows, hk), vm(rows, hk), vm(rows, hk), vm(rows, hk)]

    scratch = recurrence_scratch(hk_hg) + recurrence_scratch(hk_gl)
    if sample:
        scratch += [pltpu.VMEM((rows, 1024), F32), pltpu.VMEM((rows, 512), F32), pltpu.VMEM((rows, 384), F32)]
    lb3, nw_hg = hg_params
    wgk, bgk, nw_gl = gl_params
    o_hg, o_gl, st_hg, st_gl = pl.pallas_call(
        functools.partial(_gated_pair_kernel, sample=sample, t_valid=t_len, n_blk=n_blk),
        grid=(n_seq // per_step, n_t),
        in_specs=[hspec(1024, OFF_HG // 1024), hspec(512, OFF_GLA_VG // 512), hspec(384, OFF_GLA_QK // 384),
                  st_spec(hk_hg), st_spec(hk_gl), full(lb3), full(nw_hg), full(wgk), full(bgk), full(nw_gl)],
        out_specs=(o_spec, o_spec, st_spec(hk_hg), st_spec(hk_gl)),
        out_shape=(o_shape, o_shape, st_shape(hk_hg), st_shape(hk_gl)),
        scratch_shapes=scratch,
        compiler_params=_cparams(2),
        name="gated_sample" if sample else "gated_prompt",
    )(h_view, h_view, h_view, s0_hg, s0_gl, lb3, nw_hg, wgk, bgk, nw_gl)
    flat = lambda o: o.reshape(n_seq * t_len, GROUP_WIDTH)
    return flat(o_hg), flat(o_gl), st_hg, st_gl


def _rwkv_kernel(p_ref, sh0_ref, s0_ref, mu_ref, lora_ref, vec_ref, o_ref, s_out_ref,
                 s_s, prev_s, seg_s, lm_s, a_s, b_s, w_s, k_s, r_s, v_s, y_s, *, n_tok):
    i = pl.program_id(1)
    g8 = SEQ_GROUP

    @pl.when(i == 0)
    def _():
        s_s[...] = s0_ref[0]
        prev_s[...] = sh0_ref[0]

    seg_s[...] = _seg_ones(GROUP_WIDTH, HEAD_DIM, GROUP_WIDTH, HEAD_DIM)
    lm_s[...] = (lax.broadcasted_iota(I32, (HEAD_DIM, g8, GROUP_WIDTH), 2) % HEAD_DIM
                 == lax.broadcasted_iota(I32, (HEAD_DIM, g8, GROUP_WIDTH), 0)).astype(F32)
    seg = seg_s[...]

    p = _slab_load(p_ref)
    if n_tok > 1:
        prev = jnp.concatenate([prev_s[...], p[:-g8, :]], axis=0)
    else:
        prev = prev_s[...]
    prev_s[...] = p[(n_tok - 1) * g8:, :]
    xs = p + (prev - p) * mu_ref[...]
    r = xs[:, 0:256]
    k = xs[:, 256:512]
    v = xs[:, 512:768]
    lo = xs[:, 768:896]
    w0, a0, kkp, ka = vec_ref[0:1, :], vec_ref[1:2, :], vec_ref[2:3, :], vec_ref[3:4, :]
    rk, lnw, lnb = vec_ref[4:5, :], vec_ref[5:6, :], vec_ref[6:7, :]
    w_log = -_softplus(-(w0 + _dot3(jnp.tanh(lo), lora_ref[0]))) - 0.5
    decay = jnp.exp(-jnp.exp(w_log))
    a = _sigmoid(a0 + _dot3(lo, lora_ref[1]))
    g = _dot3(_sigmoid(lo), lora_ref[2])
    kk = k * kkp
    kk = kk * lax.rsqrt(jnp.maximum(_segsum(kk * kk, seg), 1e-24))
    k2 = k * (1.0 + (a - 1.0) * ka)
    a_s[...] = -kk
    b_s[...] = kk * a
    w_s[...] = decay
    k_s[...] = k2
    r_s[...] = r
    v_s[...] = v

    n = HEAD_DIM * g8
    slab = (HEAD_DIM, g8, GROUP_WIDTH)

    def readout(s, row):
        yb = _dot((s * r_s[pl.ds(row, g8), :][None]).reshape(n, GROUP_WIDTH).astype(BF16), seg_s[...])
        return jnp.sum(yb.reshape(slab) * lm_s[...], axis=0)

    def step(t, carry):
        r0 = pl.multiple_of(t * g8, g8)
        rp = pl.multiple_of(jnp.maximum(t - 1, 0) * g8, g8)
        s = s_s[...]
        lm = lm_s[...]
        sg = seg_s[...]
        y_s[pl.ds(rp, g8), :] = readout(s, rp)
        sa = _dot((s * a_s[pl.ds(r0, g8), :][None]).reshape(n, GROUP_WIDTH).astype(BF16), sg).reshape(slab)
        vh, vl = _split2(v_s[pl.ds(r0, g8), :])
        vm = jnp.concatenate([(vh.astype(F32)[None] * lm).reshape(n, GROUP_WIDTH).astype(BF16),
                              (vl.astype(F32)[None] * lm).reshape(n, GROUP_WIDTH).astype(BF16)], axis=0)
        vb2 = _dot(vm, sg)
        vb = (vb2[0:n] + vb2[n:]).reshape(slab)
        s_s[...] = (s * w_s[pl.ds(r0, g8), :][None] + sa * b_s[pl.ds(r0, g8), :][None]
                    + vb * k_s[pl.ds(r0, g8), :][None])
        return carry

    lax.fori_loop(0, n_tok, step, 0, unroll=4)
    last = (n_tok - 1) * g8
    y_s[pl.ds(last, g8), :] = readout(s_s[...], last)

    y = y_s[...]
    mean = _segsum(y, seg) * (1.0 / HEAD_DIM)
    yc = y - mean
    var = _segsum(yc * yc, seg) * (1.0 / HEAD_DIM)
    yn = yc * lax.rsqrt(var + RW_GN_EPS) * lnw + lnb
    bonus = _segsum(r * k2 * rk, seg) * v
    _slab_store(o_ref, (yn + bonus) * g)
    s_out_ref[0] = s_s[...]


def _rwkv_call(p_tm, shift0, s0, mu, lora, vec, n_grp, t_len):
    n_tok = min(t_len, 64)
    n_t = t_len // n_tok
    rows = n_tok * SEQ_GROUP

    def full(a):
        return pl.BlockSpec(a.shape, lambda b, i: (0,) * a.ndim)

    st_spec = pl.BlockSpec((1, HEAD_DIM, SEQ_GROUP, GROUP_WIDTH), lambda b, i: (b, 0, 0, 0))
    vm = lambda shape, dt=F32: pltpu.VMEM(shape, dt)
    out, s_out = pl.pallas_call(
        functools.partial(_rwkv_kernel, n_tok=n_tok),
        grid=(n_grp, n_t),
        in_specs=[pl.BlockSpec((None, RW_COLS // LANES, rows, LANES), lambda b, i: (b, 0, i, 0)),
                  pl.BlockSpec((1, SEQ_GROUP, RW_COLS), lambda b, i: (b, 0, 0)),
                  st_spec, full(mu), full(lora), full(vec)],
        out_specs=(pl.BlockSpec((None, GROUP_WIDTH // LANES, rows, LANES), lambda b, i: (b, 0, i, 0)), st_spec),
        out_shape=(jax.ShapeDtypeStruct((n_grp, GROUP_WIDTH // LANES, t_len * SEQ_GROUP, LANES), F32),
                   jax.ShapeDtypeStruct((n_grp, HEAD_DIM, SEQ_GROUP, GROUP_WIDTH), F32)),
        scratch_shapes=[vm((HEAD_DIM, SEQ_GROUP, GROUP_WIDTH)), vm((SEQ_GROUP, RW_COLS)),
                        vm((GROUP_WIDTH, GROUP_WIDTH), BF16), vm((HEAD_DIM, SEQ_GROUP, GROUP_WIDTH)),
                        vm((rows, GROUP_WIDTH)), vm((rows, GROUP_WIDTH)), vm((rows, GROUP_WIDTH)),
                        vm((rows, GROUP_WIDTH)), vm((rows, GROUP_WIDTH)), vm((rows, GROUP_WIDTH)),
                        vm((rows, GROUP_WIDTH))],
        compiler_params=_cparams(2),
        name="rwkv_t%d" % t_len,
    )(p_tm, shift0, s0, mu, lora, vec)
    return out, s_out


def _s5_prep_kernel(are_ref, aim_ref, ldt_ref, bre_ref, bim_ref, abre_ref, abim_ref, bbre_ref, bbim_ref):
    a_re, a_im = are_ref[...], aim_ref[...]
    dt = jnp.exp(ldt_ref[...])
    mag = jnp.exp(a_re * dt)
    ab_re = mag * jnp.cos(a_im * dt)
    ab_im = mag * jnp.sin(a_im * dt)
    den = a_re * a_re + a_im * a_im
    nr, ni = ab_re - 1.0, ab_im
    coef_re = (nr * a_re + ni * a_im) / den
    coef_im = (ni * a_re - nr * a_im) / den
    b_re, b_im = bre_ref[...], bim_ref[...]
    abre_ref[...] = ab_re
    abim_ref[...] = ab_im
    bbre_ref[...] = coef_re * b_re - coef_im * b_im
    bbim_ref[...] = coef_re * b_im + coef_im * b_re


def _s5_prep(a_re, a_im, log_dt, b_re, b_im):
    rows = DEPTH * S5_NGROUPS
    cols = S5_STATE * S5_GROUP
    rep = lambda t: jnp.repeat(t.reshape(rows, S5_STATE), S5_GROUP, axis=1)
    ldt = jnp.broadcast_to(log_dt.reshape(rows, 1), (rows, cols))
    shp = jax.ShapeDtypeStruct((rows, cols), F32)
    ab_re, ab_im, bb_re, bb_im = pl.pallas_call(
        _s5_prep_kernel, out_shape=(shp, shp, shp, shp), name="s5_prep",
    )(rep(a_re), rep(a_im), ldt, b_re.reshape(rows, cols), b_im.reshape(rows, cols))
    pick = lambda t: t.reshape(DEPTH, S5_NGROUPS, S5_STATE, S5_GROUP)[..., 0].reshape(DEPTH, 1, S5_CH)
    bb = lambda t: t.reshape(DEPTH, S5_NGROUPS, S5_STATE, S5_GROUP)
    return pick(ab_re), pick(ab_im), bb(bb_re), bb(bb_im)


def _s5_kernel(u_ref, x0_ref, ab_ref, wb_ref, wc_ref, d_ref, wglu_ref, o_ref, xT_ref,
               x_s, bu_s, xs_s, *, n_tok):
    i = pl.program_id(1)
    g8 = SEQ_GROUP

    @pl.when(i == 0)
    def _():
        x_s[...] = x0_ref[0]

    u = _slab_load(u_ref)
    bu_s[...] = _dot(u.astype(BF16), wb_ref[...])
    a_re = jnp.broadcast_to(ab_ref[0:1, :], (g8, S5_CH))
    a_im = jnp.broadcast_to(ab_ref[1:2, :], (g8, S5_CH))

    def step(t, carry):
        x_re, x_im = carry
        r0 = pl.multiple_of(t * g8, g8)
        n_re = a_re * x_re - a_im * x_im + bu_s[pl.ds(r0, g8), 0:S5_CH]
        n_im = a_re * x_im + a_im * x_re + bu_s[pl.ds(r0, g8), S5_CH:]
        xs_s[pl.ds(r0, g8), 0:S5_CH] = n_re
        xs_s[pl.ds(r0, g8), S5_CH:] = n_im
        return n_re, n_im

    x_re, x_im = lax.fori_loop(0, n_tok, step, (x_s[:, 0:S5_CH], x_s[:, S5_CH:]))
    x_s[:, 0:S5_CH] = x_re
    x_s[:, S5_CH:] = x_im
    y = _dot(xs_s[...].astype(BF16), wc_ref[...]) + d_ref[...] * u
    yg = 0.5 * y * (1.0 + jnp.tanh(0.7978845608028654 * (y + 0.044715 * (y * y * y))))
    _slab_store(o_ref, yg * _sigmoid(_dot(yg.astype(BF16), wglu_ref[...])))
    xT_ref[0] = x_s[...]


def _s5_call(u_tm, x0, ab, wb, wc, dvec, wglu, n_grp, t_len):
    n_tok = min(t_len, 64)
    n_t = t_len // n_tok
    rows = n_tok * SEQ_GROUP

    def full(a):
        return pl.BlockSpec(a.shape, lambda b, i: (0,) * a.ndim)

    st_spec = pl.BlockSpec((1, SEQ_GROUP, 2 * S5_CH), lambda b, i: (b, 0, 0))
    return pl.pallas_call(
        functools.partial(_s5_kernel, n_tok=n_tok),
        grid=(n_grp, n_t),
        in_specs=[pl.BlockSpec((None, GROUP_WIDTH // LANES, rows, LANES), lambda b, i: (b, 0, i, 0)), st_spec,
                  full(ab), full(wb), full(wc), full(dvec), full(wglu)],
        out_specs=(pl.BlockSpec((None, GROUP_WIDTH // LANES, rows, LANES), lambda b, i: (b, 0, i, 0)), st_spec),
        out_shape=(jax.ShapeDtypeStruct((n_grp, GROUP_WIDTH // LANES, t_len * SEQ_GROUP, LANES), F32),
                   jax.ShapeDtypeStruct((n_grp, SEQ_GROUP, 2 * S5_CH), F32)),
        scratch_shapes=[pltpu.VMEM((SEQ_GROUP, 2 * S5_CH), F32), pltpu.VMEM((rows, 2 * S5_CH), F32),
                        pltpu.VMEM((rows, 2 * S5_CH), F32)],
        compiler_params=_cparams(2),
        name="s5_t%d" % t_len,
    )(u_tm, x0, ab, wb, wc, dvec, wglu)


CHUNKS = D_MODEL // LANES


def _store_chunked(ref, x):
    rows = x.shape[0]
    for j in range(CHUNKS):
        ref[pl.ds(j, rows, stride=CHUNKS), :] = x[:, j * LANES:(j + 1) * LANES]


def _load_chunked(ref, rows):
    return jnp.concatenate([ref[pl.ds(j, rows, stride=CHUNKS), :] for j in range(CHUNKS)], axis=1)


def _post_mix_kernel(x_ref, oap_ref, obp_ref, ocp_ref, odp_ref, oas_ref, obs_ref, ocs_ref, ods_ref,
                     wout_ref, lnw_ref, lnb_ref, rw_ref, rb_ref,
                     x1_ref, x1c_ref, idx_ref, rank_ref, gate_ref, cnt_ref, carry_s, *, prompt_tiles):
    i = pl.program_id(0)

    @pl.when(i == 0)
    def _():
        carry_s[...] = jnp.zeros_like(carry_s)

    is_prompt = i < prompt_tiles
    seq = i % SEQ_GROUP
    pick = lambda p, s_ref: jnp.where(is_prompt, p, s_ref[...]).astype(BF16)
    mix = _dot(pick(_slab_load_rows(oap_ref, seq, ROW_TILE), oas_ref), wout_ref[0:256, :])
    mix += _dot(pick(obp_ref[...], obs_ref), wout_ref[256:512, :])
    mix += _dot(pick(ocp_ref[...], ocs_ref), wout_ref[512:768, :])
    mix += _dot(pick(_slab_load_rows(odp_ref, seq, ROW_TILE), ods_ref), wout_ref[768:1024, :])
    x1 = _layer_norm(DN_ALPHA * x_ref[...] + mix, lnw_ref[...], lnb_ref[...])
    x1_ref[...] = x1
    _store_chunked(x1c_ref, x1)

    xh, xl = _split2(x1)
    logits = _dot(xh, rw_ref[0]) + (_dot(xh, rw_ref[1]) + _dot(xl, rw_ref[0])) + rb_ref[...]
    rows = logits.shape[0]
    lane = lax.broadcasted_iota(I32, (rows, LANES), 1)
    work = logits
    sel_i, sel_v = [], []
    for _ in range(TOP_K):
        m = jnp.max(work, axis=-1, keepdims=True)
        j = jnp.min(jnp.where(work == m, lane, LANES), axis=-1, keepdims=True)
        sel_i.append(j)
        sel_v.append(m)
        work = jnp.where(lane == j, -jnp.inf, work)
    e = [jnp.exp(v - sel_v[0]) for v in sel_v]
    den = (e[0] + e[1]) + (e[2] + e[3])
    onehot = jnp.zeros((rows, LANES), F32)
    for j in sel_i:
        onehot = onehot + (lane == j).astype(F32)
    rr = lax.broadcasted_iota(I32, (rows, rows), 0)
    cc = lax.broadcasted_iota(I32, (rows, rows), 1)
    before = _dot((cc < rr).astype(BF16), onehot.astype(BF16)) + carry_s[0:1, :]
    carry_s[0:1, :] = carry_s[0:1, :] + jnp.sum(onehot, axis=0, keepdims=True)
    idx_o = jnp.zeros((rows, LANES), I32)
    rank_o = jnp.zeros((rows, LANES), I32)
    gate_o = jnp.zeros((rows, LANES), F32)
    for slot in range(TOP_K):
        j = sel_i[slot]
        rank = jnp.sum(jnp.where(lane == j, before, 0.0), axis=-1, keepdims=True)
        idx_o = jnp.where(lane == slot, j, idx_o)
        rank_o = jnp.where(lane == slot, rank.astype(I32), rank_o)
        gate_o = jnp.where(lane == slot, e[slot] / den, gate_o)
    idx_ref[...] = idx_o
    rank_ref[...] = rank_o
    gate_ref[...] = gate_o
    cnt_ref[...] = jnp.broadcast_to(carry_s[0:1, :], cnt_ref.shape)


def _post_mix(x_all, outs_p, outs_s, n_seq_p, t_len_p, wout_l, lnw, lnb, rw_l, rb_l):
    n = x_all.shape[0]
    n_t = t_len_p // ROW_TILE
    prompt_tiles = n_seq_p * n_t
    w = GROUP_WIDTH
    p_idx = lambda i: jnp.minimum(i, prompt_tiles - 1)
    p_row = lambda i: (p_idx(i) % n_seq_p) * n_t + p_idx(i) // n_seq_p
    x_row = lambda i: jnp.where(i < prompt_tiles, p_row(i), i)
    row = lambda width: pl.BlockSpec((ROW_TILE, width), lambda i: (x_row(i), 0))
    tm_p = pl.BlockSpec((w // LANES, ROW_TILE * SEQ_GROUP, LANES), lambda i: (0, p_idx(i) // n_seq_p, 0))
    bm_p = pl.BlockSpec((ROW_TILE, w), lambda i: (p_row(i), 0))
    bm_s = pl.BlockSpec((ROW_TILE, w), lambda i: (jnp.maximum(i - prompt_tiles, 0), 0))

    def full(a):
        return pl.BlockSpec(a.shape, lambda i: (0,) * a.ndim)

    meta = jax.ShapeDtypeStruct((n, LANES), I32)
    return pl.pallas_call(
        functools.partial(_post_mix_kernel, prompt_tiles=prompt_tiles),
        grid=(n // ROW_TILE,),
        in_specs=[row(D_MODEL), tm_p, bm_p, bm_p, tm_p, bm_s, bm_s, bm_s, bm_s, full(wout_l), full(lnw), full(lnb),
                  full(rw_l), full(rb_l)],
        out_specs=(row(D_MODEL), pl.BlockSpec((ROW_TILE * CHUNKS, LANES), lambda i: (x_row(i), 0)),
                   row(LANES), row(LANES), row(LANES), pl.BlockSpec((SUBLANES, LANES), lambda i: (0, 0))),
        out_shape=(jax.ShapeDtypeStruct((n, D_MODEL), F32), jax.ShapeDtypeStruct((n * CHUNKS, LANES), F32),
                   meta, meta, jax.ShapeDtypeStruct((n, LANES), F32),
                   jax.ShapeDtypeStruct((SUBLANES, LANES), F32)),
        scratch_shapes=[pltpu.VMEM((SUBLANES, LANES), F32)],
        compiler_params=_cparams(1),
        name="post_mix",
    )(x_all, *outs_p, *outs_s, wout_l, lnw, lnb, rw_l, rb_l)


def _dispatch_kernel(gend_ref, dest_ref, x_ref, xs_hbm, zero_s, sem, *, tokens, n_tiles):
    i = pl.program_id(0)

    def zero_tile(first_row):
        start = pl.multiple_of(first_row * CHUNKS, MOE_TILE * CHUNKS)
        return pltpu.make_async_copy(zero_s, xs_hbm.at[pl.ds(start, MOE_TILE * CHUNKS)], sem)

    @pl.when(i == 0)
    def _():
        zero_s[...] = jnp.zeros_like(zero_s)
        for e in range(N_EXPERTS):
            @pl.when(gend_ref[e + 1] > gend_ref[e])
            def _():
                zero_tile(gend_ref[e + 1] - MOE_TILE).start()
        for e in range(N_EXPERTS):
            @pl.when(gend_ref[e + 1] > gend_ref[e])
            def _():
                zero_tile(gend_ref[e + 1] - MOE_TILE).wait()

        def tail(t, carry):
            cp = zero_tile(t * MOE_TILE)
            cp.start()
            cp.wait()
            return carry

        lax.fori_loop(gend_ref[N_EXPERTS] // MOE_TILE, n_tiles, tail, 0)

    def row_copy(n, slot):
        src = pl.multiple_of(n * CHUNKS, CHUNKS)
        dst = pl.multiple_of(dest_ref[n * TOP_K + slot] * CHUNKS, CHUNKS)
        return pltpu.make_async_copy(x_ref.at[pl.ds(src, CHUNKS)], xs_hbm.at[pl.ds(dst, CHUNKS)], sem)

    def issue(n, carry):
        for slot in range(TOP_K):
            row_copy(n, slot).start(priority=slot % 2)
        return carry

    lax.fori_loop(0, tokens, issue, 0, unroll=8)
    for slot in range(TOP_K):
        pltpu.make_async_copy(x_ref, xs_hbm.at[pl.ds(0, tokens * CHUNKS)], sem).wait()


def _dispatch(x1c, dest_flat, gend, n_rows):
    n = x1c.shape[0] // CHUNKS
    tokens = DISPATCH_TOKENS if n % DISPATCH_TOKENS == 0 else ROW_TILE
    return pl.pallas_call(
        functools.partial(_dispatch_kernel, tokens=tokens, n_tiles=n_rows // MOE_TILE),
        grid_spec=pltpu.PrefetchScalarGridSpec(
            num_scalar_prefetch=1,
            grid=(n // tokens,),
            in_specs=[pl.BlockSpec((tokens * TOP_K,), lambda i, ge: (i,), memory_space=pltpu.SMEM),
                      pl.BlockSpec((tokens * CHUNKS, LANES), lambda i, ge: (i, 0))],
            out_specs=pl.BlockSpec(memory_space=pl.ANY),
            scratch_shapes=[pltpu.VMEM((MOE_TILE * CHUNKS, LANES), F32), pltpu.SemaphoreType.DMA(())],
        ),
        out_shape=jax.ShapeDtypeStruct((n_rows * CHUNKS, LANES), F32),
        compiler_params=_cparams(1),
        name="moe_dispatch",
    )(gend, dest_flat, x1c)


PAIR_BLOCK = 2 * LANES


def _expert_prep_kernel(w1_ref, w2_ref, w1p_ref, w2b_ref):
    src = lax.broadcasted_iota(I32, (PAIR_BLOCK, PAIR_BLOCK), 0)
    dst = lax.broadcasted_iota(I32, (PAIR_BLOCK, PAIR_BLOCK), 1)
    perm = (src == jnp.where(dst < LANES, 2 * dst, 2 * (dst - LANES) + 1)).astype(BF16)
    for c in range(2 * D_FF // PAIR_BLOCK):
        cols = slice(c * PAIR_BLOCK, (c + 1) * PAIR_BLOCK)
        w1p_ref[:, cols] = _dot(w1_ref[:, cols].astype(BF16), perm).astype(BF16)
    w2b_ref[...] = w2_ref[...].astype(BF16)


def _expert_prep(exp_w1, exp_w2):
    n_l, n_e = exp_w1.shape[:2]
    spec = lambda r, c: pl.BlockSpec((None, None, r, c), lambda i: (i // n_e, i % n_e, 0, 0))
    return pl.pallas_call(
        _expert_prep_kernel,
        grid=(n_l * n_e,),
        in_specs=[spec(D_MODEL, 2 * D_FF), spec(D_FF, D_MODEL)],
        out_specs=(spec(D_MODEL, 2 * D_FF), spec(D_FF, D_MODEL)),
        out_shape=(jax.ShapeDtypeStruct(exp_w1.shape, BF16), jax.ShapeDtypeStruct(exp_w2.shape, BF16)),
        compiler_params=_cparams(1),
        name="expert_prep",
    )(exp_w1, exp_w2)


def _expert_kernel(te_ref, nreal_ref, xs_ref, w1_ref, b1_ref, w2_ref, b2_ref, o_ref, act_s):
    i = pl.program_id(0)
    nreal = nreal_ref[0]

    def hidden(slot):
        x = _load_chunked(xs_ref, MOE_TILE).astype(BF16)
        for c in range(D_FF // PAIR_BLOCK):
            cols = slice(2 * c * PAIR_BLOCK, 2 * (c + 1) * PAIR_BLOCK)
            h = _dot(x, w1_ref[:, cols]) + b1_ref[:, cols]
            h_glu = jnp.minimum(jnp.concatenate([h[:, 0:128], h[:, 256:384]], axis=1), SWIGLU_LIMIT)
            h_lin = jnp.clip(jnp.concatenate([h[:, 128:256], h[:, 384:512]], axis=1),
                             -SWIGLU_LIMIT, SWIGLU_LIMIT)
            act = h_glu * _sigmoid(SWIGLU_ALPHA * h_glu) * (h_lin + 1.0)
            act_s[slot, :, c * PAIR_BLOCK:(c + 1) * PAIR_BLOCK] = act.astype(BF16)

    def output(slot):
        _store_chunked(o_ref, _dot(act_s[slot], w2_ref[...]) + b2_ref[...])

    @pl.when(i == 0)
    def _():
        hidden(0)

    @pl.when((i > 0) & (i < nreal))
    def _():
        output((i - 1) % 2)
        hidden(i % 2)

    @pl.when((i > 0) & (i == nreal))
    def _():
        output((i - 1) % 2)

    @pl.when(i > nreal)
    def _():
        o_ref[...] = jnp.zeros_like(o_ref)


def _experts(xs, te, nreal, w1d, b1d, w2b, b2, layer):
    n_tiles = xs.shape[0] // (MOE_TILE * CHUNKS)
    cur = lambda i: jnp.minimum(i, n_tiles - 1)
    prev = lambda i: jnp.maximum(i - 1, 0)
    return pl.pallas_call(
        _expert_kernel,
        grid_spec=pltpu.PrefetchScalarGridSpec(
            num_scalar_prefetch=2,
            grid=(n_tiles + 1,),
            in_specs=[pl.BlockSpec((MOE_TILE * CHUNKS, LANES), lambda i, te, nr: (jnp.minimum(i, nr[0] - 1), 0)),
                      pl.BlockSpec((None, None, D_MODEL, 2 * D_FF), lambda i, te, nr: (layer, te[cur(i)], 0, 0)),
                      pl.BlockSpec((None, None, 1, 2 * D_FF), lambda i, te, nr: (layer, te[cur(i)], 0, 0)),
                      pl.BlockSpec((None, None, D_FF, D_MODEL), lambda i, te, nr: (layer, te[prev(i)], 0, 0)),
                      pl.BlockSpec((None, None, 1, D_MODEL), lambda i, te, nr: (layer, te[prev(i)], 0, 0))],
            out_specs=pl.BlockSpec((MOE_TILE * CHUNKS, LANES), lambda i, te, nr: (prev(i), 0)),
            scratch_shapes=[pltpu.VMEM((2, MOE_TILE, D_FF), BF16)],
        ),
        out_shape=jax.ShapeDtypeStruct(xs.shape, F32),
        compiler_params=_cparams(1),
        name="moe_experts",
    )(te, nreal, xs, w1d, b1d, w2b, b2)


def _combine_kernel(dest_ref, dest_next_ref, gate_ref, x1_ref, lnw_ref, lnb_ref, ys_hbm, o_ref, buf_s, sem,
                    *, tokens):
    i = pl.program_id(0)
    cur = i % 2

    def fetch(dref, half):
        def issue(n, carry):
            dst = pl.multiple_of(n * CHUNKS, CHUNKS)
            for slot in range(TOP_K):
                src = pl.multiple_of(dref[n * TOP_K + slot] * CHUNKS, CHUNKS)
                pltpu.make_async_copy(ys_hbm.at[pl.ds(src, CHUNKS)], buf_s.at[half, slot, pl.ds(dst, CHUNKS)],
                                      sem.at[half]).start(priority=slot % 2)
            return carry

        lax.fori_loop(0, tokens, issue, 0, unroll=8)

    @pl.when(i == 0)
    def _():
        fetch(dest_ref, 0)

    @pl.when(i + 1 < pl.num_programs(0))
    def _():
        fetch(dest_next_ref, 1 - cur)

    for slot in range(TOP_K):
        pltpu.make_async_copy(ys_hbm.at[pl.ds(0, tokens * CHUNKS)], buf_s.at[cur, slot], sem.at[cur]).wait()
    gate = gate_ref[...]
    ffn = gate[:, 0:1] * _load_chunked(buf_s.at[cur, 0], tokens)
    for slot in range(1, TOP_K):
        ffn = ffn + gate[:, slot:slot + 1] * _load_chunked(buf_s.at[cur, slot], tokens)
    o_ref[...] = _layer_norm(DN_ALPHA * x1_ref[...] + ffn, lnw_ref[...], lnb_ref[...])


def _combine(dest_flat, gate, x1, lnw, lnb, ys):
    n = x1.shape[0]
    tokens = ROW_TILE
    last = n // tokens - 1
    return pl.pallas_call(
        functools.partial(_combine_kernel, tokens=tokens),
        grid=(n // tokens,),
        in_specs=[pl.BlockSpec((tokens * TOP_K,), lambda i: (i,), memory_space=pltpu.SMEM),
                  pl.BlockSpec((tokens * TOP_K,), lambda i: (jnp.minimum(i + 1, last),), memory_space=pltpu.SMEM),
                  pl.BlockSpec((tokens, LANES), lambda i: (i, 0)),
                  pl.BlockSpec((tokens, D_MODEL), lambda i: (i, 0)),
                  pl.BlockSpec((1, D_MODEL), lambda i: (0, 0)),
                  pl.BlockSpec((1, D_MODEL), lambda i: (0, 0)),
                  pl.BlockSpec(memory_space=pl.ANY)],
        out_specs=pl.BlockSpec((tokens, D_MODEL), lambda i: (i, 0)),
        out_shape=jax.ShapeDtypeStruct((n, D_MODEL), F32),
        scratch_shapes=[pltpu.VMEM((2, TOP_K, tokens * CHUNKS, LANES), F32), pltpu.SemaphoreType.DMA((2,))],
        compiler_params=_cparams(1),
        name="moe_combine",
    )(dest_flat, dest_flat, gate, x1, lnw, lnb, ys)


def _moe(x1, x1c, idx, rank, gate, counts, w1d, b1d, w2b, b2, lnw, lnb, layer):
    n = x1.shape[0]
    n_tiles = -(-(n * TOP_K + N_EXPERTS * (MOE_TILE - 1)) // MOE_TILE)
    cnt = counts[0, :N_EXPERTS].astype(I32)
    gsz = ((cnt + (MOE_TILE - 1)) // MOE_TILE) * MOE_TILE
    gend = jnp.cumsum(gsz)
    goff = gend - gsz
    dest = (goff[idx[:, :TOP_K]] + rank[:, :TOP_K]).reshape(-1)
    gend0 = jnp.concatenate([jnp.zeros((1,), I32), gend])
    nreal = (gend[-1:] // MOE_TILE).astype(I32)
    tile_start = jnp.arange(n_tiles, dtype=I32) * MOE_TILE
    te = jnp.minimum(jnp.sum((gend[None, :] <= tile_start[:, None]).astype(I32), axis=1), N_EXPERTS - 1)
    xs = _dispatch(x1c, dest, gend0, n_tiles * MOE_TILE)
    ys = _experts(xs, te, nreal, w1d, b1d, w2b, b2, layer)
    return _combine(dest, gate, x1, lnw, lnb, ys)


def _to_time_major(rows, n_seq, t_len):
    c = rows.shape[-1]
    x = rows.reshape(n_seq // SEQ_GROUP, SEQ_GROUP, t_len, c)
    return jnp.transpose(x, (0, 2, 1, 3)).reshape(n_seq // SEQ_GROUP, t_len * SEQ_GROUP, c)


def _from_time_major(x, n_seq, t_len):
    c = x.shape[-1]
    x = x.reshape(n_seq // SEQ_GROUP, t_len, SEQ_GROUP, c)
    return jnp.transpose(x, (0, 2, 1, 3)).reshape(n_seq * t_len, c)


def _block_diag_state(s, dk):
    st = jnp.swapaxes(s, 2, 3)
    eye = jnp.eye(N_HEADS, dtype=s.dtype)
    return jnp.einsum("bhvk,hg->bhvgk", st, eye).reshape(s.shape[0], GROUP_WIDTH, N_HEADS * dk)


def _unblock_state(st, dk):
    b = st.shape[0]
    x = st.reshape(b, N_HEADS, HEAD_DIM, N_HEADS, dk)
    x = jnp.stack([x[:, h, :, h, :] for h in range(N_HEADS)], axis=1)
    return jnp.swapaxes(x, 2, 3)


def _rwkv_state_in(s):
    b = s.shape[0]
    x = s.reshape(b // SEQ_GROUP, SEQ_GROUP, N_HEADS, HEAD_DIM, HEAD_DIM)
    return jnp.transpose(x, (0, 3, 1, 2, 4)).reshape(b // SEQ_GROUP, HEAD_DIM, SEQ_GROUP, GROUP_WIDTH)


def _rwkv_state_out(x, b):
    x = x.reshape(b // SEQ_GROUP, HEAD_DIM, SEQ_GROUP, N_HEADS, HEAD_DIM)
    return jnp.transpose(x, (0, 2, 3, 1, 4)).reshape(b, N_HEADS, HEAD_DIM, HEAD_DIM)


def _pad_rows(w, row0, n_rows):
    out = jnp.zeros((n_rows, w.shape[1]), w.dtype)
    return out.at[row0:row0 + w.shape[0]].set(w)


def kernel(x_prompt, x_sample, state_rwkv, state_rwkv_shift, state_hgrn, state_gla, state_s5_re, state_s5_im,
           w_in, rw_mu, rw_w0, rw_w2, rw_a0, rw_a2, rw_g2, rw_kk, rw_ka, rw_rk, rw_lnx_w, rw_lnx_b,
           hg_lb_logits, hg_norm_w, gla_w_gk2, gla_b_gk, gla_norm_w,
           s5_A_re, s5_A_im, s5_log_dt, s5_B_re, s5_B_im, s5_C_re, s5_C_im, s5_D, s5_w_glu,
           w_out, ln1_w, ln1_b, router_w, router_b, exp_w1, exp_b1, exp_w2, exp_b2, ln2_w, ln2_b):
    bp, tp, _ = x_prompt.shape
    bs, ts, _ = x_sample.shape
    n_p, n_s = bp * tp, bs * ts
    groups = ((0, bp, tp), (n_p, bs, ts))
    assert bp == SEQ_GROUP and tp % ROW_TILE == 0 and bs % SEQ_GROUP == 0 and n_s % ROW_TILE == 0
    assert ts <= SUB_CHUNK

    c = np.cumsum([0, RW_COLS, 1024, 784, 256])
    rw_c, hg_c, gl_c, s5_c = (w_in[:, :, c[j]:c[j + 1]] for j in range(4))
    gl_q, gl_k, gl_v, gl_lo, gl_g = (gl_c[:, :, a:b] for a, b in
                                     ((0, 128), (128, 256), (256, 512), (512, 528), (528, 784)))
    zpad = jnp.zeros((DEPTH, D_MODEL, 128 - GLA_GK_LORA), w_in.dtype)
    w_in_p = jnp.concatenate([hg_c, gl_v, gl_g, gl_q, gl_k, gl_lo, zpad, rw_c, s5_c], axis=2).astype(BF16)
    w_out_b = w_out.astype(BF16)
    w1d, w2b = _expert_prep(exp_w1, exp_w2)
    b1d = jnp.swapaxes(exp_b1.reshape(DEPTH, N_EXPERTS, 2 * D_FF // PAIR_BLOCK, LANES, 2), -1, -2)
    b1d = b1d.reshape(DEPTH, N_EXPERTS, 1, 2 * D_FF)
    b2r = exp_b2[:, :, None, :]
    rw_pad = jnp.pad(router_w, ((0, 0), (0, 0), (0, LANES - N_EXPERTS)))
    rw_hi = rw_pad.astype(BF16)
    rw_lo = (rw_pad - rw_hi.astype(F32)).astype(BF16)
    rw_split = jnp.stack([rw_hi, rw_lo], axis=1)
    rb_pad = jnp.pad(router_b, ((0, 0), (0, LANES - N_EXPERTS)), constant_values=-1e30)[:, None, :]

    lbs = jnp.cumsum(jax.nn.softmax(hg_lb_logits.astype(F32), axis=0), axis=0)
    lbs = lbs - lbs[:1]
    lb3 = jnp.stack([lbs, jnp.log(lbs), jnp.log1p(-lbs)], axis=1)
    lb3 = jnp.pad(lb3, ((0, 0), (0, SUBLANES - 3), (0, 0)))

    ab_re, ab_im, bb_re, bb_im = _s5_prep(s5_A_re, s5_A_im, s5_log_dt, s5_B_re, s5_B_im)
    eye_g = jnp.eye(S5_NGROUPS, dtype=F32)
    wb = jnp.stack([jnp.einsum("lgph,gk->lghkp", t, eye_g).reshape(DEPTH, GROUP_WIDTH, S5_CH)
                    for t in (bb_re, bb_im)], axis=2).reshape(DEPTH, GROUP_WIDTH, 2 * S5_CH).astype(BF16)
    wc = jnp.concatenate([jnp.einsum("lghp,gk->lgpkh", t, eye_g).reshape(DEPTH, S5_CH, GROUP_WIDTH)
                          for t in (s5_C_re, -s5_C_im)], axis=1).astype(BF16)
    ab = jnp.concatenate([ab_re, ab_im], axis=1)
    ab = jnp.pad(ab, ((0, 0), (0, SUBLANES - 2), (0, 0)))
    wglu_b = s5_w_glu.astype(BF16)

    lora = jnp.stack([jnp.stack([_pad_rows(rw_w2[l], 0, 128), _pad_rows(rw_a2[l], 32, 128),
                                 _pad_rows(rw_g2[l], 64, 128)]) for l in range(DEPTH)])
    rw_vec = jnp.stack([rw_w0, rw_a0, rw_kk, rw_ka, rw_rk, rw_lnx_w, rw_lnx_b, jnp.zeros_like(rw_w0)], axis=1)
    wgk = jnp.stack([_pad_rows(gla_w_gk2[l], 0, 128) for l in range(DEPTH)])

    zeros = lambda shape: jnp.zeros(shape, F32)
    st_in = (
        dict(rw=zeros((DEPTH, bp, N_HEADS, HEAD_DIM, HEAD_DIM)), sh=zeros((DEPTH, bp, RW_COLS)),
             hg=zeros((DEPTH, bp, N_HEADS, HG_DK, HEAD_DIM)), gl=zeros((DEPTH, bp, N_HEADS, GLA_DK, HEAD_DIM)),
             re=zeros((DEPTH, bp, S5_NGROUPS, S5_STATE)), im=zeros((DEPTH, bp, S5_NGROUPS, S5_STATE))),
        dict(rw=state_rwkv, sh=state_rwkv_shift, hg=state_hgrn, gl=state_gla, re=state_s5_re, im=state_s5_im),
    )
    collected = ([], [])

    x_all = jnp.concatenate([x_prompt.reshape(n_p, D_MODEL), x_sample.reshape(n_s, D_MODEL)], axis=0)
    for l in range(DEPTH):
        h_gate_p, p_tm_p, u_tm_p = _inproj_prompt(x_all, w_in_p[l], bp, tp)
        h_s = _inproj_rows(x_all, w_in_p[l], n_p, n_s)
        outs = ([], [])
        for gi, (row0, n_seq, t_len) in enumerate(groups):
            st = st_in[gi]
            n_grp = n_seq // SEQ_GROUP
            if gi == 0:
                h_gate = h_gate_p
                p_tm = p_tm_p[None]
                u_tm = u_tm_p[None]
                new_sh = _from_slabs(p_tm_p[:, (t_len - 1) * n_seq:, :])
                to_rows = lambda x_tm: x_tm[0]
            else:
                h_gate = h_s
                p_rw = h_s[:, OFF_RW:OFF_RW + RW_COLS]
                p_tm = _to_slabs(_to_time_major(p_rw, n_seq, t_len))
                u_tm = _to_slabs(_to_time_major(h_s[:, OFF_S5:OFF_S5 + GROUP_WIDTH], n_seq, t_len))
                new_sh = p_rw.reshape(n_seq, t_len, RW_COLS)[:, -1]
                to_rows = lambda x_tm: _from_time_major(_from_slabs(x_tm), n_seq, t_len)
            sh0 = st["sh"][l].reshape(n_grp, SEQ_GROUP, RW_COLS)
            oa_tm, s_rw = _rwkv_call(p_tm, sh0, _rwkv_state_in(st["rw"][l]), rw_mu[l][None, :], lora[l],
                                     rw_vec[l], n_grp, t_len)
            new_rw = _rwkv_state_out(s_rw, n_seq)
            ob, oc, s_hg, s_gl = _gated_call(
                h_gate, n_seq, t_len, _block_diag_state(st["hg"][l], HG_DK), _block_diag_state(st["gl"][l], GLA_DK),
                (lb3[l], hg_norm_w[l][None, :]), (wgk[l], gla_b_gk[l][None, :], gla_norm_w[l][None, :]))
            x0 = jnp.concatenate([st["re"][l].reshape(n_grp, SEQ_GROUP, S5_CH),
                                  st["im"][l].reshape(n_grp, SEQ_GROUP, S5_CH)], axis=-1)
            od_tm, x_t = _s5_call(u_tm, x0, ab[l], wb[l], wc[l], s5_D[l][None, :], wglu_b[l], n_grp, t_len)
            outs[gi].extend([to_rows(oa_tm), ob, oc, to_rows(od_tm)])
            x_t = x_t.reshape(n_seq, 2, S5_NGROUPS, S5_STATE)
            collected[gi].append((new_rw, new_sh, _unblock_state(s_hg, HG_DK), _unblock_state(s_gl, GLA_DK),
                                  x_t[:, 0], x_t[:, 1]))
        x1, x1c, idx, rank, gate, counts = _post_mix(x_all, outs[0], outs[1], bp, tp, w_out_b[l],
                                                     ln1_w[l][None, :], ln1_b[l][None, :],
                                                     rw_split[l], rb_pad[l])
        x_all = _moe(x1, x1c, idx, rank, gate, counts, w1d, b1d, w2b, b2r, ln2_w[l][None, :], ln2_b[l][None, :], l)

    y_prompt = x_all[:n_p].reshape(bp, tp, D_MODEL)
    y_sample = x_all[n_p:].reshape(bs, ts, D_MODEL)
    ps = [jnp.stack([layer[j] for layer in collected[0]]) for j in range(6)]
    ss = [jnp.stack([layer[j] for layer in collected[1]]) for j in range(6)]
    return (y_prompt, y_sample, *ps, *ss)
```

```python
import functools

import jax
import jax.numpy as jnp
import numpy as np
from jax import lax
from jax.experimental import pallas as pl
from jax.experimental.pallas import tpu as pltpu

F32 = jnp.float32
BF16 = jnp.bfloat16
I32 = jnp.int32

D_MODEL = 1024
DEPTH = 4
GROUP_WIDTH = 256
HEAD_DIM = 64
N_HEADS = 4
RW_COLS = 896
RW_GN_EPS = 64e-5
HG_DK = 64
GLA_DK = 32
GLA_GK_LORA = 16
GLA_GATE_NORM = 16.0
S5_NGROUPS = 16
S5_GROUP = 16
S5_STATE = 64
S5_CH = S5_NGROUPS * S5_STATE
N_EXPERTS = 32
TOP_K = 4
D_FF = 1024
SWIGLU_ALPHA = 1.702
SWIGLU_LIMIT = 7.0
DN_ALPHA = (2.0 * DEPTH) ** 0.25
LN_EPS = 1e-5

SUBLANES = 8
LANES = 128
VMEM_LIMIT_BYTES = 56 * 1024 * 1024

OFF_HG = 0
OFF_GLA_VG = 1024
OFF_GLA_QK = 1536
OFF_RW = 1920
OFF_S5 = 2816
IN_PAD = 3072

ROW_TILE = 256
POST_TILE = 512
SUB_CHUNK = 16
SEQ_GROUP = SUBLANES
MOE_TILE = 512
DISPATCH_TOKENS = 512


def _cparams(n_axes):
    return pltpu.CompilerParams(dimension_semantics=("arbitrary",) * n_axes,
                                vmem_limit_bytes=VMEM_LIMIT_BYTES)


def _dot(a, b):
    return jnp.dot(a, b, preferred_element_type=F32)


def _split2(x):
    hi = x.astype(BF16)
    lo = (x - hi.astype(F32)).astype(BF16)
    return hi, lo


def _split3(x):
    hi = x.astype(BF16)
    r = x - hi.astype(F32)
    mid = r.astype(BF16)
    lo = (r - mid.astype(F32)).astype(BF16)
    return hi, mid, lo


def _dot3(a, b):
    ah, al = _split2(a)
    bh, bl = _split2(b)
    return _dot(ah, bh) + (_dot(ah, bl) + _dot(al, bh))


def _seg_ones(n_in, seg_in, n_out, seg_out):
    r = lax.broadcasted_iota(I32, (n_in, n_out), 0) // seg_in
    c = lax.broadcasted_iota(I32, (n_in, n_out), 1) // seg_out
    return (r == c).astype(BF16)


def _segsum(x, seg):
    rows = x.shape[0]
    hi, lo = _split2(x)
    both = _dot(jnp.concatenate([hi, lo], axis=0), seg)
    return both[:rows] + both[rows:]


def _sigmoid(x):
    return 1.0 / (1.0 + jnp.exp(-x))


def _log_sigmoid(x):
    return jnp.minimum(x, 0.0) - jnp.log1p(jnp.exp(-jnp.abs(x)))


def _softplus(x):
    return jnp.maximum(x, 0.0) + jnp.log1p(jnp.exp(-jnp.abs(x)))


def _layer_norm(x, w, b):
    xc = x - jnp.mean(x, axis=-1, keepdims=True)
    var = jnp.mean(xc * xc, axis=-1, keepdims=True)
    return xc * lax.rsqrt(var + LN_EPS) * w + b


def _project3(x, w_ref, cols):
    xh, xl = _split2(x)
    return _dot(xh, w_ref[0, :, cols]) + (_dot(xl, w_ref[0, :, cols]) + _dot(xh, w_ref[1, :, cols]))


def _inproj_all(x_ref, w_ref):
    x = x_ref[...]
    return jnp.concatenate([_project3(x, w_ref, slice(0, OFF_S5)),
                            _dot(x.astype(BF16), w_ref[0, :, OFF_S5:IN_PAD])], axis=1)


def _inproj_kernel(x_ref, w_ref, *o_refs):
    h = _inproj_all(x_ref, w_ref)
    col = 0
    for o_ref in o_refs:
        o_ref[...] = h[:, col:col + o_ref.shape[-1]]
        col += o_ref.shape[-1]


def _slab_store_rows(ref, seq, x):
    for j in range(ref.shape[0]):
        ref.at[j][pl.ds(seq, x.shape[0], stride=SEQ_GROUP), :] = x[:, j * LANES:(j + 1) * LANES]


def _slab_load_rows(ref, seq, tokens):
    return jnp.concatenate([ref.at[j][pl.ds(seq, tokens, stride=SEQ_GROUP), :] for j in range(ref.shape[0])],
                           axis=1)


def _slab_load(ref):
    return jnp.concatenate([ref[j] for j in range(ref.shape[0])], axis=1)


def _slab_store(ref, x):
    for j in range(ref.shape[0]):
        ref[j] = x[:, j * LANES:(j + 1) * LANES]


def _to_slabs(x):
    s = x.reshape(x.shape[:-1] + (x.shape[-1] // LANES, LANES))
    return jnp.swapaxes(s, -2, -3)


def _from_slabs(x):
    s = jnp.swapaxes(x, -2, -3)
    return s.reshape(s.shape[:-2] + (s.shape[-2] * LANES,))


def _inproj_prompt_kernel(x_ref, w_ref, g_ref, p_ref, u_ref):
    seq = pl.program_id(1)
    h = _inproj_all(x_ref, w_ref)
    g_ref[...] = h[:, 0:OFF_RW]
    _slab_store_rows(p_ref, seq, h[:, OFF_RW:OFF_RW + RW_COLS])
    _slab_store_rows(u_ref, seq, h[:, OFF_S5:OFF_S5 + GROUP_WIDTH])


def _inproj_prompt(x_all, w_in_l, n_seq, t_len):
    n_t = t_len // ROW_TILE
    slab = lambda c: pl.BlockSpec((c // LANES, ROW_TILE * SEQ_GROUP, LANES), lambda i, b: (0, i, 0))
    return pl.pallas_call(
        _inproj_prompt_kernel,
        grid=(n_t, n_seq),
        in_specs=[pl.BlockSpec((ROW_TILE, D_MODEL), lambda i, b: (b * n_t + i, 0)),
                  pl.BlockSpec((2, D_MODEL, IN_PAD), lambda i, b: (0, 0, 0))],
        out_specs=(pl.BlockSpec((ROW_TILE, OFF_RW), lambda i, b: (b * n_t + i, 0)),
                   slab(RW_COLS), slab(GROUP_WIDTH)),
        out_shape=(jax.ShapeDtypeStruct((n_seq * t_len, OFF_RW), F32),
                   jax.ShapeDtypeStruct((RW_COLS // LANES, t_len * n_seq, LANES), F32),
                   jax.ShapeDtypeStruct((GROUP_WIDTH // LANES, t_len * n_seq, LANES), F32)),
        compiler_params=_cparams(2),
        name="inproj_prompt",
    )(x_all, w_in_l)


def _inproj_rows(x_all, w_in_l, row0, n_rows):
    blk0 = row0 // ROW_TILE
    return pl.pallas_call(
        _inproj_kernel,
        grid=(n_rows // ROW_TILE,),
        in_specs=[pl.BlockSpec((ROW_TILE, D_MODEL), lambda i: (blk0 + i, 0)),
                  pl.BlockSpec((2, D_MODEL, IN_PAD), lambda i: (0, 0, 0))],
        out_specs=pl.BlockSpec((ROW_TILE, IN_PAD), lambda i: (i, 0)),
        out_shape=jax.ShapeDtypeStruct((n_rows, IN_PAD), F32),
        compiler_params=_cparams(1),
        name="inproj_rows",
    )(x_all, w_in_l)


def _gated_tile(q, k, v, g, st_ref, o_ref, q_s, k_s, v_s, b_s, qh_s, kh_s, dt_s, *, dk):
    c = SUB_CHUNK
    rows, hk = q.shape
    rr = lax.broadcasted_iota(I32, (rows, rows), 0)
    cc = lax.broadcasted_iota(I32, (rows, rows), 1)
    same = (rr // c) == (cc // c)
    tri = jnp.concatenate([(same & (cc <= rr)).astype(BF16), same.astype(BF16)], axis=0)
    g3 = jnp.concatenate(_split3(g), axis=1)
    p = _dot(tri, g3)
    b = p[:rows, :hk] + p[:rows, hk:2 * hk] + p[:rows, 2 * hk:]
    btot = p[rows:, :hk] + p[rows:, hk:2 * hk] + p[rows:, 2 * hk:]
    q_s[...] = q
    k_s[...] = k
    v_s[...] = v
    b_s[...] = b
    qh_s[...] = q * jnp.exp(b)
    kh_s[...] = k * jnp.exp(btot - b)
    dt_s[...] = jnp.exp(btot)
    seg = _seg_ones(hk, dk, GROUP_WIDTH, HEAD_DIM)
    bd_mask = (lax.broadcasted_iota(I32, (GROUP_WIDTH, hk), 0) // HEAD_DIM
               == lax.broadcasted_iota(I32, (GROUP_WIDTH, hk), 1) // dk).astype(F32)
    t_sub = lax.broadcasted_iota(I32, (SUBLANES, hk), 0)

    def block(i):
        r0 = pl.multiple_of(i * c, c)
        vb = v_s[pl.ds(r0, c), :]
        n_sub = c // SUBLANES
        q_sub = [q_s[pl.ds(r0 + j * SUBLANES, SUBLANES), :] for j in range(n_sub)]
        b_sub = [b_s[pl.ds(r0 + j * SUBLANES, SUBLANES), :] for j in range(n_sub)]
        pieces = []
        for s in range(c):
            b_row = b_s[pl.ds(r0 + s, 1), :]
            k_row = k_s[pl.ds(r0 + s, 1), :]
            for j in range(s // SUBLANES, n_sub):
                d = jnp.where(t_sub + j * SUBLANES >= s, b_sub[j] - b_row, -jnp.inf)
                pieces.append(jnp.exp(d) * q_sub[j] * k_row)
        att = _segsum(jnp.concatenate(pieces, axis=0), seg)
        o_parts = [None] * n_sub
        row = 0
        for s in range(c):
            v_row = v_s[pl.ds(r0 + s, 1), :]
            for j in range(s // SUBLANES, n_sub):
                part = att[row:row + SUBLANES, :] * v_row
                row += SUBLANES
                o_parts[j] = part if o_parts[j] is None else o_parts[j] + part
        o = jnp.concatenate(o_parts, axis=0)
        si = i if st_ref.shape[0] > 1 else 0
        st = st_ref[si]
        o = o + lax.dot_general(qh_s[pl.ds(r0, c), :].astype(BF16), st.astype(BF16),
                                (((1,), (1,)), ((), ())), preferred_element_type=F32)
        upd = lax.dot_general(vb.astype(BF16), kh_s[pl.ds(r0, c), :].astype(BF16),
                              (((0,), (0,)), ((), ())), preferred_element_type=F32)
        st_ref[si] = st * dt_s[pl.ds(r0, 1), :] + upd * bd_mask
        o_ref[pl.ds(r0, c), :] = o

    return block


def _run_blocks(blocks, n_blk):
    def body(i, carry):
        for blk in blocks:
            blk(i)
        return carry

    lax.fori_loop(0, n_blk, body, 0, unroll=min(n_blk, max(1, 4 // len(blocks))))


def _rms_heads(o, w, gate, seg):
    ms = _segsum(o * o, seg) * (1.0 / HEAD_DIM)
    return o * lax.rsqrt(ms + LN_EPS) * w * (gate * _sigmoid(gate))


def _load_rows(ref, sample, pad_s, t_valid):
    if not sample:
        return ref[...]
    pad_s[...] = jnp.zeros_like(pad_s)
    for g in range(ref.shape[0]):
        pad_s[g * SUB_CHUNK:g * SUB_CHUNK + t_valid, :] = ref[g]
    return pad_s[...]


def _store_rows(o_ref, out, sample, t_valid):
    if not sample:
        o_ref[...] = out
        return
    for g in range(o_ref.shape[0]):
        o_ref[g] = out[g * SUB_CHUNK:g * SUB_CHUNK + t_valid, :]


def _valid_rows(rows, width, t_valid):
    return lax.broadcasted_iota(I32, (rows, width), 0) % SUB_CHUNK < t_valid


def _init_state(st_s, s0_ref, sample):
    if sample:
        st_s[...] = s0_ref[...]
    else:
        @pl.when(pl.program_id(1) == 0)
        def _():
            st_s[...] = s0_ref[...]


def _hgrn_parts(h_ref, s0_ref, lb_ref, nw_ref, o_ref, sT_ref,
                st_s, o_s, q_s, k_s, v_s, b_s, qh_s, kh_s, dt_s, pad, *, sample, t_valid):
    _init_state(st_s, s0_ref, sample)
    x = _load_rows(h_ref, sample, pad[0] if sample else None, t_valid)
    rows = x.shape[0]
    q = x[:, 0:256]
    fx = x[:, 256:512]
    iv = x[:, 512:768]
    gate = x[:, 768:1024]
    lb = lb_ref[0:1, :]
    log_lb = lb_ref[1:2, :]
    log1m_lb = lb_ref[2:3, :]
    cterm = log1m_lb + _log_sigmoid(fx)
    log_f = jnp.maximum(log_lb, cterm) + jnp.log1p(jnp.exp(-jnp.abs(log_lb - cterm)))
    key = (1.0 - lb) * _sigmoid(-fx)
    qs = q * _sigmoid(q) * (HG_DK ** -0.5)
    if sample:
        valid = _valid_rows(rows, GROUP_WIDTH, t_valid)
        log_f = jnp.where(valid, log_f, 0.0)
        key = jnp.where(valid, key, 0.0)
        iv = jnp.where(valid, iv, 0.0)
    block = _gated_tile(qs, key, iv, log_f, st_s, o_s, q_s, k_s, v_s, b_s, qh_s, kh_s, dt_s, dk=HG_DK)

    def finish():
        seg = _seg_ones(GROUP_WIDTH, HEAD_DIM, GROUP_WIDTH, HEAD_DIM)
        _store_rows(o_ref, _rms_heads(o_s[...], nw_ref[...], gate, seg), sample, t_valid)
        sT_ref[...] = st_s[...]

    return block, finish


def _gla_parts(hvg_ref, hqk_ref, s0_ref, wgk_ref, bgk_ref, nw_ref, o_ref, sT_ref,
               st_s, o_s, q_s, k_s, v_s, b_s, qh_s, kh_s, dt_s, pad, *, sample, t_valid):
    _init_state(st_s, s0_ref, sample)
    xvg = _load_rows(hvg_ref, sample, pad[0] if sample else None, t_valid)
    xqk = _load_rows(hqk_ref, sample, pad[1] if sample else None, t_valid)
    rows = xvg.shape[0]
    v = xvg[:, 0:256]
    gate = xvg[:, 256:512]
    q = xqk[:, 0:128] * (GLA_DK ** -0.5)
    k = xqk[:, 128:256]
    lo = xqk[:, 256:384]
    gk = _log_sigmoid(_dot3(lo, wgk_ref[...]) + bgk_ref[...]) * (1.0 / GLA_GATE_NORM)
    if sample:
        valid = _valid_rows(rows, 128, t_valid)
        gk = jnp.where(valid, gk, 0.0)
        k = jnp.where(valid, k, 0.0)
    block = _gated_tile(q, k, v, gk, st_s, o_s, q_s, k_s, v_s, b_s, qh_s, kh_s, dt_s, dk=GLA_DK)

    def finish():
        seg = _seg_ones(GROUP_WIDTH, HEAD_DIM, GROUP_WIDTH, HEAD_DIM)
        _store_rows(o_ref, _rms_heads(o_s[...], nw_ref[...], gate, seg), sample, t_valid)
        sT_ref[...] = st_s[...]

    return block, finish


N_GATED_SCRATCH = 9


def _gated_pair_kernel(h_ref, hvg_ref, hqk_ref, s0h_ref, s0g_ref, lb_ref, nwh_ref, wgk_ref, bgk_ref, nwg_ref,
                       oh_ref, og_ref, sTh_ref, sTg_ref, *scr, sample, t_valid, n_blk):
    hg_s, gl_s, pad = scr[:N_GATED_SCRATCH], scr[N_GATED_SCRATCH:2 * N_GATED_SCRATCH], scr[2 * N_GATED_SCRATCH:]
    blk_h, fin_h = _hgrn_parts(h_ref, s0h_ref, lb_ref, nwh_ref, oh_ref, sTh_ref, *hg_s, pad[0:1],
                               sample=sample, t_valid=t_valid)
    blk_g, fin_g = _gla_parts(hvg_ref, hqk_ref, s0g_ref, wgk_ref, bgk_ref, nwg_ref, og_ref, sTg_ref, *gl_s, pad[1:3],
                              sample=sample, t_valid=t_valid)
    _run_blocks((blk_h, blk_g), n_blk)
    fin_h()
    fin_g()


def _gated_call(h_all, n_seq, t_len, s0_hg, s0_gl, hg_params, gl_params):
    sample = t_len < SUB_CHUNK
    row0 = 0
    if sample:
        per_step = SEQ_GROUP
        rows, n_t, n_blk = per_step * SUB_CHUNK, 1, per_step
        h_view = h_all.reshape(h_all.shape[0] // t_len, t_len, h_all.shape[1])
        seq0 = row0 // (t_len * per_step)

        def hspec(width, col_block):
            return pl.BlockSpec((per_step, t_len, width), lambda b, i: (seq0 + b, 0, col_block))

        o_shape = jax.ShapeDtypeStruct((n_seq, t_len, GROUP_WIDTH), F32)
        o_spec = pl.BlockSpec((per_step, t_len, GROUP_WIDTH), lambda b, i: (b, 0, 0))
    else:
        per_step = 1
        rows = min(t_len, ROW_TILE)
        n_t, n_blk = t_len // rows, rows // SUB_CHUNK
        h_view = h_all
        blk0 = row0 // rows

        def hspec(width, col_block):
            return pl.BlockSpec((rows, width), lambda b, i: (blk0 + b * n_t + i, col_block))

        o_shape = jax.ShapeDtypeStruct((n_seq * t_len, GROUP_WIDTH), F32)
        o_spec = pl.BlockSpec((rows, GROUP_WIDTH), lambda b, i: (b * n_t + i, 0))

    hk_hg, hk_gl = N_HEADS * HG_DK, N_HEADS * GLA_DK
    st_spec = lambda hk: pl.BlockSpec((per_step, GROUP_WIDTH, hk), lambda b, i: (b, 0, 0))
    st_shape = lambda hk: jax.ShapeDtypeStruct((n_seq, GROUP_WIDTH, hk), F32)

    def full(a):
        return pl.BlockSpec(a.shape, lambda b, i: (0,) * a.ndim)

    def recurrence_scratch(hk):
        vm = lambda r, w: pltpu.VMEM((r, w), F32)
        return [pltpu.VMEM((per_step, GROUP_WIDTH, hk), F32), vm(rows, GROUP_WIDTH), vm(rows, hk), vm(rows, hk),
                vm(rows, GROUP_WIDTH), vm(rows, hk), vm(rows, hk), vm(rows, hk), vm(rows, hk)]

    scratch = recurrence_scratch(hk_hg) + recurrence_scratch(hk_gl)
    if sample:
        scratch += [pltpu.VMEM((rows, 1024), F32), pltpu.VMEM((rows, 512), F32), pltpu.VMEM((rows, 384), F32)]
    lb3, nw_hg = hg_params
    wgk, bgk, nw_gl = gl_params
    o_hg, o_gl, st_hg, st_gl = pl.pallas_call(
        functools.partial(_gated_pair_kernel, sample=sample, t_valid=t_len, n_blk=n_blk),
        grid=(n_seq // per_step, n_t),
        in_specs=[hspec(1024, OFF_HG // 1024), hspec(512, OFF_GLA_VG // 512), hspec(384, OFF_GLA_QK // 384),
                  st_spec(hk_hg), st_spec(hk_gl), full(lb3), full(nw_hg), full(wgk), full(bgk), full(nw_gl)],
        out_specs=(o_spec, o_spec, st_spec(hk_hg), st_spec(hk_gl)),
        out_shape=(o_shape, o_shape, st_shape(hk_hg), st_shape(hk_gl)),
        scratch_shapes=scratch,
        compiler_params=_cparams(2),
        name="gated_sample" if sample else "gated_prompt",
    )(h_view, h_view, h_view, s0_hg, s0_gl, lb3, nw_hg, wgk, bgk, nw_gl)
    flat = lambda o: o.reshape(n_seq * t_len, GROUP_WIDTH)
    return flat(o_hg), flat(o_gl), st_hg, st_gl


def _rwkv_kernel(p_ref, sh0_ref, s0_ref, mu_ref, lora_ref, vec_ref, o_ref, s_out_ref,
                 s_s, prev_s, seg_s, lm_s, a_s, b_s, w_s, k_s, r_s, v_s, y_s, *, n_tok):
    i = pl.program_id(1)
    g8 = SEQ_GROUP

    @pl.when(i == 0)
    def _():
        s_s[...] = s0_ref[0]
        prev_s[...] = sh0_ref[0]

    seg_s[...] = _seg_ones(GROUP_WIDTH, HEAD_DIM, GROUP_WIDTH, HEAD_DIM)
    lm_s[...] = (lax.broadcasted_iota(I32, (HEAD_DIM, g8, GROUP_WIDTH), 2) % HEAD_DIM
                 == lax.broadcasted_iota(I32, (HEAD_DIM, g8, GROUP_WIDTH), 0)).astype(F32)
    seg = seg_s[...]

    p = _slab_load(p_ref)
    if n_tok > 1:
        prev = jnp.concatenate([prev_s[...], p[:-g8, :]], axis=0)
    else:
        prev = prev_s[...]
    prev_s[...] = p[(n_tok - 1) * g8:, :]
    xs = p + (prev - p) * mu_ref[...]
    r = xs[:, 0:256]
    k = xs[:, 256:512]
    v = xs[:, 512:768]
    lo = xs[:, 768:896]
    w0, a0, kkp, ka = vec_ref[0:1, :], vec_ref[1:2, :], vec_ref[2:3, :], vec_ref[3:4, :]
    rk, lnw, lnb = vec_ref[4:5, :], vec_ref[5:6, :], vec_ref[6:7, :]
    w_log = -_softplus(-(w0 + _dot3(jnp.tanh(lo), lora_ref[0]))) - 0.5
    decay = jnp.exp(-jnp.exp(w_log))
    a = _sigmoid(a0 + _dot3(lo, lora_ref[1]))
    g = _dot3(_sigmoid(lo), lora_ref[2])
    kk = k * kkp
    kk = kk * lax.rsqrt(jnp.maximum(_segsum(kk * kk, seg), 1e-24))
    k2 = k * (1.0 + (a - 1.0) * ka)
    a_s[...] = -kk
    b_s[...] = kk * a
    w_s[...] = decay
    k_s[...] = k2
    r_s[...] = r
    v_s[...] = v

    n = HEAD_DIM * g8
    slab = (HEAD_DIM, g8, GROUP_WIDTH)

    def readout(s, row):
        yb = _dot((s * r_s[pl.ds(row, g8), :][None]).reshape(n, GROUP_WIDTH).astype(BF16), seg_s[...])
        return jnp.sum(yb.reshape(slab) * lm_s[...], axis=0)

    def step(t, carry):
        r0 = pl.multiple_of(t * g8, g8)
        rp = pl.multiple_of(jnp.maximum(t - 1, 0) * g8, g8)
        s = s_s[...]
        lm = lm_s[...]
        sg = seg_s[...]
        y_s[pl.ds(rp, g8), :] = readout(s, rp)
        sa = _dot((s * a_s[pl.ds(r0, g8), :][None]).reshape(n, GROUP_WIDTH).astype(BF16), sg).reshape(slab)
        vh, vl = _split2(v_s[pl.ds(r0, g8), :])
        vm = jnp.concatenate([(vh.astype(F32)[None] * lm).reshape(n, GROUP_WIDTH).astype(BF16),
                              (vl.astype(F32)[None] * lm).reshape(n, GROUP_WIDTH).astype(BF16)], axis=0)
        vb2 = _dot(vm, sg)
        vb = (vb2[0:n] + vb2[n:]).reshape(slab)
        s_s[...] = (s * w_s[pl.ds(r0, g8), :][None] + sa * b_s[pl.ds(r0, g8), :][None]
                    + vb * k_s[pl.ds(r0, g8), :][None])
        return carry

    lax.fori_loop(0, n_tok, step, 0, unroll=4)
    last = (n_tok - 1) * g8
    y_s[pl.ds(last, g8), :] = readout(s_s[...], last)

    y = y_s[...]
    mean = _segsum(y, seg) * (1.0 / HEAD_DIM)
    yc = y - mean
    var = _segsum(yc * yc, seg) * (1.0 / HEAD_DIM)
    yn = yc * lax.rsqrt(var + RW_GN_EPS) * lnw + lnb
    bonus = _segsum(r * k2 * rk, seg) * v
    _slab_store(o_ref, (yn + bonus) * g)
    s_out_ref[0] = s_s[...]


def _rwkv_call(p_tm, shift0, s0, mu, lora, vec, n_grp, t_len):
    n_tok = min(t_len, 64)
    n_t = t_len // n_tok
    rows = n_tok * SEQ_GROUP

    def full(a):
        return pl.BlockSpec(a.shape, lambda b, i: (0,) * a.ndim)

    st_spec = pl.BlockSpec((1, HEAD_DIM, SEQ_GROUP, GROUP_WIDTH), lambda b, i: (b, 0, 0, 0))
    vm = lambda shape, dt=F32: pltpu.VMEM(shape, dt)
    out, s_out = pl.pallas_call(
        functools.partial(_rwkv_kernel, n_tok=n_tok),
        grid=(n_grp, n_t),
        in_specs=[pl.BlockSpec((None, RW_COLS // LANES, rows, LANES), lambda b, i: (b, 0, i, 0)),
                  pl.BlockSpec((1, SEQ_GROUP, RW_COLS), lambda b, i: (b, 0, 0)),
                  st_spec, full(mu), full(lora), full(vec)],
        out_specs=(pl.BlockSpec((None, GROUP_WIDTH // LANES, rows, LANES), lambda b, i: (b, 0, i, 0)), st_spec),
        out_shape=(jax.ShapeDtypeStruct((n_grp, GROUP_WIDTH // LANES, t_len * SEQ_GROUP, LANES), F32),
                   jax.ShapeDtypeStruct((n_grp, HEAD_DIM, SEQ_GROUP, GROUP_WIDTH), F32)),
        scratch_shapes=[vm((HEAD_DIM, SEQ_GROUP, GROUP_WIDTH)), vm((SEQ_GROUP, RW_COLS)),
                        vm((GROUP_WIDTH, GROUP_WIDTH), BF16), vm((HEAD_DIM, SEQ_GROUP, GROUP_WIDTH)),
                        vm((rows, GROUP_WIDTH)), vm((rows, GROUP_WIDTH)), vm((rows, GROUP_WIDTH)),
                        vm((rows, GROUP_WIDTH)), vm((rows, GROUP_WIDTH)), vm((rows, GROUP_WIDTH)),
                        vm((rows, GROUP_WIDTH))],
        compiler_params=_cparams(2),
        name="rwkv_t%d" % t_len,
    )(p_tm, shift0, s0, mu, lora, vec)
    return out, s_out


def _s5_prep_kernel(are_ref, aim_ref, ldt_ref, bre_ref, bim_ref, abre_ref, abim_ref, bbre_ref, bbim_ref):
    a_re, a_im = are_ref[...], aim_ref[...]
    dt = jnp.exp(ldt_ref[...])
    mag = jnp.exp(a_re * dt)
    ab_re = mag * jnp.cos(a_im * dt)
    ab_im = mag * jnp.sin(a_im * dt)
    den = a_re * a_re + a_im * a_im
    nr, ni = ab_re - 1.0, ab_im
    coef_re = (nr * a_re + ni * a_im) / den
    coef_im = (ni * a_re - nr * a_im) / den
    b_re, b_im = bre_ref[...], bim_ref[...]
    abre_ref[...] = ab_re
    abim_ref[...] = ab_im
    bbre_ref[...] = coef_re * b_re - coef_im * b_im
    bbim_ref[...] = coef_re * b_im + coef_im * b_re


def _s5_prep(a_re, a_im, log_dt, b_re, b_im):
    rows = DEPTH * S5_NGROUPS
    cols = S5_STATE * S5_GROUP
    rep = lambda t: jnp.repeat(t.reshape(rows, S5_STATE), S5_GROUP, axis=1)
    ldt = jnp.broadcast_to(log_dt.reshape(rows, 1), (rows, cols))
    shp = jax.ShapeDtypeStruct((rows, cols), F32)
    ab_re, ab_im, bb_re, bb_im = pl.pallas_call(
        _s5_prep_kernel, out_shape=(shp, shp, shp, shp), name="s5_prep",
    )(rep(a_re), rep(a_im), ldt, b_re.reshape(rows, cols), b_im.reshape(rows, cols))
    pick = lambda t: t.reshape(DEPTH, S5_NGROUPS, S5_STATE, S5_GROUP)[..., 0].reshape(DEPTH, 1, S5_CH)
    bb = lambda t: t.reshape(DEPTH, S5_NGROUPS, S5_STATE, S5_GROUP)
    return pick(ab_re), pick(ab_im), bb(bb_re), bb(bb_im)


def _s5_kernel(u_ref, x0_ref, ab_ref, wb_ref, wc_ref, d_ref, wglu_ref, o_ref, xT_ref,
               x_s, bu_s, xs_s, *, n_tok):
    i = pl.program_id(1)
    g8 = SEQ_GROUP

    @pl.when(i == 0)
    def _():
        x_s[...] = x0_ref[0]

    u = _slab_load(u_ref)
    bu_s[...] = _dot(u.astype(BF16), wb_ref[...])
    a_re = jnp.broadcast_to(ab_ref[0:1, :], (g8, S5_CH))
    a_im = jnp.broadcast_to(ab_ref[1:2, :], (g8, S5_CH))

    def step(t, carry):
        x_re, x_im = carry
        r0 = pl.multiple_of(t * g8, g8)
        n_re = a_re * x_re - a_im * x_im + bu_s[pl.ds(r0, g8), 0:S5_CH]
        n_im = a_re * x_im + a_im * x_re + bu_s[pl.ds(r0, g8), S5_CH:]
        xs_s[pl.ds(r0, g8), 0:S5_CH] = n_re
        xs_s[pl.ds(r0, g8), S5_CH:] = n_im
        return n_re, n_im

    x_re, x_im = lax.fori_loop(0, n_tok, step, (x_s[:, 0:S5_CH], x_s[:, S5_CH:]))
    x_s[:, 0:S5_CH] = x_re
    x_s[:, S5_CH:] = x_im
    y = _dot(xs_s[...].astype(BF16), wc_ref[...]) + d_ref[...] * u
    yg = 0.5 * y * (1.0 + jnp.tanh(0.7978845608028654 * (y + 0.044715 * (y * y * y))))
    _slab_store(o_ref, yg * _sigmoid(_dot(yg.astype(BF16), wglu_ref[...])))
    xT_ref[0] = x_s[...]


def _s5_call(u_tm, x0, ab, wb, wc, dvec, wglu, n_grp, t_len):
    n_tok = min(t_len, 64)
    n_t = t_len // n_tok
    rows = n_tok * SEQ_GROUP

    def full(a):
        return pl.BlockSpec(a.shape, lambda b, i: (0,) * a.ndim)

    st_spec = pl.BlockSpec((1, SEQ_GROUP, 2 * S5_CH), lambda b, i: (b, 0, 0))
    return pl.pallas_call(
        functools.partial(_s5_kernel, n_tok=n_tok),
        grid=(n_grp, n_t),
        in_specs=[pl.BlockSpec((None, GROUP_WIDTH // LANES, rows, LANES), lambda b, i: (b, 0, i, 0)), st_spec,
                  full(ab), full(wb), full(wc), full(dvec), full(wglu)],
        out_specs=(pl.BlockSpec((None, GROUP_WIDTH // LANES, rows, LANES), lambda b, i: (b, 0, i, 0)), st_spec),
        out_shape=(jax.ShapeDtypeStruct((n_grp, GROUP_WIDTH // LANES, t_len * SEQ_GROUP, LANES), F32),
                   jax.ShapeDtypeStruct((n_grp, SEQ_GROUP, 2 * S5_CH), F32)),
        scratch_shapes=[pltpu.VMEM((SEQ_GROUP, 2 * S5_CH), F32), pltpu.VMEM((rows, 2 * S5_CH), F32),
                        pltpu.VMEM((rows, 2 * S5_CH), F32)],
        compiler_params=_cparams(2),
        name="s5_t%d" % t_len,
    )(u_tm, x0, ab, wb, wc, dvec, wglu)


CHUNKS = D_MODEL // LANES


def _store_chunked(ref, x):
    rows = x.shape[0]
    for j in range(CHUNKS):
        ref[pl.ds(j, rows, stride=CHUNKS), :] = x[:, j * LANES:(j + 1) * LANES]


def _load_chunked(ref, rows):
    return jnp.concatenate([ref[pl.ds(j, rows, stride=CHUNKS), :] for j in range(CHUNKS)], axis=1)


def _post_mix_kernel(x_ref, oap_ref, obp_ref, ocp_ref, odp_ref, oas_ref, obs_ref, ocs_ref, ods_ref,
                     wout_ref, lnw_ref, lnb_ref, rw_ref, rb_ref,
                     x1_ref, x1c_ref, meta_ref, gate_ref, cnt_ref, carry_s, *, prompt_tiles):
    i = pl.program_id(0)

    @pl.when(i == 0)
    def _():
        carry_s[...] = jnp.zeros_like(carry_s)

    is_prompt = i < prompt_tiles
    seq = i % SEQ_GROUP
    pick = lambda p, s_ref: jnp.where(is_prompt, p, s_ref[...])

    def part(a, j):
        ah, al = _split2(a)
        rows_j = slice(j * GROUP_WIDTH, (j + 1) * GROUP_WIDTH)
        return _dot(ah, wout_ref[0, rows_j, :]) + (_dot(al, wout_ref[0, rows_j, :]) + _dot(ah, wout_ref[1, rows_j, :]))

    mix = part(pick(_slab_load_rows(oap_ref, seq, POST_TILE), oas_ref), 0)
    mix += part(pick(obp_ref[...], obs_ref), 1)
    mix += part(pick(ocp_ref[...], ocs_ref), 2)
    mix += part(pick(_slab_load_rows(odp_ref, seq, POST_TILE), ods_ref), 3)
    x1 = _layer_norm(DN_ALPHA * x_ref[...] + mix, lnw_ref[...], lnb_ref[...])
    x1_ref[...] = x1
    _store_chunked(x1c_ref, x1)

    xh, xl = _split2(x1)
    logits = _dot(xh, rw_ref[0]) + (_dot(xh, rw_ref[1]) + _dot(xl, rw_ref[0])) + rb_ref[...]
    rows = logits.shape[0]
    lt = logits.T[0:N_EXPERTS, :]
    eid = lax.broadcasted_iota(I32, (N_EXPERTS, rows), 0)
    work = lt
    sel_i, sel_v = [], []
    for _ in range(TOP_K):
        m = jnp.max(work, axis=0, keepdims=True)
        j = jnp.min(jnp.where(work == m, eid, N_EXPERTS), axis=0, keepdims=True)
        sel_i.append(j)
        sel_v.append(m)
        work = jnp.where(eid == j, -jnp.inf, work)
    e = [jnp.exp(v - sel_v[0]) for v in sel_v]
    den = (e[0] + e[1]) + (e[2] + e[3])
    onehot = jnp.zeros((N_EXPERTS, rows), F32)
    for j in sel_i:
        onehot = onehot + (eid == j).astype(F32)
    oh = onehot.astype(BF16)
    rr = lax.broadcasted_iota(I32, (rows, rows), 0)
    cc = lax.broadcasted_iota(I32, (rows, rows), 1)
    before = _dot(oh, (rr < cc).astype(BF16)) + carry_s[:, 0:1]
    carry_s[...] = carry_s[...] + _dot(oh, jnp.ones((rows, LANES), BF16))
    ranks = [jnp.sum(jnp.where(eid == j, before, 0.0), axis=0, keepdims=True).astype(I32) for j in sel_i]
    meta_ref[...] = jnp.concatenate(sel_i + ranks, axis=0)
    gate_ref[...] = jnp.concatenate([ej / den for ej in e] + [jnp.zeros((SUBLANES - TOP_K, rows), F32)], axis=0)
    cnt_ref[...] = carry_s[...]


def _post_mix(x_all, outs_p, outs_s, n_seq_p, t_len_p, wout_l, lnw, lnb, rw_l, rb_l):
    n = x_all.shape[0]
    tile = POST_TILE
    n_t = t_len_p // tile
    prompt_tiles = n_seq_p * n_t
    w = GROUP_WIDTH
    p_idx = lambda i: jnp.minimum(i, prompt_tiles - 1)
    p_row = lambda i: (p_idx(i) % n_seq_p) * n_t + p_idx(i) // n_seq_p
    x_row = lambda i: jnp.where(i < prompt_tiles, p_row(i), i)
    row = lambda width: pl.BlockSpec((tile, width), lambda i: (x_row(i), 0))
    tm_p = pl.BlockSpec((w // LANES, tile * SEQ_GROUP, LANES), lambda i: (0, p_idx(i) // n_seq_p, 0))
    bm_p = pl.BlockSpec((tile, w), lambda i: (p_row(i), 0))
    bm_s = pl.BlockSpec((tile, w), lambda i: (jnp.maximum(i - prompt_tiles, 0), 0))
    per_token = pl.BlockSpec((SUBLANES, tile), lambda i: (0, x_row(i)))

    def full(a):
        return pl.BlockSpec(a.shape, lambda i: (0,) * a.ndim)

    x1, x1c, meta, gate, counts = pl.pallas_call(
        functools.partial(_post_mix_kernel, prompt_tiles=prompt_tiles),
        grid=(n // tile,),
        in_specs=[row(D_MODEL), tm_p, bm_p, bm_p, tm_p, bm_s, bm_s, bm_s, bm_s, full(wout_l), full(lnw), full(lnb),
                  full(rw_l), full(rb_l)],
        out_specs=(row(D_MODEL), pl.BlockSpec((tile * CHUNKS, LANES), lambda i: (x_row(i), 0)),
                   per_token, per_token, pl.BlockSpec((N_EXPERTS, LANES), lambda i: (0, 0))),
        out_shape=(jax.ShapeDtypeStruct((n, D_MODEL), F32), jax.ShapeDtypeStruct((n * CHUNKS, LANES), F32),
                   jax.ShapeDtypeStruct((SUBLANES, n), I32), jax.ShapeDtypeStruct((SUBLANES, n), F32),
                   jax.ShapeDtypeStruct((N_EXPERTS, LANES), F32)),
        scratch_shapes=[pltpu.VMEM((N_EXPERTS, LANES), F32)],
        compiler_params=_cparams(1),
        name="post_mix",
    )(x_all, *outs_p, *outs_s, wout_l, lnw, lnb, rw_l, rb_l)
    idx, rank = meta[0:TOP_K].T, meta[TOP_K:2 * TOP_K].T
    gate_cols = jnp.pad(gate[0:TOP_K].T, ((0, 0), (0, LANES - TOP_K)))
    return x1, x1c, idx, rank, gate_cols, counts[:, 0]


def _dispatch_kernel(gend_ref, dest_ref, x_ref, xs_hbm, zero_s, sem, *, tokens, n_tiles):
    i = pl.program_id(0)

    def zero_tile(first_row):
        start = pl.multiple_of(first_row * CHUNKS, MOE_TILE * CHUNKS)
        return pltpu.make_async_copy(zero_s, xs_hbm.at[pl.ds(start, MOE_TILE * CHUNKS)], sem)

    @pl.when(i == 0)
    def _():
        zero_s[...] = jnp.zeros_like(zero_s)
        for e in range(N_EXPERTS):
            @pl.when(gend_ref[e + 1] > gend_ref[e])
            def _():
                zero_tile(gend_ref[e + 1] - MOE_TILE).start()
        for e in range(N_EXPERTS):
            @pl.when(gend_ref[e + 1] > gend_ref[e])
            def _():
                zero_tile(gend_ref[e + 1] - MOE_TILE).wait()

        def tail(t, carry):
            cp = zero_tile(t * MOE_TILE)
            cp.start()
            cp.wait()
            return carry

        lax.fori_loop(gend_ref[N_EXPERTS] // MOE_TILE, n_tiles, tail, 0)

    def row_copy(n, slot):
        src = pl.multiple_of(n * CHUNKS, CHUNKS)
        dst = pl.multiple_of(dest_ref[n * TOP_K + slot] * CHUNKS, CHUNKS)
        return pltpu.make_async_copy(x_ref.at[pl.ds(src, CHUNKS)], xs_hbm.at[pl.ds(dst, CHUNKS)], sem)

    def issue(n, carry):
        for slot in range(TOP_K):
            row_copy(n, slot).start(priority=slot % 2)
        return carry

    lax.fori_loop(0, tokens, issue, 0, unroll=8)
    for slot in range(TOP_K):
        pltpu.make_async_copy(x_ref, xs_hbm.at[pl.ds(0, tokens * CHUNKS)], sem).wait()


def _dispatch(x1c, dest_flat, gend, n_rows):
    n = x1c.shape[0] // CHUNKS
    tokens = DISPATCH_TOKENS if n % DISPATCH_TOKENS == 0 else ROW_TILE
    return pl.pallas_call(
        functools.partial(_dispatch_kernel, tokens=tokens, n_tiles=n_rows // MOE_TILE),
        grid_spec=pltpu.PrefetchScalarGridSpec(
            num_scalar_prefetch=1,
            grid=(n // tokens,),
            in_specs=[pl.BlockSpec((tokens * TOP_K,), lambda i, ge: (i,), memory_space=pltpu.SMEM),
                      pl.BlockSpec((tokens * CHUNKS, LANES), lambda i, ge: (i, 0))],
            out_specs=pl.BlockSpec(memory_space=pl.ANY),
            scratch_shapes=[pltpu.VMEM((MOE_TILE * CHUNKS, LANES), F32), pltpu.SemaphoreType.DMA(())],
        ),
        out_shape=jax.ShapeDtypeStruct((n_rows * CHUNKS, LANES), F32),
        compiler_params=_cparams(1),
        name="moe_dispatch",
    )(gend, dest_flat, x1c)


PAIR_BLOCK = 2 * LANES


def _expert_prep_kernel(w1_ref, w2_ref, w1p_ref, w2b_ref):
    src = lax.broadcasted_iota(I32, (PAIR_BLOCK, PAIR_BLOCK), 0)
    dst = lax.broadcasted_iota(I32, (PAIR_BLOCK, PAIR_BLOCK), 1)
    perm = (src == jnp.where(dst < LANES, 2 * dst, 2 * (dst - LANES) + 1)).astype(BF16)
    for c in range(2 * D_FF // PAIR_BLOCK):
        cols = slice(c * PAIR_BLOCK, (c + 1) * PAIR_BLOCK)
        w1p_ref[:, cols] = _dot(w1_ref[:, cols].astype(BF16), perm).astype(BF16)
    w2b_ref[...] = w2_ref[...].astype(BF16)


def _expert_prep(exp_w1, exp_w2):
    n_l, n_e = exp_w1.shape[:2]
    spec = lambda r, c: pl.BlockSpec((None, None, r, c), lambda i: (i // n_e, i % n_e, 0, 0))
    return pl.pallas_call(
        _expert_prep_kernel,
        grid=(n_l * n_e,),
        in_specs=[spec(D_MODEL, 2 * D_FF), spec(D_FF, D_MODEL)],
        out_specs=(spec(D_MODEL, 2 * D_FF), spec(D_FF, D_MODEL)),
        out_shape=(jax.ShapeDtypeStruct(exp_w1.shape, BF16), jax.ShapeDtypeStruct(exp_w2.shape, BF16)),
        compiler_params=_cparams(1),
        name="expert_prep",
    )(exp_w1, exp_w2)


def _expert_kernel(te_ref, nreal_ref, xs_ref, w1_ref, b1_ref, w2_ref, b2_ref, o_ref, act_s):
    i = pl.program_id(0)
    nreal = nreal_ref[0]

    def hidden(slot):
        x = _load_chunked(xs_ref, MOE_TILE).astype(BF16)
        for c in range(D_FF // PAIR_BLOCK):
            cols = slice(2 * c * PAIR_BLOCK, 2 * (c + 1) * PAIR_BLOCK)
            h = _dot(x, w1_ref[:, cols]) + b1_ref[:, cols]
            h_glu = jnp.minimum(jnp.concatenate([h[:, 0:128], h[:, 256:384]], axis=1), SWIGLU_LIMIT)
            h_lin = jnp.clip(jnp.concatenate([h[:, 128:256], h[:, 384:512]], axis=1),
                             -SWIGLU_LIMIT, SWIGLU_LIMIT)
            act = h_glu * _sigmoid(SWIGLU_ALPHA * h_glu) * (h_lin + 1.0)
            act_s[slot, :, c * PAIR_BLOCK:(c + 1) * PAIR_BLOCK] = act.astype(BF16)

    def output(slot):
        _store_chunked(o_ref, _dot(act_s[slot], w2_ref[...]) + b2_ref[...])

    @pl.when(i == 0)
    def _():
        hidden(0)

    @pl.when((i > 0) & (i < nreal))
    def _():
        output((i - 1) % 2)
        hidden(i % 2)

    @pl.when((i > 0) & (i == nreal))
    def _():
        output((i - 1) % 2)

    @pl.when(i > nreal)
    def _():
        o_ref[...] = jnp.zeros_like(o_ref)


def _experts(xs, te, nreal, w1d, b1d, w2b, b2, layer):
    n_tiles = xs.shape[0] // (MOE_TILE * CHUNKS)
    cur = lambda i: jnp.minimum(i, n_tiles - 1)
    prev = lambda i: jnp.maximum(i - 1, 0)
    return pl.pallas_call(
        _expert_kernel,
        grid_spec=pltpu.PrefetchScalarGridSpec(
            num_scalar_prefetch=2,
            grid=(n_tiles + 1,),
            in_specs=[pl.BlockSpec((MOE_TILE * CHUNKS, LANES), lambda i, te, nr: (jnp.minimum(i, nr[0] - 1), 0)),
                      pl.BlockSpec((None, None, D_MODEL, 2 * D_FF), lambda i, te, nr: (layer, te[cur(i)], 0, 0)),
                      pl.BlockSpec((None, None, 1, 2 * D_FF), lambda i, te, nr: (layer, te[cur(i)], 0, 0)),
                      pl.BlockSpec((None, None, D_FF, D_MODEL), lambda i, te, nr: (layer, te[prev(i)], 0, 0)),
                      pl.BlockSpec((None, None, 1, D_MODEL), lambda i, te, nr: (layer, te[prev(i)], 0, 0))],
            out_specs=pl.BlockSpec((MOE_TILE * CHUNKS, LANES), lambda i, te, nr: (prev(i), 0)),
            scratch_shapes=[pltpu.VMEM((2, MOE_TILE, D_FF), BF16)],
        ),
        out_shape=jax.ShapeDtypeStruct(xs.shape, F32),
        compiler_params=_cparams(1),
        name="moe_experts",
    )(te, nreal, xs, w1d, b1d, w2b, b2)


def _combine_kernel(dest_ref, dest_next_ref, gate_ref, x1_ref, lnw_ref, lnb_ref, ys_hbm, o_ref, buf_s, sem,
                    *, tokens):
    i = pl.program_id(0)
    cur = i % 2

    def fetch(dref, half):
        def issue(n, carry):
            dst = pl.multiple_of(n * CHUNKS, CHUNKS)
            for slot in range(TOP_K):
                src = pl.multiple_of(dref[n * TOP_K + slot] * CHUNKS, CHUNKS)
                pltpu.make_async_copy(ys_hbm.at[pl.ds(src, CHUNKS)], buf_s.at[half, slot, pl.ds(dst, CHUNKS)],
                                      sem.at[half]).start(priority=slot % 2)
            return carry

        lax.fori_loop(0, tokens, issue, 0, unroll=8)

    @pl.when(i == 0)
    def _():
        fetch(dest_ref, 0)

    @pl.when(i + 1 < pl.num_programs(0))
    def _():
        fetch(dest_next_ref, 1 - cur)

    for slot in range(TOP_K):
        pltpu.make_async_copy(ys_hbm.at[pl.ds(0, tokens * CHUNKS)], buf_s.at[cur, slot], sem.at[cur]).wait()
    gate = gate_ref[...]
    ffn = gate[:, 0:1] * _load_chunked(buf_s.at[cur, 0], tokens)
    for slot in range(1, TOP_K):
        ffn = ffn + gate[:, slot:slot + 1] * _load_chunked(buf_s.at[cur, slot], tokens)
    o_ref[...] = _layer_norm(DN_ALPHA * x1_ref[...] + ffn, lnw_ref[...], lnb_ref[...])


def _combine(dest_flat, gate, x1, lnw, lnb, ys):
    n = x1.shape[0]
    tokens = POST_TILE
    last = n // tokens - 1
    return pl.pallas_call(
        functools.partial(_combine_kernel, tokens=tokens),
        grid=(n // tokens,),
        in_specs=[pl.BlockSpec((tokens * TOP_K,), lambda i: (i,), memory_space=pltpu.SMEM),
                  pl.BlockSpec((tokens * TOP_K,), lambda i: (jnp.minimum(i + 1, last),), memory_space=pltpu.SMEM),
                  pl.BlockSpec((tokens, LANES), lambda i: (i, 0)),
                  pl.BlockSpec((tokens, D_MODEL), lambda i: (i, 0)),
                  pl.BlockSpec((1, D_MODEL), lambda i: (0, 0)),
                  pl.BlockSpec((1, D_MODEL), lambda i: (0, 0)),
                  pl.BlockSpec(memory_space=pl.ANY)],
        out_specs=pl.BlockSpec((tokens, D_MODEL), lambda i: (i, 0)),
        out_shape=jax.ShapeDtypeStruct((n, D_MODEL), F32),
        scratch_shapes=[pltpu.VMEM((2, TOP_K, tokens * CHUNKS, LANES), F32), pltpu.SemaphoreType.DMA((2,))],
        compiler_params=_cparams(1),
        name="moe_combine",
    )(dest_flat, dest_flat, gate, x1, lnw, lnb, ys)


def _moe(x1, x1c, idx, rank, gate, counts, w1d, b1d, w2b, b2, lnw, lnb, layer):
    n = x1.shape[0]
    n_tiles = -(-(n * TOP_K + N_EXPERTS * (MOE_TILE - 1)) // MOE_TILE)
    cnt = counts.astype(I32)
    gsz = ((cnt + (MOE_TILE - 1)) // MOE_TILE) * MOE_TILE
    gend = jnp.cumsum(gsz)
    goff = gend - gsz
    dest = (goff[idx] + rank).reshape(-1)
    gend0 = jnp.concatenate([jnp.zeros((1,), I32), gend])
    nreal = (gend[-1:] // MOE_TILE).astype(I32)
    tile_start = jnp.arange(n_tiles, dtype=I32) * MOE_TILE
    te = jnp.minimum(jnp.sum((gend[None, :] <= tile_start[:, None]).astype(I32), axis=1), N_EXPERTS - 1)
    xs = _dispatch(x1c, dest, gend0, n_tiles * MOE_TILE)
    ys = _experts(xs, te, nreal, w1d, b1d, w2b, b2, layer)
    return _combine(dest, gate, x1, lnw, lnb, ys)


def _to_time_major(rows, n_seq, t_len):
    c = rows.shape[-1]
    x = rows.reshape(n_seq // SEQ_GROUP, SEQ_GROUP, t_len, c)
    return jnp.transpose(x, (0, 2, 1, 3)).reshape(n_seq // SEQ_GROUP, t_len * SEQ_GROUP, c)


def _from_time_major(x, n_seq, t_len):
    c = x.shape[-1]
    x = x.reshape(n_seq // SEQ_GROUP, t_len, SEQ_GROUP, c)
    return jnp.transpose(x, (0, 2, 1, 3)).reshape(n_seq * t_len, c)


def _block_diag_state(s, dk):
    st = jnp.swapaxes(s, 2, 3)
    eye = jnp.eye(N_HEADS, dtype=s.dtype)
    return jnp.einsum("bhvk,hg->bhvgk", st, eye).reshape(s.shape[0], GROUP_WIDTH, N_HEADS * dk)


def _unblock_state(st, dk):
    b = st.shape[0]
    x = st.reshape(b, N_HEADS, HEAD_DIM, N_HEADS, dk)
    x = jnp.stack([x[:, h, :, h, :] for h in range(N_HEADS)], axis=1)
    return jnp.swapaxes(x, 2, 3)


def _rwkv_state_in(s):
    b = s.shape[0]
    x = s.reshape(b // SEQ_GROUP, SEQ_GROUP, N_HEADS, HEAD_DIM, HEAD_DIM)
    return jnp.transpose(x, (0, 3, 1, 2, 4)).reshape(b // SEQ_GROUP, HEAD_DIM, SEQ_GROUP, GROUP_WIDTH)


def _rwkv_state_out(x, b):
    x = x.reshape(b // SEQ_GROUP, HEAD_DIM, SEQ_GROUP, N_HEADS, HEAD_DIM)
    return jnp.transpose(x, (0, 2, 3, 1, 4)).reshape(b, N_HEADS, HEAD_DIM, HEAD_DIM)


def _pad_rows(w, row0, n_rows):
    out = jnp.zeros((n_rows, w.shape[1]), w.dtype)
    return out.at[row0:row0 + w.shape[0]].set(w)


def kernel(x_prompt, x_sample, state_rwkv, state_rwkv_shift, state_hgrn, state_gla, state_s5_re, state_s5_im,
           w_in, rw_mu, rw_w0, rw_w2, rw_a0, rw_a2, rw_g2, rw_kk, rw_ka, rw_rk, rw_lnx_w, rw_lnx_b,
           hg_lb_logits, hg_norm_w, gla_w_gk2, gla_b_gk, gla_norm_w,
           s5_A_re, s5_A_im, s5_log_dt, s5_B_re, s5_B_im, s5_C_re, s5_C_im, s5_D, s5_w_glu,
           w_out, ln1_w, ln1_b, router_w, router_b, exp_w1, exp_b1, exp_w2, exp_b2, ln2_w, ln2_b):
    bp, tp, _ = x_prompt.shape
    bs, ts, _ = x_sample.shape
    n_p, n_s = bp * tp, bs * ts
    groups = ((0, bp, tp), (n_p, bs, ts))
    assert bp == SEQ_GROUP and tp % POST_TILE == 0 and bs % SEQ_GROUP == 0 and n_s % POST_TILE == 0
    assert ts <= SUB_CHUNK

    c = np.cumsum([0, RW_COLS, 1024, 784, 256])
    rw_c, hg_c, gl_c, s5_c = (w_in[:, :, c[j]:c[j + 1]] for j in range(4))
    gl_q, gl_k, gl_v, gl_lo, gl_g = (gl_c[:, :, a:b] for a, b in
                                     ((0, 128), (128, 256), (256, 512), (512, 528), (528, 784)))
    zpad = jnp.zeros((DEPTH, D_MODEL, 128 - GLA_GK_LORA), w_in.dtype)
    def two_terms(w):
        hi = w.astype(BF16)
        return jnp.stack([hi, (w - hi.astype(F32)).astype(BF16)], axis=1)

    w_in_p = two_terms(jnp.concatenate([hg_c, gl_v, gl_g, gl_q, gl_k, gl_lo, zpad, rw_c, s5_c], axis=2))
    w_out_b = two_terms(w_out)
    w1d, w2b = _expert_prep(exp_w1, exp_w2)
    b1d = jnp.swapaxes(exp_b1.reshape(DEPTH, N_EXPERTS, 2 * D_FF // PAIR_BLOCK, LANES, 2), -1, -2)
    b1d = b1d.reshape(DEPTH, N_EXPERTS, 1, 2 * D_FF)
    b2r = exp_b2[:, :, None, :]
    rw_pad = jnp.pad(router_w, ((0, 0), (0, 0), (0, LANES - N_EXPERTS)))
    rw_hi = rw_pad.astype(BF16)
    rw_lo = (rw_pad - rw_hi.astype(F32)).astype(BF16)
    rw_split = jnp.stack([rw_hi, rw_lo], axis=1)
    rb_pad = jnp.pad(router_b, ((0, 0), (0, LANES - N_EXPERTS)), constant_values=-1e30)[:, None, :]

    lbs = jnp.cumsum(jax.nn.softmax(hg_lb_logits.astype(F32), axis=0), axis=0)
    lbs = lbs - lbs[:1]
    lb3 = jnp.stack([lbs, jnp.log(lbs), jnp.log1p(-lbs)], axis=1)
    lb3 = jnp.pad(lb3, ((0, 0), (0, SUBLANES - 3), (0, 0)))

    ab_re, ab_im, bb_re, bb_im = _s5_prep(s5_A_re, s5_A_im, s5_log_dt, s5_B_re, s5_B_im)
    eye_g = jnp.eye(S5_NGROUPS, dtype=F32)
    wb = jnp.stack([jnp.einsum("lgph,gk->lghkp", t, eye_g).reshape(DEPTH, GROUP_WIDTH, S5_CH)
                    for t in (bb_re, bb_im)], axis=2).reshape(DEPTH, GROUP_WIDTH, 2 * S5_CH).astype(BF16)
    wc = jnp.concatenate([jnp.einsum("lghp,gk->lgpkh", t, eye_g).reshape(DEPTH, S5_CH, GROUP_WIDTH)
                          for t in (s5_C_re, -s5_C_im)], axis=1).astype(BF16)
    ab = jnp.concatenate([ab_re, ab_im], axis=1)
    ab = jnp.pad(ab, ((0, 0), (0, SUBLANES - 2), (0, 0)))
    wglu_b = s5_w_glu.astype(BF16)

    lora = jnp.stack([jnp.stack([_pad_rows(rw_w2[l], 0, 128), _pad_rows(rw_a2[l], 32, 128),
                                 _pad_rows(rw_g2[l], 64, 128)]) for l in range(DEPTH)])
    rw_vec = jnp.stack([rw_w0, rw_a0, rw_kk, rw_ka, rw_rk, rw_lnx_w, rw_lnx_b, jnp.zeros_like(rw_w0)], axis=1)
    wgk = jnp.stack([_pad_rows(gla_w_gk2[l], 0, 128) for l in range(DEPTH)])

    zeros = lambda shape: jnp.zeros(shape, F32)
    st_in = (
        dict(rw=zeros((DEPTH, bp, N_HEADS, HEAD_DIM, HEAD_DIM)), sh=zeros((DEPTH, bp, RW_COLS)),
             hg=zeros((DEPTH, bp, N_HEADS, HG_DK, HEAD_DIM)), gl=zeros((DEPTH, bp, N_HEADS, GLA_DK, HEAD_DIM)),
             re=zeros((DEPTH, bp, S5_NGROUPS, S5_STATE)), im=zeros((DEPTH, bp, S5_NGROUPS, S5_STATE))),
        dict(rw=state_rwkv, sh=state_rwkv_shift, hg=state_hgrn, gl=state_gla, re=state_s5_re, im=state_s5_im),
    )
    collected = ([], [])

    x_all = jnp.concatenate([x_prompt.reshape(n_p, D_MODEL), x_sample.reshape(n_s, D_MODEL)], axis=0)
    for l in range(DEPTH):
        h_gate_p, p_tm_p, u_tm_p = _inproj_prompt(x_all, w_in_p[l], bp, tp)
        h_s = _inproj_rows(x_all, w_in_p[l], n_p, n_s)
        outs = ([], [])
        for gi, (row0, n_seq, t_len) in enumerate(groups):
            st = st_in[gi]
            n_grp = n_seq // SEQ_GROUP
            if gi == 0:
                h_gate = h_gate_p
                p_tm = p_tm_p[None]
                u_tm = u_tm_p[None]
                new_sh = _from_slabs(p_tm_p[:, (t_len - 1) * n_seq:, :])
                to_rows = lambda x_tm: x_tm[0]
            else:
                h_gate = h_s
                p_rw = h_s[:, OFF_RW:OFF_RW + RW_COLS]
                p_tm = _to_slabs(_to_time_major(p_rw, n_seq, t_len))
                u_tm = _to_slabs(_to_time_major(h_s[:, OFF_S5:OFF_S5 + GROUP_WIDTH], n_seq, t_len))
                new_sh = p_rw.reshape(n_seq, t_len, RW_COLS)[:, -1]
                to_rows = lambda x_tm: _from_time_major(_from_slabs(x_tm), n_seq, t_len)
            sh0 = st["sh"][l].reshape(n_grp, SEQ_GROUP, RW_COLS)
            oa_tm, s_rw = _rwkv_call(p_tm, sh0, _rwkv_state_in(st["rw"][l]), rw_mu[l][None, :], lora[l],
                                     rw_vec[l], n_grp, t_len)
            new_rw = _rwkv_state_out(s_rw, n_seq)
            ob, oc, s_hg, s_gl = _gated_call(
                h_gate, n_seq, t_len, _block_diag_state(st["hg"][l], HG_DK), _block_diag_state(st["gl"][l], GLA_DK),
                (lb3[l], hg_norm_w[l][None, :]), (wgk[l], gla_b_gk[l][None, :], gla_norm_w[l][None, :]))
            x0 = jnp.concatenate([st["re"][l].reshape(n_grp, SEQ_GROUP, S5_CH),
                                  st["im"][l].reshape(n_grp, SEQ_GROUP, S5_CH)], axis=-1)
            od_tm, x_t = _s5_call(u_tm, x0, ab[l], wb[l], wc[l], s5_D[l][None, :], wglu_b[l], n_grp, t_len)
            outs[gi].extend([to_rows(oa_tm), ob, oc, to_rows(od_tm)])
            x_t = x_t.reshape(n_seq, 2, S5_NGROUPS, S5_STATE)
            collected[gi].append((new_rw, new_sh, _unblock_state(s_hg, HG_DK), _unblock_state(s_gl, GLA_DK),
                                  x_t[:, 0], x_t[:, 1]))
        x1, x1c, idx, rank, gate, counts = _post_mix(x_all, outs[0], outs[1], bp, tp, w_out_b[l],
                                                     ln1_w[l][None, :], ln1_b[l][None, :],
                                                     rw_split[l], rb_pad[l])
        x_all = _moe(x1, x1c, idx, rank, gate, counts, w1d, b1d, w2b, b2r, ln2_w[l][None, :], ln2_b[l][None, :], l)

    y_prompt = x_all[:n_p].reshape(bp, tp, D_MODEL)
    y_sample = x_all[n_p:].reshape(bs, ts, D_MODEL)
    ps = [jnp.stack([layer[j] for layer in collected[0]]) for j in range(6)]
    ss = [jnp.stack([layer[j] for layer in collected[1]]) for j in range(6)]
    return (y_prompt, y_sample, *ps, *ss)
```

```python
import functools

import jax
import jax.numpy as jnp
import numpy as np
from jax import lax
from jax.experimental import pallas as pl
from jax.experimental.pallas import tpu as pltpu

F32 = jnp.float32
BF16 = jnp.bfloat16
I32 = jnp.int32

D_MODEL = 1024
DEPTH = 4
GROUP_WIDTH = 256
HEAD_DIM = 64
N_HEADS = 4
RW_COLS = 896
RW_GN_EPS = 64e-5
HG_DK = 64
GLA_DK = 32
GLA_GK_LORA = 16
GLA_GATE_NORM = 16.0
S5_NGROUPS = 16
S5_GROUP = 16
S5_STATE = 64
S5_CH = S5_NGROUPS * S5_STATE
N_EXPERTS = 32
TOP_K = 4
D_FF = 1024
SWIGLU_ALPHA = 1.702
SWIGLU_LIMIT = 7.0
DN_ALPHA = (2.0 * DEPTH) ** 0.25
LN_EPS = 1e-5

SUBLANES = 8
LANES = 128
VMEM_LIMIT_BYTES = 56 * 1024 * 1024

OFF_HG = 0
OFF_GLA_VG = 1024
OFF_GLA_QK = 1536
OFF_RW = 1920
OFF_S5 = 2816
IN_PAD = 3072

ROW_TILE = 256
POST_TILE = 512
SUB_CHUNK = 16
SEQ_GROUP = SUBLANES
MOE_TILE = 512
DISPATCH_TOKENS = 512


def _cparams(n_axes):
    return pltpu.CompilerParams(dimension_semantics=("arbitrary",) * n_axes,
                                vmem_limit_bytes=VMEM_LIMIT_BYTES)


def _dot(a, b):
    return jnp.dot(a, b, preferred_element_type=F32)


def _split2(x):
    hi = x.astype(BF16)
    lo = (x - hi.astype(F32)).astype(BF16)
    return hi, lo


def _split3(x):
    hi = x.astype(BF16)
    r = x - hi.astype(F32)
    mid = r.astype(BF16)
    lo = (r - mid.astype(F32)).astype(BF16)
    return hi, mid, lo


def _dot3(a, b):
    ah, al = _split2(a)
    bh, bl = _split2(b)
    return _dot(ah, bh) + (_dot(ah, bl) + _dot(al, bh))


def _seg_ones(n_in, seg_in, n_out, seg_out):
    r = lax.broadcasted_iota(I32, (n_in, n_out), 0) // seg_in
    c = lax.broadcasted_iota(I32, (n_in, n_out), 1) // seg_out
    return (r == c).astype(BF16)


def _segsum(x, seg):
    rows = x.shape[0]
    hi, lo = _split2(x)
    both = _dot(jnp.concatenate([hi, lo], axis=0), seg)
    return both[:rows] + both[rows:]


def _sigmoid(x):
    return 1.0 / (1.0 + jnp.exp(-x))


def _log_sigmoid(x):
    return jnp.minimum(x, 0.0) - jnp.log1p(jnp.exp(-jnp.abs(x)))


def _softplus(x):
    return jnp.maximum(x, 0.0) + jnp.log1p(jnp.exp(-jnp.abs(x)))


def _layer_norm(x, w, b):
    xc = x - jnp.mean(x, axis=-1, keepdims=True)
    var = jnp.mean(xc * xc, axis=-1, keepdims=True)
    return xc * lax.rsqrt(var + LN_EPS) * w + b


def _project3(x, w_ref, cols):
    xh, xl = _split2(x)
    return _dot(xh, w_ref[0, :, cols]) + (_dot(xl, w_ref[0, :, cols]) + _dot(xh, w_ref[1, :, cols]))


def _inproj_all(x_ref, w_ref):
    x = x_ref[...]
    return jnp.concatenate([_project3(x, w_ref, slice(0, OFF_S5)),
                            _dot(x.astype(BF16), w_ref[0, :, OFF_S5:IN_PAD])], axis=1)


def _inproj_kernel(x_ref, w_ref, *o_refs):
    h = _inproj_all(x_ref, w_ref)
    col = 0
    for o_ref in o_refs:
        o_ref[...] = h[:, col:col + o_ref.shape[-1]]
        col += o_ref.shape[-1]


def _slab_store_rows(ref, seq, x):
    for j in range(ref.shape[0]):
        ref.at[j][pl.ds(seq, x.shape[0], stride=SEQ_GROUP), :] = x[:, j * LANES:(j + 1) * LANES]


def _slab_load_rows(ref, seq, tokens):
    return jnp.concatenate([ref.at[j][pl.ds(seq, tokens, stride=SEQ_GROUP), :] for j in range(ref.shape[0])],
                           axis=1)


def _slab_load(ref):
    return jnp.concatenate([ref[j] for j in range(ref.shape[0])], axis=1)


def _slab_store(ref, x):
    for j in range(ref.shape[0]):
        ref[j] = x[:, j * LANES:(j + 1) * LANES]


def _to_slabs(x):
    s = x.reshape(x.shape[:-1] + (x.shape[-1] // LANES, LANES))
    return jnp.swapaxes(s, -2, -3)


def _from_slabs(x):
    s = jnp.swapaxes(x, -2, -3)
    return s.reshape(s.shape[:-2] + (s.shape[-2] * LANES,))


def _inproj_prompt_kernel(x_ref, w_ref, g_ref, p_ref, u_ref):
    seq = pl.program_id(1)
    h = _inproj_all(x_ref, w_ref)
    g_ref[...] = h[:, 0:OFF_RW]
    _slab_store_rows(p_ref, seq, h[:, OFF_RW:OFF_RW + RW_COLS])
    _slab_store_rows(u_ref, seq, h[:, OFF_S5:OFF_S5 + GROUP_WIDTH])


def _inproj_prompt(x_all, w_in_l, n_seq, t_len):
    n_t = t_len // ROW_TILE
    slab = lambda c: pl.BlockSpec((c // LANES, ROW_TILE * SEQ_GROUP, LANES), lambda i, b: (0, i, 0))
    return pl.pallas_call(
        _inproj_prompt_kernel,
        grid=(n_t, n_seq),
        in_specs=[pl.BlockSpec((ROW_TILE, D_MODEL), lambda i, b: (b * n_t + i, 0)),
                  pl.BlockSpec((2, D_MODEL, IN_PAD), lambda i, b: (0, 0, 0))],
        out_specs=(pl.BlockSpec((ROW_TILE, OFF_RW), lambda i, b: (b * n_t + i, 0)),
                   slab(RW_COLS), slab(GROUP_WIDTH)),
        out_shape=(jax.ShapeDtypeStruct((n_seq * t_len, OFF_RW), F32),
                   jax.ShapeDtypeStruct((RW_COLS // LANES, t_len * n_seq, LANES), F32),
                   jax.ShapeDtypeStruct((GROUP_WIDTH // LANES, t_len * n_seq, LANES), F32)),
        compiler_params=_cparams(2),
        name="inproj_prompt",
    )(x_all, w_in_l)


def _inproj_rows(x_all, w_in_l, row0, n_rows):
    blk0 = row0 // ROW_TILE
    return pl.pallas_call(
        _inproj_kernel,
        grid=(n_rows // ROW_TILE,),
        in_specs=[pl.BlockSpec((ROW_TILE, D_MODEL), lambda i: (blk0 + i, 0)),
                  pl.BlockSpec((2, D_MODEL, IN_PAD), lambda i: (0, 0, 0))],
        out_specs=pl.BlockSpec((ROW_TILE, IN_PAD), lambda i: (i, 0)),
        out_shape=jax.ShapeDtypeStruct((n_rows, IN_PAD), F32),
        compiler_params=_cparams(1),
        name="inproj_rows",
    )(x_all, w_in_l)


def _gated_tile(q, k, v, g, st_ref, o_ref, q_s, k_s, v_s, b_s, qh_s, kh_s, dt_s, *, dk):
    c = SUB_CHUNK
    rows, hk = q.shape
    rr = lax.broadcasted_iota(I32, (rows, rows), 0)
    cc = lax.broadcasted_iota(I32, (rows, rows), 1)
    same = (rr // c) == (cc // c)
    tri = jnp.concatenate([(same & (cc <= rr)).astype(BF16), same.astype(BF16)], axis=0)
    g3 = jnp.concatenate(_split3(g), axis=1)
    p = _dot(tri, g3)
    b = p[:rows, :hk] + p[:rows, hk:2 * hk] + p[:rows, 2 * hk:]
    btot = p[rows:, :hk] + p[rows:, hk:2 * hk] + p[rows:, 2 * hk:]
    q_s[...] = q
    k_s[...] = k
    v_s[...] = v
    b_s[...] = b
    qh_s[...] = q * jnp.exp(b)
    kh_s[...] = k * jnp.exp(btot - b)
    dt_s[...] = jnp.exp(btot)
    seg = _seg_ones(hk, dk, GROUP_WIDTH, HEAD_DIM)
    bd_mask = (lax.broadcasted_iota(I32, (GROUP_WIDTH, hk), 0) // HEAD_DIM
               == lax.broadcasted_iota(I32, (GROUP_WIDTH, hk), 1) // dk).astype(F32)
    t_sub = lax.broadcasted_iota(I32, (SUBLANES, hk), 0)

    def block(i):
        r0 = pl.multiple_of(i * c, c)
        vb = v_s[pl.ds(r0, c), :]
        n_sub = c // SUBLANES
        q_sub = [q_s[pl.ds(r0 + j * SUBLANES, SUBLANES), :] for j in range(n_sub)]
        b_sub = [b_s[pl.ds(r0 + j * SUBLANES, SUBLANES), :] for j in range(n_sub)]
        pieces = []
        for s in range(c):
            b_row = b_s[pl.ds(r0 + s, 1), :]
            k_row = k_s[pl.ds(r0 + s, 1), :]
            for j in range(s // SUBLANES, n_sub):
                d = jnp.where(t_sub + j * SUBLANES >= s, b_sub[j] - b_row, -jnp.inf)
                pieces.append(jnp.exp(d) * q_sub[j] * k_row)
        att = _segsum(jnp.concatenate(pieces, axis=0), seg)
        o_parts = [None] * n_sub
        row = 0
        for s in range(c):
            v_row = v_s[pl.ds(r0 + s, 1), :]
            for j in range(s // SUBLANES, n_sub):
                part = att[row:row + SUBLANES, :] * v_row
                row += SUBLANES
                o_parts[j] = part if o_parts[j] is None else o_parts[j] + part
        o = jnp.concatenate(o_parts, axis=0)
        si = i if st_ref.shape[0] > 1 else 0
        st = st_ref[si]
        o = o + lax.dot_general(qh_s[pl.ds(r0, c), :].astype(BF16), st.astype(BF16),
                                (((1,), (1,)), ((), ())), preferred_element_type=F32)
        upd = lax.dot_general(vb.astype(BF16), kh_s[pl.ds(r0, c), :].astype(BF16),
                              (((0,), (0,)), ((), ())), preferred_element_type=F32)
        st_ref[si] = st * dt_s[pl.ds(r0, 1), :] + upd * bd_mask
        o_ref[pl.ds(r0, c), :] = o

    return block


def _run_blocks(blocks, n_blk):
    def body(i, carry):
        for blk in blocks:
            blk(i)
        return carry

    lax.fori_loop(0, n_blk, body, 0, unroll=min(n_blk, max(1, 4 // len(blocks))))


def _rms_heads(o, w, gate, seg):
    ms = _segsum(o * o, seg) * (1.0 / HEAD_DIM)
    return o * lax.rsqrt(ms + LN_EPS) * w * (gate * _sigmoid(gate))


def _load_rows(ref, sample, pad_s, t_valid):
    if not sample:
        return ref[...]
    pad_s[...] = jnp.zeros_like(pad_s)
    for g in range(ref.shape[0]):
        pad_s[g * SUB_CHUNK:g * SUB_CHUNK + t_valid, :] = ref[g]
    return pad_s[...]


def _store_rows(o_ref, out, sample, t_valid):
    if not sample:
        o_ref[...] = out
        return
    for g in range(o_ref.shape[0]):
        o_ref[g] = out[g * SUB_CHUNK:g * SUB_CHUNK + t_valid, :]


def _valid_rows(rows, width, t_valid):
    return lax.broadcasted_iota(I32, (rows, width), 0) % SUB_CHUNK < t_valid


def _init_state(st_s, s0_ref, sample):
    if sample:
        st_s[...] = s0_ref[...]
    else:
        @pl.when(pl.program_id(1) == 0)
        def _():
            st_s[...] = s0_ref[...]


def _hgrn_parts(h_ref, s0_ref, lb_ref, nw_ref, o_ref, sT_ref,
                st_s, o_s, q_s, k_s, v_s, b_s, qh_s, kh_s, dt_s, pad, *, sample, t_valid):
    _init_state(st_s, s0_ref, sample)
    x = _load_rows(h_ref, sample, pad[0] if sample else None, t_valid)
    rows = x.shape[0]
    q = x[:, 0:256]
    fx = x[:, 256:512]
    iv = x[:, 512:768]
    gate = x[:, 768:1024]
    lb = lb_ref[0:1, :]
    log_lb = lb_ref[1:2, :]
    log1m_lb = lb_ref[2:3, :]
    cterm = log1m_lb + _log_sigmoid(fx)
    log_f = jnp.maximum(log_lb, cterm) + jnp.log1p(jnp.exp(-jnp.abs(log_lb - cterm)))
    key = (1.0 - lb) * _sigmoid(-fx)
    qs = q * _sigmoid(q) * (HG_DK ** -0.5)
    if sample:
        valid = _valid_rows(rows, GROUP_WIDTH, t_valid)
        log_f = jnp.where(valid, log_f, 0.0)
        key = jnp.where(valid, key, 0.0)
        iv = jnp.where(valid, iv, 0.0)
    block = _gated_tile(qs, key, iv, log_f, st_s, o_s, q_s, k_s, v_s, b_s, qh_s, kh_s, dt_s, dk=HG_DK)

    def finish():
        seg = _seg_ones(GROUP_WIDTH, HEAD_DIM, GROUP_WIDTH, HEAD_DIM)
        _store_rows(o_ref, _rms_heads(o_s[...], nw_ref[...], gate, seg), sample, t_valid)
        sT_ref[...] = st_s[...]

    return block, finish


def _gla_parts(hvg_ref, hqk_ref, s0_ref, wgk_ref, bgk_ref, nw_ref, o_ref, sT_ref,
               st_s, o_s, q_s, k_s, v_s, b_s, qh_s, kh_s, dt_s, pad, *, sample, t_valid):
    _init_state(st_s, s0_ref, sample)
    xvg = _load_rows(hvg_ref, sample, pad[0] if sample else None, t_valid)
    xqk = _load_rows(hqk_ref, sample, pad[1] if sample else None, t_valid)
    rows = xvg.shape[0]
    v = xvg[:, 0:256]
    gate = xvg[:, 256:512]
    q = xqk[:, 0:128] * (GLA_DK ** -0.5)
    k = xqk[:, 128:256]
    lo = xqk[:, 256:384]
    gk = _log_sigmoid(_dot3(lo, wgk_ref[...]) + bgk_ref[...]) * (1.0 / GLA_GATE_NORM)
    if sample:
        valid = _valid_rows(rows, 128, t_valid)
        gk = jnp.where(valid, gk, 0.0)
        k = jnp.where(valid, k, 0.0)
    block = _gated_tile(q, k, v, gk, st_s, o_s, q_s, k_s, v_s, b_s, qh_s, kh_s, dt_s, dk=GLA_DK)

    def finish():
        seg = _seg_ones(GROUP_WIDTH, HEAD_DIM, GROUP_WIDTH, HEAD_DIM)
        _store_rows(o_ref, _rms_heads(o_s[...], nw_ref[...], gate, seg), sample, t_valid)
        sT_ref[...] = st_s[...]

    return block, finish


N_GATED_SCRATCH = 9


def _gated_pair_kernel(h_ref, hvg_ref, hqk_ref, s0h_ref, s0g_ref, lb_ref, nwh_ref, wgk_ref, bgk_ref, nwg_ref,
                       oh_ref, og_ref, sTh_ref, sTg_ref, *scr, sample, t_valid, n_blk):
    hg_s, gl_s, pad = scr[:N_GATED_SCRATCH], scr[N_GATED_SCRATCH:2 * N_GATED_SCRATCH], scr[2 * N_GATED_SCRATCH:]
    blk_h, fin_h = _hgrn_parts(h_ref, s0h_ref, lb_ref, nwh_ref, oh_ref, sTh_ref, *hg_s, pad[0:1],
                               sample=sample, t_valid=t_valid)
    blk_g, fin_g = _gla_parts(hvg_ref, hqk_ref, s0g_ref, wgk_ref, bgk_ref, nwg_ref, og_ref, sTg_ref, *gl_s, pad[1:3],
                              sample=sample, t_valid=t_valid)
    _run_blocks((blk_h, blk_g), n_blk)
    fin_h()
    fin_g()


def _gated_call(h_all, n_seq, t_len, s0_hg, s0_gl, hg_params, gl_params):
    sample = t_len < SUB_CHUNK
    row0 = 0
    if sample:
        per_step = SEQ_GROUP
        rows, n_t, n_blk = per_step * SUB_CHUNK, 1, per_step
        h_view = h_all.reshape(h_all.shape[0] // t_len, t_len, h_all.shape[1])
        seq0 = row0 // (t_len * per_step)

        def hspec(width, col_block):
            return pl.BlockSpec((per_step, t_len, width), lambda b, i: (seq0 + b, 0, col_block))

        o_shape = jax.ShapeDtypeStruct((n_seq, t_len, GROUP_WIDTH), F32)
        o_spec = pl.BlockSpec((per_step, t_len, GROUP_WIDTH), lambda b, i: (b, 0, 0))
    else:
        per_step = 1
        rows = min(t_len, ROW_TILE)
        n_t, n_blk = t_len // rows, rows // SUB_CHUNK
        h_view = h_all
        blk0 = row0 // rows

        def hspec(width, col_block):
            return pl.BlockSpec((rows, width), lambda b, i: (blk0 + b * n_t + i, col_block))

        o_shape = jax.ShapeDtypeStruct((n_seq * t_len, GROUP_WIDTH), F32)
        o_spec = pl.BlockSpec((rows, GROUP_WIDTH), lambda b, i: (b * n_t + i, 0))

    hk_hg, hk_gl = N_HEADS * HG_DK, N_HEADS * GLA_DK
    st_spec = lambda hk: pl.BlockSpec((per_step, GROUP_WIDTH, hk), lambda b, i: (b, 0, 0))
    st_shape = lambda hk: jax.ShapeDtypeStruct((n_seq, GROUP_WIDTH, hk), F32)

    def full(a):
        return pl.BlockSpec(a.shape, lambda b, i: (0,) * a.ndim)

    def recurrence_scratch(hk):
        vm = lambda r, w: pltpu.VMEM((r, w), F32)
        return [pltpu.VMEM((per_step, GROUP_WIDTH, hk), F32), vm(rows, GROUP_WIDTH), vm(rows, hk), vm(rows, hk),
                vm(rows, GROUP_WIDTH), vm(rows, hk), vm(rows, hk), vm(rows, hk), vm(rows, hk)]

    scratch = recurrence_scratch(hk_hg) + recurrence_scratch(hk_gl)
    if sample:
        scratch += [pltpu.VMEM((rows, 1024), F32), pltpu.VMEM((rows, 512), F32), pltpu.VMEM((rows, 384), F32)]
    lb3, nw_hg = hg_params
    wgk, bgk, nw_gl = gl_params
    o_hg, o_gl, st_hg, st_gl = pl.pallas_call(
        functools.partial(_gated_pair_kernel, sample=sample, t_valid=t_len, n_blk=n_blk),
        grid=(n_seq // per_step, n_t),
        in_specs=[hspec(1024, OFF_HG // 1024), hspec(512, OFF_GLA_VG // 512), hspec(384, OFF_GLA_QK // 384),
                  st_spec(hk_hg), st_spec(hk_gl), full(lb3), full(nw_hg), full(wgk), full(bgk), full(nw_gl)],
        out_specs=(o_spec, o_spec, st_spec(hk_hg), st_spec(hk_gl)),
        out_shape=(o_shape, o_shape, st_shape(hk_hg), st_shape(hk_gl)),
        scratch_shapes=scratch,
        compiler_params=_cparams(2),
        name="gated_sample" if sample else "gated_prompt",
    )(h_view, h_view, h_view, s0_hg, s0_gl, lb3, nw_hg, wgk, bgk, nw_gl)
    flat = lambda o: o.reshape(n_seq * t_len, GROUP_WIDTH)
    return flat(o_hg), flat(o_gl), st_hg, st_gl


def _rwkv_kernel(p_ref, sh0_ref, s0_ref, mu_ref, lora_ref, vec_ref, o_ref, s_out_ref,
                 s_s, prev_s, seg_s, lm_s, a_s, b_s, w_s, k_s, r_s, v_s, y_s, *, n_tok):
    i = pl.program_id(1)
    g8 = SEQ_GROUP

    @pl.when(i == 0)
    def _():
        s_s[...] = s0_ref[0]
        prev_s[...] = sh0_ref[0]

    seg_s[...] = _seg_ones(GROUP_WIDTH, HEAD_DIM, GROUP_WIDTH, HEAD_DIM)
    lm_s[...] = (lax.broadcasted_iota(I32, (HEAD_DIM, g8, GROUP_WIDTH), 2) % HEAD_DIM
                 == lax.broadcasted_iota(I32, (HEAD_DIM, g8, GROUP_WIDTH), 0)).astype(F32)
    seg = seg_s[...]

    p = _slab_load(p_ref)
    if n_tok > 1:
        prev = jnp.concatenate([prev_s[...], p[:-g8, :]], axis=0)
    else:
        prev = prev_s[...]
    prev_s[...] = p[(n_tok - 1) * g8:, :]
    xs = p + (prev - p) * mu_ref[...]
    r = xs[:, 0:256]
    k = xs[:, 256:512]
    v = xs[:, 512:768]
    lo = xs[:, 768:896]
    w0, a0, kkp, ka = vec_ref[0:1, :], vec_ref[1:2, :], vec_ref[2:3, :], vec_ref[3:4, :]
    rk, lnw, lnb = vec_ref[4:5, :], vec_ref[5:6, :], vec_ref[6:7, :]
    w_log = -_softplus(-(w0 + _dot3(jnp.tanh(lo), lora_ref[0]))) - 0.5
    decay = jnp.exp(-jnp.exp(w_log))
    a = _sigmoid(a0 + _dot3(lo, lora_ref[1]))
    g = _dot3(_sigmoid(lo), lora_ref[2])
    kk = k * kkp
    kk = kk * lax.rsqrt(jnp.maximum(_segsum(kk * kk, seg), 1e-24))
    k2 = k * (1.0 + (a - 1.0) * ka)
    a_s[...] = -kk
    b_s[...] = kk * a
    w_s[...] = decay
    k_s[...] = k2
    r_s[...] = r
    v_s[...] = v

    n = HEAD_DIM * g8
    slab = (HEAD_DIM, g8, GROUP_WIDTH)

    def readout(s, row):
        yb = _dot((s * r_s[pl.ds(row, g8), :][None]).reshape(n, GROUP_WIDTH).astype(BF16), seg_s[...])
        return jnp.sum(yb.reshape(slab) * lm_s[...], axis=0)

    def step(t, carry):
        r0 = pl.multiple_of(t * g8, g8)
        rp = pl.multiple_of(jnp.maximum(t - 1, 0) * g8, g8)
        s = s_s[...]
        lm = lm_s[...]
        sg = seg_s[...]
        y_s[pl.ds(rp, g8), :] = readout(s, rp)
        sa = _dot((s * a_s[pl.ds(r0, g8), :][None]).reshape(n, GROUP_WIDTH).astype(BF16), sg).reshape(slab)
        vh, vl = _split2(v_s[pl.ds(r0, g8), :])
        vm = jnp.concatenate([(vh.astype(F32)[None] * lm).reshape(n, GROUP_WIDTH).astype(BF16),
                              (vl.astype(F32)[None] * lm).reshape(n, GROUP_WIDTH).astype(BF16)], axis=0)
        vb2 = _dot(vm, sg)
        vb = (vb2[0:n] + vb2[n:]).reshape(slab)
        s_s[...] = (s * w_s[pl.ds(r0, g8), :][None] + sa * b_s[pl.ds(r0, g8), :][None]
                    + vb * k_s[pl.ds(r0, g8), :][None])
        return carry

    lax.fori_loop(0, n_tok, step, 0, unroll=4)
    last = (n_tok - 1) * g8
    y_s[pl.ds(last, g8), :] = readout(s_s[...], last)

    y = y_s[...]
    mean = _segsum(y, seg) * (1.0 / HEAD_DIM)
    yc = y - mean
    var = _segsum(yc * yc, seg) * (1.0 / HEAD_DIM)
    yn = yc * lax.rsqrt(var + RW_GN_EPS) * lnw + lnb
    bonus = _segsum(r * k2 * rk, seg) * v
    _slab_store(o_ref, (yn + bonus) * g)
    s_out_ref[0] = s_s[...]


def _rwkv_call(p_tm, shift0, s0, mu, lora, vec, n_grp, t_len):
    n_tok = min(t_len, 64)
    n_t = t_len // n_tok
    rows = n_tok * SEQ_GROUP

    def full(a):
        return pl.BlockSpec(a.shape, lambda b, i: (0,) * a.ndim)

    st_spec = pl.BlockSpec((1, HEAD_DIM, SEQ_GROUP, GROUP_WIDTH), lambda b, i: (b, 0, 0, 0))
    vm = lambda shape, dt=F32: pltpu.VMEM(shape, dt)
    out, s_out = pl.pallas_call(
        functools.partial(_rwkv_kernel, n_tok=n_tok),
        grid=(n_grp, n_t),
        in_specs=[pl.BlockSpec((None, RW_COLS // LANES, rows, LANES), lambda b, i: (b, 0, i, 0)),
                  pl.BlockSpec((1, SEQ_GROUP, RW_COLS), lambda b, i: (b, 0, 0)),
                  st_spec, full(mu), full(lora), full(vec)],
        out_specs=(pl.BlockSpec((None, GROUP_WIDTH // LANES, rows, LANES), lambda b, i: (b, 0, i, 0)), st_spec),
        out_shape=(jax.ShapeDtypeStruct((n_grp, GROUP_WIDTH // LANES, t_len * SEQ_GROUP, LANES), F32),
                   jax.ShapeDtypeStruct((n_grp, HEAD_DIM, SEQ_GROUP, GROUP_WIDTH), F32)),
        scratch_shapes=[vm((HEAD_DIM, SEQ_GROUP, GROUP_WIDTH)), vm((SEQ_GROUP, RW_COLS)),
                        vm((GROUP_WIDTH, GROUP_WIDTH), BF16), vm((HEAD_DIM, SEQ_GROUP, GROUP_WIDTH)),
                        vm((rows, GROUP_WIDTH)), vm((rows, GROUP_WIDTH)), vm((rows, GROUP_WIDTH)),
                        vm((rows, GROUP_WIDTH)), vm((rows, GROUP_WIDTH)), vm((rows, GROUP_WIDTH)),
                        vm((rows, GROUP_WIDTH))],
        compiler_params=_cparams(2),
        name="rwkv_t%d" % t_len,
    )(p_tm, shift0, s0, mu, lora, vec)
    return out, s_out


def _s5_prep_kernel(are_ref, aim_ref, ldt_ref, bre_ref, bim_ref, abre_ref, abim_ref, bbre_ref, bbim_ref):
    a_re, a_im = are_ref[...], aim_ref[...]
    dt = jnp.exp(ldt_ref[...])
    mag = jnp.exp(a_re * dt)
    ab_re = mag * jnp.cos(a_im * dt)
    ab_im = mag * jnp.sin(a_im * dt)
    den = a_re * a_re + a_im * a_im
    nr, ni = ab_re - 1.0, ab_im
    coef_re = (nr * a_re + ni * a_im) / den
    coef_im = (ni * a_re - nr * a_im) / den
    b_re, b_im = bre_ref[...], bim_ref[...]
    abre_ref[...] = ab_re
    abim_ref[...] = ab_im
    bbre_ref[...] = coef_re * b_re - coef_im * b_im
    bbim_ref[...] = coef_re * b_im + coef_im * b_re


def _s5_prep(a_re, a_im, log_dt, b_re, b_im):
    rows = DEPTH * S5_NGROUPS
    cols = S5_STATE * S5_GROUP
    rep = lambda t: jnp.repeat(t.reshape(rows, S5_STATE), S5_GROUP, axis=1)
    ldt = jnp.broadcast_to(log_dt.reshape(rows, 1), (rows, cols))
    shp = jax.ShapeDtypeStruct((rows, cols), F32)
    ab_re, ab_im, bb_re, bb_im = pl.pallas_call(
        _s5_prep_kernel, out_shape=(shp, shp, shp, shp), name="s5_prep",
    )(rep(a_re), rep(a_im), ldt, b_re.reshape(rows, cols), b_im.reshape(rows, cols))
    pick = lambda t: t.reshape(DEPTH, S5_NGROUPS, S5_STATE, S5_GROUP)[..., 0].reshape(DEPTH, 1, S5_CH)
    bb = lambda t: t.reshape(DEPTH, S5_NGROUPS, S5_STATE, S5_GROUP)
    return pick(ab_re), pick(ab_im), bb(bb_re), bb(bb_im)


def _s5_kernel(u_ref, x0_ref, ab_ref, wb_ref, wc_ref, d_ref, wglu_ref, o_ref, xT_ref,
               x_s, bu_s, xs_s, *, n_tok):
    i = pl.program_id(1)
    g8 = SEQ_GROUP

    @pl.when(i == 0)
    def _():
        x_s[...] = x0_ref[0]

    u = _slab_load(u_ref)
    bu_s[...] = _dot(u.astype(BF16), wb_ref[...])
    a_re = jnp.broadcast_to(ab_ref[0:1, :], (g8, S5_CH))
    a_im = jnp.broadcast_to(ab_ref[1:2, :], (g8, S5_CH))

    def step(t, carry):
        x_re, x_im = carry
        r0 = pl.multiple_of(t * g8, g8)
        n_re = a_re * x_re - a_im * x_im + bu_s[pl.ds(r0, g8), 0:S5_CH]
        n_im = a_re * x_im + a_im * x_re + bu_s[pl.ds(r0, g8), S5_CH:]
        xs_s[pl.ds(r0, g8), 0:S5_CH] = n_re
        xs_s[pl.ds(r0, g8), S5_CH:] = n_im
        return n_re, n_im

    x_re, x_im = lax.fori_loop(0, n_tok, step, (x_s[:, 0:S5_CH], x_s[:, S5_CH:]))
    x_s[:, 0:S5_CH] = x_re
    x_s[:, S5_CH:] = x_im
    y = _dot(xs_s[...].astype(BF16), wc_ref[...]) + d_ref[...] * u
    yg = 0.5 * y * (1.0 + jnp.tanh(0.7978845608028654 * (y + 0.044715 * (y * y * y))))
    _slab_store(o_ref, yg * _sigmoid(_dot(yg.astype(BF16), wglu_ref[...])))
    xT_ref[0] = x_s[...]


def _s5_call(u_tm, x0, ab, wb, wc, dvec, wglu, n_grp, t_len):
    n_tok = min(t_len, 128)
    n_t = t_len // n_tok
    rows = n_tok * SEQ_GROUP

    def full(a):
        return pl.BlockSpec(a.shape, lambda b, i: (0,) * a.ndim)

    st_spec = pl.BlockSpec((1, SEQ_GROUP, 2 * S5_CH), lambda b, i: (b, 0, 0))
    return pl.pallas_call(
        functools.partial(_s5_kernel, n_tok=n_tok),
        grid=(n_grp, n_t),
        in_specs=[pl.BlockSpec((None, GROUP_WIDTH // LANES, rows, LANES), lambda b, i: (b, 0, i, 0)), st_spec,
                  full(ab), full(wb), full(wc), full(dvec), full(wglu)],
        out_specs=(pl.BlockSpec((None, GROUP_WIDTH // LANES, rows, LANES), lambda b, i: (b, 0, i, 0)), st_spec),
        out_shape=(jax.ShapeDtypeStruct((n_grp, GROUP_WIDTH // LANES, t_len * SEQ_GROUP, LANES), F32),
                   jax.ShapeDtypeStruct((n_grp, SEQ_GROUP, 2 * S5_CH), F32)),
        scratch_shapes=[pltpu.VMEM((SEQ_GROUP, 2 * S5_CH), F32), pltpu.VMEM((rows, 2 * S5_CH), F32),
                        pltpu.VMEM((rows, 2 * S5_CH), F32)],
        compiler_params=_cparams(2),
        name="s5_t%d" % t_len,
    )(u_tm, x0, ab, wb, wc, dvec, wglu)


CHUNKS = D_MODEL // LANES


def _store_chunked(ref, x):
    rows = x.shape[0]
    for j in range(CHUNKS):
        ref[pl.ds(j, rows, stride=CHUNKS), :] = x[:, j * LANES:(j + 1) * LANES]


def _load_chunked(ref, rows):
    return jnp.concatenate([ref[pl.ds(j, rows, stride=CHUNKS), :] for j in range(CHUNKS)], axis=1)


def _post_mix_kernel(x_ref, oap_ref, obp_ref, ocp_ref, odp_ref, oas_ref, obs_ref, ocs_ref, ods_ref,
                     wout_ref, lnw_ref, lnb_ref, rw_ref, rb_ref,
                     x1_ref, x1c_ref, meta_ref, gate_ref, cnt_ref, carry_s, *, prompt_tiles):
    i = pl.program_id(0)

    @pl.when(i == 0)
    def _():
        carry_s[...] = jnp.zeros_like(carry_s)

    is_prompt = i < prompt_tiles
    seq = i % SEQ_GROUP
    pick = lambda p, s_ref: jnp.where(is_prompt, p, s_ref[...])

    def part(a, j):
        ah, al = _split2(a)
        rows_j = slice(j * GROUP_WIDTH, (j + 1) * GROUP_WIDTH)
        return _dot(ah, wout_ref[0, rows_j, :]) + (_dot(al, wout_ref[0, rows_j, :]) + _dot(ah, wout_ref[1, rows_j, :]))

    mix = part(pick(_slab_load_rows(oap_ref, seq, POST_TILE), oas_ref), 0)
    mix += part(pick(obp_ref[...], obs_ref), 1)
    mix += part(pick(ocp_ref[...], ocs_ref), 2)
    mix += part(pick(_slab_load_rows(odp_ref, seq, POST_TILE), ods_ref), 3)
    x1 = _layer_norm(DN_ALPHA * x_ref[...] + mix, lnw_ref[...], lnb_ref[...])
    x1_ref[...] = x1
    _store_chunked(x1c_ref, x1)

    xh, xl = _split2(x1)
    logits = _dot(xh, rw_ref[0]) + (_dot(xh, rw_ref[1]) + _dot(xl, rw_ref[0])) + rb_ref[...]
    rows = logits.shape[0]
    lt = logits.T[0:N_EXPERTS, :]
    eid = lax.broadcasted_iota(I32, (N_EXPERTS, rows), 0)
    work = lt
    sel_i, sel_v = [], []
    for _ in range(TOP_K):
        m = jnp.max(work, axis=0, keepdims=True)
        j = jnp.min(jnp.where(work == m, eid, N_EXPERTS), axis=0, keepdims=True)
        sel_i.append(j)
        sel_v.append(m)
        work = jnp.where(eid == j, -jnp.inf, work)
    e = [jnp.exp(v - sel_v[0]) for v in sel_v]
    den = (e[0] + e[1]) + (e[2] + e[3])
    onehot = jnp.zeros((N_EXPERTS, rows), F32)
    for j in sel_i:
        onehot = onehot + (eid == j).astype(F32)
    oh = onehot.astype(BF16)
    rr = lax.broadcasted_iota(I32, (rows, rows), 0)
    cc = lax.broadcasted_iota(I32, (rows, rows), 1)
    before = _dot(oh, (rr < cc).astype(BF16)) + carry_s[:, 0:1]
    carry_s[...] = carry_s[...] + _dot(oh, jnp.ones((rows, LANES), BF16))
    ranks = [jnp.sum(jnp.where(eid == j, before, 0.0), axis=0, keepdims=True).astype(I32) for j in sel_i]
    meta_ref[...] = jnp.concatenate(sel_i + ranks, axis=0)
    gate_ref[...] = jnp.concatenate([ej / den for ej in e] + [jnp.zeros((SUBLANES - TOP_K, rows), F32)], axis=0)
    cnt_ref[...] = carry_s[...]


def _post_mix(x_all, outs_p, outs_s, n_seq_p, t_len_p, wout_l, lnw, lnb, rw_l, rb_l):
    n = x_all.shape[0]
    tile = POST_TILE
    n_t = t_len_p // tile
    prompt_tiles = n_seq_p * n_t
    w = GROUP_WIDTH
    p_idx = lambda i: jnp.minimum(i, prompt_tiles - 1)
    p_row = lambda i: (p_idx(i) % n_seq_p) * n_t + p_idx(i) // n_seq_p
    x_row = lambda i: jnp.where(i < prompt_tiles, p_row(i), i)
    row = lambda width: pl.BlockSpec((tile, width), lambda i: (x_row(i), 0))
    tm_p = pl.BlockSpec((w // LANES, tile * SEQ_GROUP, LANES), lambda i: (0, p_idx(i) // n_seq_p, 0))
    bm_p = pl.BlockSpec((tile, w), lambda i: (p_row(i), 0))
    bm_s = pl.BlockSpec((tile, w), lambda i: (jnp.maximum(i - prompt_tiles, 0), 0))
    per_token = pl.BlockSpec((SUBLANES, tile), lambda i: (0, x_row(i)))

    def full(a):
        return pl.BlockSpec(a.shape, lambda i: (0,) * a.ndim)

    x1, x1c, meta, gate, counts = pl.pallas_call(
        functools.partial(_post_mix_kernel, prompt_tiles=prompt_tiles),
        grid=(n // tile,),
        in_specs=[row(D_MODEL), tm_p, bm_p, bm_p, tm_p, bm_s, bm_s, bm_s, bm_s, full(wout_l), full(lnw), full(lnb),
                  full(rw_l), full(rb_l)],
        out_specs=(row(D_MODEL), pl.BlockSpec((tile * CHUNKS, LANES), lambda i: (x_row(i), 0)),
                   per_token, per_token, pl.BlockSpec((N_EXPERTS, LANES), lambda i: (0, 0))),
        out_shape=(jax.ShapeDtypeStruct((n, D_MODEL), F32), jax.ShapeDtypeStruct((n * CHUNKS, LANES), F32),
                   jax.ShapeDtypeStruct((SUBLANES, n), I32), jax.ShapeDtypeStruct((SUBLANES, n), F32),
                   jax.ShapeDtypeStruct((N_EXPERTS, LANES), F32)),
        scratch_shapes=[pltpu.VMEM((N_EXPERTS, LANES), F32)],
        compiler_params=_cparams(1),
        name="post_mix",
    )(x_all, *outs_p, *outs_s, wout_l, lnw, lnb, rw_l, rb_l)
    idx, rank = meta[0:TOP_K].T, meta[TOP_K:2 * TOP_K].T
    gate_cols = jnp.pad(gate[0:TOP_K].T, ((0, 0), (0, LANES - TOP_K)))
    return x1, x1c, idx, rank, gate_cols, counts[:, 0]


def _dispatch_kernel(gend_ref, dest_ref, x_ref, xs_hbm, zero_s, sem, *, tokens, n_tiles):
    i = pl.program_id(0)

    def zero_tile(first_row):
        start = pl.multiple_of(first_row * CHUNKS, MOE_TILE * CHUNKS)
        return pltpu.make_async_copy(zero_s, xs_hbm.at[pl.ds(start, MOE_TILE * CHUNKS)], sem)

    @pl.when(i == 0)
    def _():
        zero_s[...] = jnp.zeros_like(zero_s)
        for e in range(N_EXPERTS):
            @pl.when(gend_ref[e + 1] > gend_ref[e])
            def _():
                zero_tile(gend_ref[e + 1] - MOE_TILE).start()
        for e in range(N_EXPERTS):
            @pl.when(gend_ref[e + 1] > gend_ref[e])
            def _():
                zero_tile(gend_ref[e + 1] - MOE_TILE).wait()

        def tail(t, carry):
            cp = zero_tile(t * MOE_TILE)
            cp.start()
            cp.wait()
            return carry

        lax.fori_loop(gend_ref[N_EXPERTS] // MOE_TILE, n_tiles, tail, 0)

    def row_copy(n, slot):
        src = pl.multiple_of(n * CHUNKS, CHUNKS)
        dst = pl.multiple_of(dest_ref[n * TOP_K + slot] * CHUNKS, CHUNKS)
        return pltpu.make_async_copy(x_ref.at[pl.ds(src, CHUNKS)], xs_hbm.at[pl.ds(dst, CHUNKS)], sem)

    def issue(n, carry):
        for slot in range(TOP_K):
            row_copy(n, slot).start(priority=slot % 2)
        return carry

    lax.fori_loop(0, tokens, issue, 0, unroll=8)
    for slot in range(TOP_K):
        pltpu.make_async_copy(x_ref, xs_hbm.at[pl.ds(0, tokens * CHUNKS)], sem).wait()


def _dispatch(x1c, dest_flat, gend, n_rows):
    n = x1c.shape[0] // CHUNKS
    tokens = DISPATCH_TOKENS if n % DISPATCH_TOKENS == 0 else ROW_TILE
    return pl.pallas_call(
        functools.partial(_dispatch_kernel, tokens=tokens, n_tiles=n_rows // MOE_TILE),
        grid_spec=pltpu.PrefetchScalarGridSpec(
            num_scalar_prefetch=1,
            grid=(n // tokens,),
            in_specs=[pl.BlockSpec((tokens * TOP_K,), lambda i, ge: (i,), memory_space=pltpu.SMEM),
                      pl.BlockSpec((tokens * CHUNKS, LANES), lambda i, ge: (i, 0))],
            out_specs=pl.BlockSpec(memory_space=pl.ANY),
            scratch_shapes=[pltpu.VMEM((MOE_TILE * CHUNKS, LANES), F32), pltpu.SemaphoreType.DMA(())],
        ),
        out_shape=jax.ShapeDtypeStruct((n_rows * CHUNKS, LANES), F32),
        compiler_params=_cparams(1),
        name="moe_dispatch",
    )(gend, dest_flat, x1c)


PAIR_BLOCK = 2 * LANES


def _expert_prep_kernel(w1_ref, w2_ref, w1p_ref, w2b_ref):
    src = lax.broadcasted_iota(I32, (PAIR_BLOCK, PAIR_BLOCK), 0)
    dst = lax.broadcasted_iota(I32, (PAIR_BLOCK, PAIR_BLOCK), 1)
    perm = (src == jnp.where(dst < LANES, 2 * dst, 2 * (dst - LANES) + 1)).astype(BF16)
    for c in range(2 * D_FF // PAIR_BLOCK):
        cols = slice(c * PAIR_BLOCK, (c + 1) * PAIR_BLOCK)
        w1p_ref[:, cols] = _dot(w1_ref[:, cols].astype(BF16), perm).astype(BF16)
    w2b_ref[...] = w2_ref[...].astype(BF16)


def _expert_prep(exp_w1, exp_w2):
    n_l, n_e = exp_w1.shape[:2]
    spec = lambda r, c: pl.BlockSpec((None, None, r, c), lambda i: (i // n_e, i % n_e, 0, 0))
    return pl.pallas_call(
        _expert_prep_kernel,
        grid=(n_l * n_e,),
        in_specs=[spec(D_MODEL, 2 * D_FF), spec(D_FF, D_MODEL)],
        out_specs=(spec(D_MODEL, 2 * D_FF), spec(D_FF, D_MODEL)),
        out_shape=(jax.ShapeDtypeStruct(exp_w1.shape, BF16), jax.ShapeDtypeStruct(exp_w2.shape, BF16)),
        compiler_params=_cparams(1),
        name="expert_prep",
    )(exp_w1, exp_w2)


def _expert_kernel(te_ref, nreal_ref, xs_ref, w1_ref, b1_ref, w2_ref, b2_ref, o_ref, act_s):
    i = pl.program_id(0)
    nreal = nreal_ref[0]

    def hidden(slot):
        x = _load_chunked(xs_ref, MOE_TILE).astype(BF16)
        for c in range(D_FF // PAIR_BLOCK):
            cols = slice(2 * c * PAIR_BLOCK, 2 * (c + 1) * PAIR_BLOCK)
            h = _dot(x, w1_ref[:, cols]) + b1_ref[:, cols]
            h_glu = jnp.minimum(jnp.concatenate([h[:, 0:128], h[:, 256:384]], axis=1), SWIGLU_LIMIT)
            h_lin = jnp.clip(jnp.concatenate([h[:, 128:256], h[:, 384:512]], axis=1),
                             -SWIGLU_LIMIT, SWIGLU_LIMIT)
            act = h_glu * _sigmoid(SWIGLU_ALPHA * h_glu) * (h_lin + 1.0)
            act_s[slot, :, c * PAIR_BLOCK:(c + 1) * PAIR_BLOCK] = act.astype(BF16)

    def output(slot):
        _store_chunked(o_ref, _dot(act_s[slot], w2_ref[...]) + b2_ref[...])

    @pl.when(i == 0)
    def _():
        hidden(0)

    @pl.when((i > 0) & (i < nreal))
    def _():
        output((i - 1) % 2)
        hidden(i % 2)

    @pl.when((i > 0) & (i == nreal))
    def _():
        output((i - 1) % 2)

    @pl.when(i > nreal)
    def _():
        o_ref[...] = jnp.zeros_like(o_ref)


def _experts(xs, te, nreal, w1d, b1d, w2b, b2, layer):
    n_tiles = xs.shape[0] // (MOE_TILE * CHUNKS)
    cur = lambda i: jnp.minimum(i, n_tiles - 1)
    prev = lambda i: jnp.maximum(i - 1, 0)
    return pl.pallas_call(
        _expert_kernel,
        grid_spec=pltpu.PrefetchScalarGridSpec(
            num_scalar_prefetch=2,
            grid=(n_tiles + 1,),
            in_specs=[pl.BlockSpec((MOE_TILE * CHUNKS, LANES), lambda i, te, nr: (jnp.minimum(i, nr[0] - 1), 0)),
                      pl.BlockSpec((None, None, D_MODEL, 2 * D_FF), lambda i, te, nr: (layer, te[cur(i)], 0, 0)),
                      pl.BlockSpec((None, None, 1, 2 * D_FF), lambda i, te, nr: (layer, te[cur(i)], 0, 0)),
                      pl.BlockSpec((None, None, D_FF, D_MODEL), lambda i, te, nr: (layer, te[prev(i)], 0, 0)),
                      pl.BlockSpec((None, None, 1, D_MODEL), lambda i, te, nr: (layer, te[prev(i)], 0, 0))],
            out_specs=pl.BlockSpec((MOE_TILE * CHUNKS, LANES), lambda i, te, nr: (prev(i), 0)),
            scratch_shapes=[pltpu.VMEM((2, MOE_TILE, D_FF), BF16)],
        ),
        out_shape=jax.ShapeDtypeStruct(xs.shape, F32),
        compiler_params=_cparams(1),
        name="moe_experts",
    )(te, nreal, xs, w1d, b1d, w2b, b2)


def _combine_kernel(dest_ref, dest_next_ref, gate_ref, x1_ref, lnw_ref, lnb_ref, ys_hbm, o_ref, buf_s, sem,
                    *, tokens):
    i = pl.program_id(0)
    cur = i % 2

    def fetch(dref, half):
        def issue(n, carry):
            dst = pl.multiple_of(n * CHUNKS, CHUNKS)
            for slot in range(TOP_K):
                src = pl.multiple_of(dref[n * TOP_K + slot] * CHUNKS, CHUNKS)
                pltpu.make_async_copy(ys_hbm.at[pl.ds(src, CHUNKS)], buf_s.at[half, slot, pl.ds(dst, CHUNKS)],
                                      sem.at[half]).start(priority=slot % 2)
            return carry

        lax.fori_loop(0, tokens, issue, 0, unroll=8)

    @pl.when(i == 0)
    def _():
        fetch(dest_ref, 0)

    @pl.when(i + 1 < pl.num_programs(0))
    def _():
        fetch(dest_next_ref, 1 - cur)

    for slot in range(TOP_K):
        pltpu.make_async_copy(ys_hbm.at[pl.ds(0, tokens * CHUNKS)], buf_s.at[cur, slot], sem.at[cur]).wait()
    gate = gate_ref[...]
    ffn = gate[:, 0:1] * _load_chunked(buf_s.at[cur, 0], tokens)
    for slot in range(1, TOP_K):
        ffn = ffn + gate[:, slot:slot + 1] * _load_chunked(buf_s.at[cur, slot], tokens)
    o_ref[...] = _layer_norm(DN_ALPHA * x1_ref[...] + ffn, lnw_ref[...], lnb_ref[...])


def _combine(dest_flat, gate, x1, lnw, lnb, ys):
    n = x1.shape[0]
    tokens = POST_TILE
    last = n // tokens - 1
    return pl.pallas_call(
        functools.partial(_combine_kernel, tokens=tokens),
        grid=(n // tokens,),
        in_specs=[pl.BlockSpec((tokens * TOP_K,), lambda i: (i,), memory_space=pltpu.SMEM),
                  pl.BlockSpec((tokens * TOP_K,), lambda i: (jnp.minimum(i + 1, last),), memory_space=pltpu.SMEM),
                  pl.BlockSpec((tokens, LANES), lambda i: (i, 0)),
                  pl.BlockSpec((tokens, D_MODEL), lambda i: (i, 0)),
                  pl.BlockSpec((1, D_MODEL), lambda i: (0, 0)),
                  pl.BlockSpec((1, D_MODEL), lambda i: (0, 0)),
                  pl.BlockSpec(memory_space=pl.ANY)],
        out_specs=pl.BlockSpec((tokens, D_MODEL), lambda i: (i, 0)),
        out_shape=jax.ShapeDtypeStruct((n, D_MODEL), F32),
        scratch_shapes=[pltpu.VMEM((2, TOP_K, tokens * CHUNKS, LANES), F32), pltpu.SemaphoreType.DMA((2,))],
        compiler_params=_cparams(1),
        name="moe_combine",
    )(dest_flat, dest_flat, gate, x1, lnw, lnb, ys)


def _moe(x1, x1c, idx, rank, gate, counts, w1d, b1d, w2b, b2, lnw, lnb, layer):
    n = x1.shape[0]
    n_tiles = -(-(n * TOP_K + N_EXPERTS * (MOE_TILE - 1)) // MOE_TILE)
    cnt = counts.astype(I32)
    gsz = ((cnt + (MOE_TILE - 1)) // MOE_TILE) * MOE_TILE
    gend = jnp.cumsum(gsz)
    goff = gend - gsz
    dest = (goff[idx] + rank).reshape(-1)
    gend0 = jnp.concatenate([jnp.zeros((1,), I32), gend])
    nreal = (gend[-1:] // MOE_TILE).astype(I32)
    tile_start = jnp.arange(n_tiles, dtype=I32) * MOE_TILE
    te = jnp.minimum(jnp.sum((gend[None, :] <= tile_start[:, None]).astype(I32), axis=1), N_EXPERTS - 1)
    xs = _dispatch(x1c, dest, gend0, n_tiles * MOE_TILE)
    ys = _experts(xs, te, nreal, w1d, b1d, w2b, b2, layer)
    return _combine(dest, gate, x1, lnw, lnb, ys)


def _to_time_major(rows, n_seq, t_len):
    c = rows.shape[-1]
    x = rows.reshape(n_seq // SEQ_GROUP, SEQ_GROUP, t_len, c)
    return jnp.transpose(x, (0, 2, 1, 3)).reshape(n_seq // SEQ_GROUP, t_len * SEQ_GROUP, c)


def _from_time_major(x, n_seq, t_len):
    c = x.shape[-1]
    x = x.reshape(n_seq // SEQ_GROUP, t_len, SEQ_GROUP, c)
    return jnp.transpose(x, (0, 2, 1, 3)).reshape(n_seq * t_len, c)


def _block_diag_state(s, dk):
    st = jnp.swapaxes(s, 2, 3)
    eye = jnp.eye(N_HEADS, dtype=s.dtype)
    return jnp.einsum("bhvk,hg->bhvgk", st, eye).reshape(s.shape[0], GROUP_WIDTH, N_HEADS * dk)


def _unblock_state(st, dk):
    b = st.shape[0]
    x = st.reshape(b, N_HEADS, HEAD_DIM, N_HEADS, dk)
    x = jnp.stack([x[:, h, :, h, :] for h in range(N_HEADS)], axis=1)
    return jnp.swapaxes(x, 2, 3)


def _rwkv_state_in(s):
    b = s.shape[0]
    x = s.reshape(b // SEQ_GROUP, SEQ_GROUP, N_HEADS, HEAD_DIM, HEAD_DIM)
    return jnp.transpose(x, (0, 3, 1, 2, 4)).reshape(b // SEQ_GROUP, HEAD_DIM, SEQ_GROUP, GROUP_WIDTH)


def _rwkv_state_out(x, b):
    x = x.reshape(b // SEQ_GROUP, HEAD_DIM, SEQ_GROUP, N_HEADS, HEAD_DIM)
    return jnp.transpose(x, (0, 2, 3, 1, 4)).reshape(b, N_HEADS, HEAD_DIM, HEAD_DIM)


def _pad_rows(w, row0, n_rows):
    out = jnp.zeros((n_rows, w.shape[1]), w.dtype)
    return out.at[row0:row0 + w.shape[0]].set(w)


def kernel(x_prompt, x_sample, state_rwkv, state_rwkv_shift, state_hgrn, state_gla, state_s5_re, state_s5_im,
           w_in, rw_mu, rw_w0, rw_w2, rw_a0, rw_a2, rw_g2, rw_kk, rw_ka, rw_rk, rw_lnx_w, rw_lnx_b,
           hg_lb_logits, hg_norm_w, gla_w_gk2, gla_b_gk, gla_norm_w,
           s5_A_re, s5_A_im, s5_log_dt, s5_B_re, s5_B_im, s5_C_re, s5_C_im, s5_D, s5_w_glu,
           w_out, ln1_w, ln1_b, router_w, router_b, exp_w1, exp_b1, exp_w2, exp_b2, ln2_w, ln2_b):
    bp, tp, _ = x_prompt.shape
    bs, ts, _ = x_sample.shape
    n_p, n_s = bp * tp, bs * ts
    groups = ((0, bp, tp), (n_p, bs, ts))
    assert bp == SEQ_GROUP and tp % POST_TILE == 0 and bs % SEQ_GROUP == 0 and n_s % POST_TILE == 0
    assert ts <= SUB_CHUNK

    c = np.cumsum([0, RW_COLS, 1024, 784, 256])
    rw_c, hg_c, gl_c, s5_c = (w_in[:, :, c[j]:c[j + 1]] for j in range(4))
    gl_q, gl_k, gl_v, gl_lo, gl_g = (gl_c[:, :, a:b] for a, b in
                                     ((0, 128), (128, 256), (256, 512), (512, 528), (528, 784)))
    zpad = jnp.zeros((DEPTH, D_MODEL, 128 - GLA_GK_LORA), w_in.dtype)
    def two_terms(w):
        hi = w.astype(BF16)
        return jnp.stack([hi, (w - hi.astype(F32)).astype(BF16)], axis=1)

    w_in_p = two_terms(jnp.concatenate([hg_c, gl_v, gl_g, gl_q, gl_k, gl_lo, zpad, rw_c, s5_c], axis=2))
    w_out_b = two_terms(w_out)
    w1d, w2b = _expert_prep(exp_w1, exp_w2)
    b1d = jnp.swapaxes(exp_b1.reshape(DEPTH, N_EXPERTS, 2 * D_FF // PAIR_BLOCK, LANES, 2), -1, -2)
    b1d = b1d.reshape(DEPTH, N_EXPERTS, 1, 2 * D_FF)
    b2r = exp_b2[:, :, None, :]
    rw_pad = jnp.pad(router_w, ((0, 0), (0, 0), (0, LANES - N_EXPERTS)))
    rw_hi = rw_pad.astype(BF16)
    rw_lo = (rw_pad - rw_hi.astype(F32)).astype(BF16)
    rw_split = jnp.stack([rw_hi, rw_lo], axis=1)
    rb_pad = jnp.pad(router_b, ((0, 0), (0, LANES - N_EXPERTS)), constant_values=-1e30)[:, None, :]

    lbs = jnp.cumsum(jax.nn.softmax(hg_lb_logits.astype(F32), axis=0), axis=0)
    lbs = lbs - lbs[:1]
    lb3 = jnp.stack([lbs, jnp.log(lbs), jnp.log1p(-lbs)], axis=1)
    lb3 = jnp.pad(lb3, ((0, 0), (0, SUBLANES - 3), (0, 0)))

    ab_re, ab_im, bb_re, bb_im = _s5_prep(s5_A_re, s5_A_im, s5_log_dt, s5_B_re, s5_B_im)
    eye_g = jnp.eye(S5_NGROUPS, dtype=F32)
    wb = jnp.stack([jnp.einsum("lgph,gk->lghkp", t, eye_g).reshape(DEPTH, GROUP_WIDTH, S5_CH)
                    for t in (bb_re, bb_im)], axis=2).reshape(DEPTH, GROUP_WIDTH, 2 * S5_CH).astype(BF16)
    wc = jnp.concatenate([jnp.einsum("lghp,gk->lgpkh", t, eye_g).reshape(DEPTH, S5_CH, GROUP_WIDTH)
                          for t in (s5_C_re, -s5_C_im)], axis=1).astype(BF16)
    ab = jnp.concatenate([ab_re, ab_im], axis=1)
    ab = jnp.pad(ab, ((0, 0), (0, SUBLANES - 2), (0, 0)))
    wglu_b = s5_w_glu.astype(BF16)

    lora = jnp.stack([jnp.stack([_pad_rows(rw_w2[l], 0, 128), _pad_rows(rw_a2[l], 32, 128),
                                 _pad_rows(rw_g2[l], 64, 128)]) for l in range(DEPTH)])
    rw_vec = jnp.stack([rw_w0, rw_a0, rw_kk, rw_ka, rw_rk, rw_lnx_w, rw_lnx_b, jnp.zeros_like(rw_w0)], axis=1)
    wgk = jnp.stack([_pad_rows(gla_w_gk2[l], 0, 128) for l in range(DEPTH)])

    zeros = lambda shape: jnp.zeros(shape, F32)
    st_in = (
        dict(rw=zeros((DEPTH, bp, N_HEADS, HEAD_DIM, HEAD_DIM)), sh=zeros((DEPTH, bp, RW_COLS)),
             hg=zeros((DEPTH, bp, N_HEADS, HG_DK, HEAD_DIM)), gl=zeros((DEPTH, bp, N_HEADS, GLA_DK, HEAD_DIM)),
             re=zeros((DEPTH, bp, S5_NGROUPS, S5_STATE)), im=zeros((DEPTH, bp, S5_NGROUPS, S5_STATE))),
        dict(rw=state_rwkv, sh=state_rwkv_shift, hg=state_hgrn, gl=state_gla, re=state_s5_re, im=state_s5_im),
    )
    def states_in(st, n_seq):
        n_grp = n_seq // SEQ_GROUP
        flat = lambda t: t.reshape((DEPTH * n_seq,) + t.shape[2:])
        return dict(
            rw=_rwkv_state_in(flat(st["rw"])).reshape(DEPTH, n_grp, HEAD_DIM, SEQ_GROUP, GROUP_WIDTH),
            sh=st["sh"].reshape(DEPTH, n_grp, SEQ_GROUP, RW_COLS),
            hg=_block_diag_state(flat(st["hg"]), HG_DK).reshape(DEPTH, n_seq, GROUP_WIDTH, N_HEADS * HG_DK),
            gl=_block_diag_state(flat(st["gl"]), GLA_DK).reshape(DEPTH, n_seq, GROUP_WIDTH, N_HEADS * GLA_DK),
            x0=jnp.concatenate([st["re"].reshape(DEPTH, n_grp, SEQ_GROUP, S5_CH),
                                st["im"].reshape(DEPTH, n_grp, SEQ_GROUP, S5_CH)], axis=-1))

    def states_out(per_layer, n_seq):
        s_rw, new_sh, s_hg, s_gl, x_t = (jnp.stack([layer[j] for layer in per_layer]) for j in range(5))
        flat = lambda t: t.reshape((t.shape[0] * t.shape[1],) + t.shape[2:])
        x_t = x_t.reshape(DEPTH, n_seq, 2, S5_NGROUPS, S5_STATE)
        return [_rwkv_state_out(flat(s_rw), DEPTH * n_seq).reshape(DEPTH, n_seq, N_HEADS, HEAD_DIM, HEAD_DIM),
                new_sh,
                _unblock_state(flat(s_hg), HG_DK).reshape(DEPTH, n_seq, N_HEADS, HG_DK, HEAD_DIM),
                _unblock_state(flat(s_gl), GLA_DK).reshape(DEPTH, n_seq, N_HEADS, GLA_DK, HEAD_DIM),
                x_t[:, :, 0], x_t[:, :, 1]]

    st_k = (states_in(st_in[0], bp), states_in(st_in[1], bs))
    collected = ([], [])

    x_all = jnp.concatenate([x_prompt.reshape(n_p, D_MODEL), x_sample.reshape(n_s, D_MODEL)], axis=0)
    for l in range(DEPTH):
        h_gate_p, p_tm_p, u_tm_p = _inproj_prompt(x_all, w_in_p[l], bp, tp)
        h_s = _inproj_rows(x_all, w_in_p[l], n_p, n_s)
        outs = ([], [])
        for gi, (row0, n_seq, t_len) in enumerate(groups):
            st = st_k[gi]
            n_grp = n_seq // SEQ_GROUP
            if gi == 0:
                h_gate = h_gate_p
                p_tm = p_tm_p[None]
                u_tm = u_tm_p[None]
                new_sh = _from_slabs(p_tm_p[:, (t_len - 1) * n_seq:, :])
                to_rows = lambda x_tm: x_tm[0]
            else:
                h_gate = h_s
                p_rw = h_s[:, OFF_RW:OFF_RW + RW_COLS]
                p_tm = _to_slabs(_to_time_major(p_rw, n_seq, t_len))
                u_tm = _to_slabs(_to_time_major(h_s[:, OFF_S5:OFF_S5 + GROUP_WIDTH], n_seq, t_len))
                new_sh = p_rw.reshape(n_seq, t_len, RW_COLS)[:, -1]
                to_rows = lambda x_tm: _from_time_major(_from_slabs(x_tm), n_seq, t_len)
            oa_tm, s_rw = _rwkv_call(p_tm, st["sh"][l], st["rw"][l], rw_mu[l][None, :], lora[l],
                                     rw_vec[l], n_grp, t_len)
            ob, oc, s_hg, s_gl = _gated_call(
                h_gate, n_seq, t_len, st["hg"][l], st["gl"][l],
                (lb3[l], hg_norm_w[l][None, :]), (wgk[l], gla_b_gk[l][None, :], gla_norm_w[l][None, :]))
            od_tm, x_t = _s5_call(u_tm, st["x0"][l], ab[l], wb[l], wc[l], s5_D[l][None, :], wglu_b[l], n_grp, t_len)
            outs[gi].extend([to_rows(oa_tm), ob, oc, to_rows(od_tm)])
            collected[gi].append((s_rw, new_sh, s_hg, s_gl, x_t))
        x1, x1c, idx, rank, gate, counts = _post_mix(x_all, outs[0], outs[1], bp, tp, w_out_b[l],
                                                     ln1_w[l][None, :], ln1_b[l][None, :],
                                                     rw_split[l], rb_pad[l])
        x_all = _moe(x1, x1c, idx, rank, gate, counts, w1d, b1d, w2b, b2r, ln2_w[l][None, :], ln2_b[l][None, :], l)

    y_prompt = x_all[:n_p].reshape(bp, tp, D_MODEL)
    y_sample = x_all[n_p:].reshape(bs, ts, D_MODEL)
    return (y_prompt, y_sample, *states_out(collected[0], bp), *states_out(collected[1], bs))
```

```python
import functools

import jax
import jax.numpy as jnp
import numpy as np
from jax import lax
from jax.experimental import pallas as pl
from jax.experimental.pallas import tpu as pltpu

F32 = jnp.float32
BF16 = jnp.bfloat16
I32 = jnp.int32

D_MODEL = 1024
DEPTH = 4
GROUP_WIDTH = 256
HEAD_DIM = 64
N_HEADS = 4
RW_COLS = 896
RW_GN_EPS = 64e-5
HG_DK = 64
GLA_DK = 32
GLA_GK_LORA = 16
GLA_GATE_NORM = 16.0
S5_NGROUPS = 16
S5_GROUP = 16
S5_STATE = 64
S5_CH = S5_NGROUPS * S5_STATE
N_EXPERTS = 32
TOP_K = 4
D_FF = 1024
SWIGLU_ALPHA = 1.702
SWIGLU_LIMIT = 7.0
DN_ALPHA = (2.0 * DEPTH) ** 0.25
LN_EPS = 1e-5

SUBLANES = 8
LANES = 128
VMEM_LIMIT_BYTES = 56 * 1024 * 1024

OFF_HG = 0
OFF_GLA_VG = 1024
OFF_GLA_QK = 1536
OFF_RW = 1920
OFF_S5 = 2816
IN_PAD = 3072

ROW_TILE = 256
POST_TILE = 512
SUB_CHUNK = 16
SEQ_GROUP = SUBLANES
MOE_TILE = 512
DISPATCH_TOKENS = 512


def _cparams(n_axes):
    return pltpu.CompilerParams(dimension_semantics=("arbitrary",) * n_axes,
                                vmem_limit_bytes=VMEM_LIMIT_BYTES)


def _dot(a, b):
    return jnp.dot(a, b, preferred_element_type=F32)


def _split2(x):
    hi = x.astype(BF16)
    lo = (x - hi.astype(F32)).astype(BF16)
    return hi, lo


def _split3(x):
    hi = x.astype(BF16)
    r = x - hi.astype(F32)
    mid = r.astype(BF16)
    lo = (r - mid.astype(F32)).astype(BF16)
    return hi, mid, lo


def _dot3(a, b):
    ah, al = _split2(a)
    bh, bl = _split2(b)
    return _dot(ah, bh) + (_dot(ah, bl) + _dot(al, bh))


def _seg_ones(n_in, seg_in, n_out, seg_out):
    r = lax.broadcasted_iota(I32, (n_in, n_out), 0) // seg_in
    c = lax.broadcasted_iota(I32, (n_in, n_out), 1) // seg_out
    return (r == c).astype(BF16)


def _segsum(x, seg):
    rows = x.shape[0]
    hi, lo = _split2(x)
    both = _dot(jnp.concatenate([hi, lo], axis=0), seg)
    return both[:rows] + both[rows:]


def _sigmoid(x):
    return 1.0 / (1.0 + jnp.exp(-x))


def _log_sigmoid(x):
    return jnp.minimum(x, 0.0) - jnp.log1p(jnp.exp(-jnp.abs(x)))


def _softplus(x):
    return jnp.maximum(x, 0.0) + jnp.log1p(jnp.exp(-jnp.abs(x)))


def _layer_norm(x, w, b):
    xc = x - jnp.mean(x, axis=-1, keepdims=True)
    var = jnp.mean(xc * xc, axis=-1, keepdims=True)
    return xc * lax.rsqrt(var + LN_EPS) * w + b


def _project3(x, w_ref, cols):
    xh, xl = _split2(x)
    return _dot(xh, w_ref[0, :, cols]) + (_dot(xl, w_ref[0, :, cols]) + _dot(xh, w_ref[1, :, cols]))


def _inproj_all(x_ref, w_ref):
    x = x_ref[...]
    return jnp.concatenate([_project3(x, w_ref, slice(0, OFF_S5)),
                            _dot(x.astype(BF16), w_ref[0, :, OFF_S5:IN_PAD])], axis=1)


def _inproj_kernel(x_ref, w_ref, *o_refs):
    h = _inproj_all(x_ref, w_ref)
    col = 0
    for o_ref in o_refs:
        o_ref[...] = h[:, col:col + o_ref.shape[-1]]
        col += o_ref.shape[-1]


def _slab_store_rows(ref, seq, x):
    for j in range(ref.shape[0]):
        ref.at[j][pl.ds(seq, x.shape[0], stride=SEQ_GROUP), :] = x[:, j * LANES:(j + 1) * LANES]


def _slab_load_rows(ref, seq, tokens):
    return jnp.concatenate([ref.at[j][pl.ds(seq, tokens, stride=SEQ_GROUP), :] for j in range(ref.shape[0])],
                           axis=1)


def _slab_load(ref):
    return jnp.concatenate([ref[j] for j in range(ref.shape[0])], axis=1)


def _slab_store(ref, x):
    for j in range(ref.shape[0]):
        ref[j] = x[:, j * LANES:(j + 1) * LANES]


def _to_slabs(x):
    s = x.reshape(x.shape[:-1] + (x.shape[-1] // LANES, LANES))
    return jnp.swapaxes(s, -2, -3)


def _from_slabs(x):
    s = jnp.swapaxes(x, -2, -3)
    return s.reshape(s.shape[:-2] + (s.shape[-2] * LANES,))


def _inproj_prompt_kernel(x_ref, w_ref, g_ref, p_ref, u_ref):
    seq = pl.program_id(1)
    h = _inproj_all(x_ref, w_ref)
    g_ref[...] = h[:, 0:OFF_RW]
    _slab_store_rows(p_ref, seq, h[:, OFF_RW:OFF_RW + RW_COLS])
    _slab_store_rows(u_ref, seq, h[:, OFF_S5:OFF_S5 + GROUP_WIDTH])


def _inproj_prompt(x_all, w_in_l, n_seq, t_len):
    n_t = t_len // ROW_TILE
    slab = lambda c: pl.BlockSpec((c // LANES, ROW_TILE * SEQ_GROUP, LANES), lambda i, b: (0, i, 0))
    return pl.pallas_call(
        _inproj_prompt_kernel,
        grid=(n_t, n_seq),
        in_specs=[pl.BlockSpec((ROW_TILE, D_MODEL), lambda i, b: (b * n_t + i, 0)),
                  pl.BlockSpec((2, D_MODEL, IN_PAD), lambda i, b: (0, 0, 0))],
        out_specs=(pl.BlockSpec((ROW_TILE, OFF_RW), lambda i, b: (b * n_t + i, 0)),
                   slab(RW_COLS), slab(GROUP_WIDTH)),
        out_shape=(jax.ShapeDtypeStruct((n_seq * t_len, OFF_RW), F32),
                   jax.ShapeDtypeStruct((RW_COLS // LANES, t_len * n_seq, LANES), F32),
                   jax.ShapeDtypeStruct((GROUP_WIDTH // LANES, t_len * n_seq, LANES), F32)),
        compiler_params=_cparams(2),
        name="inproj_prompt",
    )(x_all, w_in_l)


def _inproj_rows(x_all, w_in_l, row0, n_rows):
    blk0 = row0 // ROW_TILE
    return pl.pallas_call(
        _inproj_kernel,
        grid=(n_rows // ROW_TILE,),
        in_specs=[pl.BlockSpec((ROW_TILE, D_MODEL), lambda i: (blk0 + i, 0)),
                  pl.BlockSpec((2, D_MODEL, IN_PAD), lambda i: (0, 0, 0))],
        out_specs=pl.BlockSpec((ROW_TILE, IN_PAD), lambda i: (i, 0)),
        out_shape=jax.ShapeDtypeStruct((n_rows, IN_PAD), F32),
        compiler_params=_cparams(1),
        name="inproj_rows",
    )(x_all, w_in_l)


def _gated_tile(q, k, v, g, st_ref, o_ref, q_s, k_s, v_s, b_s, qh_s, kh_s, dt_s, *, dk):
    c = SUB_CHUNK
    rows, hk = q.shape
    rr = lax.broadcasted_iota(I32, (rows, rows), 0)
    cc = lax.broadcasted_iota(I32, (rows, rows), 1)
    same = (rr // c) == (cc // c)
    tri = jnp.concatenate([(same & (cc <= rr)).astype(BF16), same.astype(BF16)], axis=0)
    g3 = jnp.concatenate(_split3(g), axis=1)
    p = _dot(tri, g3)
    b = p[:rows, :hk] + p[:rows, hk:2 * hk] + p[:rows, 2 * hk:]
    btot = p[rows:, :hk] + p[rows:, hk:2 * hk] + p[rows:, 2 * hk:]
    q_s[...] = q
    k_s[...] = k
    v_s[...] = v
    b_s[...] = b
    qh_s[...] = q * jnp.exp(b)
    kh_s[...] = k * jnp.exp(btot - b)
    dt_s[...] = jnp.exp(btot)
    seg = _seg_ones(hk, dk, GROUP_WIDTH, HEAD_DIM)
    bd_mask = (lax.broadcasted_iota(I32, (GROUP_WIDTH, hk), 0) // HEAD_DIM
               == lax.broadcasted_iota(I32, (GROUP_WIDTH, hk), 1) // dk).astype(F32)
    t_sub = lax.broadcasted_iota(I32, (SUBLANES, hk), 0)

    def block(i):
        r0 = pl.multiple_of(i * c, c)
        vb = v_s[pl.ds(r0, c), :]
        n_sub = c // SUBLANES
        q_sub = [q_s[pl.ds(r0 + j * SUBLANES, SUBLANES), :] for j in range(n_sub)]
        b_sub = [b_s[pl.ds(r0 + j * SUBLANES, SUBLANES), :] for j in range(n_sub)]
        pieces = []
        for s in range(c):
            b_row = b_s[pl.ds(r0 + s, 1), :]
            k_row = k_s[pl.ds(r0 + s, 1), :]
            for j in range(s // SUBLANES, n_sub):
                d = jnp.where(t_sub + j * SUBLANES >= s, b_sub[j] - b_row, -jnp.inf)
                pieces.append(jnp.exp(d) * q_sub[j] * k_row)
        att = _segsum(jnp.concatenate(pieces, axis=0), seg)
        o_parts = [None] * n_sub
        row = 0
        for s in range(c):
            v_row = v_s[pl.ds(r0 + s, 1), :]
            for j in range(s // SUBLANES, n_sub):
                part = att[row:row + SUBLANES, :] * v_row
                row += SUBLANES
                o_parts[j] = part if o_parts[j] is None else o_parts[j] + part
        o = jnp.concatenate(o_parts, axis=0)
        si = i if st_ref.shape[0] > 1 else 0
        st = st_ref[si]
        o = o + lax.dot_general(qh_s[pl.ds(r0, c), :].astype(BF16), st.astype(BF16),
                                (((1,), (1,)), ((), ())), preferred_element_type=F32)
        upd = lax.dot_general(vb.astype(BF16), kh_s[pl.ds(r0, c), :].astype(BF16),
                              (((0,), (0,)), ((), ())), preferred_element_type=F32)
        st_ref[si] = st * dt_s[pl.ds(r0, 1), :] + upd * bd_mask
        o_ref[pl.ds(r0, c), :] = o

    return block


def _run_blocks(blocks, n_blk):
    def body(i, carry):
        for blk in blocks:
            blk(i)
        return carry

    lax.fori_loop(0, n_blk, body, 0, unroll=min(n_blk, max(1, 8 // len(blocks))))


def _rms_heads(o, w, gate, seg):
    ms = _segsum(o * o, seg) * (1.0 / HEAD_DIM)
    return o * lax.rsqrt(ms + LN_EPS) * w * (gate * _sigmoid(gate))


def _load_rows(ref, sample, pad_s, t_valid):
    if not sample:
        return ref[...]
    pad_s[...] = jnp.zeros_like(pad_s)
    for g in range(ref.shape[0]):
        pad_s[g * SUB_CHUNK:g * SUB_CHUNK + t_valid, :] = ref[g]
    return pad_s[...]


def _store_rows(o_ref, out, sample, t_valid):
    if not sample:
        o_ref[...] = out
        return
    for g in range(o_ref.shape[0]):
        o_ref[g] = out[g * SUB_CHUNK:g * SUB_CHUNK + t_valid, :]


def _valid_rows(rows, width, t_valid):
    return lax.broadcasted_iota(I32, (rows, width), 0) % SUB_CHUNK < t_valid


def _init_state(st_s, s0_ref, sample):
    if sample:
        st_s[...] = s0_ref[...]
    else:
        @pl.when(pl.program_id(1) == 0)
        def _():
            st_s[...] = s0_ref[...]


def _hgrn_parts(h_ref, s0_ref, lb_ref, nw_ref, o_ref, sT_ref,
                st_s, o_s, q_s, k_s, v_s, b_s, qh_s, kh_s, dt_s, pad, *, sample, t_valid):
    _init_state(st_s, s0_ref, sample)
    x = _load_rows(h_ref, sample, pad[0] if sample else None, t_valid)
    rows = x.shape[0]
    q = x[:, 0:256]
    fx = x[:, 256:512]
    iv = x[:, 512:768]
    gate = x[:, 768:1024]
    lb = lb_ref[0:1, :]
    log_lb = lb_ref[1:2, :]
    log1m_lb = lb_ref[2:3, :]
    cterm = log1m_lb + _log_sigmoid(fx)
    log_f = jnp.maximum(log_lb, cterm) + jnp.log1p(jnp.exp(-jnp.abs(log_lb - cterm)))
    key = (1.0 - lb) * _sigmoid(-fx)
    qs = q * _sigmoid(q) * (HG_DK ** -0.5)
    if sample:
        valid = _valid_rows(rows, GROUP_WIDTH, t_valid)
        log_f = jnp.where(valid, log_f, 0.0)
        key = jnp.where(valid, key, 0.0)
        iv = jnp.where(valid, iv, 0.0)
    block = _gated_tile(qs, key, iv, log_f, st_s, o_s, q_s, k_s, v_s, b_s, qh_s, kh_s, dt_s, dk=HG_DK)

    def finish():
        seg = _seg_ones(GROUP_WIDTH, HEAD_DIM, GROUP_WIDTH, HEAD_DIM)
        _store_rows(o_ref, _rms_heads(o_s[...], nw_ref[...], gate, seg), sample, t_valid)
        sT_ref[...] = st_s[...]

    return block, finish


def _gla_parts(hvg_ref, hqk_ref, s0_ref, wgk_ref, bgk_ref, nw_ref, o_ref, sT_ref,
               st_s, o_s, q_s, k_s, v_s, b_s, qh_s, kh_s, dt_s, pad, *, sample, t_valid):
    _init_state(st_s, s0_ref, sample)
    xvg = _load_rows(hvg_ref, sample, pad[0] if sample else None, t_valid)
    xqk = _load_rows(hqk_ref, sample, pad[1] if sample else None, t_valid)
    rows = xvg.shape[0]
    v = xvg[:, 0:256]
    gate = xvg[:, 256:512]
    q = xqk[:, 0:128] * (GLA_DK ** -0.5)
    k = xqk[:, 128:256]
    lo = xqk[:, 256:384]
    gk = _log_sigmoid(_dot3(lo, wgk_ref[...]) + bgk_ref[...]) * (1.0 / GLA_GATE_NORM)
    if sample:
        valid = _valid_rows(rows, 128, t_valid)
        gk = jnp.where(valid, gk, 0.0)
        k = jnp.where(valid, k, 0.0)
    block = _gated_tile(q, k, v, gk, st_s, o_s, q_s, k_s, v_s, b_s, qh_s, kh_s, dt_s, dk=GLA_DK)

    def finish():
        seg = _seg_ones(GROUP_WIDTH, HEAD_DIM, GROUP_WIDTH, HEAD_DIM)
        _store_rows(o_ref, _rms_heads(o_s[...], nw_ref[...], gate, seg), sample, t_valid)
        sT_ref[...] = st_s[...]

    return block, finish


N_GATED_SCRATCH = 9


def _gated_pair_kernel(h_ref, hvg_ref, hqk_ref, s0h_ref, s0g_ref, lb_ref, nwh_ref, wgk_ref, bgk_ref, nwg_ref,
                       oh_ref, og_ref, sTh_ref, sTg_ref, *scr, sample, t_valid, n_blk):
    hg_s, gl_s, pad = scr[:N_GATED_SCRATCH], scr[N_GATED_SCRATCH:2 * N_GATED_SCRATCH], scr[2 * N_GATED_SCRATCH:]
    blk_h, fin_h = _hgrn_parts(h_ref, s0h_ref, lb_ref, nwh_ref, oh_ref, sTh_ref, *hg_s, pad[0:1],
                               sample=sample, t_valid=t_valid)
    blk_g, fin_g = _gla_parts(hvg_ref, hqk_ref, s0g_ref, wgk_ref, bgk_ref, nwg_ref, og_ref, sTg_ref, *gl_s, pad[1:3],
                              sample=sample, t_valid=t_valid)
    _run_blocks((blk_h, blk_g), n_blk)
    fin_h()
    fin_g()


def _gated_call(h_all, n_seq, t_len, s0_hg, s0_gl, hg_params, gl_params):
    sample = t_len < SUB_CHUNK
    row0 = 0
    if sample:
        per_step = SEQ_GROUP
        rows, n_t, n_blk = per_step * SUB_CHUNK, 1, per_step
        h_view = h_all.reshape(h_all.shape[0] // t_len, t_len, h_all.shape[1])
        seq0 = row0 // (t_len * per_step)

        def hspec(width, col_block):
            return pl.BlockSpec((per_step, t_len, width), lambda b, i: (seq0 + b, 0, col_block))

        o_shape = jax.ShapeDtypeStruct((n_seq, t_len, GROUP_WIDTH), F32)
        o_spec = pl.BlockSpec((per_step, t_len, GROUP_WIDTH), lambda b, i: (b, 0, 0))
    else:
        per_step = 1
        rows = min(t_len, ROW_TILE)
        n_t, n_blk = t_len // rows, rows // SUB_CHUNK
        h_view = h_all
        blk0 = row0 // rows

        def hspec(width, col_block):
            return pl.BlockSpec((rows, width), lambda b, i: (blk0 + b * n_t + i, col_block))

        o_shape = jax.ShapeDtypeStruct((n_seq * t_len, GROUP_WIDTH), F32)
        o_spec = pl.BlockSpec((rows, GROUP_WIDTH), lambda b, i: (b * n_t + i, 0))

    hk_hg, hk_gl = N_HEADS * HG_DK, N_HEADS * GLA_DK
    st_spec = lambda hk: pl.BlockSpec((per_step, GROUP_WIDTH, hk), lambda b, i: (b, 0, 0))
    st_shape = lambda hk: jax.ShapeDtypeStruct((n_seq, GROUP_WIDTH, hk), F32)

    def full(a):
        return pl.BlockSpec(a.shape, lambda b, i: (0,) * a.ndim)

    def recurrence_scratch(hk):
        vm = lambda r, w: pltpu.VMEM((r, w), F32)
        return [pltpu.VMEM((per_step, GROUP_WIDTH, hk), F32), vm(rows, GROUP_WIDTH), vm(rows, hk), vm(rows, hk),
                vm(rows, GROUP_WIDTH), vm(rows, hk), vm(rows, hk), vm(rows, hk), vm(rows, hk)]

    scratch = recurrence_scratch(hk_hg) + recurrence_scratch(hk_gl)
    if sample:
        scratch += [pltpu.VMEM((rows, 1024), F32), pltpu.VMEM((rows, 512), F32), pltpu.VMEM((rows, 384), F32)]
    lb3, nw_hg = hg_params
    wgk, bgk, nw_gl = gl_params
    o_hg, o_gl, st_hg, st_gl = pl.pallas_call(
        functools.partial(_gated_pair_kernel, sample=sample, t_valid=t_len, n_blk=n_blk),
        grid=(n_seq // per_step, n_t),
        in_specs=[hspec(1024, OFF_HG // 1024), hspec(512, OFF_GLA_VG // 512), hspec(384, OFF_GLA_QK // 384),
                  st_spec(hk_hg), st_spec(hk_gl), full(lb3), full(nw_hg), full(wgk), full(bgk), full(nw_gl)],
        out_specs=(o_spec, o_spec, st_spec(hk_hg), st_spec(hk_gl)),
        out_shape=(o_shape, o_shape, st_shape(hk_hg), st_shape(hk_gl)),
        scratch_shapes=scratch,
        compiler_params=_cparams(2),
        name="gated_sample" if sample else "gated_prompt",
    )(h_view, h_view, h_view, s0_hg, s0_gl, lb3, nw_hg, wgk, bgk, nw_gl)
    flat = lambda o: o.reshape(n_seq * t_len, GROUP_WIDTH)
    return flat(o_hg), flat(o_gl), st_hg, st_gl


def _rwkv_kernel(p_ref, sh0_ref, s0_ref, mu_ref, lora_ref, vec_ref, o_ref, s_out_ref,
                 s_s, prev_s, seg_s, lm_s, a_s, b_s, w_s, k_s, r_s, v_s, y_s, *, n_tok):
    i = pl.program_id(1)
    g8 = SEQ_GROUP

    @pl.when(i == 0)
    def _():
        s_s[...] = s0_ref[0]
        prev_s[...] = sh0_ref[0]

    seg_s[...] = _seg_ones(GROUP_WIDTH, HEAD_DIM, GROUP_WIDTH, HEAD_DIM)
    lm_s[...] = (lax.broadcasted_iota(I32, (HEAD_DIM, g8, GROUP_WIDTH), 2) % HEAD_DIM
                 == lax.broadcasted_iota(I32, (HEAD_DIM, g8, GROUP_WIDTH), 0)).astype(F32)
    seg = seg_s[...]

    p = _slab_load(p_ref)
    if n_tok > 1:
        prev = jnp.concatenate([prev_s[...], p[:-g8, :]], axis=0)
    else:
        prev = prev_s[...]
    prev_s[...] = p[(n_tok - 1) * g8:, :]
    xs = p + (prev - p) * mu_ref[...]
    r = xs[:, 0:256]
    k = xs[:, 256:512]
    v = xs[:, 512:768]
    lo = xs[:, 768:896]
    w0, a0, kkp, ka = vec_ref[0:1, :], vec_ref[1:2, :], vec_ref[2:3, :], vec_ref[3:4, :]
    rk, lnw, lnb = vec_ref[4:5, :], vec_ref[5:6, :], vec_ref[6:7, :]
    w_log = -_softplus(-(w0 + _dot3(jnp.tanh(lo), lora_ref[0]))) - 0.5
    decay = jnp.exp(-jnp.exp(w_log))
    a = _sigmoid(a0 + _dot3(lo, lora_ref[1]))
    g = _dot3(_sigmoid(lo), lora_ref[2])
    kk = k * kkp
    kk = kk * lax.rsqrt(jnp.maximum(_segsum(kk * kk, seg), 1e-24))
    k2 = k * (1.0 + (a - 1.0) * ka)
    a_s[...] = -kk
    b_s[...] = kk * a
    w_s[...] = decay
    k_s[...] = k2
    r_s[...] = r
    v_s[...] = v

    n = HEAD_DIM * g8
    slab = (HEAD_DIM, g8, GROUP_WIDTH)

    def readout(s, row):
        yb = _dot((s * r_s[pl.ds(row, g8), :][None]).reshape(n, GROUP_WIDTH).astype(BF16), seg_s[...])
        return jnp.sum(yb.reshape(slab) * lm_s[...], axis=0)

    def step(t, carry):
        r0 = pl.multiple_of(t * g8, g8)
        rp = pl.multiple_of(jnp.maximum(t - 1, 0) * g8, g8)
        s = s_s[...]
        lm = lm_s[...]
        sg = seg_s[...]
        y_s[pl.ds(rp, g8), :] = readout(s, rp)
        sa = _dot((s * a_s[pl.ds(r0, g8), :][None]).reshape(n, GROUP_WIDTH).astype(BF16), sg).reshape(slab)
        vh, vl = _split2(v_s[pl.ds(r0, g8), :])
        vm = jnp.concatenate([(vh.astype(F32)[None] * lm).reshape(n, GROUP_WIDTH).astype(BF16),
                              (vl.astype(F32)[None] * lm).reshape(n, GROUP_WIDTH).astype(BF16)], axis=0)
        vb2 = _dot(vm, sg)
        vb = (vb2[0:n] + vb2[n:]).reshape(slab)
        s_s[...] = (s * w_s[pl.ds(r0, g8), :][None] + sa * b_s[pl.ds(r0, g8), :][None]
                    + vb * k_s[pl.ds(r0, g8), :][None])
        return carry

    lax.fori_loop(0, n_tok, step, 0, unroll=8)
    last = (n_tok - 1) * g8
    y_s[pl.ds(last, g8), :] = readout(s_s[...], last)

    y = y_s[...]
    mean = _segsum(y, seg) * (1.0 / HEAD_DIM)
    yc = y - mean
    var = _segsum(yc * yc, seg) * (1.0 / HEAD_DIM)
    yn = yc * lax.rsqrt(var + RW_GN_EPS) * lnw + lnb
    bonus = _segsum(r * k2 * rk, seg) * v
    _slab_store(o_ref, (yn + bonus) * g)
    s_out_ref[0] = s_s[...]


def _rwkv_call(p_tm, shift0, s0, mu, lora, vec, n_grp, t_len):
    n_tok = min(t_len, 64)
    n_t = t_len // n_tok
    rows = n_tok * SEQ_GROUP

    def full(a):
        return pl.BlockSpec(a.shape, lambda b, i: (0,) * a.ndim)

    st_spec = pl.BlockSpec((1, HEAD_DIM, SEQ_GROUP, GROUP_WIDTH), lambda b, i: (b, 0, 0, 0))
    vm = lambda shape, dt=F32: pltpu.VMEM(shape, dt)
    out, s_out = pl.pallas_call(
        functools.partial(_rwkv_kernel, n_tok=n_tok),
        grid=(n_grp, n_t),
        in_specs=[pl.BlockSpec((None, RW_COLS // LANES, rows, LANES), lambda b, i: (b, 0, i, 0)),
                  pl.BlockSpec((1, SEQ_GROUP, RW_COLS), lambda b, i: (b, 0, 0)),
                  st_spec, full(mu), full(lora), full(vec)],
        out_specs=(pl.BlockSpec((None, GROUP_WIDTH // LANES, rows, LANES), lambda b, i: (b, 0, i, 0)), st_spec),
        out_shape=(jax.ShapeDtypeStruct((n_grp, GROUP_WIDTH // LANES, t_len * SEQ_GROUP, LANES), F32),
                   jax.ShapeDtypeStruct((n_grp, HEAD_DIM, SEQ_GROUP, GROUP_WIDTH), F32)),
        scratch_shapes=[vm((HEAD_DIM, SEQ_GROUP, GROUP_WIDTH)), vm((SEQ_GROUP, RW_COLS)),
                        vm((GROUP_WIDTH, GROUP_WIDTH), BF16), vm((HEAD_DIM, SEQ_GROUP, GROUP_WIDTH)),
                        vm((rows, GROUP_WIDTH)), vm((rows, GROUP_WIDTH)), vm((rows, GROUP_WIDTH)),
                        vm((rows, GROUP_WIDTH)), vm((rows, GROUP_WIDTH)), vm((rows, GROUP_WIDTH)),
                        vm((rows, GROUP_WIDTH))],
        compiler_params=_cparams(2),
        name="rwkv_t%d" % t_len,
    )(p_tm, shift0, s0, mu, lora, vec)
    return out, s_out


def _s5_prep_kernel(are_ref, aim_ref, ldt_ref, bre_ref, bim_ref, abre_ref, abim_ref, bbre_ref, bbim_ref):
    a_re, a_im = are_ref[...], aim_ref[...]
    dt = jnp.exp(ldt_ref[...])
    mag = jnp.exp(a_re * dt)
    ab_re = mag * jnp.cos(a_im * dt)
    ab_im = mag * jnp.sin(a_im * dt)
    den = a_re * a_re + a_im * a_im
    nr, ni = ab_re - 1.0, ab_im
    coef_re = (nr * a_re + ni * a_im) / den
    coef_im = (ni * a_re - nr * a_im) / den
    b_re, b_im = bre_ref[...], bim_ref[...]
    abre_ref[...] = ab_re
    abim_ref[...] = ab_im
    bbre_ref[...] = coef_re * b_re - coef_im * b_im
    bbim_ref[...] = coef_re * b_im + coef_im * b_re


def _s5_prep(a_re, a_im, log_dt, b_re, b_im):
    rows = DEPTH * S5_NGROUPS
    cols = S5_STATE * S5_GROUP
    rep = lambda t: jnp.repeat(t.reshape(rows, S5_STATE), S5_GROUP, axis=1)
    ldt = jnp.broadcast_to(log_dt.reshape(rows, 1), (rows, cols))
    shp = jax.ShapeDtypeStruct((rows, cols), F32)
    ab_re, ab_im, bb_re, bb_im = pl.pallas_call(
        _s5_prep_kernel, out_shape=(shp, shp, shp, shp), name="s5_prep",
    )(rep(a_re), rep(a_im), ldt, b_re.reshape(rows, cols), b_im.reshape(rows, cols))
    pick = lambda t: t.reshape(DEPTH, S5_NGROUPS, S5_STATE, S5_GROUP)[..., 0].reshape(DEPTH, 1, S5_CH)
    bb = lambda t: t.reshape(DEPTH, S5_NGROUPS, S5_STATE, S5_GROUP)
    return pick(ab_re), pick(ab_im), bb(bb_re), bb(bb_im)


def _s5_kernel(u_ref, x0_ref, ab_ref, wb_ref, wc_ref, d_ref, wglu_ref, o_ref, xT_ref,
               x_s, bu_s, xs_s, *, n_tok):
    i = pl.program_id(1)
    g8 = SEQ_GROUP

    @pl.when(i == 0)
    def _():
        x_s[...] = x0_ref[0]

    u = _slab_load(u_ref)
    bu_s[...] = _dot(u.astype(BF16), wb_ref[...])
    a_re = jnp.broadcast_to(ab_ref[0:1, :], (g8, S5_CH))
    a_im = jnp.broadcast_to(ab_ref[1:2, :], (g8, S5_CH))

    def step(t, carry):
        x_re, x_im = carry
        r0 = pl.multiple_of(t * g8, g8)
        n_re = a_re * x_re - a_im * x_im + bu_s[pl.ds(r0, g8), 0:S5_CH]
        n_im = a_re * x_im + a_im * x_re + bu_s[pl.ds(r0, g8), S5_CH:]
        xs_s[pl.ds(r0, g8), 0:S5_CH] = n_re
        xs_s[pl.ds(r0, g8), S5_CH:] = n_im
        return n_re, n_im

    x_re, x_im = lax.fori_loop(0, n_tok, step, (x_s[:, 0:S5_CH], x_s[:, S5_CH:]))
    x_s[:, 0:S5_CH] = x_re
    x_s[:, S5_CH:] = x_im
    y = _dot(xs_s[...].astype(BF16), wc_ref[...]) + d_ref[...] * u
    yg = 0.5 * y * (1.0 + jnp.tanh(0.7978845608028654 * (y + 0.044715 * (y * y * y))))
    _slab_store(o_ref, yg * _sigmoid(_dot(yg.astype(BF16), wglu_ref[...])))
    xT_ref[0] = x_s[...]


def _s5_call(u_tm, x0, ab, wb, wc, dvec, wglu, n_grp, t_len):
    n_tok = min(t_len, 128)
    n_t = t_len // n_tok
    rows = n_tok * SEQ_GROUP

    def full(a):
        return pl.BlockSpec(a.shape, lambda b, i: (0,) * a.ndim)

    st_spec = pl.BlockSpec((1, SEQ_GROUP, 2 * S5_CH), lambda b, i: (b, 0, 0))
    return pl.pallas_call(
        functools.partial(_s5_kernel, n_tok=n_tok),
        grid=(n_grp, n_t),
        in_specs=[pl.BlockSpec((None, GROUP_WIDTH // LANES, rows, LANES), lambda b, i: (b, 0, i, 0)), st_spec,
                  full(ab), full(wb), full(wc), full(dvec), full(wglu)],
        out_specs=(pl.BlockSpec((None, GROUP_WIDTH // LANES, rows, LANES), lambda b, i: (b, 0, i, 0)), st_spec),
        out_shape=(jax.ShapeDtypeStruct((n_grp, GROUP_WIDTH // LANES, t_len * SEQ_GROUP, LANES), F32),
                   jax.ShapeDtypeStruct((n_grp, SEQ_GROUP, 2 * S5_CH), F32)),
        scratch_shapes=[pltpu.VMEM((SEQ_GROUP, 2 * S5_CH), F32), pltpu.VMEM((rows, 2 * S5_CH), F32),
                        pltpu.VMEM((rows, 2 * S5_CH), F32)],
        compiler_params=_cparams(2),
        name="s5_t%d" % t_len,
    )(u_tm, x0, ab, wb, wc, dvec, wglu)


CHUNKS = D_MODEL // LANES


def _store_chunked(ref, x):
    rows = x.shape[0]
    for j in range(CHUNKS):
        ref[pl.ds(j, rows, stride=CHUNKS), :] = x[:, j * LANES:(j + 1) * LANES]


def _load_chunked(ref, rows):
    return jnp.concatenate([ref[pl.ds(j, rows, stride=CHUNKS), :] for j in range(CHUNKS)], axis=1)


def _post_mix_kernel(x_ref, oap_ref, obp_ref, ocp_ref, odp_ref, oas_ref, obs_ref, ocs_ref, ods_ref,
                     wout_ref, lnw_ref, lnb_ref, rw_ref, rb_ref,
                     x1_ref, x1c_ref, meta_ref, gate_ref, cnt_ref, carry_s, *, prompt_tiles):
    i = pl.program_id(0)

    @pl.when(i == 0)
    def _():
        carry_s[...] = jnp.zeros_like(carry_s)

    is_prompt = i < prompt_tiles
    seq = i % SEQ_GROUP
    pick = lambda p, s_ref: jnp.where(is_prompt, p, s_ref[...])

    def part(a, j):
        ah, al = _split2(a)
        rows_j = slice(j * GROUP_WIDTH, (j + 1) * GROUP_WIDTH)
        return _dot(ah, wout_ref[0, rows_j, :]) + (_dot(al, wout_ref[0, rows_j, :]) + _dot(ah, wout_ref[1, rows_j, :]))

    mix = part(pick(_slab_load_rows(oap_ref, seq, POST_TILE), oas_ref), 0)
    mix += part(pick(obp_ref[...], obs_ref), 1)
    mix += part(pick(ocp_ref[...], ocs_ref), 2)
    mix += part(pick(_slab_load_rows(odp_ref, seq, POST_TILE), ods_ref), 3)
    x1 = _layer_norm(DN_ALPHA * x_ref[...] + mix, lnw_ref[...], lnb_ref[...])
    x1_ref[...] = x1
    _store_chunked(x1c_ref, x1)

    xh, xl = _split2(x1)
    logits = _dot(xh, rw_ref[0]) + (_dot(xh, rw_ref[1]) + _dot(xl, rw_ref[0])) + rb_ref[...]
    rows = logits.shape[0]
    lt = logits.T[0:N_EXPERTS, :]
    eid = lax.broadcasted_iota(I32, (N_EXPERTS, rows), 0)
    work = lt
    sel_i, sel_v = [], []
    for _ in range(TOP_K):
        m = jnp.max(work, axis=0, keepdims=True)
        j = jnp.min(jnp.where(work == m, eid, N_EXPERTS), axis=0, keepdims=True)
        sel_i.append(j)
        sel_v.append(m)
        work = jnp.where(eid == j, -jnp.inf, work)
    e = [jnp.exp(v - sel_v[0]) for v in sel_v]
    den = (e[0] + e[1]) + (e[2] + e[3])
    onehot = jnp.zeros((N_EXPERTS, rows), F32)
    for j in sel_i:
        onehot = onehot + (eid == j).astype(F32)
    oh = onehot.astype(BF16)
    rr = lax.broadcasted_iota(I32, (rows, rows), 0)
    cc = lax.broadcasted_iota(I32, (rows, rows), 1)
    before = _dot(oh, (rr < cc).astype(BF16)) + carry_s[:, 0:1]
    carry_s[...] = carry_s[...] + _dot(oh, jnp.ones((rows, LANES), BF16))
    ranks = [jnp.sum(jnp.where(eid == j, before, 0.0), axis=0, keepdims=True).astype(I32) for j in sel_i]
    meta_ref[...] = jnp.concatenate(sel_i + ranks, axis=0)
    gate_ref[...] = jnp.concatenate([ej / den for ej in e] + [jnp.zeros((SUBLANES - TOP_K, rows), F32)], axis=0)
    cnt_ref[...] = carry_s[...]


def _post_mix(x_all, outs_p, outs_s, n_seq_p, t_len_p, wout_l, lnw, lnb, rw_l, rb_l):
    n = x_all.shape[0]
    tile = POST_TILE
    n_t = t_len_p // tile
    prompt_tiles = n_seq_p * n_t
    w = GROUP_WIDTH
    p_idx = lambda i: jnp.minimum(i, prompt_tiles - 1)
    p_row = lambda i: (p_idx(i) % n_seq_p) * n_t + p_idx(i) // n_seq_p
    x_row = lambda i: jnp.where(i < prompt_tiles, p_row(i), i)
    row = lambda width: pl.BlockSpec((tile, width), lambda i: (x_row(i), 0))
    tm_p = pl.BlockSpec((w // LANES, tile * SEQ_GROUP, LANES), lambda i: (0, p_idx(i) // n_seq_p, 0))
    bm_p = pl.BlockSpec((tile, w), lambda i: (p_row(i), 0))
    bm_s = pl.BlockSpec((tile, w), lambda i: (jnp.maximum(i - prompt_tiles, 0), 0))
    per_token = pl.BlockSpec((SUBLANES, tile), lambda i: (0, x_row(i)))

    def full(a):
        return pl.BlockSpec(a.shape, lambda i: (0,) * a.ndim)

    x1, x1c, meta, gate, counts = pl.pallas_call(
        functools.partial(_post_mix_kernel, prompt_tiles=prompt_tiles),
        grid=(n // tile,),
        in_specs=[row(D_MODEL), tm_p, bm_p, bm_p, tm_p, bm_s, bm_s, bm_s, bm_s, full(wout_l), full(lnw), full(lnb),
                  full(rw_l), full(rb_l)],
        out_specs=(row(D_MODEL), pl.BlockSpec((tile * CHUNKS, LANES), lambda i: (x_row(i), 0)),
                   per_token, per_token, pl.BlockSpec((N_EXPERTS, LANES), lambda i: (0, 0))),
        out_shape=(jax.ShapeDtypeStruct((n, D_MODEL), F32), jax.ShapeDtypeStruct((n * CHUNKS, LANES), F32),
                   jax.ShapeDtypeStruct((SUBLANES, n), I32), jax.ShapeDtypeStruct((SUBLANES, n), F32),
                   jax.ShapeDtypeStruct((N_EXPERTS, LANES), F32)),
        scratch_shapes=[pltpu.VMEM((N_EXPERTS, LANES), F32)],
        compiler_params=_cparams(1),
        name="post_mix",
    )(x_all, *outs_p, *outs_s, wout_l, lnw, lnb, rw_l, rb_l)
    idx, rank = meta[0:TOP_K].T, meta[TOP_K:2 * TOP_K].T
    gate_cols = jnp.pad(gate[0:TOP_K].T, ((0, 0), (0, LANES - TOP_K)))
    return x1, x1c, idx, rank, gate_cols, counts[:, 0]


def _dispatch_kernel(gend_ref, dest_ref, x_ref, xs_hbm, zero_s, sem, *, tokens, n_tiles):
    i = pl.program_id(0)

    def zero_tile(first_row):
        start = pl.multiple_of(first_row * CHUNKS, MOE_TILE * CHUNKS)
        return pltpu.make_async_copy(zero_s, xs_hbm.at[pl.ds(start, MOE_TILE * CHUNKS)], sem)

    @pl.when(i == 0)
    def _():
        zero_s[...] = jnp.zeros_like(zero_s)
        for e in range(N_EXPERTS):
            @pl.when(gend_ref[e + 1] > gend_ref[e])
            def _():
                zero_tile(gend_ref[e + 1] - MOE_TILE).start()
        for e in range(N_EXPERTS):
            @pl.when(gend_ref[e + 1] > gend_ref[e])
            def _():
                zero_tile(gend_ref[e + 1] - MOE_TILE).wait()

        def tail(t, carry):
            cp = zero_tile(t * MOE_TILE)
            cp.start()
            cp.wait()
            return carry

        lax.fori_loop(gend_ref[N_EXPERTS] // MOE_TILE, n_tiles, tail, 0)

    def row_copy(n, slot):
        src = pl.multiple_of(n * CHUNKS, CHUNKS)
        dst = pl.multiple_of(dest_ref[n * TOP_K + slot] * CHUNKS, CHUNKS)
        return pltpu.make_async_copy(x_ref.at[pl.ds(src, CHUNKS)], xs_hbm.at[pl.ds(dst, CHUNKS)], sem)

    def issue(n, carry):
        for slot in range(TOP_K):
            row_copy(n, slot).start(priority=slot % 2)
        return carry

    lax.fori_loop(0, tokens, issue, 0, unroll=8)
    for slot in range(TOP_K):
        pltpu.make_async_copy(x_ref, xs_hbm.at[pl.ds(0, tokens * CHUNKS)], sem).wait()


def _dispatch(x1c, dest_flat, gend, n_rows):
    n = x1c.shape[0] // CHUNKS
    tokens = DISPATCH_TOKENS if n % DISPATCH_TOKENS == 0 else ROW_TILE
    return pl.pallas_call(
        functools.partial(_dispatch_kernel, tokens=tokens, n_tiles=n_rows // MOE_TILE),
        grid_spec=pltpu.PrefetchScalarGridSpec(
            num_scalar_prefetch=1,
            grid=(n // tokens,),
            in_specs=[pl.BlockSpec((tokens * TOP_K,), lambda i, ge: (i,), memory_space=pltpu.SMEM),
                      pl.BlockSpec((tokens * CHUNKS, LANES), lambda i, ge: (i, 0))],
            out_specs=pl.BlockSpec(memory_space=pl.ANY),
            scratch_shapes=[pltpu.VMEM((MOE_TILE * CHUNKS, LANES), F32), pltpu.SemaphoreType.DMA(())],
        ),
        out_shape=jax.ShapeDtypeStruct((n_rows * CHUNKS, LANES), F32),
        compiler_params=_cparams(1),
        name="moe_dispatch",
    )(gend, dest_flat, x1c)


PAIR_BLOCK = 2 * LANES


def _expert_prep_kernel(w1_ref, w2_ref, w1p_ref, w2b_ref):
    src = lax.broadcasted_iota(I32, (PAIR_BLOCK, PAIR_BLOCK), 0)
    dst = lax.broadcasted_iota(I32, (PAIR_BLOCK, PAIR_BLOCK), 1)
    perm = (src == jnp.where(dst < LANES, 2 * dst, 2 * (dst - LANES) + 1)).astype(BF16)
    for c in range(2 * D_FF // PAIR_BLOCK):
        cols = slice(c * PAIR_BLOCK, (c + 1) * PAIR_BLOCK)
        w1p_ref[:, cols] = _dot(w1_ref[:, cols].astype(BF16), perm).astype(BF16)
    w2b_ref[...] = w2_ref[...].astype(BF16)


def _expert_prep(exp_w1, exp_w2):
    n_l, n_e = exp_w1.shape[:2]
    spec = lambda r, c: pl.BlockSpec((None, None, r, c), lambda i: (i // n_e, i % n_e, 0, 0))
    return pl.pallas_call(
        _expert_prep_kernel,
        grid=(n_l * n_e,),
        in_specs=[spec(D_MODEL, 2 * D_FF), spec(D_FF, D_MODEL)],
        out_specs=(spec(D_MODEL, 2 * D_FF), spec(D_FF, D_MODEL)),
        out_shape=(jax.ShapeDtypeStruct(exp_w1.shape, BF16), jax.ShapeDtypeStruct(exp_w2.shape, BF16)),
        compiler_params=_cparams(1),
        name="expert_prep",
    )(exp_w1, exp_w2)


def _expert_kernel(te_ref, nreal_ref, xs_ref, w1_ref, b1_ref, w2_ref, b2_ref, o_ref, act_s):
    i = pl.program_id(0)
    nreal = nreal_ref[0]

    def hidden(slot):
        x = _load_chunked(xs_ref, MOE_TILE).astype(BF16)
        for c in range(D_FF // PAIR_BLOCK):
            cols = slice(2 * c * PAIR_BLOCK, 2 * (c + 1) * PAIR_BLOCK)
            h = _dot(x, w1_ref[:, cols]) + b1_ref[:, cols]
            h_glu = jnp.minimum(jnp.concatenate([h[:, 0:128], h[:, 256:384]], axis=1), SWIGLU_LIMIT)
            h_lin = jnp.clip(jnp.concatenate([h[:, 128:256], h[:, 384:512]], axis=1),
                             -SWIGLU_LIMIT, SWIGLU_LIMIT)
            act = h_glu * _sigmoid(SWIGLU_ALPHA * h_glu) * (h_lin + 1.0)
            act_s[slot, :, c * PAIR_BLOCK:(c + 1) * PAIR_BLOCK] = act.astype(BF16)

    def output(slot):
        _store_chunked(o_ref, _dot(act_s[slot], w2_ref[...]) + b2_ref[...])

    @pl.when(i == 0)
    def _():
        hidden(0)

    @pl.when((i > 0) & (i < nreal))
    def _():
        output((i - 1) % 2)
        hidden(i % 2)

    @pl.when((i > 0) & (i == nreal))
    def _():
        output((i - 1) % 2)

    @pl.when(i > nreal)
    def _():
        o_ref[...] = jnp.zeros_like(o_ref)


def _experts(xs, te, nreal, w1d, b1d, w2b, b2, layer):
    n_tiles = xs.shape[0] // (MOE_TILE * CHUNKS)
    cur = lambda i: jnp.minimum(i, n_tiles - 1)
    prev = lambda i: jnp.maximum(i - 1, 0)
    return pl.pallas_call(
        _expert_kernel,
        grid_spec=pltpu.PrefetchScalarGridSpec(
            num_scalar_prefetch=2,
            grid=(n_tiles + 1,),
            in_specs=[pl.BlockSpec((MOE_TILE * CHUNKS, LANES), lambda i, te, nr: (jnp.minimum(i, nr[0] - 1), 0)),
                      pl.BlockSpec((None, None, D_MODEL, 2 * D_FF), lambda i, te, nr: (layer, te[cur(i)], 0, 0)),
                      pl.BlockSpec((None, None, 1, 2 * D_FF), lambda i, te, nr: (layer, te[cur(i)], 0, 0)),
                      pl.BlockSpec((None, None, D_FF, D_MODEL), lambda i, te, nr: (layer, te[prev(i)], 0, 0)),
                      pl.BlockSpec((None, None, 1, D_MODEL), lambda i, te, nr: (layer, te[prev(i)], 0, 0))],
            out_specs=pl.BlockSpec((MOE_TILE * CHUNKS, LANES), lambda i, te, nr: (prev(i), 0)),
            scratch_shapes=[pltpu.VMEM((2, MOE_TILE, D_FF), BF16)],
        ),
        out_shape=jax.ShapeDtypeStruct(xs.shape, F32),
        compiler_params=_cparams(1),
        name="moe_experts",
    )(te, nreal, xs, w1d, b1d, w2b, b2)


def _combine_kernel(dest_ref, dest_next_ref, gate_ref, x1_ref, lnw_ref, lnb_ref, ys_hbm, o_ref, buf_s, sem,
                    *, tokens):
    i = pl.program_id(0)
    cur = i % 2

    def fetch(dref, half):
        def issue(n, carry):
            dst = pl.multiple_of(n * CHUNKS, CHUNKS)
            for slot in range(TOP_K):
                src = pl.multiple_of(dref[n * TOP_K + slot] * CHUNKS, CHUNKS)
                pltpu.make_async_copy(ys_hbm.at[pl.ds(src, CHUNKS)], buf_s.at[half, slot, pl.ds(dst, CHUNKS)],
                                      sem.at[half]).start(priority=slot % 2)
            return carry

        lax.fori_loop(0, tokens, issue, 0, unroll=8)

    @pl.when(i == 0)
    def _():
        fetch(dest_ref, 0)

    @pl.when(i + 1 < pl.num_programs(0))
    def _():
        fetch(dest_next_ref, 1 - cur)

    for slot in range(TOP_K):
        pltpu.make_async_copy(ys_hbm.at[pl.ds(0, tokens * CHUNKS)], buf_s.at[cur, slot], sem.at[cur]).wait()
    gate = gate_ref[...]
    ffn = gate[:, 0:1] * _load_chunked(buf_s.at[cur, 0], tokens)
    for slot in range(1, TOP_K):
        ffn = ffn + gate[:, slot:slot + 1] * _load_chunked(buf_s.at[cur, slot], tokens)
    o_ref[...] = _layer_norm(DN_ALPHA * x1_ref[...] + ffn, lnw_ref[...], lnb_ref[...])


def _combine(dest_flat, gate, x1, lnw, lnb, ys):
    n = x1.shape[0]
    tokens = POST_TILE
    last = n // tokens - 1
    return pl.pallas_call(
        functools.partial(_combine_kernel, tokens=tokens),
        grid=(n // tokens,),
        in_specs=[pl.BlockSpec((tokens * TOP_K,), lambda i: (i,), memory_space=pltpu.SMEM),
                  pl.BlockSpec((tokens * TOP_K,), lambda i: (jnp.minimum(i + 1, last),), memory_space=pltpu.SMEM),
                  pl.BlockSpec((tokens, LANES), lambda i: (i, 0)),
                  pl.BlockSpec((tokens, D_MODEL), lambda i: (i, 0)),
                  pl.BlockSpec((1, D_MODEL), lambda i: (0, 0)),
                  pl.BlockSpec((1, D_MODEL), lambda i: (0, 0)),
                  pl.BlockSpec(memory_space=pl.ANY)],
        out_specs=pl.BlockSpec((tokens, D_MODEL), lambda i: (i, 0)),
        out_shape=jax.ShapeDtypeStruct((n, D_MODEL), F32),
        scratch_shapes=[pltpu.VMEM((2, TOP_K, tokens * CHUNKS, LANES), F32), pltpu.SemaphoreType.DMA((2,))],
        compiler_params=_cparams(1),
        name="moe_combine",
    )(dest_flat, dest_flat, gate, x1, lnw, lnb, ys)


def _moe(x1, x1c, idx, rank, gate, counts, w1d, b1d, w2b, b2, lnw, lnb, layer):
    n = x1.shape[0]
    n_tiles = -(-(n * TOP_K + N_EXPERTS * (MOE_TILE - 1)) // MOE_TILE)
    cnt = counts.astype(I32)
    gsz = ((cnt + (MOE_TILE - 1)) // MOE_TILE) * MOE_TILE
    gend = jnp.cumsum(gsz)
    goff = gend - gsz
    dest = (goff[idx] + rank).reshape(-1)
    gend0 = jnp.concatenate([jnp.zeros((1,), I32), gend])
    nreal = (gend[-1:] // MOE_TILE).astype(I32)
    tile_start = jnp.arange(n_tiles, dtype=I32) * MOE_TILE
    te = jnp.minimum(jnp.sum((gend[None, :] <= tile_start[:, None]).astype(I32), axis=1), N_EXPERTS - 1)
    xs = _dispatch(x1c, dest, gend0, n_tiles * MOE_TILE)
    ys = _experts(xs, te, nreal, w1d, b1d, w2b, b2, layer)
    return _combine(dest, gate, x1, lnw, lnb, ys)


def _to_time_major(rows, n_seq, t_len):
    c = rows.shape[-1]
    x = rows.reshape(n_seq // SEQ_GROUP, SEQ_GROUP, t_len, c)
    return jnp.transpose(x, (0, 2, 1, 3)).reshape(n_seq // SEQ_GROUP, t_len * SEQ_GROUP, c)


def _from_time_major(x, n_seq, t_len):
    c = x.shape[-1]
    x = x.reshape(n_seq // SEQ_GROUP, t_len, SEQ_GROUP, c)
    return jnp.transpose(x, (0, 2, 1, 3)).reshape(n_seq * t_len, c)


def _block_diag_state(s, dk):
    st = jnp.swapaxes(s, 2, 3)
    eye = jnp.eye(N_HEADS, dtype=s.dtype)
    return jnp.einsum("bhvk,hg->bhvgk", st, eye).reshape(s.shape[0], GROUP_WIDTH, N_HEADS * dk)


def _unblock_state(st, dk):
    b = st.shape[0]
    x = st.reshape(b, N_HEADS, HEAD_DIM, N_HEADS, dk)
    x = jnp.stack([x[:, h, :, h, :] for h in range(N_HEADS)], axis=1)
    return jnp.swapaxes(x, 2, 3)


def _rwkv_state_in(s):
    b = s.shape[0]
    x = s.reshape(b // SEQ_GROUP, SEQ_GROUP, N_HEADS, HEAD_DIM, HEAD_DIM)
    return jnp.transpose(x, (0, 3, 1, 2, 4)).reshape(b // SEQ_GROUP, HEAD_DIM, SEQ_GROUP, GROUP_WIDTH)


def _rwkv_state_out(x, b):
    x = x.reshape(b // SEQ_GROUP, HEAD_DIM, SEQ_GROUP, N_HEADS, HEAD_DIM)
    return jnp.transpose(x, (0, 2, 3, 1, 4)).reshape(b, N_HEADS, HEAD_DIM, HEAD_DIM)


def _pad_rows(w, row0, n_rows):
    out = jnp.zeros((n_rows, w.shape[1]), w.dtype)
    return out.at[row0:row0 + w.shape[0]].set(w)


def kernel(x_prompt, x_sample, state_rwkv, state_rwkv_shift, state_hgrn, state_gla, state_s5_re, state_s5_im,
           w_in, rw_mu, rw_w0, rw_w2, rw_a0, rw_a2, rw_g2, rw_kk, rw_ka, rw_rk, rw_lnx_w, rw_lnx_b,
           hg_lb_logits, hg_norm_w, gla_w_gk2, gla_b_gk, gla_norm_w,
           s5_A_re, s5_A_im, s5_log_dt, s5_B_re, s5_B_im, s5_C_re, s5_C_im, s5_D, s5_w_glu,
           w_out, ln1_w, ln1_b, router_w, router_b, exp_w1, exp_b1, exp_w2, exp_b2, ln2_w, ln2_b):
    bp, tp, _ = x_prompt.shape
    bs, ts, _ = x_sample.shape
    n_p, n_s = bp * tp, bs * ts
    groups = ((0, bp, tp), (n_p, bs, ts))
    assert bp == SEQ_GROUP and tp % POST_TILE == 0 and bs % SEQ_GROUP == 0 and n_s % POST_TILE == 0
    assert ts <= SUB_CHUNK

    c = np.cumsum([0, RW_COLS, 1024, 784, 256])
    rw_c, hg_c, gl_c, s5_c = (w_in[:, :, c[j]:c[j + 1]] for j in range(4))
    gl_q, gl_k, gl_v, gl_lo, gl_g = (gl_c[:, :, a:b] for a, b in
                                     ((0, 128), (128, 256), (256, 512), (512, 528), (528, 784)))
    zpad = jnp.zeros((DEPTH, D_MODEL, 128 - GLA_GK_LORA), w_in.dtype)
    def two_terms(w):
        hi = w.astype(BF16)
        return jnp.stack([hi, (w - hi.astype(F32)).astype(BF16)], axis=1)

    w_in_p = two_terms(jnp.concatenate([hg_c, gl_v, gl_g, gl_q, gl_k, gl_lo, zpad, rw_c, s5_c], axis=2))
    w_out_b = two_terms(w_out)
    w1d, w2b = _expert_prep(exp_w1, exp_w2)
    b1d = jnp.swapaxes(exp_b1.reshape(DEPTH, N_EXPERTS, 2 * D_FF // PAIR_BLOCK, LANES, 2), -1, -2)
    b1d = b1d.reshape(DEPTH, N_EXPERTS, 1, 2 * D_FF)
    b2r = exp_b2[:, :, None, :]
    rw_pad = jnp.pad(router_w, ((0, 0), (0, 0), (0, LANES - N_EXPERTS)))
    rw_hi = rw_pad.astype(BF16)
    rw_lo = (rw_pad - rw_hi.astype(F32)).astype(BF16)
    rw_split = jnp.stack([rw_hi, rw_lo], axis=1)
    rb_pad = jnp.pad(router_b, ((0, 0), (0, LANES - N_EXPERTS)), constant_values=-1e30)[:, None, :]

    lbs = jnp.cumsum(jax.nn.softmax(hg_lb_logits.astype(F32), axis=0), axis=0)
    lbs = lbs - lbs[:1]
    lb3 = jnp.stack([lbs, jnp.log(lbs), jnp.log1p(-lbs)], axis=1)
    lb3 = jnp.pad(lb3, ((0, 0), (0, SUBLANES - 3), (0, 0)))

    ab_re, ab_im, bb_re, bb_im = _s5_prep(s5_A_re, s5_A_im, s5_log_dt, s5_B_re, s5_B_im)
    eye_g = jnp.eye(S5_NGROUPS, dtype=F32)
    wb = jnp.stack([jnp.einsum("lgph,gk->lghkp", t, eye_g).reshape(DEPTH, GROUP_WIDTH, S5_CH)
                    for t in (bb_re, bb_im)], axis=2).reshape(DEPTH, GROUP_WIDTH, 2 * S5_CH).astype(BF16)
    wc = jnp.concatenate([jnp.einsum("lghp,gk->lgpkh", t, eye_g).reshape(DEPTH, S5_CH, GROUP_WIDTH)
                          for t in (s5_C_re, -s5_C_im)], axis=1).astype(BF16)
    ab = jnp.concatenate([ab_re, ab_im], axis=1)
    ab = jnp.pad(ab, ((0, 0), (0, SUBLANES - 2), (0, 0)))
    wglu_b = s5_w_glu.astype(BF16)

    lora = jnp.stack([jnp.stack([_pad_rows(rw_w2[l], 0, 128), _pad_rows(rw_a2[l], 32, 128),
                                 _pad_rows(rw_g2[l], 64, 128)]) for l in range(DEPTH)])
    rw_vec = jnp.stack([rw_w0, rw_a0, rw_kk, rw_ka, rw_rk, rw_lnx_w, rw_lnx_b, jnp.zeros_like(rw_w0)], axis=1)
    wgk = jnp.stack([_pad_rows(gla_w_gk2[l], 0, 128) for l in range(DEPTH)])

    zeros = lambda shape: jnp.zeros(shape, F32)
    st_in = (
        dict(rw=zeros((DEPTH, bp, N_HEADS, HEAD_DIM, HEAD_DIM)), sh=zeros((DEPTH, bp, RW_COLS)),
             hg=zeros((DEPTH, bp, N_HEADS, HG_DK, HEAD_DIM)), gl=zeros((DEPTH, bp, N_HEADS, GLA_DK, HEAD_DIM)),
             re=zeros((DEPTH, bp, S5_NGROUPS, S5_STATE)), im=zeros((DEPTH, bp, S5_NGROUPS, S5_STATE))),
        dict(rw=state_rwkv, sh=state_rwkv_shift, hg=state_hgrn, gl=state_gla, re=state_s5_re, im=state_s5_im),
    )
    collected = ([], [])

    x_all = jnp.concatenate([x_prompt.reshape(n_p, D_MODEL), x_sample.reshape(n_s, D_MODEL)], axis=0)
    for l in range(DEPTH):
        h_gate_p, p_tm_p, u_tm_p = _inproj_prompt(x_all, w_in_p[l], bp, tp)
        h_s = _inproj_rows(x_all, w_in_p[l], n_p, n_s)
        outs = ([], [])
        for gi, (row0, n_seq, t_len) in enumerate(groups):
            st = st_in[gi]
            n_grp = n_seq // SEQ_GROUP
            if gi == 0:
                h_gate = h_gate_p
                p_tm = p_tm_p[None]
                u_tm = u_tm_p[None]
                new_sh = _from_slabs(p_tm_p[:, (t_len - 1) * n_seq:, :])
                to_rows = lambda x_tm: x_tm[0]
            else:
                h_gate = h_s
                p_rw = h_s[:, OFF_RW:OFF_RW + RW_COLS]
                p_tm = _to_slabs(_to_time_major(p_rw, n_seq, t_len))
                u_tm = _to_slabs(_to_time_major(h_s[:, OFF_S5:OFF_S5 + GROUP_WIDTH], n_seq, t_len))
                new_sh = p_rw.reshape(n_seq, t_len, RW_COLS)[:, -1]
                to_rows = lambda x_tm: _from_time_major(_from_slabs(x_tm), n_seq, t_len)
            sh0 = st["sh"][l].reshape(n_grp, SEQ_GROUP, RW_COLS)
            oa_tm, s_rw = _rwkv_call(p_tm, sh0, _rwkv_state_in(st["rw"][l]), rw_mu[l][None, :], lora[l],
                                     rw_vec[l], n_grp, t_len)
            new_rw = _rwkv_state_out(s_rw, n_seq)
            ob, oc, s_hg, s_gl = _gated_call(
                h_gate, n_seq, t_len, _block_diag_state(st["hg"][l], HG_DK), _block_diag_state(st["gl"][l], GLA_DK),
                (lb3[l], hg_norm_w[l][None, :]), (wgk[l], gla_b_gk[l][None, :], gla_norm_w[l][None, :]))
            x0 = jnp.concatenate([st["re"][l].reshape(n_grp, SEQ_GROUP, S5_CH),
                                  st["im"][l].reshape(n_grp, SEQ_GROUP, S5_CH)], axis=-1)
            od_tm, x_t = _s5_call(u_tm, x0, ab[l], wb[l], wc[l], s5_D[l][None, :], wglu_b[l], n_grp, t_len)
            outs[gi].extend([to_rows(oa_tm), ob, oc, to_rows(od_tm)])
            x_t = x_t.reshape(n_seq, 2, S5_NGROUPS, S5_STATE)
            collected[gi].append((new_rw, new_sh, _unblock_state(s_hg, HG_DK), _unblock_state(s_gl, GLA_DK),
                                  x_t[:, 0], x_t[:, 1]))
        x1, x1c, idx, rank, gate, counts = _post_mix(x_all, outs[0], outs[1], bp, tp, w_out_b[l],
                                                     ln1_w[l][None, :], ln1_b[l][None, :],
                                                     rw_split[l], rb_pad[l])
        x_all = _moe(x1, x1c, idx, rank, gate, counts, w1d, b1d, w2b, b2r, ln2_w[l][None, :], ln2_b[l][None, :], l)

    y_prompt = x_all[:n_p].reshape(bp, tp, D_MODEL)
    y_sample = x_all[n_p:].reshape(bs, ts, D_MODEL)
    ps = [jnp.stack([layer[j] for layer in collected[0]]) for j in range(6)]
    ss = [jnp.stack([layer[j] for layer in collected[1]]) for j in range(6)]
    return (y_prompt, y_sample, *ps, *ss)
```

```python
import functools

import jax
import jax.numpy as jnp
import numpy as np
from jax import lax
from jax.experimental import pallas as pl
from jax.experimental.pallas import tpu as pltpu

F32 = jnp.float32
BF16 = jnp.bfloat16
I32 = jnp.int32

D_MODEL = 1024
DEPTH = 4
GROUP_WIDTH = 256
HEAD_DIM = 64
N_HEADS = 4
RW_COLS = 896
RW_GN_EPS = 64e-5
HG_DK = 64
GLA_DK = 32
GLA_GK_LORA = 16
GLA_GATE_NORM = 16.0
S5_NGROUPS = 16
S5_GROUP = 16
S5_STATE = 64
S5_CH = S5_NGROUPS * S5_STATE
N_EXPERTS = 32
TOP_K = 4
D_FF = 1024
SWIGLU_ALPHA = 1.702
SWIGLU_LIMIT = 7.0
DN_ALPHA = (2.0 * DEPTH) ** 0.25
LN_EPS = 1e-5

SUBLANES = 8
LANES = 128
VMEM_LIMIT_BYTES = 56 * 1024 * 1024

OFF_HG = 0
OFF_GLA_VG = 1024
OFF_GLA_QK = 1536
OFF_RW = 1920
OFF_S5 = 2816
IN_PAD = 3072

ROW_TILE = 256
POST_TILE = 512
SUB_CHUNK = 16
SEQ_GROUP = SUBLANES
MOE_TILE = 512
DISPATCH_TOKENS = 512


def _cparams(n_axes):
    return pltpu.CompilerParams(dimension_semantics=("arbitrary",) * n_axes,
                                vmem_limit_bytes=VMEM_LIMIT_BYTES)


def _dot(a, b):
    return jnp.dot(a, b, preferred_element_type=F32)


def _split2(x):
    hi = x.astype(BF16)
    lo = (x - hi.astype(F32)).astype(BF16)
    return hi, lo


def _split3(x):
    hi = x.astype(BF16)
    r = x - hi.astype(F32)
    mid = r.astype(BF16)
    lo = (r - mid.astype(F32)).astype(BF16)
    return hi, mid, lo


def _dot3(a, b):
    ah, al = _split2(a)
    bh, bl = _split2(b)
    return _dot(ah, bh) + (_dot(ah, bl) + _dot(al, bh))


def _seg_ones(n_in, seg_in, n_out, seg_out):
    r = lax.broadcasted_iota(I32, (n_in, n_out), 0) // seg_in
    c = lax.broadcasted_iota(I32, (n_in, n_out), 1) // seg_out
    return (r == c).astype(BF16)


def _segsum(x, seg):
    rows = x.shape[0]
    hi, lo = _split2(x)
    both = _dot(jnp.concatenate([hi, lo], axis=0), seg)
    return both[:rows] + both[rows:]


def _sigmoid(x):
    return 1.0 / (1.0 + jnp.exp(-x))


def _log_sigmoid(x):
    return jnp.minimum(x, 0.0) - jnp.log1p(jnp.exp(-jnp.abs(x)))


def _softplus(x):
    return jnp.maximum(x, 0.0) + jnp.log1p(jnp.exp(-jnp.abs(x)))


def _layer_norm(x, w, b):
    xc = x - jnp.mean(x, axis=-1, keepdims=True)
    var = jnp.mean(xc * xc, axis=-1, keepdims=True)
    return xc * lax.rsqrt(var + LN_EPS) * w + b


def _project3(x, w_ref, cols):
    xh, xl = _split2(x)
    return _dot(xh, w_ref[0, :, cols]) + (_dot(xl, w_ref[0, :, cols]) + _dot(xh, w_ref[1, :, cols]))


def _inproj_all(x_ref, w_ref):
    x = x_ref[...]
    return jnp.concatenate([_project3(x, w_ref, slice(0, OFF_S5)),
                            _dot(x.astype(BF16), w_ref[0, :, OFF_S5:IN_PAD])], axis=1)


def _inproj_kernel(x_ref, w_ref, *o_refs):
    h = _inproj_all(x_ref, w_ref)
    col = 0
    for o_ref in o_refs:
        o_ref[...] = h[:, col:col + o_ref.shape[-1]]
        col += o_ref.shape[-1]


def _slab_store_rows(ref, seq, x):
    for j in range(ref.shape[0]):
        ref.at[j][pl.ds(seq, x.shape[0], stride=SEQ_GROUP), :] = x[:, j * LANES:(j + 1) * LANES]


def _slab_load_rows(ref, seq, tokens):
    return jnp.concatenate([ref.at[j][pl.ds(seq, tokens, stride=SEQ_GROUP), :] for j in range(ref.shape[0])],
                           axis=1)


def _slab_load(ref):
    return jnp.concatenate([ref[j] for j in range(ref.shape[0])], axis=1)


def _slab_store(ref, x):
    for j in range(ref.shape[0]):
        ref[j] = x[:, j * LANES:(j + 1) * LANES]


def _to_slabs(x):
    s = x.reshape(x.shape[:-1] + (x.shape[-1] // LANES, LANES))
    return jnp.swapaxes(s, -2, -3)


def _from_slabs(x):
    s = jnp.swapaxes(x, -2, -3)
    return s.reshape(s.shape[:-2] + (s.shape[-2] * LANES,))


def _inproj_prompt_kernel(x_ref, w_ref, g_ref, p_ref, u_ref):
    seq = pl.program_id(1)
    h = _inproj_all(x_ref, w_ref)
    g_ref[...] = h[:, 0:OFF_RW]
    _slab_store_rows(p_ref, seq, h[:, OFF_RW:OFF_RW + RW_COLS])
    _slab_store_rows(u_ref, seq, h[:, OFF_S5:OFF_S5 + GROUP_WIDTH])


def _inproj_prompt(x_all, w_in_l, n_seq, t_len):
    n_t = t_len // ROW_TILE
    slab = lambda c: pl.BlockSpec((c // LANES, ROW_TILE * SEQ_GROUP, LANES), lambda i, b: (0, i, 0))
    return pl.pallas_call(
        _inproj_prompt_kernel,
        grid=(n_t, n_seq),
        in_specs=[pl.BlockSpec((ROW_TILE, D_MODEL), lambda i, b: (b * n_t + i, 0)),
                  pl.BlockSpec((2, D_MODEL, IN_PAD), lambda i, b: (0, 0, 0))],
        out_specs=(pl.BlockSpec((ROW_TILE, OFF_RW), lambda i, b: (b * n_t + i, 0)),
                   slab(RW_COLS), slab(GROUP_WIDTH)),
        out_shape=(jax.ShapeDtypeStruct((n_seq * t_len, OFF_RW), F32),
                   jax.ShapeDtypeStruct((RW_COLS // LANES, t_len * n_seq, LANES), F32),
                   jax.ShapeDtypeStruct((GROUP_WIDTH // LANES, t_len * n_seq, LANES), F32)),
        compiler_params=_cparams(2),
        name="inproj_prompt",
    )(x_all, w_in_l)


def _inproj_rows(x_all, w_in_l, row0, n_rows):
    blk0 = row0 // ROW_TILE
    return pl.pallas_call(
        _inproj_kernel,
        grid=(n_rows // ROW_TILE,),
        in_specs=[pl.BlockSpec((ROW_TILE, D_MODEL), lambda i: (blk0 + i, 0)),
                  pl.BlockSpec((2, D_MODEL, IN_PAD), lambda i: (0, 0, 0))],
        out_specs=pl.BlockSpec((ROW_TILE, IN_PAD), lambda i: (i, 0)),
        out_shape=jax.ShapeDtypeStruct((n_rows, IN_PAD), F32),
        compiler_params=_cparams(1),
        name="inproj_rows",
    )(x_all, w_in_l)


def _gated_tile(q, k, v, g, st_ref, o_ref, q_s, k_s, v_s, b_s, qh_s, kh_s, dt_s, *, dk):
    c = SUB_CHUNK
    rows, hk = q.shape
    rr = lax.broadcasted_iota(I32, (rows, rows), 0)
    cc = lax.broadcasted_iota(I32, (rows, rows), 1)
    same = (rr // c) == (cc // c)
    tri = jnp.concatenate([(same & (cc <= rr)).astype(BF16), same.astype(BF16)], axis=0)
    g3 = jnp.concatenate(_split3(g), axis=1)
    p = _dot(tri, g3)
    b = p[:rows, :hk] + p[:rows, hk:2 * hk] + p[:rows, 2 * hk:]
    btot = p[rows:, :hk] + p[rows:, hk:2 * hk] + p[rows:, 2 * hk:]
    q_s[...] = q
    k_s[...] = k
    v_s[...] = v
    b_s[...] = b
    qh_s[...] = q * jnp.exp(b)
    kh_s[...] = k * jnp.exp(btot - b)
    dt_s[...] = jnp.exp(btot)
    seg = _seg_ones(hk, dk, GROUP_WIDTH, HEAD_DIM)
    bd_mask = (lax.broadcasted_iota(I32, (GROUP_WIDTH, hk), 0) // HEAD_DIM
               == lax.broadcasted_iota(I32, (GROUP_WIDTH, hk), 1) // dk).astype(F32)
    t_sub = lax.broadcasted_iota(I32, (SUBLANES, hk), 0)

    def block(i):
        r0 = pl.multiple_of(i * c, c)
        vb = v_s[pl.ds(r0, c), :]
        n_sub = c // SUBLANES
        q_sub = [q_s[pl.ds(r0 + j * SUBLANES, SUBLANES), :] for j in range(n_sub)]
        b_sub = [b_s[pl.ds(r0 + j * SUBLANES, SUBLANES), :] for j in range(n_sub)]
        pieces = []
        for s in range(c):
            b_row = b_s[pl.ds(r0 + s, 1), :]
            k_row = k_s[pl.ds(r0 + s, 1), :]
            for j in range(s // SUBLANES, n_sub):
                d = jnp.where(t_sub + j * SUBLANES >= s, b_sub[j] - b_row, -jnp.inf)
                pieces.append(jnp.exp(d) * q_sub[j] * k_row)
        att = _segsum(jnp.concatenate(pieces, axis=0), seg)
        o_parts = [None] * n_sub
        row = 0
        for s in range(c):
            v_row = v_s[pl.ds(r0 + s, 1), :]
            for j in range(s // SUBLANES, n_sub):
                part = att[row:row + SUBLANES, :] * v_row
                row += SUBLANES
                o_parts[j] = part if o_parts[j] is None else o_parts[j] + part
        o = jnp.concatenate(o_parts, axis=0)
        si = i if st_ref.shape[0] > 1 else 0
        st = st_ref[si]
        o = o + lax.dot_general(qh_s[pl.ds(r0, c), :].astype(BF16), st.astype(BF16),
                                (((1,), (1,)), ((), ())), preferred_element_type=F32)
        upd = lax.dot_general(vb.astype(BF16), kh_s[pl.ds(r0, c), :].astype(BF16),
                              (((0,), (0,)), ((), ())), preferred_element_type=F32)
        st_ref[si] = st * dt_s[pl.ds(r0, 1), :] + upd * bd_mask
        o_ref[pl.ds(r0, c), :] = o

    return block


def _run_blocks(blocks, n_blk):
    def body(i, carry):
        for blk in blocks:
            blk(i)
        return carry

    lax.fori_loop(0, n_blk, body, 0, unroll=min(n_blk, max(1, 16 // len(blocks))))


def _rms_heads(o, w, gate, seg):
    ms = _segsum(o * o, seg) * (1.0 / HEAD_DIM)
    return o * lax.rsqrt(ms + LN_EPS) * w * (gate * _sigmoid(gate))


def _load_rows(ref, sample, pad_s, t_valid):
    if not sample:
        return ref[...]
    pad_s[...] = jnp.zeros_like(pad_s)
    for g in range(ref.shape[0]):
        pad_s[g * SUB_CHUNK:g * SUB_CHUNK + t_valid, :] = ref[g]
    return pad_s[...]


def _store_rows(o_ref, out, sample, t_valid):
    if not sample:
        o_ref[...] = out
        return
    for g in range(o_ref.shape[0]):
        o_ref[g] = out[g * SUB_CHUNK:g * SUB_CHUNK + t_valid, :]


def _valid_rows(rows, width, t_valid):
    return lax.broadcasted_iota(I32, (rows, width), 0) % SUB_CHUNK < t_valid


def _init_state(st_s, s0_ref, sample):
    if sample:
        st_s[...] = s0_ref[...]
    else:
        @pl.when(pl.program_id(1) == 0)
        def _():
            st_s[...] = s0_ref[...]


def _hgrn_parts(h_ref, s0_ref, lb_ref, nw_ref, o_ref, sT_ref,
                st_s, o_s, q_s, k_s, v_s, b_s, qh_s, kh_s, dt_s, pad, *, sample, t_valid):
    _init_state(st_s, s0_ref, sample)
    x = _load_rows(h_ref, sample, pad[0] if sample else None, t_valid)
    rows = x.shape[0]
    q = x[:, 0:256]
    fx = x[:, 256:512]
    iv = x[:, 512:768]
    gate = x[:, 768:1024]
    lb = lb_ref[0:1, :]
    log_lb = lb_ref[1:2, :]
    log1m_lb = lb_ref[2:3, :]
    cterm = log1m_lb + _log_sigmoid(fx)
    log_f = jnp.maximum(log_lb, cterm) + jnp.log1p(jnp.exp(-jnp.abs(log_lb - cterm)))
    key = (1.0 - lb) * _sigmoid(-fx)
    qs = q * _sigmoid(q) * (HG_DK ** -0.5)
    if sample:
        valid = _valid_rows(rows, GROUP_WIDTH, t_valid)
        log_f = jnp.where(valid, log_f, 0.0)
        key = jnp.where(valid, key, 0.0)
        iv = jnp.where(valid, iv, 0.0)
    block = _gated_tile(qs, key, iv, log_f, st_s, o_s, q_s, k_s, v_s, b_s, qh_s, kh_s, dt_s, dk=HG_DK)

    def finish():
        seg = _seg_ones(GROUP_WIDTH, HEAD_DIM, GROUP_WIDTH, HEAD_DIM)
        _store_rows(o_ref, _rms_heads(o_s[...], nw_ref[...], gate, seg), sample, t_valid)
        sT_ref[...] = st_s[...]

    return block, finish


def _gla_parts(hvg_ref, hqk_ref, s0_ref, wgk_ref, bgk_ref, nw_ref, o_ref, sT_ref,
               st_s, o_s, q_s, k_s, v_s, b_s, qh_s, kh_s, dt_s, pad, *, sample, t_valid):
    _init_state(st_s, s0_ref, sample)
    xvg = _load_rows(hvg_ref, sample, pad[0] if sample else None, t_valid)
    xqk = _load_rows(hqk_ref, sample, pad[1] if sample else None, t_valid)
    rows = xvg.shape[0]
    v = xvg[:, 0:256]
    gate = xvg[:, 256:512]
    q = xqk[:, 0:128] * (GLA_DK ** -0.5)
    k = xqk[:, 128:256]
    lo = xqk[:, 256:384]
    gk = _log_sigmoid(_dot3(lo, wgk_ref[...]) + bgk_ref[...]) * (1.0 / GLA_GATE_NORM)
    if sample:
        valid = _valid_rows(rows, 128, t_valid)
        gk = jnp.where(valid, gk, 0.0)
        k = jnp.where(valid, k, 0.0)
    block = _gated_tile(q, k, v, gk, st_s, o_s, q_s, k_s, v_s, b_s, qh_s, kh_s, dt_s, dk=GLA_DK)

    def finish():
        seg = _seg_ones(GROUP_WIDTH, HEAD_DIM, GROUP_WIDTH, HEAD_DIM)
        _store_rows(o_ref, _rms_heads(o_s[...], nw_ref[...], gate, seg), sample, t_valid)
        sT_ref[...] = st_s[...]

    return block, finish


N_GATED_SCRATCH = 9


def _gated_pair_kernel(h_ref, hvg_ref, hqk_ref, s0h_ref, s0g_ref, lb_ref, nwh_ref, wgk_ref, bgk_ref, nwg_ref,
                       oh_ref, og_ref, sTh_ref, sTg_ref, *scr, sample, t_valid, n_blk):
    hg_s, gl_s, pad = scr[:N_GATED_SCRATCH], scr[N_GATED_SCRATCH:2 * N_GATED_SCRATCH], scr[2 * N_GATED_SCRATCH:]
    blk_h, fin_h = _hgrn_parts(h_ref, s0h_ref, lb_ref, nwh_ref, oh_ref, sTh_ref, *hg_s, pad[0:1],
                               sample=sample, t_valid=t_valid)
    blk_g, fin_g = _gla_parts(hvg_ref, hqk_ref, s0g_ref, wgk_ref, bgk_ref, nwg_ref, og_ref, sTg_ref, *gl_s, pad[1:3],
                              sample=sample, t_valid=t_valid)
    _run_blocks((blk_h, blk_g), n_blk)
    fin_h()
    fin_g()


def _gated_call(h_all, n_seq, t_len, s0_hg, s0_gl, hg_params, gl_params):
    sample = t_len < SUB_CHUNK
    row0 = 0
    if sample:
        per_step = SEQ_GROUP
        rows, n_t, n_blk = per_step * SUB_CHUNK, 1, per_step
        h_view = h_all.reshape(h_all.shape[0] // t_len, t_len, h_all.shape[1])
        seq0 = row0 // (t_len * per_step)

        def hspec(width, col_block):
            return pl.BlockSpec((per_step, t_len, width), lambda b, i: (seq0 + b, 0, col_block))

        o_shape = jax.ShapeDtypeStruct((n_seq, t_len, GROUP_WIDTH), F32)
        o_spec = pl.BlockSpec((per_step, t_len, GROUP_WIDTH), lambda b, i: (b, 0, 0))
    else:
        per_step = 1
        rows = min(t_len, ROW_TILE)
        n_t, n_blk = t_len // rows, rows // SUB_CHUNK
        h_view = h_all
        blk0 = row0 // rows

        def hspec(width, col_block):
            return pl.BlockSpec((rows, width), lambda b, i: (blk0 + b * n_t + i, col_block))

        o_shape = jax.ShapeDtypeStruct((n_seq * t_len, GROUP_WIDTH), F32)
        o_spec = pl.BlockSpec((rows, GROUP_WIDTH), lambda b, i: (b * n_t + i, 0))

    hk_hg, hk_gl = N_HEADS * HG_DK, N_HEADS * GLA_DK
    st_spec = lambda hk: pl.BlockSpec((per_step, GROUP_WIDTH, hk), lambda b, i: (b, 0, 0))
    st_shape = lambda hk: jax.ShapeDtypeStruct((n_seq, GROUP_WIDTH, hk), F32)

    def full(a):
        return pl.BlockSpec(a.shape, lambda b, i: (0,) * a.ndim)

    def recurrence_scratch(hk):
        vm = lambda r, w: pltpu.VMEM((r, w), F32)
        return [pltpu.VMEM((per_step, GROUP_WIDTH, hk), F32), vm(rows, GROUP_WIDTH), vm(rows, hk), vm(rows, hk),
                vm(rows, GROUP_WIDTH), vm(rows, hk), vm(rows, hk), vm(rows, hk), vm(rows, hk)]

    scratch = recurrence_scratch(hk_hg) + recurrence_scratch(hk_gl)
    if sample:
        scratch += [pltpu.VMEM((rows, 1024), F32), pltpu.VMEM((rows, 512), F32), pltpu.VMEM((rows, 384), F32)]
    lb3, nw_hg = hg_params
    wgk, bgk, nw_gl = gl_params
    o_hg, o_gl, st_hg, st_gl = pl.pallas_call(
        functools.partial(_gated_pair_kernel, sample=sample, t_valid=t_len, n_blk=n_blk),
        grid=(n_seq // per_step, n_t),
        in_specs=[hspec(1024, OFF_HG // 1024), hspec(512, OFF_GLA_VG // 512), hspec(384, OFF_GLA_QK // 384),
                  st_spec(hk_hg), st_spec(hk_gl), full(lb3), full(nw_hg), full(wgk), full(bgk), full(nw_gl)],
        out_specs=(o_spec, o_spec, st_spec(hk_hg), st_spec(hk_gl)),
        out_shape=(o_shape, o_shape, st_shape(hk_hg), st_shape(hk_gl)),
        scratch_shapes=scratch,
        compiler_params=_cparams(2),
        name="gated_sample" if sample else "gated_prompt",
    )(h_view, h_view, h_view, s0_hg, s0_gl, lb3, nw_hg, wgk, bgk, nw_gl)
    flat = lambda o: o.reshape(n_seq * t_len, GROUP_WIDTH)
    return flat(o_hg), flat(o_gl), st_hg, st_gl


def _rwkv_kernel(p_ref, sh0_ref, s0_ref, mu_ref, lora_ref, vec_ref, o_ref, s_out_ref,
                 s_s, prev_s, seg_s, lm_s, a_s, b_s, w_s, k_s, r_s, v_s, y_s, *, n_tok):
    i = pl.program_id(1)
    g8 = SEQ_GROUP

    @pl.when(i == 0)
    def _():
        s_s[...] = s0_ref[0]
        prev_s[...] = sh0_ref[0]

    seg_s[...] = _seg_ones(GROUP_WIDTH, HEAD_DIM, GROUP_WIDTH, HEAD_DIM)
    lm_s[...] = (lax.broadcasted_iota(I32, (HEAD_DIM, g8, GROUP_WIDTH), 2) % HEAD_DIM
                 == lax.broadcasted_iota(I32, (HEAD_DIM, g8, GROUP_WIDTH), 0)).astype(F32)
    seg = seg_s[...]

    p = _slab_load(p_ref)
    if n_tok > 1:
        prev = jnp.concatenate([prev_s[...], p[:-g8, :]], axis=0)
    else:
        prev = prev_s[...]
    prev_s[...] = p[(n_tok - 1) * g8:, :]
    xs = p + (prev - p) * mu_ref[...]
    r = xs[:, 0:256]
    k = xs[:, 256:512]
    v = xs[:, 512:768]
    lo = xs[:, 768:896]
    w0, a0, kkp, ka = vec_ref[0:1, :], vec_ref[1:2, :], vec_ref[2:3, :], vec_ref[3:4, :]
    rk, lnw, lnb = vec_ref[4:5, :], vec_ref[5:6, :], vec_ref[6:7, :]
    w_log = -_softplus(-(w0 + _dot3(jnp.tanh(lo), lora_ref[0]))) - 0.5
    decay = jnp.exp(-jnp.exp(w_log))
    a = _sigmoid(a0 + _dot3(lo, lora_ref[1]))
    g = _dot3(_sigmoid(lo), lora_ref[2])
    kk = k * kkp
    kk = kk * lax.rsqrt(jnp.maximum(_segsum(kk * kk, seg), 1e-24))
    k2 = k * (1.0 + (a - 1.0) * ka)
    a_s[...] = -kk
    b_s[...] = kk * a
    w_s[...] = decay
    k_s[...] = k2
    r_s[...] = r
    v_s[...] = v

    n = HEAD_DIM * g8
    slab = (HEAD_DIM, g8, GROUP_WIDTH)

    def readout(s, row):
        yb = _dot((s * r_s[pl.ds(row, g8), :][None]).reshape(n, GROUP_WIDTH).astype(BF16), seg_s[...])
        return jnp.sum(yb.reshape(slab) * lm_s[...], axis=0)

    def step(t, carry):
        r0 = pl.multiple_of(t * g8, g8)
        rp = pl.multiple_of(jnp.maximum(t - 1, 0) * g8, g8)
        s = s_s[...]
        lm = lm_s[...]
        sg = seg_s[...]
        y_s[pl.ds(rp, g8), :] = readout(s, rp)
        sa = _dot((s * a_s[pl.ds(r0, g8), :][None]).reshape(n, GROUP_WIDTH).astype(BF16), sg).reshape(slab)
        vh, vl = _split2(v_s[pl.ds(r0, g8), :])
        vm = jnp.concatenate([(vh.astype(F32)[None] * lm).reshape(n, GROUP_WIDTH).astype(BF16),
                              (vl.astype(F32)[None] * lm).reshape(n, GROUP_WIDTH).astype(BF16)], axis=0)
        vb2 = _dot(vm, sg)
        vb = (vb2[0:n] + vb2[n:]).reshape(slab)
        s_s[...] = (s * w_s[pl.ds(r0, g8), :][None] + sa * b_s[pl.ds(r0, g8), :][None]
                    + vb * k_s[pl.ds(r0, g8), :][None])
        return carry

    lax.fori_loop(0, n_tok, step, 0, unroll=16)
    last = (n_tok - 1) * g8
    y_s[pl.ds(last, g8), :] = readout(s_s[...], last)

    y = y_s[...]
    mean = _segsum(y, seg) * (1.0 / HEAD_DIM)
    yc = y - mean
    var = _segsum(yc * yc, seg) * (1.0 / HEAD_DIM)
    yn = yc * lax.rsqrt(var + RW_GN_EPS) * lnw + lnb
    bonus = _segsum(r * k2 * rk, seg) * v
    _slab_store(o_ref, (yn + bonus) * g)
    s_out_ref[0] = s_s[...]


def _rwkv_call(p_tm, shift0, s0, mu, lora, vec, n_grp, t_len):
    n_tok = min(t_len, 64)
    n_t = t_len // n_tok
    rows = n_tok * SEQ_GROUP

    def full(a):
        return pl.BlockSpec(a.shape, lambda b, i: (0,) * a.ndim)

    st_spec = pl.BlockSpec((1, HEAD_DIM, SEQ_GROUP, GROUP_WIDTH), lambda b, i: (b, 0, 0, 0))
    vm = lambda shape, dt=F32: pltpu.VMEM(shape, dt)
    out, s_out = pl.pallas_call(
        functools.partial(_rwkv_kernel, n_tok=n_tok),
        grid=(n_grp, n_t),
        in_specs=[pl.BlockSpec((None, RW_COLS // LANES, rows, LANES), lambda b, i: (b, 0, i, 0)),
                  pl.BlockSpec((1, SEQ_GROUP, RW_COLS), lambda b, i: (b, 0, 0)),
                  st_spec, full(mu), full(lora), full(vec)],
        out_specs=(pl.BlockSpec((None, GROUP_WIDTH // LANES, rows, LANES), lambda b, i: (b, 0, i, 0)), st_spec),
        out_shape=(jax.ShapeDtypeStruct((n_grp, GROUP_WIDTH // LANES, t_len * SEQ_GROUP, LANES), F32),
                   jax.ShapeDtypeStruct((n_grp, HEAD_DIM, SEQ_GROUP, GROUP_WIDTH), F32)),
        scratch_shapes=[vm((HEAD_DIM, SEQ_GROUP, GROUP_WIDTH)), vm((SEQ_GROUP, RW_COLS)),
                        vm((GROUP_WIDTH, GROUP_WIDTH), BF16), vm((HEAD_DIM, SEQ_GROUP, GROUP_WIDTH)),
                        vm((rows, GROUP_WIDTH)), vm((rows, GROUP_WIDTH)), vm((rows, GROUP_WIDTH)),
                        vm((rows, GROUP_WIDTH)), vm((rows, GROUP_WIDTH)), vm((rows, GROUP_WIDTH)),
                        vm((rows, GROUP_WIDTH))],
        compiler_params=_cparams(2),
        name="rwkv_t%d" % t_len,
    )(p_tm, shift0, s0, mu, lora, vec)
    return out, s_out


def _s5_prep_kernel(are_ref, aim_ref, ldt_ref, bre_ref, bim_ref, abre_ref, abim_ref, bbre_ref, bbim_ref):
    a_re, a_im = are_ref[...], aim_ref[...]
    dt = jnp.exp(ldt_ref[...])
    mag = jnp.exp(a_re * dt)
    ab_re = mag * jnp.cos(a_im * dt)
    ab_im = mag * jnp.sin(a_im * dt)
    den = a_re * a_re + a_im * a_im
    nr, ni = ab_re - 1.0, ab_im
    coef_re = (nr * a_re + ni * a_im) / den
    coef_im = (ni * a_re - nr * a_im) / den
    b_re, b_im = bre_ref[...], bim_ref[...]
    abre_ref[...] = ab_re
    abim_ref[...] = ab_im
    bbre_ref[...] = coef_re * b_re - coef_im * b_im
    bbim_ref[...] = coef_re * b_im + coef_im * b_re


def _s5_prep(a_re, a_im, log_dt, b_re, b_im):
    rows = DEPTH * S5_NGROUPS
    cols = S5_STATE * S5_GROUP
    rep = lambda t: jnp.repeat(t.reshape(rows, S5_STATE), S5_GROUP, axis=1)
    ldt = jnp.broadcast_to(log_dt.reshape(rows, 1), (rows, cols))
    shp = jax.ShapeDtypeStruct((rows, cols), F32)
    ab_re, ab_im, bb_re, bb_im = pl.pallas_call(
        _s5_prep_kernel, out_shape=(shp, shp, shp, shp), name="s5_prep",
    )(rep(a_re), rep(a_im), ldt, b_re.reshape(rows, cols), b_im.reshape(rows, cols))
    pick = lambda t: t.reshape(DEPTH, S5_NGROUPS, S5_STATE, S5_GROUP)[..., 0].reshape(DEPTH, 1, S5_CH)
    bb = lambda t: t.reshape(DEPTH, S5_NGROUPS, S5_STATE, S5_GROUP)
    return pick(ab_re), pick(ab_im), bb(bb_re), bb(bb_im)


def _s5_kernel(u_ref, x0_ref, ab_ref, wb_ref, wc_ref, d_ref, wglu_ref, o_ref, xT_ref,
               x_s, bu_s, xs_s, *, n_tok):
    i = pl.program_id(1)
    g8 = SEQ_GROUP

    @pl.when(i == 0)
    def _():
        x_s[...] = x0_ref[0]

    u = _slab_load(u_ref)
    bu_s[...] = _dot(u.astype(BF16), wb_ref[...])
    a_re = jnp.broadcast_to(ab_ref[0:1, :], (g8, S5_CH))
    a_im = jnp.broadcast_to(ab_ref[1:2, :], (g8, S5_CH))

    def step(t, carry):
        x_re, x_im = carry
        r0 = pl.multiple_of(t * g8, g8)
        n_re = a_re * x_re - a_im * x_im + bu_s[pl.ds(r0, g8), 0:S5_CH]
        n_im = a_re * x_im + a_im * x_re + bu_s[pl.ds(r0, g8), S5_CH:]
        xs_s[pl.ds(r0, g8), 0:S5_CH] = n_re
        xs_s[pl.ds(r0, g8), S5_CH:] = n_im
        return n_re, n_im

    x_re, x_im = lax.fori_loop(0, n_tok, step, (x_s[:, 0:S5_CH], x_s[:, S5_CH:]))
    x_s[:, 0:S5_CH] = x_re
    x_s[:, S5_CH:] = x_im
    y = _dot(xs_s[...].astype(BF16), wc_ref[...]) + d_ref[...] * u
    yg = 0.5 * y * (1.0 + jnp.tanh(0.7978845608028654 * (y + 0.044715 * (y * y * y))))
    _slab_store(o_ref, yg * _sigmoid(_dot(yg.astype(BF16), wglu_ref[...])))
    xT_ref[0] = x_s[...]


def _s5_call(u_tm, x0, ab, wb, wc, dvec, wglu, n_grp, t_len):
    n_tok = min(t_len, 128)
    n_t = t_len // n_tok
    rows = n_tok * SEQ_GROUP

    def full(a):
        return pl.BlockSpec(a.shape, lambda b, i: (0,) * a.ndim)

    st_spec = pl.BlockSpec((1, SEQ_GROUP, 2 * S5_CH), lambda b, i: (b, 0, 0))
    return pl.pallas_call(
        functools.partial(_s5_kernel, n_tok=n_tok),
        grid=(n_grp, n_t),
        in_specs=[pl.BlockSpec((None, GROUP_WIDTH // LANES, rows, LANES), lambda b, i: (b, 0, i, 0)), st_spec,
                  full(ab), full(wb), full(wc), full(dvec), full(wglu)],
        out_specs=(pl.BlockSpec((None, GROUP_WIDTH // LANES, rows, LANES), lambda b, i: (b, 0, i, 0)), st_spec),
        out_shape=(jax.ShapeDtypeStruct((n_grp, GROUP_WIDTH // LANES, t_len * SEQ_GROUP, LANES), F32),
                   jax.ShapeDtypeStruct((n_grp, SEQ_GROUP, 2 * S5_CH), F32)),
        scratch_shapes=[pltpu.VMEM((SEQ_GROUP, 2 * S5_CH), F32), pltpu.VMEM((rows, 2 * S5_CH), F32),
                        pltpu.VMEM((rows, 2 * S5_CH), F32)],
        compiler_params=_cparams(2),
        name="s5_t%d" % t_len,
    )(u_tm, x0, ab, wb, wc, dvec, wglu)


CHUNKS = D_MODEL // LANES


def _store_chunked(ref, x):
    rows = x.shape[0]
    for j in range(CHUNKS):
        ref[pl.ds(j, rows, stride=CHUNKS), :] = x[:, j * LANES:(j + 1) * LANES]


def _load_chunked(ref, rows):
    return jnp.concatenate([ref[pl.ds(j, rows, stride=CHUNKS), :] for j in range(CHUNKS)], axis=1)


def _post_mix_kernel(x_ref, oap_ref, obp_ref, ocp_ref, odp_ref, oas_ref, obs_ref, ocs_ref, ods_ref,
                     wout_ref, lnw_ref, lnb_ref, rw_ref, rb_ref,
                     x1_ref, x1c_ref, meta_ref, gate_ref, cnt_ref, carry_s, *, prompt_tiles):
    i = pl.program_id(0)

    @pl.when(i == 0)
    def _():
        carry_s[...] = jnp.zeros_like(carry_s)

    is_prompt = i < prompt_tiles
    seq = i % SEQ_GROUP
    pick = lambda p, s_ref: jnp.where(is_prompt, p, s_ref[...])

    def part(a, j):
        ah, al = _split2(a)
        rows_j = slice(j * GROUP_WIDTH, (j + 1) * GROUP_WIDTH)
        return _dot(ah, wout_ref[0, rows_j, :]) + (_dot(al, wout_ref[0, rows_j, :]) + _dot(ah, wout_ref[1, rows_j, :]))

    mix = part(pick(_slab_load_rows(oap_ref, seq, POST_TILE), oas_ref), 0)
    mix += part(pick(obp_ref[...], obs_ref), 1)
    mix += part(pick(ocp_ref[...], ocs_ref), 2)
    mix += part(pick(_slab_load_rows(odp_ref, seq, POST_TILE), ods_ref), 3)
    x1 = _layer_norm(DN_ALPHA * x_ref[...] + mix, lnw_ref[...], lnb_ref[...])
    x1_ref[...] = x1
    _store_chunked(x1c_ref, x1)

    xh, xl = _split2(x1)
    logits = _dot(xh, rw_ref[0]) + (_dot(xh, rw_ref[1]) + _dot(xl, rw_ref[0])) + rb_ref[...]
    rows = logits.shape[0]
    lt = logits.T[0:N_EXPERTS, :]
    eid = lax.broadcasted_iota(I32, (N_EXPERTS, rows), 0)
    work = lt
    sel_i, sel_v = [], []
    for _ in range(TOP_K):
        m = jnp.max(work, axis=0, keepdims=True)
        j = jnp.min(jnp.where(work == m, eid, N_EXPERTS), axis=0, keepdims=True)
        sel_i.append(j)
        sel_v.append(m)
        work = jnp.where(eid == j, -jnp.inf, work)
    e = [jnp.exp(v - sel_v[0]) for v in sel_v]
    den = (e[0] + e[1]) + (e[2] + e[3])
    onehot = jnp.zeros((N_EXPERTS, rows), F32)
    for j in sel_i:
        onehot = onehot + (eid == j).astype(F32)
    oh = onehot.astype(BF16)
    rr = lax.broadcasted_iota(I32, (rows, rows), 0)
    cc = lax.broadcasted_iota(I32, (rows, rows), 1)
    before = _dot(oh, (rr < cc).astype(BF16)) + carry_s[:, 0:1]
    carry_s[...] = carry_s[...] + _dot(oh, jnp.ones((rows, LANES), BF16))
    ranks = [jnp.sum(jnp.where(eid == j, before, 0.0), axis=0, keepdims=True).astype(I32) for j in sel_i]
    meta_ref[...] = jnp.concatenate(sel_i + ranks, axis=0)
    gate_ref[...] = jnp.concatenate([ej / den for ej in e] + [jnp.zeros((SUBLANES - TOP_K, rows), F32)], axis=0)
    cnt_ref[...] = carry_s[...]


def _post_mix(x_all, outs_p, outs_s, n_seq_p, t_len_p, wout_l, lnw, lnb, rw_l, rb_l):
    n = x_all.shape[0]
    tile = POST_TILE
    n_t = t_len_p // tile
    prompt_tiles = n_seq_p * n_t
    w = GROUP_WIDTH
    p_idx = lambda i: jnp.minimum(i, prompt_tiles - 1)
    p_row = lambda i: (p_idx(i) % n_seq_p) * n_t + p_idx(i) // n_seq_p
    x_row = lambda i: jnp.where(i < prompt_tiles, p_row(i), i)
    row = lambda width: pl.BlockSpec((tile, width), lambda i: (x_row(i), 0))
    tm_p = pl.BlockSpec((w // LANES, tile * SEQ_GROUP, LANES), lambda i: (0, p_idx(i) // n_seq_p, 0))
    bm_p = pl.BlockSpec((tile, w), lambda i: (p_row(i), 0))
    bm_s = pl.BlockSpec((tile, w), lambda i: (jnp.maximum(i - prompt_tiles, 0), 0))
    per_token = pl.BlockSpec((SUBLANES, tile), lambda i: (0, x_row(i)))

    def full(a):
        return pl.BlockSpec(a.shape, lambda i: (0,) * a.ndim)

    x1, x1c, meta, gate, counts = pl.pallas_call(
        functools.partial(_post_mix_kernel, prompt_tiles=prompt_tiles),
        grid=(n // tile,),
        in_specs=[row(D_MODEL), tm_p, bm_p, bm_p, tm_p, bm_s, bm_s, bm_s, bm_s, full(wout_l), full(lnw), full(lnb),
                  full(rw_l), full(rb_l)],
        out_specs=(row(D_MODEL), pl.BlockSpec((tile * CHUNKS, LANES), lambda i: (x_row(i), 0)),
                   per_token, per_token, pl.BlockSpec((N_EXPERTS, LANES), lambda i: (0, 0))),
        out_shape=(jax.ShapeDtypeStruct((n, D_MODEL), F32), jax.ShapeDtypeStruct((n * CHUNKS, LANES), F32),
                   jax.ShapeDtypeStruct((SUBLANES, n), I32), jax.ShapeDtypeStruct((SUBLANES, n), F32),
                   jax.ShapeDtypeStruct((N_EXPERTS, LANES), F32)),
        scratch_shapes=[pltpu.VMEM((N_EXPERTS, LANES), F32)],
        compiler_params=_cparams(1),
        name="post_mix",
    )(x_all, *outs_p, *outs_s, wout_l, lnw, lnb, rw_l, rb_l)
    idx, rank = meta[0:TOP_K].T, meta[TOP_K:2 * TOP_K].T
    gate_cols = jnp.pad(gate[0:TOP_K].T, ((0, 0), (0, LANES - TOP_K)))
    return x1, x1c, idx, rank, gate_cols, counts[:, 0]


def _dispatch_kernel(gend_ref, dest_ref, x_ref, xs_hbm, zero_s, sem, *, tokens, n_tiles):
    i = pl.program_id(0)

    def zero_tile(first_row):
        start = pl.multiple_of(first_row * CHUNKS, MOE_TILE * CHUNKS)
        return pltpu.make_async_copy(zero_s, xs_hbm.at[pl.ds(start, MOE_TILE * CHUNKS)], sem)

    @pl.when(i == 0)
    def _():
        zero_s[...] = jnp.zeros_like(zero_s)
        for e in range(N_EXPERTS):
            @pl.when(gend_ref[e + 1] > gend_ref[e])
            def _():
                zero_tile(gend_ref[e + 1] - MOE_TILE).start()
        for e in range(N_EXPERTS):
            @pl.when(gend_ref[e + 1] > gend_ref[e])
            def _():
                zero_tile(gend_ref[e + 1] - MOE_TILE).wait()

        def tail(t, carry):
            cp = zero_tile(t * MOE_TILE)
            cp.start()
            cp.wait()
            return carry

        lax.fori_loop(gend_ref[N_EXPERTS] // MOE_TILE, n_tiles, tail, 0)

    def row_copy(n, slot):
        src = pl.multiple_of(n * CHUNKS, CHUNKS)
        dst = pl.multiple_of(dest_ref[n * TOP_K + slot] * CHUNKS, CHUNKS)
        return pltpu.make_async_copy(x_ref.at[pl.ds(src, CHUNKS)], xs_hbm.at[pl.ds(dst, CHUNKS)], sem)

    def issue(n, carry):
        for slot in range(TOP_K):
            row_copy(n, slot).start(priority=slot % 2)
        return carry

    lax.fori_loop(0, tokens, issue, 0, unroll=8)
    for slot in range(TOP_K):
        pltpu.make_async_copy(x_ref, xs_hbm.at[pl.ds(0, tokens * CHUNKS)], sem).wait()


def _dispatch(x1c, dest_flat, gend, n_rows):
    n = x1c.shape[0] // CHUNKS
    tokens = DISPATCH_TOKENS if n % DISPATCH_TOKENS == 0 else ROW_TILE
    return pl.pallas_call(
        functools.partial(_dispatch_kernel, tokens=tokens, n_tiles=n_rows // MOE_TILE),
        grid_spec=pltpu.PrefetchScalarGridSpec(
            num_scalar_prefetch=1,
            grid=(n // tokens,),
            in_specs=[pl.BlockSpec((tokens * TOP_K,), lambda i, ge: (i,), memory_space=pltpu.SMEM),
                      pl.BlockSpec((tokens * CHUNKS, LANES), lambda i, ge: (i, 0))],
            out_specs=pl.BlockSpec(memory_space=pl.ANY),
            scratch_shapes=[pltpu.VMEM((MOE_TILE * CHUNKS, LANES), F32), pltpu.SemaphoreType.DMA(())],
        ),
        out_shape=jax.ShapeDtypeStruct((n_rows * CHUNKS, LANES), F32),
        compiler_params=_cparams(1),
        name="moe_dispatch",
    )(gend, dest_flat, x1c)


PAIR_BLOCK = 2 * LANES


def _expert_prep_kernel(w1_ref, w2_ref, w1p_ref, w2b_ref):
    src = lax.broadcasted_iota(I32, (PAIR_BLOCK, PAIR_BLOCK), 0)
    dst = lax.broadcasted_iota(I32, (PAIR_BLOCK, PAIR_BLOCK), 1)
    perm = (src == jnp.where(dst < LANES, 2 * dst, 2 * (dst - LANES) + 1)).astype(BF16)
    for c in range(2 * D_FF // PAIR_BLOCK):
        cols = slice(c * PAIR_BLOCK, (c + 1) * PAIR_BLOCK)
        w1p_ref[:, cols] = _dot(w1_ref[:, cols].astype(BF16), perm).astype(BF16)
    w2b_ref[...] = w2_ref[...].astype(BF16)


def _expert_prep(exp_w1, exp_w2):
    n_l, n_e = exp_w1.shape[:2]
    spec = lambda r, c: pl.BlockSpec((None, None, r, c), lambda i: (i // n_e, i % n_e, 0, 0))
    return pl.pallas_call(
        _expert_prep_kernel,
        grid=(n_l * n_e,),
        in_specs=[spec(D_MODEL, 2 * D_FF), spec(D_FF, D_MODEL)],
        out_specs=(spec(D_MODEL, 2 * D_FF), spec(D_FF, D_MODEL)),
        out_shape=(jax.ShapeDtypeStruct(exp_w1.shape, BF16), jax.ShapeDtypeStruct(exp_w2.shape, BF16)),
        compiler_params=_cparams(1),
        name="expert_prep",
    )(exp_w1, exp_w2)


def _expert_kernel(te_ref, nreal_ref, xs_ref, w1_ref, b1_ref, w2_ref, b2_ref, o_ref, act_s):
    i = pl.program_id(0)
    nreal = nreal_ref[0]

    def hidden(slot):
        x = _load_chunked(xs_ref, MOE_TILE).astype(BF16)
        for c in range(D_FF // PAIR_BLOCK):
            cols = slice(2 * c * PAIR_BLOCK, 2 * (c + 1) * PAIR_BLOCK)
            h = _dot(x, w1_ref[:, cols]) + b1_ref[:, cols]
            h_glu = jnp.minimum(jnp.concatenate([h[:, 0:128], h[:, 256:384]], axis=1), SWIGLU_LIMIT)
            h_lin = jnp.clip(jnp.concatenate([h[:, 128:256], h[:, 384:512]], axis=1),
                             -SWIGLU_LIMIT, SWIGLU_LIMIT)
            act = h_glu * _sigmoid(SWIGLU_ALPHA * h_glu) * (h_lin + 1.0)
            act_s[slot, :, c * PAIR_BLOCK:(c + 1) * PAIR_BLOCK] = act.astype(BF16)

    def output(slot):
        _store_chunked(o_ref, _dot(act_s[slot], w2_ref[...]) + b2_ref[...])

    @pl.when(i == 0)
    def _():
        hidden(0)

    @pl.when((i > 0) & (i < nreal))
    def _():
        output((i - 1) % 2)
        hidden(i % 2)

    @pl.when((i > 0) & (i == nreal))
    def _():
        output((i - 1) % 2)

    @pl.when(i > nreal)
    def _():
        o_ref[...] = jnp.zeros_like(o_ref)


def _experts(xs, te, nreal, w1d, b1d, w2b, b2, layer):
    n_tiles = xs.shape[0] // (MOE_TILE * CHUNKS)
    cur = lambda i: jnp.minimum(i, n_tiles - 1)
    prev = lambda i: jnp.maximum(i - 1, 0)
    return pl.pallas_call(
        _expert_kernel,
        grid_spec=pltpu.PrefetchScalarGridSpec(
            num_scalar_prefetch=2,
            grid=(n_tiles + 1,),
            in_specs=[pl.BlockSpec((MOE_TILE * CHUNKS, LANES), lambda i, te, nr: (jnp.minimum(i, nr[0] - 1), 0)),
                      pl.BlockSpec((None, None, D_MODEL, 2 * D_FF), lambda i, te, nr: (layer, te[cur(i)], 0, 0)),
                      pl.BlockSpec((None, None, 1, 2 * D_FF), lambda i, te, nr: (layer, te[cur(i)], 0, 0)),
                      pl.BlockSpec((None, None, D_FF, D_MODEL), lambda i, te, nr: (layer, te[prev(i)], 0, 0)),
                      pl.BlockSpec((None, None, 1, D_MODEL), lambda i, te, nr: (layer, te[prev(i)], 0, 0))],
            out_specs=pl.BlockSpec((MOE_TILE * CHUNKS, LANES), lambda i, te, nr: (prev(i), 0)),
            scratch_shapes=[pltpu.VMEM((2, MOE_TILE, D_FF), BF16)],
        ),
        out_shape=jax.ShapeDtypeStruct(xs.shape, F32),
        compiler_params=_cparams(1),
        name="moe_experts",
    )(te, nreal, xs, w1d, b1d, w2b, b2)


def _combine_kernel(dest_ref, dest_next_ref, gate_ref, x1_ref, lnw_ref, lnb_ref, ys_hbm, o_ref, buf_s, sem,
                    *, tokens):
    i = pl.program_id(0)
    cur = i % 2

    def fetch(dref, half):
        def issue(n, carry):
            dst = pl.multiple_of(n * CHUNKS, CHUNKS)
            for slot in range(TOP_K):
                src = pl.multiple_of(dref[n * TOP_K + slot] * CHUNKS, CHUNKS)
                pltpu.make_async_copy(ys_hbm.at[pl.ds(src, CHUNKS)], buf_s.at[half, slot, pl.ds(dst, CHUNKS)],
                                      sem.at[half]).start(priority=slot % 2)
            return carry

        lax.fori_loop(0, tokens, issue, 0, unroll=8)

    @pl.when(i == 0)
    def _():
        fetch(dest_ref, 0)

    @pl.when(i + 1 < pl.num_programs(0))
    def _():
        fetch(dest_next_ref, 1 - cur)

    for slot in range(TOP_K):
        pltpu.make_async_copy(ys_hbm.at[pl.ds(0, tokens * CHUNKS)], buf_s.at[cur, slot], sem.at[cur]).wait()
    gate = gate_ref[...]
    ffn = gate[:, 0:1] * _load_chunked(buf_s.at[cur, 0], tokens)
    for slot in range(1, TOP_K):
        ffn = ffn + gate[:, slot:slot + 1] * _load_chunked(buf_s.at[cur, slot], tokens)
    o_ref[...] = _layer_norm(DN_ALPHA * x1_ref[...] + ffn, lnw_ref[...], lnb_ref[...])


def _combine(dest_flat, gate, x1, lnw, lnb, ys):
    n = x1.shape[0]
    tokens = POST_TILE
    last = n // tokens - 1
    return pl.pallas_call(
        functools.partial(_combine_kernel, tokens=tokens),
        grid=(n // tokens,),
        in_specs=[pl.BlockSpec((tokens * TOP_K,), lambda i: (i,), memory_space=pltpu.SMEM),
                  pl.BlockSpec((tokens * TOP_K,), lambda i: (jnp.minimum(i + 1, last),), memory_space=pltpu.SMEM),
                  pl.BlockSpec((tokens, LANES), lambda i: (i, 0)),
                  pl.BlockSpec((tokens, D_MODEL), lambda i: (i, 0)),
                  pl.BlockSpec((1, D_MODEL), lambda i: (0, 0)),
                  pl.BlockSpec((1, D_MODEL), lambda i: (0, 0)),
                  pl.BlockSpec(memory_space=pl.ANY)],
        out_specs=pl.BlockSpec((tokens, D_MODEL), lambda i: (i, 0)),
        out_shape=jax.ShapeDtypeStruct((n, D_MODEL), F32),
        scratch_shapes=[pltpu.VMEM((2, TOP_K, tokens * CHUNKS, LANES), F32), pltpu.SemaphoreType.DMA((2,))],
        compiler_params=_cparams(1),
        name="moe_combine",
    )(dest_flat, dest_flat, gate, x1, lnw, lnb, ys)


def _moe(x1, x1c, idx, rank, gate, counts, w1d, b1d, w2b, b2, lnw, lnb, layer):
    n = x1.shape[0]
    n_tiles = -(-(n * TOP_K + N_EXPERTS * (MOE_TILE - 1)) // MOE_TILE)
    cnt = counts.astype(I32)
    gsz = ((cnt + (MOE_TILE - 1)) // MOE_TILE) * MOE_TILE
    gend = jnp.cumsum(gsz)
    goff = gend - gsz
    dest = (goff[idx] + rank).reshape(-1)
    gend0 = jnp.concatenate([jnp.zeros((1,), I32), gend])
    nreal = (gend[-1:] // MOE_TILE).astype(I32)
    tile_start = jnp.arange(n_tiles, dtype=I32) * MOE_TILE
    te = jnp.minimum(jnp.sum((gend[None, :] <= tile_start[:, None]).astype(I32), axis=1), N_EXPERTS - 1)
    xs = _dispatch(x1c, dest, gend0, n_tiles * MOE_TILE)
    ys = _experts(xs, te, nreal, w1d, b1d, w2b, b2, layer)
    return _combine(dest, gate, x1, lnw, lnb, ys)


def _to_time_major(rows, n_seq, t_len):
    c = rows.shape[-1]
    x = rows.reshape(n_seq // SEQ_GROUP, SEQ_GROUP, t_len, c)
    return jnp.transpose(x, (0, 2, 1, 3)).reshape(n_seq // SEQ_GROUP, t_len * SEQ_GROUP, c)


def _from_time_major(x, n_seq, t_len):
    c = x.shape[-1]
    x = x.reshape(n_seq // SEQ_GROUP, t_len, SEQ_GROUP, c)
    return jnp.transpose(x, (0, 2, 1, 3)).reshape(n_seq * t_len, c)


def _block_diag_state(s, dk):
    st = jnp.swapaxes(s, 2, 3)
    eye = jnp.eye(N_HEADS, dtype=s.dtype)
    return jnp.einsum("bhvk,hg->bhvgk", st, eye).reshape(s.shape[0], GROUP_WIDTH, N_HEADS * dk)


def _unblock_state(st, dk):
    b = st.shape[0]
    x = st.reshape(b, N_HEADS, HEAD_DIM, N_HEADS, dk)
    x = jnp.stack([x[:, h, :, h, :] for h in range(N_HEADS)], axis=1)
    return jnp.swapaxes(x, 2, 3)


def _rwkv_state_in(s):
    b = s.shape[0]
    x = s.reshape(b // SEQ_GROUP, SEQ_GROUP, N_HEADS, HEAD_DIM, HEAD_DIM)
    return jnp.transpose(x, (0, 3, 1, 2, 4)).reshape(b // SEQ_GROUP, HEAD_DIM, SEQ_GROUP, GROUP_WIDTH)


def _rwkv_state_out(x, b):
    x = x.reshape(b // SEQ_GROUP, HEAD_DIM, SEQ_GROUP, N_HEADS, HEAD_DIM)
    return jnp.transpose(x, (0, 2, 3, 1, 4)).reshape(b, N_HEADS, HEAD_DIM, HEAD_DIM)


def _pad_rows(w, row0, n_rows):
    out = jnp.zeros((n_rows, w.shape[1]), w.dtype)
    return out.at[row0:row0 + w.shape[0]].set(w)


def kernel(x_prompt, x_sample, state_rwkv, state_rwkv_shift, state_hgrn, state_gla, state_s5_re, state_s5_im,
           w_in, rw_mu, rw_w0, rw_w2, rw_a0, rw_a2, rw_g2, rw_kk, rw_ka, rw_rk, rw_lnx_w, rw_lnx_b,
           hg_lb_logits, hg_norm_w, gla_w_gk2, gla_b_gk, gla_norm_w,
           s5_A_re, s5_A_im, s5_log_dt, s5_B_re, s5_B_im, s5_C_re, s5_C_im, s5_D, s5_w_glu,
           w_out, ln1_w, ln1_b, router_w, router_b, exp_w1, exp_b1, exp_w2, exp_b2, ln2_w, ln2_b):
    bp, tp, _ = x_prompt.shape
    bs, ts, _ = x_sample.shape
    n_p, n_s = bp * tp, bs * ts
    groups = ((0, bp, tp), (n_p, bs, ts))
    assert bp == SEQ_GROUP and tp % POST_TILE == 0 and bs % SEQ_GROUP == 0 and n_s % POST_TILE == 0
    assert ts <= SUB_CHUNK

    c = np.cumsum([0, RW_COLS, 1024, 784, 256])
    rw_c, hg_c, gl_c, s5_c = (w_in[:, :, c[j]:c[j + 1]] for j in range(4))
    gl_q, gl_k, gl_v, gl_lo, gl_g = (gl_c[:, :, a:b] for a, b in
                                     ((0, 128), (128, 256), (256, 512), (512, 528), (528, 784)))
    zpad = jnp.zeros((DEPTH, D_MODEL, 128 - GLA_GK_LORA), w_in.dtype)
    def two_terms(w):
        hi = w.astype(BF16)
        return jnp.stack([hi, (w - hi.astype(F32)).astype(BF16)], axis=1)

    w_in_p = two_terms(jnp.concatenate([hg_c, gl_v, gl_g, gl_q, gl_k, gl_lo, zpad, rw_c, s5_c], axis=2))
    w_out_b = two_terms(w_out)
    w1d, w2b = _expert_prep(exp_w1, exp_w2)
    b1d = jnp.swapaxes(exp_b1.reshape(DEPTH, N_EXPERTS, 2 * D_FF // PAIR_BLOCK, LANES, 2), -1, -2)
    b1d = b1d.reshape(DEPTH, N_EXPERTS, 1, 2 * D_FF)
    b2r = exp_b2[:, :, None, :]
    rw_pad = jnp.pad(router_w, ((0, 0), (0, 0), (0, LANES - N_EXPERTS)))
    rw_hi = rw_pad.astype(BF16)
    rw_lo = (rw_pad - rw_hi.astype(F32)).astype(BF16)
    rw_split = jnp.stack([rw_hi, rw_lo], axis=1)
    rb_pad = jnp.pad(router_b, ((0, 0), (0, LANES - N_EXPERTS)), constant_values=-1e30)[:, None, :]

    lbs = jnp.cumsum(jax.nn.softmax(hg_lb_logits.astype(F32), axis=0), axis=0)
    lbs = lbs - lbs[:1]
    lb3 = jnp.stack([lbs, jnp.log(lbs), jnp.log1p(-lbs)], axis=1)
    lb3 = jnp.pad(lb3, ((0, 0), (0, SUBLANES - 3), (0, 0)))

    ab_re, ab_im, bb_re, bb_im = _s5_prep(s5_A_re, s5_A_im, s5_log_dt, s5_B_re, s5_B_im)
    eye_g = jnp.eye(S5_NGROUPS, dtype=F32)
    wb = jnp.stack([jnp.einsum("lgph,gk->lghkp", t, eye_g).reshape(DEPTH, GROUP_WIDTH, S5_CH)
                    for t in (bb_re, bb_im)], axis=2).reshape(DEPTH, GROUP_WIDTH, 2 * S5_CH).astype(BF16)
    wc = jnp.concatenate([jnp.einsum("lghp,gk->lgpkh", t, eye_g).reshape(DEPTH, S5_CH, GROUP_WIDTH)
                          for t in (s5_C_re, -s5_C_im)], axis=1).astype(BF16)
    ab = jnp.concatenate([ab_re, ab_im], axis=1)
    ab = jnp.pad(ab, ((0, 0), (0, SUBLANES - 2), (0, 0)))
    wglu_b = s5_w_glu.astype(BF16)

    lora = jnp.stack([jnp.stack([_pad_rows(rw_w2[l], 0, 128), _pad_rows(rw_a2[l], 32, 128),
                                 _pad_rows(rw_g2[l], 64, 128)]) for l in range(DEPTH)])
    rw_vec = jnp.stack([rw_w0, rw_a0, rw_kk, rw_ka, rw_rk, rw_lnx_w, rw_lnx_b, jnp.zeros_like(rw_w0)], axis=1)
    wgk = jnp.stack([_pad_rows(gla_w_gk2[l], 0, 128) for l in range(DEPTH)])

    zeros = lambda shape: jnp.zeros(shape, F32)
    st_in = (
        dict(rw=zeros((DEPTH, bp, N_HEADS, HEAD_DIM, HEAD_DIM)), sh=zeros((DEPTH, bp, RW_COLS)),
             hg=zeros((DEPTH, bp, N_HEADS, HG_DK, HEAD_DIM)), gl=zeros((DEPTH, bp, N_HEADS, GLA_DK, HEAD_DIM)),
             re=zeros((DEPTH, bp, S5_NGROUPS, S5_STATE)), im=zeros((DEPTH, bp, S5_NGROUPS, S5_STATE))),
        dict(rw=state_rwkv, sh=state_rwkv_shift, hg=state_hgrn, gl=state_gla, re=state_s5_re, im=state_s5_im),
    )
    collected = ([], [])

    x_all = jnp.concatenate([x_prompt.reshape(n_p, D_MODEL), x_sample.reshape(n_s, D_MODEL)], axis=0)
    for l in range(DEPTH):
        h_gate_p, p_tm_p, u_tm_p = _inproj_prompt(x_all, w_in_p[l], bp, tp)
        h_s = _inproj_rows(x_all, w_in_p[l], n_p, n_s)
        outs = ([], [])
        for gi, (row0, n_seq, t_len) in enumerate(groups):
            st = st_in[gi]
            n_grp = n_seq // SEQ_GROUP
            if gi == 0:
                h_gate = h_gate_p
                p_tm = p_tm_p[None]
                u_tm = u_tm_p[None]
                new_sh = _from_slabs(p_tm_p[:, (t_len - 1) * n_seq:, :])
                to_rows = lambda x_tm: x_tm[0]
            else:
                h_gate = h_s
                p_rw = h_s[:, OFF_RW:OFF_RW + RW_COLS]
                p_tm = _to_slabs(_to_time_major(p_rw, n_seq, t_len))
                u_tm = _to_slabs(_to_time_major(h_s[:, OFF_S5:OFF_S5 + GROUP_WIDTH], n_seq, t_len))
                new_sh = p_rw.reshape(n_seq, t_len, RW_COLS)[:, -1]
                to_rows = lambda x_tm: _from_time_major(_from_slabs(x_tm), n_seq, t_len)
            sh0 = st["sh"][l].reshape(n_grp, SEQ_GROUP, RW_COLS)
            oa_tm, s_rw = _rwkv_call(p_tm, sh0, _rwkv_state_in(st["rw"][l]), rw_mu[l][None, :], lora[l],
                                     rw_vec[l], n_grp, t_len)
            new_rw = _rwkv_state_out(s_rw, n_seq)
            ob, oc, s_hg, s_gl = _gated_call(
                h_gate, n_seq, t_len, _block_diag_state(st["hg"][l], HG_DK), _block_diag_state(st["gl"][l], GLA_DK),
                (lb3[l], hg_norm_w[l][None, :]), (wgk[l], gla_b_gk[l][None, :], gla_norm_w[l][None, :]))
            x0 = jnp.concatenate([st["re"][l].reshape(n_grp, SEQ_GROUP, S5_CH),
                                  st["im"][l].reshape(n_grp, SEQ_GROUP, S5_CH)], axis=-1)
            od_tm, x_t = _s5_call(u_tm, x0, ab[l], wb[l], wc[l], s5_D[l][None, :], wglu_b[l], n_grp, t_len)
            outs[gi].extend([to_rows(oa_tm), ob, oc, to_rows(od_tm)])
            x_t = x_t.reshape(n_seq, 2, S5_NGROUPS, S5_STATE)
            collected[gi].append((new_rw, new_sh, _unblock_state(s_hg, HG_DK), _unblock_state(s_gl, GLA_DK),
                                  x_t[:, 0], x_t[:, 1]))
        x1, x1c, idx, rank, gate, counts = _post_mix(x_all, outs[0], outs[1], bp, tp, w_out_b[l],
                                                     ln1_w[l][None, :], ln1_b[l][None, :],
                                                     rw_split[l], rb_pad[l])
        x_all = _moe(x1, x1c, idx, rank, gate, counts, w1d, b1d, w2b, b2r, ln2_w[l][None, :], ln2_b[l][None, :], l)

    y_prompt = x_all[:n_p].reshape(bp, tp, D_MODEL)
    y_sample = x_all[n_p:].reshape(bs, ts, D_MODEL)
    ps = [jnp.stack([layer[j] for layer in collected[0]]) for j in range(6)]
    ss = [jnp.stack([layer[j] for layer in collected[1]]) for j in range(6)]
    return (y_prompt, y_sample, *ps, *ss)
```
